```python
import math
import jax, jax.numpy as jnp
from jax import lax
import numpy as np

D_MODEL = 1024
BATCH = 2
SEQ = 8192
DEPTH = 1

SSM_HEADS = 8
SSM_HEAD_DIM = 64
D_SSM = SSM_HEADS * SSM_HEAD_DIM
SSM_GROUPS = 2
SSM_STATE = 128
CONV_WIDTH = 4
CHUNK = 256
ATTN_HEADS = 8
ATTN_KV_HEADS = 2
ATTN_HEAD_DIM = 64
D_ATTN = ATTN_HEADS * ATTN_HEAD_DIM
IDX_HEADS = 8
IDX_DIM = 64
TOPK_MAX = 256
Q_BLOCK = 128
N_BUCKETS = 32
MAX_DISTANCE = 128
D_FF = 4 * D_MODEL
D_XBC = D_SSM + 2 * SSM_GROUPS * SSM_STATE
D_IN = (D_SSM + D_XBC + SSM_HEADS
        + D_ATTN + 2 * ATTN_KV_HEADS * ATTN_HEAD_DIM
        + IDX_HEADS * IDX_DIM + IDX_DIM + IDX_HEADS)
D_MIX = D_SSM + D_ATTN
EPS = 1e-6

kernel_name = 'hybrid_ssd_dsa_sandwich_adaln_layer'


def rms_norm(x, w):
    xf = x.astype(jnp.float32)
    y = xf * lax.rsqrt(jnp.mean(xf * xf, axis=-1, keepdims=True) + EPS)
    return (y * w.astype(jnp.float32)).astype(x.dtype)


def t5_bucket(dist):
    n = jnp.maximum(dist, 0)
    max_exact = N_BUCKETS // 2
    nf = jnp.maximum(n, 1).astype(jnp.float32)
    large = max_exact + (jnp.log(nf / max_exact) / math.log(MAX_DISTANCE / max_exact)
                         * (N_BUCKETS - max_exact)).astype(jnp.int32)
    large = jnp.minimum(large, N_BUCKETS - 1)
    return jnp.where(n < max_exact, n, large)


def causal_dwconv(u, w, b):
    ch = u.shape[-1]
    out = lax.conv_general_dilated(
        u, w[:, None, :].astype(u.dtype), window_strides=(1,),
        padding=((w.shape[0] - 1, 0),), dimension_numbers=('NWC', 'WIO', 'NWC'),
        feature_group_count=ch)
    return out + b


def ssd_scan(xh, dt, a, bm, cm, d_skip):
    bsz, t, h, p = xh.shape
    g, n = bm.shape[2], bm.shape[3]
    r = h // g
    lc = math.gcd(t, CHUNK)
    nc = t // lc
    xdt = (xh.astype(jnp.float32) * dt[..., None]).reshape(bsz, nc, lc, g, r, p)
    adt = (a * dt).reshape(bsz, nc, lc, h)
    a_cs = jnp.cumsum(adt, axis=2)
    bc = bm.astype(jnp.float32).reshape(bsz, nc, lc, g, n)
    cc = cm.astype(jnp.float32).reshape(bsz, nc, lc, g, n)
    causal = jnp.tril(jnp.ones((lc, lc), dtype=bool))
    seg = a_cs[:, :, :, None, :] - a_cs[:, :, None, :, :]
    lmat = jnp.exp(jnp.where(causal[None, None, :, :, None], seg, -jnp.inf))
    lmat = lmat.reshape(bsz, nc, lc, lc, g, r)
    cb = jnp.einsum('bclgn,bcsgn->bclsg', cc, bc)
    y_diag = jnp.einsum('bclsgr,bcsgrp->bclgrp', cb[..., None] * lmat, xdt)
    decay_states = jnp.exp(a_cs[:, :, -1:, :] - a_cs).reshape(bsz, nc, lc, g, r)
    states = jnp.einsum('bclgn,bclgrp->bcgrpn', bc, xdt * decay_states[..., None])
    chunk_decay = jnp.exp(a_cs[:, :, -1, :]).reshape(bsz, nc, g, r)

    def step(hs, inp):
        s_c, dec_c = inp
        return hs * dec_c[..., None, None] + s_c, hs

    h0 = jnp.zeros_like(states[:, 0])
    _, prev = lax.scan(step, h0, (jnp.moveaxis(states, 1, 0), jnp.moveaxis(chunk_decay, 1, 0)))
    prev = jnp.moveaxis(prev, 0, 1)
    y_off = jnp.einsum('bclgn,bcgrpn->bclgrp', cc, prev) * \
        jnp.exp(a_cs).reshape(bsz, nc, lc, g, r)[..., None]
    y = y_diag + y_off + xh.astype(jnp.float32).reshape(bsz, nc, lc, g, r, p) * \
        d_skip.astype(jnp.float32).reshape(g, r)[:, :, None]
    return y.reshape(bsz, t, h * p).astype(xh.dtype)


def dsa_attention(q, k, v, q_idx, k_idx, w_idx, rel_bias):
    bsz, t, hkv, r, dh = q.shape
    topk = min(TOPK_MAX, t // 4)
    nblk = t // Q_BLOCK
    spos = jnp.arange(t, dtype=jnp.int32)

    def to_blocks(arr):
        return jnp.moveaxis(arr.reshape(bsz, nblk, Q_BLOCK, *arr.shape[2:]), 1, 0)

    def block(inp):
        qb, qib, wb, t0 = inp
        tpos = t0 + jnp.arange(Q_BLOCK, dtype=jnp.int32)
        s = jax.nn.relu(jnp.einsum('bqhd,bsd->bqhs', qib, k_idx).astype(jnp.float32)
                        * (IDX_DIM ** -0.5))
        score = jnp.einsum('bqh,bqhs->bqs', wb.astype(jnp.float32) * (IDX_HEADS ** -0.5), s)
        score = jnp.where((spos[None, :] <= tpos[:, None])[None], score, -jnp.inf)
        _, idx = lax.top_k(score, topk)
        kg = jax.vmap(lambda kb, ib: kb[ib])(k, idx)
        vg = jax.vmap(lambda vb, ib: vb[ib])(v, idx)
        logits = jnp.einsum('bqgrd,bqkgd->bqgrk', qb, kg).astype(jnp.float32) * (dh ** -0.5)
        bias = rel_bias[t5_bucket(tpos[None, :, None] - idx)].astype(jnp.float32)
        bias = jnp.transpose(bias.reshape(bsz, Q_BLOCK, topk, hkv, r), (0, 1, 3, 4, 2))
        valid = (idx <= tpos[None, :, None])[:, :, None, None, :]
        p = jax.nn.softmax(jnp.where(valid, logits + bias, -jnp.inf), axis=-1)
        return jnp.einsum('bqgrk,bqkgd->bqgrd', p.astype(vg.dtype), vg)

    t0s = jnp.arange(nblk, dtype=jnp.int32) * Q_BLOCK
    out = lax.map(block, (to_blocks(q), to_blocks(q_idx), to_blocks(w_idx), t0s))
    return jnp.moveaxis(out, 0, 1).reshape(bsz, t, hkv * r * dh)


def setup_inputs(seed: int = 0) -> dict:
    key = jax.random.key(seed)
    ks = jax.random.split(key, 24)
    f32 = jnp.float32

    def nrm(k, shape, scale):
        return jax.random.normal(k, shape, f32) * scale

    def gain(k, shape):
        return 1.0 + 0.05 * jax.random.normal(k, shape, f32)

    dt0 = jnp.exp(jax.random.uniform(ks[9], (DEPTH, SSM_HEADS), f32,
                                     math.log(1e-3), math.log(1e-1)))
    return {
        'x': nrm(ks[0], (BATCH, SEQ, D_MODEL), 1.0),
        'c': nrm(ks[1], (BATCH, D_MODEL), 1.0),
        'w_ada': nrm(ks[2], (DEPTH, D_MODEL, 6 * D_MODEL), D_MODEL ** -0.5),
        'b_ada': nrm(ks[3], (DEPTH, 6 * D_MODEL), 0.02),
        'norm1_pre': gain(ks[4], (DEPTH, D_MODEL)),
        'norm1_post': gain(ks[5], (DEPTH, D_MODEL)),
        'w_in': nrm(ks[6], (DEPTH, D_MODEL, D_IN), D_MODEL ** -0.5),
        'conv_w': nrm(ks[7], (DEPTH, CONV_WIDTH, D_XBC), CONV_WIDTH ** -0.5),
        'conv_b': nrm(ks[8], (DEPTH, D_XBC), 0.02),
        'dt_bias': dt0 + jnp.log(-jnp.expm1(-dt0)),
        'a_log': jnp.log(jax.random.uniform(ks[10], (DEPTH, SSM_HEADS), f32, 1.0, 16.0)),
        'd_skip': gain(ks[11], (DEPTH, SSM_HEADS)),
        'ssm_norm': gain(ks[12], (DEPTH, D_SSM)),
        'rel_bias': nrm(ks[13], (N_BUCKETS, ATTN_HEADS), 0.5),
        'attn_norm': gain(ks[14], (DEPTH, D_ATTN)),
        'w_out': nrm(ks[15], (DEPTH, D_MIX, D_MODEL), D_MIX ** -0.5),
        'norm2_pre': gain(ks[16], (DEPTH, D_MODEL)),
        'norm2_post': gain(ks[17], (DEPTH, D_MODEL)),
        'w_mlp_in': nrm(ks[18], (DEPTH, D_MODEL, D_FF), D_MODEL ** -0.5),
        'w_mlp_out': nrm(ks[19], (DEPTH, D_FF, D_MODEL), D_FF ** -0.5),
    }


def reference(x, c, w_ada, b_ada, norm1_pre, norm1_post, w_in, conv_w, conv_b, dt_bias,
              a_log, d_skip, ssm_norm, rel_bias, attn_norm, w_out, norm2_pre, norm2_post,
              w_mlp_in, w_mlp_out):
    bsz, t, _ = x.shape
    gq = ATTN_HEADS // ATTN_KV_HEADS
    sizes = [D_SSM, D_XBC, SSM_HEADS, D_ATTN, ATTN_KV_HEADS * ATTN_HEAD_DIM,
             ATTN_KV_HEADS * ATTN_HEAD_DIM, IDX_HEADS * IDX_DIM, IDX_DIM, IDX_HEADS]
    points = np.cumsum(sizes)[:-1].tolist()
    for l in range(DEPTH):
        mod = jax.nn.silu(c) @ w_ada[l] + b_ada[l]
        shift1, scale1, gate1, shift2, scale2, gate2 = [m[:, None, :] for m in jnp.split(mod, 6, axis=-1)]

        h = rms_norm(x, norm1_pre[l]) * (1.0 + scale1) + shift1
        proj = h @ w_in[l]
        z, xbc, dt_raw, q, k, v, q_idx, k_idx, w_idx = jnp.split(proj, points, axis=-1)

        xbc = jax.nn.silu(causal_dwconv(xbc, conv_w[l], conv_b[l]))
        xs, bm, cm = jnp.split(xbc, [D_SSM, D_SSM + SSM_GROUPS * SSM_STATE], axis=-1)
        dt = jax.nn.softplus(dt_raw.astype(jnp.float32) + dt_bias[l].astype(jnp.float32))
        a = -jnp.exp(a_log[l].astype(jnp.float32))
        y_ssd = ssd_scan(xs.reshape(bsz, t, SSM_HEADS, SSM_HEAD_DIM), dt, a,
                         bm.reshape(bsz, t, SSM_GROUPS, SSM_STATE),
                         cm.reshape(bsz, t, SSM_GROUPS, SSM_STATE), d_skip[l])
        y_ssd = rms_norm(y_ssd * jax.nn.silu(z), ssm_norm[l])

        y_att = dsa_attention(
            q.reshape(bsz, t, ATTN_KV_HEADS, gq, ATTN_HEAD_DIM),
            k.reshape(bsz, t, ATTN_KV_HEADS, ATTN_HEAD_DIM),
            v.reshape(bsz, t, ATTN_KV_HEADS, ATTN_HEAD_DIM),
            q_idx.reshape(bsz, t, IDX_HEADS, IDX_DIM), k_idx, w_idx, rel_bias)
        y_att = rms_norm(y_att, attn_norm[l])

        mix = jnp.concatenate([y_ssd, y_att], axis=-1) @ w_out[l]
        x = x + gate1 * rms_norm(mix, norm1_post[l])

        h = rms_norm(x, norm2_pre[l]) * (1.0 + scale2) + shift2
        f = jnp.square(jax.nn.relu(h @ w_mlp_in[l])) @ w_mlp_out[l]
        x = x + gate2 * rms_norm(f, norm2_post[l])
    return x
```

```python
import functools
import math

import numpy as np
import jax
import jax.numpy as jnp
from jax import lax
from jax.experimental import pallas as pl
from jax.experimental.pallas import tpu as pltpu

D_MODEL = 1024
SSM_HEADS = 8
SSM_HEAD_DIM = 64
D_SSM = SSM_HEADS * SSM_HEAD_DIM
SSM_GROUPS = 2
SSM_STATE = 128
CONV_WIDTH = 4
CHUNK = 256
ATTN_HEADS = 8
ATTN_KV_HEADS = 2
ATTN_HEAD_DIM = 64
D_ATTN = ATTN_HEADS * ATTN_HEAD_DIM
D_KV = ATTN_KV_HEADS * ATTN_HEAD_DIM
IDX_HEADS = 8
IDX_DIM = 64
TOPK_MAX = 256
N_BUCKETS = 32
MAX_DISTANCE = 128
D_FF = 4 * D_MODEL
D_BC = SSM_GROUPS * SSM_STATE
D_XBC = D_SSM + 2 * D_BC
EPS = 1e-6

LANES = 128
MISC_W = LANES
MISC_KIDX = 0
MISC_WIDX = IDX_DIM
MISC_DT = IDX_DIM + IDX_HEADS
OFF_Z = 0
OFF_XBC = OFF_Z + D_SSM
OFF_Q = OFF_XBC + D_XBC
OFF_K = OFF_Q + D_ATTN
OFF_V = OFF_K + D_KV
OFF_QI = OFF_V + D_KV
OFF_MISC = OFF_QI + IDX_HEADS * IDX_DIM
D_IN_PAD = OFF_MISC + MISC_W

BLK = 128
ROW_TILE = 512
NEG = -1e30
INT_MIN = -2 ** 31

F32 = jnp.float32
BF16 = jnp.bfloat16
HI = lax.Precision.HIGHEST
NT = (((1,), (1,)), ((), ()))


def _silu(x):
    return x / (1.0 + jnp.exp(-x))


def _rms(x, w):
    return x * lax.rsqrt(jnp.mean(x * x, axis=-1, keepdims=True) + EPS) * w


def _ada_kernel(c_ref, w_ref, b_ref, o_ref):
    s = _silu(c_ref[...])
    o_ref[...] = jnp.dot(s, w_ref[...], precision=HI, preferred_element_type=F32) + b_ref[...]


def _ada(c_pad, w, b):
    n = w.shape[1]
    tn = D_MODEL
    return pl.pallas_call(
        _ada_kernel,
        grid=(n // tn,),
        in_specs=[pl.BlockSpec((8, D_MODEL), lambda j: (0, 0)),
                  pl.BlockSpec((D_MODEL, tn), lambda j: (0, j)),
                  pl.BlockSpec((1, tn), lambda j: (0, j))],
        out_specs=pl.BlockSpec((8, tn), lambda j: (0, j)),
        out_shape=jax.ShapeDtypeStruct((8, n), F32),
        name="ada",
    )(c_pad, w, b)


def _inproj_kernel(x_ref, mod_ref, nw_ref, w_ref,
                   z_ref, xbc_ref, q_ref, k_ref, v_ref, qi_ref, misc_ref, miscb_ref):
    x = x_ref[0]
    h = _rms(x, nw_ref[...]) * (1.0 + mod_ref[0, 1:2, :]) + mod_ref[0, 0:1, :]
    hb = h.astype(BF16)

    def seg(lo, width):
        return jnp.dot(hb, w_ref[:, lo:lo + width], preferred_element_type=F32)

    z_ref[0] = seg(OFF_Z, D_SSM)
    xbc_ref[0] = seg(OFF_XBC, D_XBC)
    q_ref[0] = (seg(OFF_Q, D_ATTN) * (ATTN_HEAD_DIM ** -0.5)).astype(BF16)
    k_ref[0] = seg(OFF_K, D_KV).astype(BF16)
    v_ref[0] = seg(OFF_V, D_KV).astype(BF16)
    qi_ref[0] = seg(OFF_QI, IDX_HEADS * IDX_DIM).astype(BF16)
    misc = seg(OFF_MISC, MISC_W)
    misc_ref[0] = misc
    miscb_ref[0] = misc.astype(BF16)


def _inproj(x, mod, nw, w_perm):
    bsz, t, d = x.shape
    tm = min(ROW_TILE, t)

    def tok(width, dtype):
        return (pl.BlockSpec((1, tm, width), lambda b, i: (b, i, 0)),
                jax.ShapeDtypeStruct((bsz, t, width), dtype))

    outs = [tok(D_SSM, F32), tok(D_XBC, F32), tok(D_ATTN, BF16), tok(D_KV, BF16), tok(D_KV, BF16),
            tok(IDX_HEADS * IDX_DIM, BF16), tok(MISC_W, F32), tok(MISC_W, BF16)]
    return pl.pallas_call(
        _inproj_kernel,
        grid=(bsz, t // tm),
        in_specs=[pl.BlockSpec((1, tm, d), lambda b, i: (b, i, 0)),
                  pl.BlockSpec((1, 6, d), lambda b, i: (b, 0, 0)),
                  pl.BlockSpec((1, d), lambda b, i: (0, 0)),
                  pl.BlockSpec((d, D_IN_PAD), lambda b, i: (0, 0))],
        out_specs=[o[0] for o in outs],
        out_shape=[o[1] for o in outs],
        compiler_params=pltpu.CompilerParams(
            dimension_semantics=("arbitrary", "arbitrary"), vmem_limit_bytes=48 * 2 ** 20),
        name="inproj",
    )(x, mod, nw, w_perm)


def _ssd_kernel(xbc_ref, z_ref, misc_ref, cw_ref, cb_ref, dtb_ref, a_ref, e_ref, dskip_ref, nw_ref,
                o_ref, xbuf, state):
    c = pl.program_id(1)
    lc = xbc_ref.shape[1]

    @pl.when(c == 0)
    def _():
        xbuf[0:8, :] = jnp.zeros((8, D_XBC), F32)
        state[...] = jnp.zeros(state.shape, F32)

    @pl.when(c > 0)
    def _():
        xbuf[0:8, :] = xbuf[lc:lc + 8, :]

    xbuf[8:lc + 8, :] = xbc_ref[0]
    u = cb_ref[...]
    for k in range(CONV_WIDTH):
        u = u + xbuf[pl.ds(8 - (CONV_WIDTH - 1) + k, lc), :] * cw_ref[k:k + 1, :]
    u = _silu(u)
    xs = u[:, :D_SSM]
    bm = u[:, D_SSM:D_SSM + D_BC]
    cm = u[:, D_SSM + D_BC:]

    raw = misc_ref[0] + dtb_ref[...]
    dt_t = jnp.maximum(raw, 0.0) + jnp.log1p(jnp.exp(-jnp.abs(raw)))
    adt_t = dt_t * a_ref[...]
    row = lax.broadcasted_iota(jnp.int32, (lc, lc), 0)
    col = lax.broadcasted_iota(jnp.int32, (lc, lc), 1)
    causal = col <= row
    tril = causal.astype(F32)
    acs_t = jnp.dot(tril, adt_t, precision=HI, preferred_element_type=F32)
    expand = e_ref[...]
    dt_e = jnp.dot(dt_t, expand, precision=HI, preferred_element_type=F32)
    acs_e = jnp.dot(acs_t, expand, precision=HI, preferred_element_type=F32)
    acs_row = jnp.transpose(acs_t)

    xdt = xs * dt_e
    a_last = acs_e[lc - 1:lc, :]
    xdec = (xdt * jnp.exp(a_last - acs_e)).astype(BF16)
    xdt_b = xdt.astype(BF16)
    bm_b = bm.astype(BF16)
    cm_b = cm.astype(BF16)
    bm_t = jnp.transpose(bm).astype(BF16)

    heads_per_group = SSM_HEADS // SSM_GROUPS
    gw = heads_per_group * SSM_HEAD_DIM
    y_parts = []
    off_parts = []
    for g in range(SSM_GROUPS):
        bg = bm_b[:, g * SSM_STATE:(g + 1) * SSM_STATE]
        cg = cm_b[:, g * SSM_STATE:(g + 1) * SSM_STATE]
        cb = lax.dot_general(cg, bg, NT, preferred_element_type=F32)
        for r in range(heads_per_group):
            hd = g * heads_per_group + r
            a_col = acs_t[:, MISC_DT + hd:MISC_DT + hd + 1]
            a_row = acs_row[MISC_DT + hd:MISC_DT + hd + 1, :]
            lmat = jnp.exp(jnp.where(causal, a_col - a_row, -jnp.inf))
            mh = (cb * lmat).astype(BF16)
            y_parts.append(jnp.dot(mh, xdt_b[:, hd * SSM_HEAD_DIM:(hd + 1) * SSM_HEAD_DIM],
                                   preferred_element_type=F32))
        prev = state[g]
        off_parts.append(jnp.dot(cg, prev.astype(BF16), preferred_element_type=F32))
        st = jnp.dot(bm_t[g * SSM_STATE:(g + 1) * SSM_STATE, :], xdec[:, g * gw:(g + 1) * gw],
                     preferred_element_type=F32)
        state[g] = prev * jnp.exp(a_last[:, g * gw:(g + 1) * gw]) + st
    y = (jnp.concatenate(y_parts, axis=1)
         + jnp.concatenate(off_parts, axis=1) * jnp.exp(acs_e)
         + xs * dskip_ref[...])
    o_ref[0] = _rms(y * _silu(z_ref[0]), nw_ref[...]).astype(BF16)


def _ssd(xbc, z, misc, cw, cb, dtb_t, a_t, expand, dskip_e, nw):
    bsz, t, _ = xbc.shape
    lc = math.gcd(t, CHUNK)
    nc = t // lc
    gw = (SSM_HEADS // SSM_GROUPS) * SSM_HEAD_DIM

    def const(shape):
        return pl.BlockSpec(shape, lambda b, c: tuple(0 for _ in shape))

    return pl.pallas_call(
        _ssd_kernel,
        grid=(bsz, nc),
        in_specs=[pl.BlockSpec((1, lc, D_XBC), lambda b, c: (b, c, 0)),
                  pl.BlockSpec((1, lc, D_SSM), lambda b, c: (b, c, 0)),
                  pl.BlockSpec((1, lc, MISC_W), lambda b, c: (b, c, 0)),
                  const((CONV_WIDTH, D_XBC)), const((1, D_XBC)), const((1, MISC_W)), const((1, MISC_W)),
                  const((MISC_W, D_SSM)), const((1, D_SSM)), const((1, D_SSM))],
        out_specs=pl.BlockSpec((1, lc, D_SSM), lambda b, c: (b, c, 0)),
        out_shape=jax.ShapeDtypeStruct((bsz, t, D_SSM), BF16),
        scratch_shapes=[pltpu.VMEM((lc + 8, D_XBC), F32),
                        pltpu.VMEM((SSM_GROUPS, SSM_STATE, gw), F32)],
        compiler_params=pltpu.CompilerParams(
            dimension_semantics=("arbitrary", "arbitrary"), vmem_limit_bytes=48 * 2 ** 20),
        name="ssd",
    )(xbc, z, misc, cw, cb, dtb_t, a_t, expand, dskip_e, nw)


def _bucket_tiles():
    s = np.arange(BLK)[:, None]
    t = np.arange(BLK)[None, :]
    max_exact = N_BUCKETS // 2
    tiles = []
    for d0 in (0, BLK):
        n = np.maximum(d0 + t - s, 0)
        large = {}
        for dt_ in (np.float32, np.float64):
            nf = np.maximum(n, 1).astype(dt_)
            lg = max_exact + (np.log(nf / dt_(max_exact)) / dt_(math.log(MAX_DISTANCE / max_exact))
                              * dt_(N_BUCKETS - max_exact)).astype(np.int32)
            large[dt_] = np.minimum(lg, N_BUCKETS - 1)
        assert (large[np.float32] == large[np.float64]).all()
        tiles.append(np.where(n < max_exact, n, large[np.float64]).astype(np.int32))
    assert BLK + 1 >= MAX_DISTANCE
    return np.stack(tiles)


def _to_key(x):
    bits = lax.bitcast_convert_type(x, jnp.int32)
    key = bits ^ (lax.shift_right_arithmetic(bits, 31) & jnp.int32(0x7FFFFFFF))
    return jnp.where(key == -1, 0, key)


def _dsa_kernel(relb_ref, bkt_ref, q_ref, qi_ref, miscq_ref, k_ref, vt_ref, kidx_ref, nw_ref,
                o_ref, keys, bias_s, m_s, l_s, acc_s, jlim, *, topk, idx_bits):
    b = pl.program_id(0)
    qi = pl.program_id(1)
    tq = BLK
    i32 = jnp.int32

    @pl.when((b == 0) & (qi == 0))
    def _init_bias():
        for h in range(ATTN_HEADS):
            for d in range(2):
                bk = bkt_ref[d]
                acc = jnp.zeros((BLK, tq), F32)
                for n in range(N_BUCKETS):
                    acc = jnp.where(bk == n, relb_ref[n, h], acc)
                bias_s[h, d] = acc

    row_i = lax.broadcasted_iota(i32, (BLK, tq), 0)
    col_i = lax.broadcasted_iota(i32, (BLK, tq), 1)
    tri = row_i <= col_i

    w_t = jnp.transpose(miscq_ref[0])[MISC_WIDX:MISC_WIDX + IDX_HEADS, :] * (
        IDX_HEADS ** -0.5 * IDX_DIM ** -0.5)

    def idx_block(j, diag):
        s0 = pl.multiple_of(j * BLK, BLK)
        kb = kidx_ref[0, pl.ds(s0, BLK), MISC_KIDX:MISC_KIDX + IDX_DIM]
        acc = jnp.zeros((BLK, tq), F32)
        for h in range(IDX_HEADS):
            qh = qi_ref[0, :, h * IDX_DIM:(h + 1) * IDX_DIM]
            s = lax.dot_general(kb, qh, NT, preferred_element_type=F32)
            acc = acc + w_t[h:h + 1, :] * jnp.maximum(s, 0.0)
        if diag:
            acc = jnp.where(tri, acc, -jnp.inf)
        keys[pl.ds(s0, BLK), :] = _to_key(acc)

    def idx_body(j, carry):
        idx_block(j, False)
        return carry

    lax.fori_loop(0, qi, idx_body, 0)
    idx_block(qi, True)
    nblk = qi + 1

    def count(pred):
        def body(i, cnt):
            s0 = pl.multiple_of(i * BLK, BLK)
            return cnt + pred(keys[pl.ds(s0, BLK), :], s0).astype(i32)
        cnt = lax.fori_loop(0, nblk, body, jnp.zeros((BLK, tq), i32))
        return jnp.sum(cnt, axis=0, keepdims=True)

    def bit_body(it, thr):
        cand = thr + lax.shift_left(i32(1), i32(31) - it)
        c = count(lambda kk, s0: kk >= cand)
        return jnp.where(c >= topk, cand, thr)

    thr = lax.fori_loop(0, 32, bit_body, jnp.full((1, tq), INT_MIN, i32))
    n_gt = count(lambda kk, s0: kk > thr)
    n_eq = count(lambda kk, s0: kk == thr)
    need = topk - n_gt
    jlim[...] = jnp.full((1, tq), 2 ** idx_bits - 1, i32)

    @pl.when(jnp.max((n_eq > need).astype(i32)) > 0)
    def _ties():
        def jbody(it, p):
            cand = p + lax.shift_left(i32(1), i32(idx_bits - 1) - it)
            c = count(lambda kk, s0: (kk == thr) & (s0 + row_i < cand))
            return jnp.where(c < need, cand, p)
        jlim[...] = lax.fori_loop(0, idx_bits, jbody, jnp.zeros((1, tq), i32))

    jl = jlim[...]

    for h in range(ATTN_HEADS):
        m_s[h] = jnp.full((1, tq), NEG, F32)
        l_s[h] = jnp.zeros((1, tq), F32)
        acc_s[h] = jnp.zeros((ATTN_HEAD_DIM, tq), F32)

    rep = ATTN_HEADS // ATTN_KV_HEADS

    def att_block(j, mode):
        s0 = pl.multiple_of(j * BLK, BLK)
        kk = keys[pl.ds(s0, BLK), :]
        sel = (kk > thr) | ((kk == thr) & (s0 + row_i <= jl))
        if mode == 0:
            sel = sel & tri
        kblk = k_ref[0, pl.ds(s0, BLK), :]
        vtb = vt_ref[0, j]
        for h in range(ATTN_HEADS):
            g = h // rep
            kg = kblk[:, g * ATTN_HEAD_DIM:(g + 1) * ATTN_HEAD_DIM]
            qh = q_ref[0, :, h * ATTN_HEAD_DIM:(h + 1) * ATTN_HEAD_DIM]
            s = lax.dot_general(kg, qh, NT, preferred_element_type=F32)
            bias = relb_ref[N_BUCKETS - 1, h] if mode == 2 else bias_s[h, mode]
            s = jnp.where(sel, s + bias, NEG)
            m_old = m_s[h]
            m_new = jnp.maximum(m_old, jnp.max(s, axis=0, keepdims=True))
            alpha = jnp.exp(m_old - m_new)
            p = jnp.exp(s - m_new)
            l_s[h] = alpha * l_s[h] + jnp.sum(p, axis=0, keepdims=True)
            pv = jnp.dot(vtb[g * ATTN_HEAD_DIM:(g + 1) * ATTN_HEAD_DIM, :], p.astype(BF16),
                         preferred_element_type=F32)
            acc_s[h] = acc_s[h] * alpha + pv
            m_s[h] = m_new

    def far_body(j, carry):
        att_block(j, 2)
        return carry

    lax.fori_loop(0, jnp.maximum(qi - 1, 0), far_body, 0)

    @pl.when(qi >= 1)
    def _sub():
        att_block(qi - 1, 1)

    att_block(qi, 0)

    out_t = jnp.concatenate([acc_s[h] / l_s[h] for h in range(ATTN_HEADS)], axis=0)
    o_ref[0] = _rms(jnp.transpose(out_t), nw_ref[...]).astype(BF16)


def _dsa(rel_bias, q, qi, misc, k, vt, miscb, nw):
    bsz, t, _ = q.shape
    nq = t // BLK
    topk = min(TOPK_MAX, t // 4)
    idx_bits = max(1, (t - 1).bit_length())
    bkt = jnp.asarray(_bucket_tiles())
    kern = functools.partial(_dsa_kernel, topk=topk, idx_bits=idx_bits)
    return pl.pallas_call(
        kern,
        grid=(bsz, nq),
        in_specs=[pl.BlockSpec(memory_space=pltpu.SMEM),
                  pl.BlockSpec((2, BLK, BLK), lambda b, i: (0, 0, 0)),
                  pl.BlockSpec((1, BLK, D_ATTN), lambda b, i: (b, i, 0)),
                  pl.BlockSpec((1, BLK, IDX_HEADS * IDX_DIM), lambda b, i: (b, i, 0)),
                  pl.BlockSpec((1, BLK, MISC_W), lambda b, i: (b, i, 0)),
                  pl.BlockSpec((1, t, D_KV), lambda b, i: (b, 0, 0)),
                  pl.BlockSpec((1, nq, D_KV, BLK), lambda b, i: (b, 0, 0, 0)),
                  pl.BlockSpec((1, t, MISC_W), lambda b, i: (b, 0, 0)),
                  pl.BlockSpec((1, D_ATTN), lambda b, i: (0, 0))],
        out_specs=pl.BlockSpec((1, BLK, D_ATTN), lambda b, i: (b, i, 0)),
        out_shape=jax.ShapeDtypeStruct((bsz, t, D_ATTN), BF16),
        scratch_shapes=[pltpu.VMEM((t, BLK), jnp.int32),
                        pltpu.VMEM((ATTN_HEADS, 2, BLK, BLK), F32),
                        pltpu.VMEM((ATTN_HEADS, 1, BLK), F32),
                        pltpu.VMEM((ATTN_HEADS, 1, BLK), F32),
                        pltpu.VMEM((ATTN_HEADS, ATTN_HEAD_DIM, BLK), F32),
                        pltpu.VMEM((1, BLK), jnp.int32)],
        compiler_params=pltpu.CompilerParams(
            dimension_semantics=("arbitrary", "arbitrary"), vmem_limit_bytes=48 * 2 ** 20),
        name="dsa",
    )(rel_bias, bkt, q, qi, misc, k, vt, miscb, nw)


def _tail_kernel(x_ref, ys_ref, ya_ref, mod_ref, wo_ref, n1_ref, n2_ref, n3_ref, w1_ref, w2_ref, o_ref):
    x = x_ref[0]
    mix = (jnp.dot(ys_ref[0], wo_ref[0:D_SSM, :], preferred_element_type=F32)
           + jnp.dot(ya_ref[0], wo_ref[D_SSM:, :], preferred_element_type=F32))
    x1 = x + mod_ref[0, 2:3, :] * _rms(mix, n1_ref[...])
    h = (_rms(x1, n2_ref[...]) * (1.0 + mod_ref[0, 4:5, :]) + mod_ref[0, 3:4, :]).astype(BF16)
    f = jnp.zeros(x.shape, F32)
    step = D_MODEL
    for c0 in range(0, D_FF, step):
        a = jnp.maximum(jnp.dot(h, w1_ref[:, c0:c0 + step], preferred_element_type=F32), 0.0)
        f = f + jnp.dot((a * a).astype(BF16), w2_ref[c0:c0 + step, :], preferred_element_type=F32)
    o_ref[0] = x1 + mod_ref[0, 5:6, :] * _rms(f, n3_ref[...])


def _tail(x, ys, ya, mod, wo, n1, n2, n3, w1, w2):
    bsz, t, d = x.shape
    tm = min(ROW_TILE, t)

    def const(shape):
        return pl.BlockSpec(shape, lambda b, i: tuple(0 for _ in shape))

    return pl.pallas_call(
        _tail_kernel,
        grid=(bsz, t // tm),
        in_specs=[pl.BlockSpec((1, tm, d), lambda b, i: (b, i, 0)),
                  pl.BlockSpec((1, tm, D_SSM), lambda b, i: (b, i, 0)),
                  pl.BlockSpec((1, tm, D_ATTN), lambda b, i: (b, i, 0)),
                  pl.BlockSpec((1, 6, d), lambda b, i: (b, 0, 0)),
                  const((D_SSM + D_ATTN, d)), const((1, d)), const((1, d)), const((1, d)),
                  const((d, D_FF)), const((D_FF, d))],
        out_specs=pl.BlockSpec((1, tm, d), lambda b, i: (b, i, 0)),
        out_shape=jax.ShapeDtypeStruct((bsz, t, d), F32),
        compiler_params=pltpu.CompilerParams(
            dimension_semantics=("arbitrary", "arbitrary"), vmem_limit_bytes=60 * 2 ** 20),
        name="tail",
    )(x, ys, ya, mod, wo, n1, n2, n3, w1, w2)


def _permute_w_in(w):
    sizes = [D_SSM, D_XBC, SSM_HEADS, D_ATTN, D_KV, D_KV, IDX_HEADS * IDX_DIM, IDX_DIM, IDX_HEADS]
    pts = np.cumsum(sizes)[:-1].tolist()
    z, xbc, dt, q, k, v, qidx, kidx, widx = jnp.split(w, pts, axis=-1)
    pad = jnp.zeros((w.shape[0], MISC_W - IDX_DIM - IDX_HEADS - SSM_HEADS), w.dtype)
    return jnp.concatenate([z, xbc, q, k, v, qidx, kidx, widx, dt, pad], axis=-1).astype(BF16)


def _layer(x, c_pad, w_ada, b_ada, norm1_pre, norm1_post, w_in, conv_w, conv_b, dt_bias, a_log, d_skip,
           ssm_norm, rel_bias, attn_norm, w_out, norm2_pre, norm2_post, w_mlp_in, w_mlp_out):
    bsz, t, d = x.shape
    mod = _ada(c_pad, w_ada, b_ada[None, :])[:bsz].reshape(bsz, 6, d)
    z, xbc, q, k, v, qi, misc, miscb = _inproj(x, mod, norm1_pre[None, :], _permute_w_in(w_in))

    lane = jnp.arange(MISC_W)
    is_dt = (lane >= MISC_DT) & (lane < MISC_DT + SSM_HEADS)
    head_of = jnp.clip(lane - MISC_DT, 0, SSM_HEADS - 1)
    dtb_t = jnp.where(is_dt, dt_bias[head_of], 0.0)[None, :]
    a_t = jnp.where(is_dt, -jnp.exp(a_log.astype(F32))[head_of], 0.0)[None, :]
    expand = ((lane[:, None] - MISC_DT) == (jnp.arange(D_SSM)[None, :] // SSM_HEAD_DIM)).astype(F32)
    dskip_e = jnp.repeat(d_skip.astype(F32), SSM_HEAD_DIM)[None, :]
    y_ssd = _ssd(xbc, z, misc, conv_w, conv_b[None, :], dtb_t, a_t, expand, dskip_e, ssm_norm[None, :])

    nq = t // BLK
    vt = jnp.transpose(v.reshape(bsz, nq, BLK, D_KV), (0, 1, 3, 2))
    y_att = _dsa(rel_bias, q, qi, misc, k, vt, miscb, attn_norm[None, :])

    return _tail(x, y_ssd, y_att, mod, w_out.astype(BF16), norm1_post[None, :], norm2_pre[None, :],
                 norm2_post[None, :], w_mlp_in.astype(BF16), w_mlp_out.astype(BF16))


def kernel(x, c, w_ada, b_ada, norm1_pre, norm1_post, w_in, conv_w, conv_b, dt_bias, a_log, d_skip,
           ssm_norm, rel_bias, attn_norm, w_out, norm2_pre, norm2_post, w_mlp_in, w_mlp_out):
    bsz = x.shape[0]
    assert bsz <= 8 and x.shape[1] % BLK == 0 and x.shape[2] == D_MODEL
    c_pad = jnp.zeros((8, D_MODEL), F32).at[:bsz].set(c)
    for l in range(w_ada.shape[0]):
        x = _layer(x, c_pad, w_ada[l], b_ada[l], norm1_pre[l], norm1_post[l], w_in[l], conv_w[l], conv_b[l],
                   dt_bias[l], a_log[l], d_skip[l], ssm_norm[l], rel_bias, attn_norm[l], w_out[l],
                   norm2_pre[l], norm2_post[l], w_mlp_in[l], w_mlp_out[l])
    return x
```

```python
import functools
import math

import numpy as np
import jax
import jax.numpy as jnp
from jax import lax
from jax.experimental import pallas as pl
from jax.experimental.pallas import tpu as pltpu

D_MODEL = 1024
SSM_HEADS = 8
SSM_HEAD_DIM = 64
D_SSM = SSM_HEADS * SSM_HEAD_DIM
SSM_GROUPS = 2
SSM_STATE = 128
CONV_WIDTH = 4
CHUNK = 256
ATTN_HEADS = 8
ATTN_KV_HEADS = 2
ATTN_HEAD_DIM = 64
D_ATTN = ATTN_HEADS * ATTN_HEAD_DIM
D_KV = ATTN_KV_HEADS * ATTN_HEAD_DIM
IDX_HEADS = 8
IDX_DIM = 64
TOPK_MAX = 256
N_BUCKETS = 32
MAX_DISTANCE = 128
D_FF = 4 * D_MODEL
D_BC = SSM_GROUPS * SSM_STATE
D_XBC = D_SSM + 2 * D_BC
EPS = 1e-6

LANES = 128
MISC_W = LANES
MISC_KIDX = 0
MISC_WIDX = IDX_DIM
MISC_DT = IDX_DIM + IDX_HEADS
OFF_Z = 0
OFF_XBC = OFF_Z + D_SSM
OFF_Q = OFF_XBC + D_XBC
OFF_K = OFF_Q + D_ATTN
OFF_V = OFF_K + D_KV
OFF_QI = OFF_V + D_KV
OFF_MISC = OFF_QI + IDX_HEADS * IDX_DIM
D_IN_PAD = OFF_MISC + MISC_W

BLK = 128
IDX_CHUNK = 256
ATT_FAR_BLOCKS = 4
LOG2E = math.log2(math.e)
ROW_TILE = 512
NEG = -1e30
INT_MIN = -2 ** 31

F32 = jnp.float32
BF16 = jnp.bfloat16
HI = lax.Precision.HIGHEST
NT = (((1,), (1,)), ((), ()))


def _silu(x):
    return x / (1.0 + jnp.exp(-x))


def _rms(x, w):
    return x * lax.rsqrt(jnp.mean(x * x, axis=-1, keepdims=True) + EPS) * w


def _ada_kernel(c_ref, w_ref, b_ref, o_ref):
    s = _silu(c_ref[...])
    o_ref[...] = jnp.dot(s, w_ref[...], precision=HI, preferred_element_type=F32) + b_ref[...]


def _ada(c_pad, w, b):
    n = w.shape[1]
    tn = D_MODEL
    return pl.pallas_call(
        _ada_kernel,
        grid=(n // tn,),
        in_specs=[pl.BlockSpec((8, D_MODEL), lambda j: (0, 0)),
                  pl.BlockSpec((D_MODEL, tn), lambda j: (0, j)),
                  pl.BlockSpec((1, tn), lambda j: (0, j))],
        out_specs=pl.BlockSpec((8, tn), lambda j: (0, j)),
        out_shape=jax.ShapeDtypeStruct((8, n), F32),
        name="ada",
    )(c_pad, w, b)


def _inproj_kernel(x_ref, mod_ref, nw_ref, w_ref,
                   z_ref, xbc_ref, q_ref, k_ref, v_ref, qi_ref, misc_ref, miscb_ref):
    x = x_ref[0]
    h = _rms(x, nw_ref[...]) * (1.0 + mod_ref[0, 1:2, :]) + mod_ref[0, 0:1, :]
    hb = h.astype(BF16)

    def seg(lo, width):
        return jnp.dot(hb, w_ref[:, lo:lo + width], preferred_element_type=F32)

    z_ref[0] = seg(OFF_Z, D_SSM)
    xbc_ref[0] = seg(OFF_XBC, D_XBC)
    q_ref[0] = (seg(OFF_Q, D_ATTN) * (ATTN_HEAD_DIM ** -0.5 * LOG2E)).astype(BF16)
    k_ref[0] = seg(OFF_K, D_KV).astype(BF16)
    v_ref[0] = seg(OFF_V, D_KV).astype(BF16)
    qi_ref[0] = seg(OFF_QI, IDX_HEADS * IDX_DIM).astype(BF16)
    misc = seg(OFF_MISC, MISC_W)
    misc_ref[0] = misc
    miscb_ref[0] = misc.astype(BF16)


def _inproj(x, mod, nw, w_perm):
    bsz, t, d = x.shape
    tm = min(ROW_TILE, t)

    def tok(width, dtype):
        return (pl.BlockSpec((1, tm, width), lambda b, i: (b, i, 0)),
                jax.ShapeDtypeStruct((bsz, t, width), dtype))

    outs = [tok(D_SSM, F32), tok(D_XBC, F32), tok(D_ATTN, BF16), tok(D_KV, BF16), tok(D_KV, BF16),
            tok(IDX_HEADS * IDX_DIM, BF16), tok(MISC_W, F32), tok(MISC_W, BF16)]
    return pl.pallas_call(
        _inproj_kernel,
        grid=(bsz, t // tm),
        in_specs=[pl.BlockSpec((1, tm, d), lambda b, i: (b, i, 0)),
                  pl.BlockSpec((1, 6, d), lambda b, i: (b, 0, 0)),
                  pl.BlockSpec((1, d), lambda b, i: (0, 0)),
                  pl.BlockSpec((d, D_IN_PAD), lambda b, i: (0, 0))],
        out_specs=[o[0] for o in outs],
        out_shape=[o[1] for o in outs],
        compiler_params=pltpu.CompilerParams(
            dimension_semantics=("arbitrary", "arbitrary"), vmem_limit_bytes=48 * 2 ** 20),
        name="inproj",
    )(x, mod, nw, w_perm)


def _ssd_kernel(xbc_ref, z_ref, misc_ref, cw_ref, cb_ref, dtb_ref, a_ref, e_ref, dskip_ref, nw_ref,
                o_ref, xbuf, state):
    c = pl.program_id(1)
    lc = xbc_ref.shape[1]

    @pl.when(c == 0)
    def _():
        xbuf[0:8, :] = jnp.zeros((8, D_XBC), F32)
        state[...] = jnp.zeros(state.shape, F32)

    @pl.when(c > 0)
    def _():
        xbuf[0:8, :] = xbuf[lc:lc + 8, :]

    xbuf[8:lc + 8, :] = xbc_ref[0]
    u = cb_ref[...]
    for k in range(CONV_WIDTH):
        u = u + xbuf[pl.ds(8 - (CONV_WIDTH - 1) + k, lc), :] * cw_ref[k:k + 1, :]
    u = _silu(u)
    xs = u[:, :D_SSM]
    bm = u[:, D_SSM:D_SSM + D_BC]
    cm = u[:, D_SSM + D_BC:]

    raw = misc_ref[0] + dtb_ref[...]
    dt_t = jnp.maximum(raw, 0.0) + jnp.log1p(jnp.exp(-jnp.abs(raw)))
    adt_t = dt_t * a_ref[...]
    row = lax.broadcasted_iota(jnp.int32, (lc, lc), 0)
    col = lax.broadcasted_iota(jnp.int32, (lc, lc), 1)
    causal = col <= row
    tril = causal.astype(F32)
    acs_t = jnp.dot(tril, adt_t, precision=HI, preferred_element_type=F32)
    expand = e_ref[...]
    dt_e = jnp.dot(dt_t, expand, precision=HI, preferred_element_type=F32)
    acs_e = jnp.dot(acs_t, expand, precision=HI, preferred_element_type=F32)
    acs_row = jnp.transpose(acs_t)

    xdt = xs * dt_e
    a_last = acs_e[lc - 1:lc, :]
    xdec = (xdt * jnp.exp(a_last - acs_e)).astype(BF16)
    xdt_b = xdt.astype(BF16)
    bm_b = bm.astype(BF16)
    cm_b = cm.astype(BF16)
    bm_t = jnp.transpose(bm).astype(BF16)

    heads_per_group = SSM_HEADS // SSM_GROUPS
    gw = heads_per_group * SSM_HEAD_DIM
    y_parts = []
    off_parts = []
    for g in range(SSM_GROUPS):
        bg = bm_b[:, g * SSM_STATE:(g + 1) * SSM_STATE]
        cg = cm_b[:, g * SSM_STATE:(g + 1) * SSM_STATE]
        cb = lax.dot_general(cg, bg, NT, preferred_element_type=F32)
        for r in range(heads_per_group):
            hd = g * heads_per_group + r
            a_col = acs_t[:, MISC_DT + hd:MISC_DT + hd + 1]
            a_row = acs_row[MISC_DT + hd:MISC_DT + hd + 1, :]
            lmat = jnp.exp(jnp.where(causal, a_col - a_row, -jnp.inf))
            mh = (cb * lmat).astype(BF16)
            y_parts.append(jnp.dot(mh, xdt_b[:, hd * SSM_HEAD_DIM:(hd + 1) * SSM_HEAD_DIM],
                                   preferred_element_type=F32))
        prev = state[g]
        off_parts.append(jnp.dot(cg, prev.astype(BF16), preferred_element_type=F32))
        st = jnp.dot(bm_t[g * SSM_STATE:(g + 1) * SSM_STATE, :], xdec[:, g * gw:(g + 1) * gw],
                     preferred_element_type=F32)
        state[g] = prev * jnp.exp(a_last[:, g * gw:(g + 1) * gw]) + st
    y = (jnp.concatenate(y_parts, axis=1)
         + jnp.concatenate(off_parts, axis=1) * jnp.exp(acs_e)
         + xs * dskip_ref[...])
    o_ref[0] = _rms(y * _silu(z_ref[0]), nw_ref[...]).astype(BF16)


def _ssd(xbc, z, misc, cw, cb, dtb_t, a_t, expand, dskip_e, nw):
    bsz, t, _ = xbc.shape
    lc = math.gcd(t, CHUNK)
    nc = t // lc
    gw = (SSM_HEADS // SSM_GROUPS) * SSM_HEAD_DIM

    def const(shape):
        return pl.BlockSpec(shape, lambda b, c: tuple(0 for _ in shape))

    return pl.pallas_call(
        _ssd_kernel,
        grid=(bsz, nc),
        in_specs=[pl.BlockSpec((1, lc, D_XBC), lambda b, c: (b, c, 0)),
                  pl.BlockSpec((1, lc, D_SSM), lambda b, c: (b, c, 0)),
                  pl.BlockSpec((1, lc, MISC_W), lambda b, c: (b, c, 0)),
                  const((CONV_WIDTH, D_XBC)), const((1, D_XBC)), const((1, MISC_W)), const((1, MISC_W)),
                  const((MISC_W, D_SSM)), const((1, D_SSM)), const((1, D_SSM))],
        out_specs=pl.BlockSpec((1, lc, D_SSM), lambda b, c: (b, c, 0)),
        out_shape=jax.ShapeDtypeStruct((bsz, t, D_SSM), BF16),
        scratch_shapes=[pltpu.VMEM((lc + 8, D_XBC), F32),
                        pltpu.VMEM((SSM_GROUPS, SSM_STATE, gw), F32)],
        compiler_params=pltpu.CompilerParams(
            dimension_semantics=("arbitrary", "arbitrary"), vmem_limit_bytes=48 * 2 ** 20),
        name="ssd",
    )(xbc, z, misc, cw, cb, dtb_t, a_t, expand, dskip_e, nw)


def _bucket_tiles():
    s = np.arange(BLK)[:, None]
    t = np.arange(BLK)[None, :]
    max_exact = N_BUCKETS // 2
    tiles = []
    for d0 in (0, BLK):
        n = np.maximum(d0 + t - s, 0)
        large = {}
        for dt_ in (np.float32, np.float64):
            nf = np.maximum(n, 1).astype(dt_)
            lg = max_exact + (np.log(nf / dt_(max_exact)) / dt_(math.log(MAX_DISTANCE / max_exact))
                              * dt_(N_BUCKETS - max_exact)).astype(np.int32)
            large[dt_] = np.minimum(lg, N_BUCKETS - 1)
        assert (large[np.float32] == large[np.float64]).all()
        tiles.append(np.where(n < max_exact, n, large[np.float64]).astype(np.int32))
    assert BLK + 1 >= MAX_DISTANCE
    return np.stack(tiles)


def _to_key(x):
    bits = lax.bitcast_convert_type(x, jnp.int32)
    key = bits ^ (lax.shift_right_arithmetic(bits, 31) & jnp.int32(0x7FFFFFFF))
    return jnp.where(key == -1, 0, key)


def _dsa_kernel(relb_ref, bkt_ref, q_ref, qi_ref, miscq_ref, k_ref, vt_ref, kidx_ref, nw_ref,
                o_ref, keys, bias_s, qcat, qicat, wcat, m_s, l_s, acc_s, jlim, *, topk, idx_bits):
    b = pl.program_id(0)
    qi = pl.program_id(1)
    tq = BLK
    i32 = jnp.int32
    rep = ATTN_HEADS // ATTN_KV_HEADS

    @pl.when((b == 0) & (qi == 0))
    def _init_bias():
        for d in range(2):
            bk = bkt_ref[d]
            for h in range(ATTN_HEADS):
                far = relb_ref[N_BUCKETS - 1, h]
                acc = jnp.zeros((BLK, tq), F32)
                for n in range(N_BUCKETS - 1):
                    acc = jnp.where(bk == n, relb_ref[n, h] - far, acc)
                bias_s[d, h // rep, :, (h % rep) * BLK:(h % rep + 1) * BLK] = acc * LOG2E

    row_i = lax.broadcasted_iota(i32, (BLK, tq), 0)
    col_i = lax.broadcasted_iota(i32, (BLK, tq), 1)
    tri = row_i <= col_i

    q_t = jnp.transpose(q_ref[0].astype(F32))
    for h in range(ATTN_HEADS):
        qcat[h // rep, :, (h % rep) * BLK:(h % rep + 1) * BLK] = (
            q_t[h * ATTN_HEAD_DIM:(h + 1) * ATTN_HEAD_DIM, :].astype(BF16))
    qi_t = jnp.transpose(qi_ref[0].astype(F32))
    w_t = jnp.transpose(miscq_ref[0])[MISC_WIDX:MISC_WIDX + IDX_HEADS, :] * (
        IDX_HEADS ** -0.5 * IDX_DIM ** -0.5)
    for h in range(IDX_HEADS):
        qicat[:, h * BLK:(h + 1) * BLK] = qi_t[h * IDX_DIM:(h + 1) * IDX_DIM, :].astype(BF16)
        wcat[:, h * BLK:(h + 1) * BLK] = jnp.broadcast_to(w_t[h:h + 1, :], (8, BLK))

    def idx_block(s0, ts, diag):
        kb = kidx_ref[0, pl.ds(s0, ts), MISC_KIDX:MISC_KIDX + IDX_DIM]
        s = jnp.dot(kb, qicat[...], preferred_element_type=F32)
        r = jnp.maximum(s, 0.0) * wcat[0:1, :]
        acc = r[:, 0:BLK]
        for h in range(1, IDX_HEADS):
            acc = acc + r[:, h * BLK:(h + 1) * BLK]
        if diag:
            acc = jnp.where(tri, acc, -jnp.inf)
        keys[pl.ds(s0, ts), :] = _to_key(acc)

    def idx_body(i, carry):
        idx_block(pl.multiple_of(i * IDX_CHUNK, IDX_CHUNK), IDX_CHUNK, False)
        return carry

    per = IDX_CHUNK // BLK
    lax.fori_loop(0, qi // per, idx_body, 0)

    def idx_rem(j, carry):
        idx_block(pl.multiple_of(j * BLK, BLK), BLK, False)
        return carry

    lax.fori_loop((qi // per) * per, qi, idx_rem, 0)
    idx_block(pl.multiple_of(qi * BLK, BLK), BLK, True)
    nblk = qi + 1

    def count(pred):
        def body(i, cnt):
            s0 = pl.multiple_of(i * BLK, BLK)
            return cnt + pred(keys[pl.ds(s0, BLK), :], s0).astype(i32)
        cnt = lax.fori_loop(0, nblk, body, jnp.zeros((BLK, tq), i32))
        return jnp.sum(cnt, axis=0, keepdims=True)

    def bit_body(it, thr):
        cand = thr + lax.shift_left(i32(1), i32(31) - it)
        c = count(lambda kk, s0: kk >= cand)
        return jnp.where(c >= topk, cand, thr)

    thr = lax.fori_loop(0, 32, bit_body, jnp.full((1, tq), INT_MIN, i32))
    n_gt = count(lambda kk, s0: kk > thr)
    n_eq = count(lambda kk, s0: kk == thr)
    need = topk - n_gt
    jlim[...] = jnp.full((1, tq), 2 ** idx_bits - 1, i32)

    @pl.when(jnp.max((n_eq > need).astype(i32)) > 0)
    def _ties():
        def jbody(it, p):
            cand = p + lax.shift_left(i32(1), i32(idx_bits - 1) - it)
            c = count(lambda kk, s0: (kk == thr) & (s0 + row_i < cand))
            return jnp.where(c < need, cand, p)
        jlim[...] = lax.fori_loop(0, idx_bits, jbody, jnp.zeros((1, tq), i32))

    jl = jlim[...]

    gw = rep * tq
    for g in range(ATTN_KV_HEADS):
        m_s[g] = jnp.full((1, gw), NEG, F32)
        l_s[g] = jnp.zeros((1, gw), F32)
        acc_s[g] = jnp.zeros((ATTN_HEAD_DIM, gw), F32)

    def att_block(j0, nb, mode):
        ts = nb * BLK
        s0 = pl.multiple_of(j0 * BLK, BLK)
        kk = keys[pl.ds(s0, ts), :]
        s_idx = s0 + lax.broadcasted_iota(i32, (ts, tq), 0)
        sel = (kk > thr) | ((kk == thr) & (s_idx <= jl))
        if mode == 0:
            sel = sel & tri
        pen = jnp.where(sel, 0.0, NEG)
        pen = jnp.concatenate([pen] * rep, axis=1)
        kblk = k_ref[0, pl.ds(s0, ts), :]
        vtb = jnp.concatenate([vt_ref[0, j0 + i] for i in range(nb)], axis=1)
        for g in range(ATTN_KV_HEADS):
            kg = kblk[:, g * ATTN_HEAD_DIM:(g + 1) * ATTN_HEAD_DIM]
            s = jnp.dot(kg, qcat[g], preferred_element_type=F32) + pen
            if mode < 2:
                s = s + bias_s[mode, g]
            m_old = m_s[g]
            m_new = jnp.maximum(m_old, jnp.max(s, axis=0, keepdims=True))
            alpha = jnp.exp2(m_old - m_new)
            p = jnp.exp2(s - m_new)
            l_s[g] = alpha * l_s[g] + jnp.sum(p, axis=0, keepdims=True)
            pv = jnp.dot(vtb[g * ATTN_HEAD_DIM:(g + 1) * ATTN_HEAD_DIM, :], p.astype(BF16),
                         preferred_element_type=F32)
            acc_s[g] = acc_s[g] * alpha + pv
            m_s[g] = m_new

    nfar = jnp.maximum(qi - 1, 0)

    def far_body(i, carry):
        att_block(i * ATT_FAR_BLOCKS, ATT_FAR_BLOCKS, 2)
        return carry

    lax.fori_loop(0, nfar // ATT_FAR_BLOCKS, far_body, 0)

    def far_rem(j, carry):
        att_block(j, 1, 2)
        return carry

    lax.fori_loop((nfar // ATT_FAR_BLOCKS) * ATT_FAR_BLOCKS, nfar, far_rem, 0)

    @pl.when(qi >= 1)
    def _sub():
        att_block(qi - 1, 1, 1)

    att_block(qi, 1, 0)

    parts = []
    for g in range(ATTN_KV_HEADS):
        o = acc_s[g] * (1.0 / l_s[g])
        parts += [o[:, r * tq:(r + 1) * tq] for r in range(rep)]
    out_t = jnp.concatenate(parts, axis=0)
    o_ref[0] = _rms(jnp.transpose(out_t), nw_ref[...]).astype(BF16)


def _dsa(rel_bias, q, qi, misc, k, vt, miscb, nw):
    bsz, t, _ = q.shape
    nq = t // BLK
    topk = min(TOPK_MAX, t // 4)
    idx_bits = max(1, (t - 1).bit_length())
    gw = (ATTN_HEADS // ATTN_KV_HEADS) * BLK
    bkt = jnp.asarray(_bucket_tiles())
    kern = functools.partial(_dsa_kernel, topk=topk, idx_bits=idx_bits)
    return pl.pallas_call(
        kern,
        grid=(bsz, nq),
        in_specs=[pl.BlockSpec(memory_space=pltpu.SMEM),
                  pl.BlockSpec((2, BLK, BLK), lambda b, i: (0, 0, 0)),
                  pl.BlockSpec((1, BLK, D_ATTN), lambda b, i: (b, i, 0)),
                  pl.BlockSpec((1, BLK, IDX_HEADS * IDX_DIM), lambda b, i: (b, i, 0)),
                  pl.BlockSpec((1, BLK, MISC_W), lambda b, i: (b, i, 0)),
                  pl.BlockSpec((1, t, D_KV), lambda b, i: (b, 0, 0)),
                  pl.BlockSpec((1, nq, D_KV, BLK), lambda b, i: (b, 0, 0, 0)),
                  pl.BlockSpec((1, t, MISC_W), lambda b, i: (b, 0, 0)),
                  pl.BlockSpec((1, D_ATTN), lambda b, i: (0, 0))],
        out_specs=pl.BlockSpec((1, BLK, D_ATTN), lambda b, i: (b, i, 0)),
        out_shape=jax.ShapeDtypeStruct((bsz, t, D_ATTN), BF16),
        scratch_shapes=[pltpu.VMEM((t, BLK), jnp.int32),
                        pltpu.VMEM((2, ATTN_KV_HEADS, BLK, gw), F32),
                        pltpu.VMEM((ATTN_KV_HEADS, ATTN_HEAD_DIM, gw), BF16),
                        pltpu.VMEM((IDX_DIM, IDX_HEADS * BLK), BF16),
                        pltpu.VMEM((8, IDX_HEADS * BLK), F32),
                        pltpu.VMEM((ATTN_KV_HEADS, 1, gw), F32),
                        pltpu.VMEM((ATTN_KV_HEADS, 1, gw), F32),
                        pltpu.VMEM((ATTN_KV_HEADS, ATTN_HEAD_DIM, gw), F32),
                        pltpu.VMEM((1, BLK), jnp.int32)],
        compiler_params=pltpu.CompilerParams(
            dimension_semantics=("arbitrary", "arbitrary"), vmem_limit_bytes=48 * 2 ** 20),
        name="dsa",
    )(rel_bias, bkt, q, qi, misc, k, vt, miscb, nw)


def _tail_kernel(x_ref, ys_ref, ya_ref, mod_ref, wo_ref, n1_ref, n2_ref, n3_ref, w1_ref, w2_ref, o_ref):
    x = x_ref[0]
    mix = (jnp.dot(ys_ref[0], wo_ref[0:D_SSM, :], preferred_element_type=F32)
           + jnp.dot(ya_ref[0], wo_ref[D_SSM:, :], preferred_element_type=F32))
    x1 = x + mod_ref[0, 2:3, :] * _rms(mix, n1_ref[...])
    h = (_rms(x1, n2_ref[...]) * (1.0 + mod_ref[0, 4:5, :]) + mod_ref[0, 3:4, :]).astype(BF16)
    f = jnp.zeros(x.shape, F32)
    step = D_MODEL
    for c0 in range(0, D_FF, step):
        a = jnp.maximum(jnp.dot(h, w1_ref[:, c0:c0 + step], preferred_element_type=F32), 0.0)
        f = f + jnp.dot((a * a).astype(BF16), w2_ref[c0:c0 + step, :], preferred_element_type=F32)
    o_ref[0] = x1 + mod_ref[0, 5:6, :] * _rms(f, n3_ref[...])


def _tail(x, ys, ya, mod, wo, n1, n2, n3, w1, w2):
    bsz, t, d = x.shape
    tm = min(ROW_TILE, t)

    def const(shape):
        return pl.BlockSpec(shape, lambda b, i: tuple(0 for _ in shape))

    return pl.pallas_call(
        _tail_kernel,
        grid=(bsz, t // tm),
        in_specs=[pl.BlockSpec((1, tm, d), lambda b, i: (b, i, 0)),
                  pl.BlockSpec((1, tm, D_SSM), lambda b, i: (b, i, 0)),
                  pl.BlockSpec((1, tm, D_ATTN), lambda b, i: (b, i, 0)),
                  pl.BlockSpec((1, 6, d), lambda b, i: (b, 0, 0)),
                  const((D_SSM + D_ATTN, d)), const((1, d)), const((1, d)), const((1, d)),
                  const((d, D_FF)), const((D_FF, d))],
        out_specs=pl.BlockSpec((1, tm, d), lambda b, i: (b, i, 0)),
        out_shape=jax.ShapeDtypeStruct((bsz, t, d), F32),
        compiler_params=pltpu.CompilerParams(
            dimension_semantics=("arbitrary", "arbitrary"), vmem_limit_bytes=60 * 2 ** 20),
        name="tail",
    )(x, ys, ya, mod, wo, n1, n2, n3, w1, w2)


def _permute_w_in(w):
    sizes = [D_SSM, D_XBC, SSM_HEADS, D_ATTN, D_KV, D_KV, IDX_HEADS * IDX_DIM, IDX_DIM, IDX_HEADS]
    pts = np.cumsum(sizes)[:-1].tolist()
    z, xbc, dt, q, k, v, qidx, kidx, widx = jnp.split(w, pts, axis=-1)
    pad = jnp.zeros((w.shape[0], MISC_W - IDX_DIM - IDX_HEADS - SSM_HEADS), w.dtype)
    return jnp.concatenate([z, xbc, q, k, v, qidx, kidx, widx, dt, pad], axis=-1).astype(BF16)


def _layer(x, c_pad, w_ada, b_ada, norm1_pre, norm1_post, w_in, conv_w, conv_b, dt_bias, a_log, d_skip,
           ssm_norm, rel_bias, attn_norm, w_out, norm2_pre, norm2_post, w_mlp_in, w_mlp_out):
    bsz, t, d = x.shape
    mod = _ada(c_pad, w_ada, b_ada[None, :])[:bsz].reshape(bsz, 6, d)
    z, xbc, q, k, v, qi, misc, miscb = _inproj(x, mod, norm1_pre[None, :], _permute_w_in(w_in))

    lane = jnp.arange(MISC_W)
    is_dt = (lane >= MISC_DT) & (lane < MISC_DT + SSM_HEADS)
    head_of = jnp.clip(lane - MISC_DT, 0, SSM_HEADS - 1)
    dtb_t = jnp.where(is_dt, dt_bias[head_of], 0.0)[None, :]
    a_t = jnp.where(is_dt, -jnp.exp(a_log.astype(F32))[head_of], 0.0)[None, :]
    expand = ((lane[:, None] - MISC_DT) == (jnp.arange(D_SSM)[None, :] // SSM_HEAD_DIM)).astype(F32)
    dskip_e = jnp.repeat(d_skip.astype(F32), SSM_HEAD_DIM)[None, :]
    y_ssd = _ssd(xbc, z, misc, conv_w, conv_b[None, :], dtb_t, a_t, expand, dskip_e, ssm_norm[None, :])

    nq = t // BLK
    vt = jnp.transpose(v.reshape(bsz, nq, BLK, D_KV), (0, 1, 3, 2))
    y_att = _dsa(rel_bias, q, qi, misc, k, vt, miscb, attn_norm[None, :])

    return _tail(x, y_ssd, y_att, mod, w_out.astype(BF16), norm1_post[None, :], norm2_pre[None, :],
                 norm2_post[None, :], w_mlp_in.astype(BF16), w_mlp_out.astype(BF16))


def kernel(x, c, w_ada, b_ada, norm1_pre, norm1_post, w_in, conv_w, conv_b, dt_bias, a_log, d_skip,
           ssm_norm, rel_bias, attn_norm, w_out, norm2_pre, norm2_post, w_mlp_in, w_mlp_out):
    bsz = x.shape[0]
    assert bsz <= 8 and x.shape[1] % BLK == 0 and x.shape[2] == D_MODEL
    c_pad = jnp.zeros((8, D_MODEL), F32).at[:bsz].set(c)
    for l in range(w_ada.shape[0]):
        x = _layer(x, c_pad, w_ada[l], b_ada[l], norm1_pre[l], norm1_post[l], w_in[l], conv_w[l], conv_b[l],
                   dt_bias[l], a_log[l], d_skip[l], ssm_norm[l], rel_bias, attn_norm[l], w_out[l],
                   norm2_pre[l], norm2_post[l], w_mlp_in[l], w_mlp_out[l])
    return x
```

```python
import functools
import math

import numpy as np
import jax
import jax.numpy as jnp
from jax import lax
from jax.experimental import pallas as pl
from jax.experimental.pallas import tpu as pltpu

D_MODEL = 1024
SSM_HEADS = 8
SSM_HEAD_DIM = 64
D_SSM = SSM_HEADS * SSM_HEAD_DIM
SSM_GROUPS = 2
SSM_STATE = 128
CONV_WIDTH = 4
CHUNK = 256
ATTN_HEADS = 8
ATTN_KV_HEADS = 2
ATTN_HEAD_DIM = 64
D_ATTN = ATTN_HEADS * ATTN_HEAD_DIM
D_KV = ATTN_KV_HEADS * ATTN_HEAD_DIM
IDX_HEADS = 8
IDX_DIM = 64
TOPK_MAX = 256
N_BUCKETS = 32
MAX_DISTANCE = 128
D_FF = 4 * D_MODEL
D_BC = SSM_GROUPS * SSM_STATE
D_XBC = D_SSM + 2 * D_BC
EPS = 1e-6

LANES = 128
MISC_W = LANES
MISC_KIDX = 0
MISC_WIDX = IDX_DIM
MISC_DT = IDX_DIM + IDX_HEADS
OFF_Z = 0
OFF_XBC = OFF_Z + D_SSM
OFF_Q = OFF_XBC + D_XBC
OFF_K = OFF_Q + D_ATTN
OFF_V = OFF_K + D_KV
OFF_QI = OFF_V + D_KV
OFF_MISC = OFF_QI + IDX_HEADS * IDX_DIM
D_IN_PAD = OFF_MISC + MISC_W

BLK = 128
DSA_CHUNK = 512
V_ROWS = 80
DEN_LO = 2.0 ** -80
DEN_HI = 2.0 ** 80
F32_HUGE = 3.0e38
SEL_GROUP = 4
SEL_BLIND_GROUPS = 3
LOG2E = math.log2(math.e)
ROW_TILE = 512
NEG = -1e30
INT_MIN = -2 ** 31

F32 = jnp.float32
BF16 = jnp.bfloat16
HI = lax.Precision.HIGHEST
NT = (((1,), (1,)), ((), ()))


def _silu(x):
    return x / (1.0 + jnp.exp(-x))


def _rms(x, w):
    return x * lax.rsqrt(jnp.mean(x * x, axis=-1, keepdims=True) + EPS) * w


def _ada_kernel(c_ref, w_ref, b_ref, o_ref):
    s = _silu(c_ref[...])
    o_ref[...] = jnp.dot(s, w_ref[...], precision=HI, preferred_element_type=F32) + b_ref[...]


def _ada(c_pad, w, b):
    n = w.shape[1]
    tn = D_MODEL
    return pl.pallas_call(
        _ada_kernel,
        grid=(n // tn,),
        in_specs=[pl.BlockSpec((8, D_MODEL), lambda j: (0, 0)),
                  pl.BlockSpec((D_MODEL, tn), lambda j: (0, j)),
                  pl.BlockSpec((1, tn), lambda j: (0, j))],
        out_specs=pl.BlockSpec((8, tn), lambda j: (0, j)),
        out_shape=jax.ShapeDtypeStruct((8, n), F32),
        name="ada",
    )(c_pad, w, b)


def _inproj_kernel(x_ref, mod_ref, nw_ref, w_ref,
                   z_ref, xbc_ref, q_ref, k_ref, v_ref, qi_ref, misc_ref, miscb_ref):
    x = x_ref[0]
    h = _rms(x, nw_ref[...]) * (1.0 + mod_ref[0, 1:2, :]) + mod_ref[0, 0:1, :]
    hb = h.astype(BF16)

    def seg(lo, width):
        return jnp.dot(hb, w_ref[:, lo:lo + width], preferred_element_type=F32)

    z_ref[0] = seg(OFF_Z, D_SSM)
    xbc_ref[0] = seg(OFF_XBC, D_XBC)
    q_ref[0] = (seg(OFF_Q, D_ATTN) * (ATTN_HEAD_DIM ** -0.5 * LOG2E)).astype(BF16)
    k_ref[0] = seg(OFF_K, D_KV).astype(BF16)
    v_ref[0] = seg(OFF_V, D_KV).astype(BF16)
    qi_ref[0] = seg(OFF_QI, IDX_HEADS * IDX_DIM).astype(BF16)
    misc = seg(OFF_MISC, MISC_W)
    misc_ref[0] = misc
    miscb_ref[0] = misc.astype(BF16)


def _inproj(x, mod, nw, w_perm):
    bsz, t, d = x.shape
    tm = min(ROW_TILE, t)

    def tok(width, dtype):
        return (pl.BlockSpec((1, tm, width), lambda b, i: (b, i, 0)),
                jax.ShapeDtypeStruct((bsz, t, width), dtype))

    outs = [tok(D_SSM, F32), tok(D_XBC, F32), tok(D_ATTN, BF16), tok(D_KV, BF16), tok(D_KV, BF16),
            tok(IDX_HEADS * IDX_DIM, BF16), tok(MISC_W, F32), tok(MISC_W, BF16)]
    return pl.pallas_call(
        _inproj_kernel,
        grid=(bsz, t // tm),
        in_specs=[pl.BlockSpec((1, tm, d), lambda b, i: (b, i, 0)),
                  pl.BlockSpec((1, 6, d), lambda b, i: (b, 0, 0)),
                  pl.BlockSpec((1, d), lambda b, i: (0, 0)),
                  pl.BlockSpec((d, D_IN_PAD), lambda b, i: (0, 0))],
        out_specs=[o[0] for o in outs],
        out_shape=[o[1] for o in outs],
        compiler_params=pltpu.CompilerParams(
            dimension_semantics=("arbitrary", "arbitrary"), vmem_limit_bytes=48 * 2 ** 20),
        name="inproj",
    )(x, mod, nw, w_perm)


def _ssd_kernel(xbc_ref, z_ref, misc_ref, cw_ref, cb_ref, dtb_ref, a_ref, e_ref, dskip_ref, nw_ref,
                o_ref, xbuf, state):
    c = pl.program_id(1)
    lc = xbc_ref.shape[1]

    @pl.when(c == 0)
    def _():
        xbuf[0:8, :] = jnp.zeros((8, D_XBC), F32)
        state[...] = jnp.zeros(state.shape, F32)

    @pl.when(c > 0)
    def _():
        xbuf[0:8, :] = xbuf[lc:lc + 8, :]

    xbuf[8:lc + 8, :] = xbc_ref[0]
    u = cb_ref[...]
    for k in range(CONV_WIDTH):
        u = u + xbuf[pl.ds(8 - (CONV_WIDTH - 1) + k, lc), :] * cw_ref[k:k + 1, :]
    u = _silu(u)
    xs = u[:, :D_SSM]
    bm = u[:, D_SSM:D_SSM + D_BC]
    cm = u[:, D_SSM + D_BC:]

    raw = misc_ref[0] + dtb_ref[...]
    dt_t = jnp.maximum(raw, 0.0) + jnp.log1p(jnp.exp(-jnp.abs(raw)))
    adt_t = dt_t * a_ref[...]
    row = lax.broadcasted_iota(jnp.int32, (lc, lc), 0)
    col = lax.broadcasted_iota(jnp.int32, (lc, lc), 1)
    causal = col <= row
    tril = causal.astype(F32)
    acs_t = jnp.dot(tril, adt_t, precision=HI, preferred_element_type=F32)
    expand = e_ref[...]
    dt_e = jnp.dot(dt_t, expand, precision=HI, preferred_element_type=F32)
    acs_e = jnp.dot(acs_t, expand, precision=HI, preferred_element_type=F32)
    acs_row = jnp.transpose(acs_t)

    xdt = xs * dt_e
    a_last = acs_e[lc - 1:lc, :]
    xdec = (xdt * jnp.exp(a_last - acs_e)).astype(BF16)
    xdt_b = xdt.astype(BF16)
    bm_b = bm.astype(BF16)
    cm_b = cm.astype(BF16)
    bm_t = jnp.transpose(bm).astype(BF16)

    heads_per_group = SSM_HEADS // SSM_GROUPS
    gw = heads_per_group * SSM_HEAD_DIM
    y_parts = []
    off_parts = []
    for g in range(SSM_GROUPS):
        bg = bm_b[:, g * SSM_STATE:(g + 1) * SSM_STATE]
        cg = cm_b[:, g * SSM_STATE:(g + 1) * SSM_STATE]
        cb = lax.dot_general(cg, bg, NT, preferred_element_type=F32)
        for r in range(heads_per_group):
            hd = g * heads_per_group + r
            a_col = acs_t[:, MISC_DT + hd:MISC_DT + hd + 1]
            a_row = acs_row[MISC_DT + hd:MISC_DT + hd + 1, :]
            lmat = jnp.exp(jnp.where(causal, a_col - a_row, -jnp.inf))
            mh = (cb * lmat).astype(BF16)
            y_parts.append(jnp.dot(mh, xdt_b[:, hd * SSM_HEAD_DIM:(hd + 1) * SSM_HEAD_DIM],
                                   preferred_element_type=F32))
        prev = state[g]
        off_parts.append(jnp.dot(cg, prev.astype(BF16), preferred_element_type=F32))
        st = jnp.dot(bm_t[g * SSM_STATE:(g + 1) * SSM_STATE, :], xdec[:, g * gw:(g + 1) * gw],
                     preferred_element_type=F32)
        state[g] = prev * jnp.exp(a_last[:, g * gw:(g + 1) * gw]) + st
    y = (jnp.concatenate(y_parts, axis=1)
         + jnp.concatenate(off_parts, axis=1) * jnp.exp(acs_e)
         + xs * dskip_ref[...])
    o_ref[0] = _rms(y * _silu(z_ref[0]), nw_ref[...]).astype(BF16)


def _ssd(xbc, z, misc, cw, cb, dtb_t, a_t, expand, dskip_e, nw):
    bsz, t, _ = xbc.shape
    lc = math.gcd(t, CHUNK)
    nc = t // lc
    gw = (SSM_HEADS // SSM_GROUPS) * SSM_HEAD_DIM

    def const(shape):
        return pl.BlockSpec(shape, lambda b, c: tuple(0 for _ in shape))

    return pl.pallas_call(
        _ssd_kernel,
        grid=(bsz, nc),
        in_specs=[pl.BlockSpec((1, lc, D_XBC), lambda b, c: (b, c, 0)),
                  pl.BlockSpec((1, lc, D_SSM), lambda b, c: (b, c, 0)),
                  pl.BlockSpec((1, lc, MISC_W), lambda b, c: (b, c, 0)),
                  const((CONV_WIDTH, D_XBC)), const((1, D_XBC)), const((1, MISC_W)), const((1, MISC_W)),
                  const((MISC_W, D_SSM)), const((1, D_SSM)), const((1, D_SSM))],
        out_specs=pl.BlockSpec((1, lc, D_SSM), lambda b, c: (b, c, 0)),
        out_shape=jax.ShapeDtypeStruct((bsz, t, D_SSM), BF16),
        scratch_shapes=[pltpu.VMEM((lc + 8, D_XBC), F32),
                        pltpu.VMEM((SSM_GROUPS, SSM_STATE, gw), F32)],
        compiler_params=pltpu.CompilerParams(
            dimension_semantics=("arbitrary", "arbitrary"), vmem_limit_bytes=48 * 2 ** 20),
        name="ssd",
    )(xbc, z, misc, cw, cb, dtb_t, a_t, expand, dskip_e, nw)


def _bucket_tiles():
    s = np.arange(BLK)[:, None]
    t = np.arange(BLK)[None, :]
    max_exact = N_BUCKETS // 2
    tiles = []
    for d0 in (0, BLK):
        n = np.maximum(d0 + t - s, 0)
        large = {}
        for dt_ in (np.float32, np.float64):
            nf = np.maximum(n, 1).astype(dt_)
            lg = max_exact + (np.log(nf / dt_(max_exact)) / dt_(math.log(MAX_DISTANCE / max_exact))
                              * dt_(N_BUCKETS - max_exact)).astype(np.int32)
            large[dt_] = np.minimum(lg, N_BUCKETS - 1)
        assert (large[np.float32] == large[np.float64]).all()
        tiles.append(np.where(n < max_exact, n, large[np.float64]).astype(np.int32))
    assert BLK + 1 >= MAX_DISTANCE
    return np.stack(tiles)


def _to_key(x):
    bits = lax.bitcast_convert_type(x, jnp.int32)
    key = bits ^ (lax.shift_right_arithmetic(bits, 31) & jnp.int32(0x7FFFFFFF))
    return jnp.where(key == -1, 0, key)


def _dsa_kernel(relb_ref, bkt_ref, q_ref, qi_ref, miscq_ref, k_ref, vt_ref, kidx_ref, nw_ref,
                o_ref, keys, bias_s, qcat, qicat, wcat, m_s, acc_s, *, topk, idx_bits):
    b = pl.program_id(0)
    qi = pl.program_id(1)
    tq = BLK
    i32 = jnp.int32
    rep = ATTN_HEADS // ATTN_KV_HEADS
    gw = rep * tq
    ch = DSA_CHUNK
    ch_blks = ch // BLK

    @pl.when((b == 0) & (qi == 0))
    def _init_bias():
        for d in range(2):
            bk = bkt_ref[d]
            for h in range(ATTN_HEADS):
                far = relb_ref[N_BUCKETS - 1, h]
                acc = jnp.zeros((BLK, tq), F32)
                for n in range(N_BUCKETS - 1):
                    acc = jnp.where(bk == n, relb_ref[n, h] - far, acc)
                bias_s[d, h // rep, :, (h % rep) * BLK:(h % rep + 1) * BLK] = acc * LOG2E
        bias_s[2] = jnp.zeros((ATTN_KV_HEADS, BLK, gw), F32)

    q_t = jnp.transpose(q_ref[0].astype(F32))
    for h in range(ATTN_HEADS):
        qcat[h // rep, :, (h % rep) * BLK:(h % rep + 1) * BLK] = (
            q_t[h * ATTN_HEAD_DIM:(h + 1) * ATTN_HEAD_DIM, :].astype(BF16))
    qi_t = jnp.transpose(qi_ref[0].astype(F32))
    w_t = jnp.transpose(miscq_ref[0])[MISC_WIDX:MISC_WIDX + IDX_HEADS, :] * (
        IDX_HEADS ** -0.5 * IDX_DIM ** -0.5)
    for h in range(IDX_HEADS):
        qicat[:, h * BLK:(h + 1) * BLK] = qi_t[h * IDX_DIM:(h + 1) * IDX_DIM, :].astype(BF16)
        wcat[:, h * BLK:(h + 1) * BLK] = jnp.broadcast_to(w_t[h:h + 1, :], (8, BLK))

    nch = (qi + ch_blks) // ch_blks
    row_c = lax.broadcasted_iota(i32, (ch, tq), 0)
    t_idx = qi * BLK + lax.broadcasted_iota(i32, (ch, tq), 1)

    def chunk_start(i):
        return pl.multiple_of(i * ch, ch)

    def idx_chunk(i, carry):
        s0 = chunk_start(i)
        kb = kidx_ref[0, pl.ds(s0, ch), MISC_KIDX:MISC_KIDX + IDX_DIM]
        s = jnp.dot(kb, qicat[...], preferred_element_type=F32)
        r = jnp.maximum(s, 0.0) * wcat[0:1, :]
        acc = r[:, 0:BLK]
        for h in range(1, IDX_HEADS):
            acc = acc + r[:, h * BLK:(h + 1) * BLK]
        acc = jnp.where(s0 + row_c <= t_idx, acc, -jnp.inf)
        keys[pl.ds(s0, ch), :] = _to_key(acc)
        return carry

    lax.fori_loop(0, nch, idx_chunk, 0)

    def count(pred):
        def body(i, cnt):
            s0 = chunk_start(i)
            hit = pred(keys[pl.ds(s0, ch), :], s0).astype(i32)
            n = ch
            while n > 8:
                n //= 2
                hit = hit[:n] + hit[n:2 * n]
            return cnt + hit
        cnt = lax.fori_loop(0, nch, body, jnp.zeros((8, tq), i32))
        return jnp.sum(cnt, axis=0, keepdims=True)

    def sel_group(grp, thr, c_thr):
        for i in range(SEL_GROUP):
            cand = thr + lax.shift_left(i32(1), i32(31 - i) - grp * SEL_GROUP)
            c = count(lambda kk, s0: kk >= cand)
            ok = c >= topk
            thr, c_thr = jnp.where(ok, cand, thr), jnp.where(ok, c, c_thr)
        return thr, c_thr

    def sel_cond(st):
        grp, _, c_thr = st
        return (grp < 32 // SEL_GROUP) & (jnp.max(jnp.abs(c_thr - topk)) > 0)

    def sel_body(st):
        grp, thr, c_thr = st
        return (grp + 1,) + sel_group(grp, thr, c_thr)

    st = (jnp.full((1, tq), INT_MIN, i32), jnp.zeros((1, tq), i32) + nch * ch)
    st = lax.fori_loop(0, SEL_BLIND_GROUPS, lambda grp, s: sel_group(grp, *s), st)
    _, thr, _ = lax.while_loop(sel_cond, sel_body, (i32(SEL_BLIND_GROUPS),) + st)
    n_gt = count(lambda kk, s0: kk > thr)
    n_eq = count(lambda kk, s0: kk == thr)
    need = topk - n_gt

    @pl.when(jnp.max((n_eq > need).astype(i32)) > 0)
    def _ties():
        def jbody(it, p):
            cand = p + lax.shift_left(i32(1), i32(idx_bits - 1) - it)
            c = count(lambda kk, s0: (kk == thr) & (s0 + row_c < cand))
            return jnp.where(c < need, cand, p)
        last = lax.fori_loop(0, idx_bits, jbody, jnp.zeros((1, tq), i32))

        def demote(i, carry):
            s0 = chunk_start(i)
            kk = keys[pl.ds(s0, ch), :]
            keys[pl.ds(s0, ch), :] = jnp.where((kk == thr) & (s0 + row_c > last), INT_MIN, kk)
            return carry
        lax.fori_loop(0, nch, demote, 0)

    def pen_of(s0, causal):
        sel = keys[pl.ds(s0, ch), :] >= thr
        if causal:
            sel = sel & (s0 + row_c <= t_idx)
        pen = jnp.where(sel, 0.0, NEG)
        return jnp.concatenate([pen] * rep, axis=1)

    def near_bias(i, g):
        return jnp.concatenate(
            [bias_s[jnp.clip(qi - (i * ch_blks + u), 0, 2), g] for u in range(ch_blks)], axis=0)

    def vt_chunk(i, g):
        return jnp.concatenate([vt_ref[0, i * ch_blks + u, g] for u in range(ch_blks)], axis=1)

    s0d = pl.multiple_of(qi * BLK, BLK)
    kd = k_ref[0, pl.ds(s0d, BLK), :]
    tri = lax.broadcasted_iota(i32, (BLK, tq), 0) <= lax.broadcasted_iota(i32, (BLK, tq), 1)
    tri_pen = jnp.concatenate([jnp.where(tri, 0.0, NEG)] * rep, axis=1)
    for g in range(ATTN_KV_HEADS):
        sd = jnp.dot(kd[:, g * ATTN_HEAD_DIM:(g + 1) * ATTN_HEAD_DIM], qcat[g],
                     preferred_element_type=F32) + bias_s[0, g] + tri_pen
        m_s[g] = jnp.max(sd, axis=0, keepdims=True)
        acc_s[g] = jnp.zeros((V_ROWS, gw), F32)

    def att_chunk(i, near):
        s0 = chunk_start(i)
        pen = pen_of(s0, near)
        kblk = k_ref[0, pl.ds(s0, ch), :]
        for g in range(ATTN_KV_HEADS):
            s = jnp.dot(kblk[:, g * ATTN_HEAD_DIM:(g + 1) * ATTN_HEAD_DIM], qcat[g],
                        preferred_element_type=F32)
            if near:
                s = s + near_bias(i, g)
            p = jnp.exp2(s - m_s[g] + pen).astype(BF16)
            acc_s[g] += jnp.dot(vt_chunk(i, g), p, preferred_element_type=F32)

    def far_body(i, carry):
        att_chunk(i, False)
        return carry

    def near_body(i, carry):
        att_chunk(i, True)
        return carry

    n_far = jnp.maximum(qi - 1, 0) // ch_blks
    lax.fori_loop(0, n_far, far_body, 0)
    lax.fori_loop(n_far, nch, near_body, 0)

    n_ok = jnp.zeros((1, gw), i32)
    for g in range(ATTN_KV_HEADS):
        a = acc_s[g]
        den = a[ATTN_HEAD_DIM:ATTN_HEAD_DIM + 1, :]
        top = jnp.max(jnp.abs(a[0:ATTN_HEAD_DIM, :]), axis=0, keepdims=True)
        n_ok = n_ok + ((den > DEN_LO) & (den < DEN_HI) & (top < F32_HUGE)).astype(i32)

    @pl.when(jnp.min(n_ok) < ATTN_KV_HEADS)
    def _exact():
        for g in range(ATTN_KV_HEADS):
            m_s[g] = jnp.full((1, gw), NEG, F32)
            acc_s[g] = jnp.zeros((V_ROWS, gw), F32)

        def body(i, carry):
            s0 = chunk_start(i)
            pen = pen_of(s0, True)
            kblk = k_ref[0, pl.ds(s0, ch), :]
            for g in range(ATTN_KV_HEADS):
                s = jnp.dot(kblk[:, g * ATTN_HEAD_DIM:(g + 1) * ATTN_HEAD_DIM], qcat[g],
                            preferred_element_type=F32) + near_bias(i, g) + pen
                m_old = m_s[g]
                m_new = jnp.maximum(m_old, jnp.max(s, axis=0, keepdims=True))
                p = jnp.exp2(s - m_new).astype(BF16)
                acc_s[g] = acc_s[g] * jnp.exp2(m_old - m_new) + jnp.dot(
                    vt_chunk(i, g), p, preferred_element_type=F32)
                m_s[g] = m_new
            return carry
        lax.fori_loop(0, nch, body, 0)

    parts = []
    for g in range(ATTN_KV_HEADS):
        a = acc_s[g]
        o = a[0:ATTN_HEAD_DIM, :] * (1.0 / a[ATTN_HEAD_DIM:ATTN_HEAD_DIM + 1, :])
        parts += [o[:, r * tq:(r + 1) * tq] for r in range(rep)]
    out_t = jnp.concatenate(parts, axis=0)
    o_ref[0] = _rms(jnp.transpose(out_t), nw_ref[...]).astype(BF16)


def _dsa(rel_bias, q, qi, misc, k, vt, miscb, nw):
    bsz, t, _ = q.shape
    nq = t // BLK
    topk = min(TOPK_MAX, t // 4)
    idx_bits = max(1, (t - 1).bit_length())
    gw = (ATTN_HEADS // ATTN_KV_HEADS) * BLK
    bkt = jnp.asarray(_bucket_tiles())
    kern = functools.partial(_dsa_kernel, topk=topk, idx_bits=idx_bits)
    return pl.pallas_call(
        kern,
        grid=(bsz, nq),
        in_specs=[pl.BlockSpec(memory_space=pltpu.SMEM),
                  pl.BlockSpec((2, BLK, BLK), lambda b, i: (0, 0, 0)),
                  pl.BlockSpec((1, BLK, D_ATTN), lambda b, i: (b, i, 0)),
                  pl.BlockSpec((1, BLK, IDX_HEADS * IDX_DIM), lambda b, i: (b, i, 0)),
                  pl.BlockSpec((1, BLK, MISC_W), lambda b, i: (b, i, 0)),
                  pl.BlockSpec((1, t, D_KV), lambda b, i: (b, 0, 0)),
                  pl.BlockSpec((1, nq, ATTN_KV_HEADS, V_ROWS, BLK), lambda b, i: (b, 0, 0, 0, 0)),
                  pl.BlockSpec((1, t, MISC_W), lambda b, i: (b, 0, 0)),
                  pl.BlockSpec((1, D_ATTN), lambda b, i: (0, 0))],
        out_specs=pl.BlockSpec((1, BLK, D_ATTN), lambda b, i: (b, i, 0)),
        out_shape=jax.ShapeDtypeStruct((bsz, t, D_ATTN), BF16),
        scratch_shapes=[pltpu.VMEM((t, BLK), jnp.int32),
                        pltpu.VMEM((3, ATTN_KV_HEADS, BLK, gw), F32),
                        pltpu.VMEM((ATTN_KV_HEADS, ATTN_HEAD_DIM, gw), BF16),
                        pltpu.VMEM((IDX_DIM, IDX_HEADS * BLK), BF16),
                        pltpu.VMEM((8, IDX_HEADS * BLK), F32),
                        pltpu.VMEM((ATTN_KV_HEADS, 1, gw), F32),
                        pltpu.VMEM((ATTN_KV_HEADS, V_ROWS, gw), F32)],
        compiler_params=pltpu.CompilerParams(
            dimension_semantics=("arbitrary", "arbitrary"), vmem_limit_bytes=48 * 2 ** 20),
        name="dsa",
    )(rel_bias, bkt, q, qi, misc, k, vt, miscb, nw)


def _tail_kernel(x_ref, ys_ref, ya_ref, mod_ref, wo_ref, n1_ref, n2_ref, n3_ref, w1_ref, w2_ref, o_ref):
    x = x_ref[0]
    mix = (jnp.dot(ys_ref[0], wo_ref[0:D_SSM, :], preferred_element_type=F32)
           + jnp.dot(ya_ref[0], wo_ref[D_SSM:, :], preferred_element_type=F32))
    x1 = x + mod_ref[0, 2:3, :] * _rms(mix, n1_ref[...])
    h = (_rms(x1, n2_ref[...]) * (1.0 + mod_ref[0, 4:5, :]) + mod_ref[0, 3:4, :]).astype(BF16)
    f = jnp.zeros(x.shape, F32)
    step = D_MODEL
    for c0 in range(0, D_FF, step):
        a = jnp.maximum(jnp.dot(h, w1_ref[:, c0:c0 + step], preferred_element_type=F32), 0.0)
        f = f + jnp.dot((a * a).astype(BF16), w2_ref[c0:c0 + step, :], preferred_element_type=F32)
    o_ref[0] = x1 + mod_ref[0, 5:6, :] * _rms(f, n3_ref[...])


def _tail(x, ys, ya, mod, wo, n1, n2, n3, w1, w2):
    bsz, t, d = x.shape
    tm = min(ROW_TILE, t)

    def const(shape):
        return pl.BlockSpec(shape, lambda b, i: tuple(0 for _ in shape))

    return pl.pallas_call(
        _tail_kernel,
        grid=(bsz, t // tm),
        in_specs=[pl.BlockSpec((1, tm, d), lambda b, i: (b, i, 0)),
                  pl.BlockSpec((1, tm, D_SSM), lambda b, i: (b, i, 0)),
                  pl.BlockSpec((1, tm, D_ATTN), lambda b, i: (b, i, 0)),
                  pl.BlockSpec((1, 6, d), lambda b, i: (b, 0, 0)),
                  const((D_SSM + D_ATTN, d)), const((1, d)), const((1, d)), const((1, d)),
                  const((d, D_FF)), const((D_FF, d))],
        out_specs=pl.BlockSpec((1, tm, d), lambda b, i: (b, i, 0)),
        out_shape=jax.ShapeDtypeStruct((bsz, t, d), F32),
        compiler_params=pltpu.CompilerParams(
            dimension_semantics=("arbitrary", "arbitrary"), vmem_limit_bytes=60 * 2 ** 20),
        name="tail",
    )(x, ys, ya, mod, wo, n1, n2, n3, w1, w2)


def _permute_w_in(w):
    sizes = [D_SSM, D_XBC, SSM_HEADS, D_ATTN, D_KV, D_KV, IDX_HEADS * IDX_DIM, IDX_DIM, IDX_HEADS]
    pts = np.cumsum(sizes)[:-1].tolist()
    z, xbc, dt, q, k, v, qidx, kidx, widx = jnp.split(w, pts, axis=-1)
    pad = jnp.zeros((w.shape[0], MISC_W - IDX_DIM - IDX_HEADS - SSM_HEADS), w.dtype)
    return jnp.concatenate([z, xbc, q, k, v, qidx, kidx, widx, dt, pad], axis=-1).astype(BF16)


def _layer(x, c_pad, w_ada, b_ada, norm1_pre, norm1_post, w_in, conv_w, conv_b, dt_bias, a_log, d_skip,
           ssm_norm, rel_bias, attn_norm, w_out, norm2_pre, norm2_post, w_mlp_in, w_mlp_out):
    bsz, t, d = x.shape
    mod = _ada(c_pad, w_ada, b_ada[None, :])[:bsz].reshape(bsz, 6, d)
    z, xbc, q, k, v, qi, misc, miscb = _inproj(x, mod, norm1_pre[None, :], _permute_w_in(w_in))

    lane = jnp.arange(MISC_W)
    is_dt = (lane >= MISC_DT) & (lane < MISC_DT + SSM_HEADS)
    head_of = jnp.clip(lane - MISC_DT, 0, SSM_HEADS - 1)
    dtb_t = jnp.where(is_dt, dt_bias[head_of], 0.0)[None, :]
    a_t = jnp.where(is_dt, -jnp.exp(a_log.astype(F32))[head_of], 0.0)[None, :]
    expand = ((lane[:, None] - MISC_DT) == (jnp.arange(D_SSM)[None, :] // SSM_HEAD_DIM)).astype(F32)
    dskip_e = jnp.repeat(d_skip.astype(F32), SSM_HEAD_DIM)[None, :]
    y_ssd = _ssd(xbc, z, misc, conv_w, conv_b[None, :], dtb_t, a_t, expand, dskip_e, ssm_norm[None, :])

    nq = t // BLK
    v4 = v.reshape(bsz, t, ATTN_KV_HEADS, ATTN_HEAD_DIM)
    v4 = jnp.concatenate([v4, jnp.ones((bsz, t, ATTN_KV_HEADS, 1), BF16),
                          jnp.zeros((bsz, t, ATTN_KV_HEADS, V_ROWS - ATTN_HEAD_DIM - 1), BF16)], axis=-1)
    vt = jnp.transpose(v4.reshape(bsz, nq, BLK, ATTN_KV_HEADS, V_ROWS), (0, 1, 3, 4, 2))
    y_att = _dsa(rel_bias, q, qi, misc, k, vt, miscb, attn_norm[None, :])

    return _tail(x, y_ssd, y_att, mod, w_out.astype(BF16), norm1_post[None, :], norm2_pre[None, :],
                 norm2_post[None, :], w_mlp_in.astype(BF16), w_mlp_out.astype(BF16))


def kernel(x, c, w_ada, b_ada, norm1_pre, norm1_post, w_in, conv_w, conv_b, dt_bias, a_log, d_skip,
           ssm_norm, rel_bias, attn_norm, w_out, norm2_pre, norm2_post, w_mlp_in, w_mlp_out):
    bsz = x.shape[0]
    assert bsz <= 8 and x.shape[1] % DSA_CHUNK == 0 and x.shape[2] == D_MODEL
    c_pad = jnp.zeros((8, D_MODEL), F32).at[:bsz].set(c)
    for l in range(w_ada.shape[0]):
        x = _layer(x, c_pad, w_ada[l], b_ada[l], norm1_pre[l], norm1_post[l], w_in[l], conv_w[l], conv_b[l],
                   dt_bias[l], a_log[l], d_skip[l], ssm_norm[l], rel_bias, attn_norm[l], w_out[l],
                   norm2_pre[l], norm2_post[l], w_mlp_in[l], w_mlp_out[l])
    return x
```

```python
import functools
import math

import numpy as np
import jax
import jax.numpy as jnp
from jax import lax
from jax.experimental import pallas as pl
from jax.experimental.pallas import tpu as pltpu

D_MODEL = 1024
SSM_HEADS = 8
SSM_HEAD_DIM = 64
D_SSM = SSM_HEADS * SSM_HEAD_DIM
SSM_GROUPS = 2
SSM_STATE = 128
CONV_WIDTH = 4
CHUNK = 256
ATTN_HEADS = 8
ATTN_KV_HEADS = 2
ATTN_HEAD_DIM = 64
D_ATTN = ATTN_HEADS * ATTN_HEAD_DIM
D_KV = ATTN_KV_HEADS * ATTN_HEAD_DIM
IDX_HEADS = 8
IDX_DIM = 64
TOPK_MAX = 256
N_BUCKETS = 32
MAX_DISTANCE = 128
D_FF = 4 * D_MODEL
D_BC = SSM_GROUPS * SSM_STATE
D_XBC = D_SSM + 2 * D_BC
EPS = 1e-6

LANES = 128
MISC_W = LANES
MISC_KIDX = 0
MISC_WIDX = IDX_DIM
MISC_DT = IDX_DIM + IDX_HEADS
OFF_Z = 0
OFF_XBC = OFF_Z + D_SSM
OFF_Q = OFF_XBC + D_XBC
OFF_K = OFF_Q + D_ATTN
OFF_V = OFF_K + D_KV
OFF_QI = OFF_V + D_KV
OFF_MISC = OFF_QI + IDX_HEADS * IDX_DIM
D_IN_PAD = OFF_MISC + MISC_W

BLK = 128
DSA_CHUNK = 512
V_ROWS = 80
DEN_LO = 2.0 ** -80
DEN_HI = 2.0 ** 80
F32_HUGE = 3.0e38
LOG2E = math.log2(math.e)
ROW_TILE = 512
NEG = -1e30
INT_MIN = -2 ** 31

F32 = jnp.float32
BF16 = jnp.bfloat16
HI = lax.Precision.HIGHEST
NT = (((1,), (1,)), ((), ()))


def _silu(x):
    return x / (1.0 + jnp.exp(-x))


def _rms(x, w):
    return x * lax.rsqrt(jnp.mean(x * x, axis=-1, keepdims=True) + EPS) * w


def _ada_kernel(c_ref, w_ref, b_ref, o_ref):
    s = _silu(c_ref[...])
    o_ref[...] = jnp.dot(s, w_ref[...], precision=HI, preferred_element_type=F32) + b_ref[...]


def _ada(c_pad, w, b):
    n = w.shape[1]
    tn = D_MODEL
    return pl.pallas_call(
        _ada_kernel,
        grid=(n // tn,),
        in_specs=[pl.BlockSpec((8, D_MODEL), lambda j: (0, 0)),
                  pl.BlockSpec((D_MODEL, tn), lambda j: (0, j)),
                  pl.BlockSpec((1, tn), lambda j: (0, j))],
        out_specs=pl.BlockSpec((8, tn), lambda j: (0, j)),
        out_shape=jax.ShapeDtypeStruct((8, n), F32),
        name="ada",
    )(c_pad, w, b)


def _inproj_kernel(x_ref, mod_ref, nw_ref, w_ref,
                   z_ref, xbc_ref, q_ref, k_ref, v_ref, qi_ref, misc_ref, miscb_ref):
    x = x_ref[0]
    h = _rms(x, nw_ref[...]) * (1.0 + mod_ref[0, 1:2, :]) + mod_ref[0, 0:1, :]
    hb = h.astype(BF16)

    def seg(lo, width):
        return jnp.dot(hb, w_ref[:, lo:lo + width], preferred_element_type=F32)

    z_ref[0] = seg(OFF_Z, D_SSM)
    xbc_ref[0] = seg(OFF_XBC, D_XBC)
    q_ref[0] = (seg(OFF_Q, D_ATTN) * (ATTN_HEAD_DIM ** -0.5 * LOG2E)).astype(BF16)
    k_ref[0] = seg(OFF_K, D_KV).astype(BF16)
    v_ref[0] = seg(OFF_V, D_KV).astype(BF16)
    qi_ref[0] = seg(OFF_QI, IDX_HEADS * IDX_DIM).astype(BF16)
    misc = seg(OFF_MISC, MISC_W)
    misc_ref[0] = misc
    miscb_ref[0] = misc.astype(BF16)


def _inproj(x, mod, nw, w_perm):
    bsz, t, d = x.shape
    tm = min(ROW_TILE, t)

    def tok(width, dtype):
        return (pl.BlockSpec((1, tm, width), lambda b, i: (b, i, 0)),
                jax.ShapeDtypeStruct((bsz, t, width), dtype))

    outs = [tok(D_SSM, F32), tok(D_XBC, F32), tok(D_ATTN, BF16), tok(D_KV, BF16), tok(D_KV, BF16),
            tok(IDX_HEADS * IDX_DIM, BF16), tok(MISC_W, F32), tok(MISC_W, BF16)]
    return pl.pallas_call(
        _inproj_kernel,
        grid=(bsz, t // tm),
        in_specs=[pl.BlockSpec((1, tm, d), lambda b, i: (b, i, 0)),
                  pl.BlockSpec((1, 6, d), lambda b, i: (b, 0, 0)),
                  pl.BlockSpec((1, d), lambda b, i: (0, 0)),
                  pl.BlockSpec((d, D_IN_PAD), lambda b, i: (0, 0))],
        out_specs=[o[0] for o in outs],
        out_shape=[o[1] for o in outs],
        compiler_params=pltpu.CompilerParams(
            dimension_semantics=("arbitrary", "arbitrary"), vmem_limit_bytes=48 * 2 ** 20),
        name="inproj",
    )(x, mod, nw, w_perm)


def _ssd_kernel(xbc_ref, z_ref, misc_ref, cw_ref, cb_ref, dtb_ref, a_ref, e_ref, dskip_ref, nw_ref,
                o_ref, xbuf, state):
    c = pl.program_id(1)
    lc = xbc_ref.shape[1]

    @pl.when(c == 0)
    def _():
        xbuf[0:8, :] = jnp.zeros((8, D_XBC), F32)
        state[...] = jnp.zeros(state.shape, F32)

    @pl.when(c > 0)
    def _():
        xbuf[0:8, :] = xbuf[lc:lc + 8, :]

    xbuf[8:lc + 8, :] = xbc_ref[0]
    u = cb_ref[...]
    for k in range(CONV_WIDTH):
        u = u + xbuf[pl.ds(8 - (CONV_WIDTH - 1) + k, lc), :] * cw_ref[k:k + 1, :]
    u = _silu(u)
    xs = u[:, :D_SSM]
    bm = u[:, D_SSM:D_SSM + D_BC]
    cm = u[:, D_SSM + D_BC:]

    raw = misc_ref[0] + dtb_ref[...]
    dt_t = jnp.maximum(raw, 0.0) + jnp.log1p(jnp.exp(-jnp.abs(raw)))
    adt_t = dt_t * a_ref[...]
    row = lax.broadcasted_iota(jnp.int32, (lc, lc), 0)
    col = lax.broadcasted_iota(jnp.int32, (lc, lc), 1)
    causal = col <= row
    tril = causal.astype(F32)
    acs_t = jnp.dot(tril, adt_t, precision=HI, preferred_element_type=F32)
    expand = e_ref[...]
    dt_e = jnp.dot(dt_t, expand, precision=HI, preferred_element_type=F32)
    acs_e = jnp.dot(acs_t, expand, precision=HI, preferred_element_type=F32)
    acs_row = jnp.transpose(acs_t)

    xdt = xs * dt_e
    a_last = acs_e[lc - 1:lc, :]
    xdec = (xdt * jnp.exp(a_last - acs_e)).astype(BF16)
    xdt_b = xdt.astype(BF16)
    bm_b = bm.astype(BF16)
    cm_b = cm.astype(BF16)
    bm_t = jnp.transpose(bm).astype(BF16)

    heads_per_group = SSM_HEADS // SSM_GROUPS
    gw = heads_per_group * SSM_HEAD_DIM
    y_parts = []
    off_parts = []
    for g in range(SSM_GROUPS):
        bg = bm_b[:, g * SSM_STATE:(g + 1) * SSM_STATE]
        cg = cm_b[:, g * SSM_STATE:(g + 1) * SSM_STATE]
        cb = lax.dot_general(cg, bg, NT, preferred_element_type=F32)
        for r in range(heads_per_group):
            hd = g * heads_per_group + r
            a_col = acs_t[:, MISC_DT + hd:MISC_DT + hd + 1]
            a_row = acs_row[MISC_DT + hd:MISC_DT + hd + 1, :]
            lmat = jnp.exp(jnp.where(causal, a_col - a_row, -jnp.inf))
            mh = (cb * lmat).astype(BF16)
            y_parts.append(jnp.dot(mh, xdt_b[:, hd * SSM_HEAD_DIM:(hd + 1) * SSM_HEAD_DIM],
                                   preferred_element_type=F32))
        prev = state[g]
        off_parts.append(jnp.dot(cg, prev.astype(BF16), preferred_element_type=F32))
        st = jnp.dot(bm_t[g * SSM_STATE:(g + 1) * SSM_STATE, :], xdec[:, g * gw:(g + 1) * gw],
                     preferred_element_type=F32)
        state[g] = prev * jnp.exp(a_last[:, g * gw:(g + 1) * gw]) + st
    y = (jnp.concatenate(y_parts, axis=1)
         + jnp.concatenate(off_parts, axis=1) * jnp.exp(acs_e)
         + xs * dskip_ref[...])
    o_ref[0] = _rms(y * _silu(z_ref[0]), nw_ref[...]).astype(BF16)


def _ssd(xbc, z, misc, cw, cb, dtb_t, a_t, expand, dskip_e, nw):
    bsz, t, _ = xbc.shape
    lc = math.gcd(t, CHUNK)
    nc = t // lc
    gw = (SSM_HEADS // SSM_GROUPS) * SSM_HEAD_DIM

    def const(shape):
        return pl.BlockSpec(shape, lambda b, c: tuple(0 for _ in shape))

    return pl.pallas_call(
        _ssd_kernel,
        grid=(bsz, nc),
        in_specs=[pl.BlockSpec((1, lc, D_XBC), lambda b, c: (b, c, 0)),
                  pl.BlockSpec((1, lc, D_SSM), lambda b, c: (b, c, 0)),
                  pl.BlockSpec((1, lc, MISC_W), lambda b, c: (b, c, 0)),
                  const((CONV_WIDTH, D_XBC)), const((1, D_XBC)), const((1, MISC_W)), const((1, MISC_W)),
                  const((MISC_W, D_SSM)), const((1, D_SSM)), const((1, D_SSM))],
        out_specs=pl.BlockSpec((1, lc, D_SSM), lambda b, c: (b, c, 0)),
        out_shape=jax.ShapeDtypeStruct((bsz, t, D_SSM), BF16),
        scratch_shapes=[pltpu.VMEM((lc + 8, D_XBC), F32),
                        pltpu.VMEM((SSM_GROUPS, SSM_STATE, gw), F32)],
        compiler_params=pltpu.CompilerParams(
            dimension_semantics=("arbitrary", "arbitrary"), vmem_limit_bytes=48 * 2 ** 20),
        name="ssd",
    )(xbc, z, misc, cw, cb, dtb_t, a_t, expand, dskip_e, nw)


def _bucket_tiles():
    s = np.arange(BLK)[:, None]
    t = np.arange(BLK)[None, :]
    max_exact = N_BUCKETS // 2
    tiles = []
    for d0 in (0, BLK):
        n = np.maximum(d0 + t - s, 0)
        large = {}
        for dt_ in (np.float32, np.float64):
            nf = np.maximum(n, 1).astype(dt_)
            lg = max_exact + (np.log(nf / dt_(max_exact)) / dt_(math.log(MAX_DISTANCE / max_exact))
                              * dt_(N_BUCKETS - max_exact)).astype(np.int32)
            large[dt_] = np.minimum(lg, N_BUCKETS - 1)
        assert (large[np.float32] == large[np.float64]).all()
        tiles.append(np.where(n < max_exact, n, large[np.float64]).astype(np.int32))
    assert BLK + 1 >= MAX_DISTANCE
    return np.stack(tiles)


def _to_key(x):
    bits = lax.bitcast_convert_type(x, jnp.int32)
    key = bits ^ (lax.shift_right_arithmetic(bits, 31) & jnp.int32(0x7FFFFFFF))
    return jnp.where(key == -1, 0, key)


def _transpose32(rows):
    rows = list(rows)
    mask, j = 0x0000FFFF, 16
    while j:
        k = 0
        while k < 32:
            t = (rows[k] ^ lax.shift_right_logical(rows[k + j], jnp.int32(j))) & jnp.int32(mask)
            rows[k] = rows[k] ^ t
            rows[k + j] = rows[k + j] ^ lax.shift_left(t, jnp.int32(j))
            k = (k + j + 1) & ~j
        j >>= 1
        mask ^= (mask << j) & 0xFFFFFFFF
    return rows


def _dsa_kernel(relb_ref, bkt_ref, q_ref, qi_ref, miscq_ref, k_ref, vt_ref, kidx_ref, nw_ref,
                o_ref, keys, planes, live, bias_s, qcat, qicat, wcat, m_s, acc_s, *, topk, idx_bits):
    b = pl.program_id(0)
    qi = pl.program_id(1)
    tq = BLK
    i32 = jnp.int32
    rep = ATTN_HEADS // ATTN_KV_HEADS
    gw = rep * tq
    ch = DSA_CHUNK
    ch_blks = ch // BLK

    @pl.when((b == 0) & (qi == 0))
    def _init_bias():
        for d in range(2):
            bk = bkt_ref[d]
            for h in range(ATTN_HEADS):
                far = relb_ref[N_BUCKETS - 1, h]
                acc = jnp.zeros((BLK, tq), F32)
                for n in range(N_BUCKETS - 1):
                    acc = jnp.where(bk == n, relb_ref[n, h] - far, acc)
                bias_s[d, h // rep, :, (h % rep) * BLK:(h % rep + 1) * BLK] = acc * LOG2E
        bias_s[2] = jnp.zeros((ATTN_KV_HEADS, BLK, gw), F32)

    q_t = jnp.transpose(q_ref[0].astype(F32))
    for h in range(ATTN_HEADS):
        qcat[h // rep, :, (h % rep) * BLK:(h % rep + 1) * BLK] = (
            q_t[h * ATTN_HEAD_DIM:(h + 1) * ATTN_HEAD_DIM, :].astype(BF16))
    qi_t = jnp.transpose(qi_ref[0].astype(F32))
    w_t = jnp.transpose(miscq_ref[0])[MISC_WIDX:MISC_WIDX + IDX_HEADS, :] * (
        IDX_HEADS ** -0.5 * IDX_DIM ** -0.5)
    for h in range(IDX_HEADS):
        qicat[:, h * BLK:(h + 1) * BLK] = qi_t[h * IDX_DIM:(h + 1) * IDX_DIM, :].astype(BF16)
        wcat[:, h * BLK:(h + 1) * BLK] = jnp.broadcast_to(w_t[h:h + 1, :], (8, BLK))

    nch = (qi + ch_blks) // ch_blks
    row_c = lax.broadcasted_iota(i32, (ch, tq), 0)
    t_idx = qi * BLK + lax.broadcasted_iota(i32, (ch, tq), 1)

    def chunk_start(i):
        return pl.multiple_of(i * ch, ch)

    def idx_chunk(i, carry):
        s0 = chunk_start(i)
        kb = kidx_ref[0, pl.ds(s0, ch), MISC_KIDX:MISC_KIDX + IDX_DIM]
        s = jnp.dot(kb, qicat[...], preferred_element_type=F32)
        r = jnp.maximum(s, 0.0) * wcat[0:1, :]
        acc = r[:, 0:BLK]
        for h in range(1, IDX_HEADS):
            acc = acc + r[:, h * BLK:(h + 1) * BLK]
        acc = jnp.where(s0 + row_c <= t_idx, acc, -jnp.inf)
        key = _to_key(acc)
        keys[pl.ds(s0, ch), :] = key
        ku = (key ^ i32(INT_MIN)).reshape(ch // (32 * 8), 32, 8, tq)
        words = _transpose32([ku[:, j] for j in range(32)])
        r0 = pl.multiple_of(i * (ch // 32), ch // 32)
        for w in range(32):
            planes[w, pl.ds(r0, ch // 32), :] = words[w].reshape(ch // 32, tq)
        return carry

    lax.fori_loop(0, nch, idx_chunk, 0)

    nrows = live.shape[0]
    live[...] = jnp.where(lax.broadcasted_iota(i32, (nrows, tq), 0) < nch * (ch // 32), i32(-1), i32(0))

    def ones_in(words):
        return jnp.sum(lax.population_count(words), axis=0, keepdims=True)

    def bit_body(w, st):
        thr_u, n_above, n_set = st
        take = n_above + n_set >= topk
        thr_u = jnp.where(take, thr_u | lax.shift_left(i32(1), i32(31) - w), thr_u)
        n_above = jnp.where(take, n_above, n_above + n_set)
        still = live[...] & (planes[w] ^ jnp.where(take, i32(0), i32(-1)))
        live[...] = still
        return thr_u, n_above, ones_in(still & planes[jnp.minimum(w + 1, 31)])

    zero = jnp.zeros((1, tq), i32)
    thr_u, n_gt, _ = lax.fori_loop(0, 32, bit_body, (zero, zero, ones_in(live[...] & planes[0])))
    thr = thr_u ^ i32(INT_MIN)
    n_eq = ones_in(live[...])
    need = topk - n_gt

    def count(pred):
        def body(i, cnt):
            s0 = chunk_start(i)
            hit = pred(keys[pl.ds(s0, ch), :], s0).astype(i32)
            n = ch
            while n > 8:
                n //= 2
                hit = hit[:n] + hit[n:2 * n]
            return cnt + hit
        cnt = lax.fori_loop(0, nch, body, jnp.zeros((8, tq), i32))
        return jnp.sum(cnt, axis=0, keepdims=True)

    @pl.when(jnp.max((n_eq > need).astype(i32)) > 0)
    def _ties():
        def jbody(it, p):
            cand = p + lax.shift_left(i32(1), i32(idx_bits - 1) - it)
            c = count(lambda kk, s0: (kk == thr) & (s0 + row_c < cand))
            return jnp.where(c < need, cand, p)
        last = lax.fori_loop(0, idx_bits, jbody, jnp.zeros((1, tq), i32))

        def demote(i, carry):
            s0 = chunk_start(i)
            kk = keys[pl.ds(s0, ch), :]
            keys[pl.ds(s0, ch), :] = jnp.where((kk == thr) & (s0 + row_c > last), INT_MIN, kk)
            return carry
        lax.fori_loop(0, nch, demote, 0)

    def pen_of(s0, causal):
        sel = keys[pl.ds(s0, ch), :] >= thr
        if causal:
            sel = sel & (s0 + row_c <= t_idx)
        pen = jnp.where(sel, 0.0, NEG)
        return jnp.concatenate([pen] * rep, axis=1)

    def near_bias(i, g):
        return jnp.concatenate(
            [bias_s[jnp.clip(qi - (i * ch_blks + u), 0, 2), g] for u in range(ch_blks)], axis=0)

    def vt_chunk(i, g):
        return jnp.concatenate([vt_ref[0, i * ch_blks + u, g] for u in range(ch_blks)], axis=1)

    s0d = pl.multiple_of(qi * BLK, BLK)
    kd = k_ref[0, pl.ds(s0d, BLK), :]
    tri = lax.broadcasted_iota(i32, (BLK, tq), 0) <= lax.broadcasted_iota(i32, (BLK, tq), 1)
    tri_pen = jnp.concatenate([jnp.where(tri, 0.0, NEG)] * rep, axis=1)
    for g in range(ATTN_KV_HEADS):
        sd = jnp.dot(kd[:, g * ATTN_HEAD_DIM:(g + 1) * ATTN_HEAD_DIM], qcat[g],
                     preferred_element_type=F32) + bias_s[0, g] + tri_pen
        m_s[g] = jnp.max(sd, axis=0, keepdims=True)
        acc_s[g] = jnp.zeros((V_ROWS, gw), F32)

    def att_chunk(i, near):
        s0 = chunk_start(i)
        pen = pen_of(s0, near)
        kblk = k_ref[0, pl.ds(s0, ch), :]
        for g in range(ATTN_KV_HEADS):
            s = jnp.dot(kblk[:, g * ATTN_HEAD_DIM:(g + 1) * ATTN_HEAD_DIM], qcat[g],
                        preferred_element_type=F32)
            if near:
                s = s + near_bias(i, g)
            p = jnp.exp2(s - m_s[g] + pen).astype(BF16)
            acc_s[g] += jnp.dot(vt_chunk(i, g), p, preferred_element_type=F32)

    def far_body(i, carry):
        att_chunk(i, False)
        return carry

    def near_body(i, carry):
        att_chunk(i, True)
        return carry

    n_far = jnp.maximum(qi - 1, 0) // ch_blks
    lax.fori_loop(0, n_far, far_body, 0)
    lax.fori_loop(n_far, nch, near_body, 0)

    n_ok = jnp.zeros((1, gw), i32)
    for g in range(ATTN_KV_HEADS):
        a = acc_s[g]
        den = a[ATTN_HEAD_DIM:ATTN_HEAD_DIM + 1, :]
        top = jnp.max(jnp.abs(a[0:ATTN_HEAD_DIM, :]), axis=0, keepdims=True)
        n_ok = n_ok + ((den > DEN_LO) & (den < DEN_HI) & (top < F32_HUGE)).astype(i32)

    @pl.when(jnp.min(n_ok) < ATTN_KV_HEADS)
    def _exact():
        for g in range(ATTN_KV_HEADS):
            m_s[g] = jnp.full((1, gw), NEG, F32)
            acc_s[g] = jnp.zeros((V_ROWS, gw), F32)

        def body(i, carry):
            s0 = chunk_start(i)
            pen = pen_of(s0, True)
            kblk = k_ref[0, pl.ds(s0, ch), :]
            for g in range(ATTN_KV_HEADS):
                s = jnp.dot(kblk[:, g * ATTN_HEAD_DIM:(g + 1) * ATTN_HEAD_DIM], qcat[g],
                            preferred_element_type=F32) + near_bias(i, g) + pen
                m_old = m_s[g]
                m_new = jnp.maximum(m_old, jnp.max(s, axis=0, keepdims=True))
                p = jnp.exp2(s - m_new).astype(BF16)
                acc_s[g] = acc_s[g] * jnp.exp2(m_old - m_new) + jnp.dot(
                    vt_chunk(i, g), p, preferred_element_type=F32)
                m_s[g] = m_new
            return carry
        lax.fori_loop(0, nch, body, 0)

    parts = []
    for g in range(ATTN_KV_HEADS):
        a = acc_s[g]
        o = a[0:ATTN_HEAD_DIM, :] * (1.0 / a[ATTN_HEAD_DIM:ATTN_HEAD_DIM + 1, :])
        parts += [o[:, r * tq:(r + 1) * tq] for r in range(rep)]
    out_t = jnp.concatenate(parts, axis=0)
    o_ref[0] = _rms(jnp.transpose(out_t), nw_ref[...]).astype(BF16)


def _dsa(rel_bias, q, qi, misc, k, vt, miscb, nw):
    bsz, t, _ = q.shape
    nq = t // BLK
    topk = min(TOPK_MAX, t // 4)
    idx_bits = max(1, (t - 1).bit_length())
    gw = (ATTN_HEADS // ATTN_KV_HEADS) * BLK
    bkt = jnp.asarray(_bucket_tiles())
    kern = functools.partial(_dsa_kernel, topk=topk, idx_bits=idx_bits)
    return pl.pallas_call(
        kern,
        grid=(bsz, nq),
        in_specs=[pl.BlockSpec(memory_space=pltpu.SMEM),
                  pl.BlockSpec((2, BLK, BLK), lambda b, i: (0, 0, 0)),
                  pl.BlockSpec((1, BLK, D_ATTN), lambda b, i: (b, i, 0)),
                  pl.BlockSpec((1, BLK, IDX_HEADS * IDX_DIM), lambda b, i: (b, i, 0)),
                  pl.BlockSpec((1, BLK, MISC_W), lambda b, i: (b, i, 0)),
                  pl.BlockSpec((1, t, D_KV), lambda b, i: (b, 0, 0)),
                  pl.BlockSpec((1, nq, ATTN_KV_HEADS, V_ROWS, BLK), lambda b, i: (b, 0, 0, 0, 0)),
                  pl.BlockSpec((1, t, MISC_W), lambda b, i: (b, 0, 0)),
                  pl.BlockSpec((1, D_ATTN), lambda b, i: (0, 0))],
        out_specs=pl.BlockSpec((1, BLK, D_ATTN), lambda b, i: (b, i, 0)),
        out_shape=jax.ShapeDtypeStruct((bsz, t, D_ATTN), BF16),
        scratch_shapes=[pltpu.VMEM((t, BLK), jnp.int32),
                        pltpu.VMEM((32, t // 32, BLK), jnp.int32),
                        pltpu.VMEM((t // 32, BLK), jnp.int32),
                        pltpu.VMEM((3, ATTN_KV_HEADS, BLK, gw), F32),
                        pltpu.VMEM((ATTN_KV_HEADS, ATTN_HEAD_DIM, gw), BF16),
                        pltpu.VMEM((IDX_DIM, IDX_HEADS * BLK), BF16),
                        pltpu.VMEM((8, IDX_HEADS * BLK), F32),
                        pltpu.VMEM((ATTN_KV_HEADS, 1, gw), F32),
                        pltpu.VMEM((ATTN_KV_HEADS, V_ROWS, gw), F32)],
        compiler_params=pltpu.CompilerParams(
            dimension_semantics=("arbitrary", "arbitrary"), vmem_limit_bytes=48 * 2 ** 20),
        name="dsa",
    )(rel_bias, bkt, q, qi, misc, k, vt, miscb, nw)


def _tail_kernel(x_ref, ys_ref, ya_ref, mod_ref, wo_ref, n1_ref, n2_ref, n3_ref, w1_ref, w2_ref, o_ref):
    x = x_ref[0]
    mix = (jnp.dot(ys_ref[0], wo_ref[0:D_SSM, :], preferred_element_type=F32)
           + jnp.dot(ya_ref[0], wo_ref[D_SSM:, :], preferred_element_type=F32))
    x1 = x + mod_ref[0, 2:3, :] * _rms(mix, n1_ref[...])
    h = (_rms(x1, n2_ref[...]) * (1.0 + mod_ref[0, 4:5, :]) + mod_ref[0, 3:4, :]).astype(BF16)
    f = jnp.zeros(x.shape, F32)
    step = D_MODEL
    for c0 in range(0, D_FF, step):
        a = jnp.maximum(jnp.dot(h, w1_ref[:, c0:c0 + step], preferred_element_type=F32), 0.0)
        f = f + jnp.dot((a * a).astype(BF16), w2_ref[c0:c0 + step, :], preferred_element_type=F32)
    o_ref[0] = x1 + mod_ref[0, 5:6, :] * _rms(f, n3_ref[...])


def _tail(x, ys, ya, mod, wo, n1, n2, n3, w1, w2):
    bsz, t, d = x.shape
    tm = min(ROW_TILE, t)

    def const(shape):
        return pl.BlockSpec(shape, lambda b, i: tuple(0 for _ in shape))

    return pl.pallas_call(
        _tail_kernel,
        grid=(bsz, t // tm),
        in_specs=[pl.BlockSpec((1, tm, d), lambda b, i: (b, i, 0)),
                  pl.BlockSpec((1, tm, D_SSM), lambda b, i: (b, i, 0)),
                  pl.BlockSpec((1, tm, D_ATTN), lambda b, i: (b, i, 0)),
                  pl.BlockSpec((1, 6, d), lambda b, i: (b, 0, 0)),
                  const((D_SSM + D_ATTN, d)), const((1, d)), const((1, d)), const((1, d)),
                  const((d, D_FF)), const((D_FF, d))],
        out_specs=pl.BlockSpec((1, tm, d), lambda b, i: (b, i, 0)),
        out_shape=jax.ShapeDtypeStruct((bsz, t, d), F32),
        compiler_params=pltpu.CompilerParams(
            dimension_semantics=("arbitrary", "arbitrary"), vmem_limit_bytes=60 * 2 ** 20),
        name="tail",
    )(x, ys, ya, mod, wo, n1, n2, n3, w1, w2)


def _permute_w_in(w):
    sizes = [D_SSM, D_XBC, SSM_HEADS, D_ATTN, D_KV, D_KV, IDX_HEADS * IDX_DIM, IDX_DIM, IDX_HEADS]
    pts = np.cumsum(sizes)[:-1].tolist()
    z, xbc, dt, q, k, v, qidx, kidx, widx = jnp.split(w, pts, axis=-1)
    pad = jnp.zeros((w.shape[0], MISC_W - IDX_DIM - IDX_HEADS - SSM_HEADS), w.dtype)
    return jnp.concatenate([z, xbc, q, k, v, qidx, kidx, widx, dt, pad], axis=-1).astype(BF16)


def _layer(x, c_pad, w_ada, b_ada, norm1_pre, norm1_post, w_in, conv_w, conv_b, dt_bias, a_log, d_skip,
           ssm_norm, rel_bias, attn_norm, w_out, norm2_pre, norm2_post, w_mlp_in, w_mlp_out):
    bsz, t, d = x.shape
    mod = _ada(c_pad, w_ada, b_ada[None, :])[:bsz].reshape(bsz, 6, d)
    z, xbc, q, k, v, qi, misc, miscb = _inproj(x, mod, norm1_pre[None, :], _permute_w_in(w_in))

    lane = jnp.arange(MISC_W)
    is_dt = (lane >= MISC_DT) & (lane < MISC_DT + SSM_HEADS)
    head_of = jnp.clip(lane - MISC_DT, 0, SSM_HEADS - 1)
    dtb_t = jnp.where(is_dt, dt_bias[head_of], 0.0)[None, :]
    a_t = jnp.where(is_dt, -jnp.exp(a_log.astype(F32))[head_of], 0.0)[None, :]
    expand = ((lane[:, None] - MISC_DT) == (jnp.arange(D_SSM)[None, :] // SSM_HEAD_DIM)).astype(F32)
    dskip_e = jnp.repeat(d_skip.astype(F32), SSM_HEAD_DIM)[None, :]
    y_ssd = _ssd(xbc, z, misc, conv_w, conv_b[None, :], dtb_t, a_t, expand, dskip_e, ssm_norm[None, :])

    nq = t // BLK
    v4 = v.reshape(bsz, t, ATTN_KV_HEADS, ATTN_HEAD_DIM)
    v4 = jnp.concatenate([v4, jnp.ones((bsz, t, ATTN_KV_HEADS, 1), BF16),
                          jnp.zeros((bsz, t, ATTN_KV_HEADS, V_ROWS - ATTN_HEAD_DIM - 1), BF16)], axis=-1)
    vt = jnp.transpose(v4.reshape(bsz, nq, BLK, ATTN_KV_HEADS, V_ROWS), (0, 1, 3, 4, 2))
    y_att = _dsa(rel_bias, q, qi, misc, k, vt, miscb, attn_norm[None, :])

    return _tail(x, y_ssd, y_att, mod, w_out.astype(BF16), norm1_post[None, :], norm2_pre[None, :],
                 norm2_post[None, :], w_mlp_in.astype(BF16), w_mlp_out.astype(BF16))


def kernel(x, c, w_ada, b_ada, norm1_pre, norm1_post, w_in, conv_w, conv_b, dt_bias, a_log, d_skip,
           ssm_norm, rel_bias, attn_norm, w_out, norm2_pre, norm2_post, w_mlp_in, w_mlp_out):
    bsz = x.shape[0]
    assert bsz <= 8 and x.shape[1] % DSA_CHUNK == 0 and x.shape[2] == D_MODEL
    c_pad = jnp.zeros((8, D_MODEL), F32).at[:bsz].set(c)
    for l in range(w_ada.shape[0]):
        x = _layer(x, c_pad, w_ada[l], b_ada[l], norm1_pre[l], norm1_post[l], w_in[l], conv_w[l], conv_b[l],
                   dt_bias[l], a_log[l], d_skip[l], ssm_norm[l], rel_bias, attn_norm[l], w_out[l],
                   norm2_pre[l], norm2_post[l], w_mlp_in[l], w_mlp_out[l])
    return x
```

```python
import functools
import math

import numpy as np
import jax
import jax.numpy as jnp
from jax import lax
from jax.experimental import pallas as pl
from jax.experimental.pallas import tpu as pltpu

D_MODEL = 1024
SSM_HEADS = 8
SSM_HEAD_DIM = 64
D_SSM = SSM_HEADS * SSM_HEAD_DIM
SSM_GROUPS = 2
SSM_STATE = 128
CONV_WIDTH = 4
CHUNK = 256
ATTN_HEADS = 8
ATTN_KV_HEADS = 2
ATTN_HEAD_DIM = 64
D_ATTN = ATTN_HEADS * ATTN_HEAD_DIM
D_KV = ATTN_KV_HEADS * ATTN_HEAD_DIM
IDX_HEADS = 8
IDX_DIM = 64
TOPK_MAX = 256
N_BUCKETS = 32
MAX_DISTANCE = 128
D_FF = 4 * D_MODEL
D_BC = SSM_GROUPS * SSM_STATE
D_XBC = D_SSM + 2 * D_BC
EPS = 1e-6

LANES = 128
MISC_W = LANES
MISC_KIDX = 0
MISC_WIDX = IDX_DIM
MISC_DT = IDX_DIM + IDX_HEADS
OFF_Z = 0
OFF_XBC = OFF_Z + D_SSM
OFF_Q = OFF_XBC + D_XBC
OFF_K = OFF_Q + D_ATTN
OFF_V = OFF_K + D_KV
OFF_QI = OFF_V + D_KV
OFF_MISC = OFF_QI + IDX_HEADS * IDX_DIM
D_IN_PAD = OFF_MISC + MISC_W

BLK = 128
DSA_CHUNK = 512
V_ROWS = 80
DEN_LO = 2.0 ** -80
DEN_HI = 2.0 ** 80
F32_HUGE = 3.0e38
LOG2E = math.log2(math.e)
ROW_TILE = 512
NEG = -1e30
INT_MIN = -2 ** 31

F32 = jnp.float32
BF16 = jnp.bfloat16
HI = lax.Precision.HIGHEST
NT = (((1,), (1,)), ((), ()))


def _silu(x):
    return x / (1.0 + jnp.exp(-x))


def _rms(x, w):
    return x * lax.rsqrt(jnp.mean(x * x, axis=-1, keepdims=True) + EPS) * w


def _ada_kernel(c_ref, w_ref, b_ref, o_ref):
    s = _silu(c_ref[...])
    o_ref[...] = jnp.dot(s, w_ref[...], precision=HI, preferred_element_type=F32) + b_ref[...]


def _ada(c_pad, w, b):
    n = w.shape[1]
    tn = D_MODEL
    return pl.pallas_call(
        _ada_kernel,
        grid=(n // tn,),
        in_specs=[pl.BlockSpec((8, D_MODEL), lambda j: (0, 0)),
                  pl.BlockSpec((D_MODEL, tn), lambda j: (0, j)),
                  pl.BlockSpec((1, tn), lambda j: (0, j))],
        out_specs=pl.BlockSpec((8, tn), lambda j: (0, j)),
        out_shape=jax.ShapeDtypeStruct((8, n), F32),
        name="ada",
    )(c_pad, w, b)


def _inproj_kernel(x_ref, mod_ref, nw_ref, w_ref,
                   z_ref, xbc_ref, q_ref, k_ref, v_ref, qi_ref, misc_ref, miscb_ref):
    x = x_ref[0]
    h = _rms(x, nw_ref[...]) * (1.0 + mod_ref[0, 1:2, :]) + mod_ref[0, 0:1, :]
    hb = h.astype(BF16)

    def seg(lo, width):
        return jnp.dot(hb, w_ref[:, lo:lo + width], preferred_element_type=F32)

    z_ref[0] = seg(OFF_Z, D_SSM)
    xbc_ref[0] = seg(OFF_XBC, D_XBC)
    q_ref[0] = (seg(OFF_Q, D_ATTN) * (ATTN_HEAD_DIM ** -0.5 * LOG2E)).astype(BF16)
    k_ref[0] = seg(OFF_K, D_KV).astype(BF16)
    v_ref[0] = seg(OFF_V, D_KV).astype(BF16)
    qi_ref[0] = seg(OFF_QI, IDX_HEADS * IDX_DIM).astype(BF16)
    misc = seg(OFF_MISC, MISC_W)
    misc_ref[0] = misc
    miscb_ref[0] = misc.astype(BF16)


def _inproj(x, mod, nw, w_perm):
    bsz, t, d = x.shape
    tm = min(ROW_TILE, t)

    def tok(width, dtype):
        return (pl.BlockSpec((1, tm, width), lambda b, i: (b, i, 0)),
                jax.ShapeDtypeStruct((bsz, t, width), dtype))

    outs = [tok(D_SSM, F32), tok(D_XBC, F32), tok(D_ATTN, BF16), tok(D_KV, BF16), tok(D_KV, BF16),
            tok(IDX_HEADS * IDX_DIM, BF16), tok(MISC_W, F32), tok(MISC_W, BF16)]
    return pl.pallas_call(
        _inproj_kernel,
        grid=(bsz, t // tm),
        in_specs=[pl.BlockSpec((1, tm, d), lambda b, i: (b, i, 0)),
                  pl.BlockSpec((1, 6, d), lambda b, i: (b, 0, 0)),
                  pl.BlockSpec((1, d), lambda b, i: (0, 0)),
                  pl.BlockSpec((d, D_IN_PAD), lambda b, i: (0, 0))],
        out_specs=[o[0] for o in outs],
        out_shape=[o[1] for o in outs],
        compiler_params=pltpu.CompilerParams(
            dimension_semantics=("arbitrary", "arbitrary"), vmem_limit_bytes=48 * 2 ** 20),
        name="inproj",
    )(x, mod, nw, w_perm)


def _ssd_kernel(xbc_ref, z_ref, misc_ref, cw_ref, cb_ref, dtb_ref, a_ref, e_ref, dskip_ref, nw_ref,
                o_ref, xbuf, state):
    c = pl.program_id(1)
    lc = xbc_ref.shape[1]

    @pl.when(c == 0)
    def _():
        xbuf[0:8, :] = jnp.zeros((8, D_XBC), F32)
        state[...] = jnp.zeros(state.shape, F32)

    @pl.when(c > 0)
    def _():
        xbuf[0:8, :] = xbuf[lc:lc + 8, :]

    xbuf[8:lc + 8, :] = xbc_ref[0]
    u = cb_ref[...]
    for k in range(CONV_WIDTH):
        u = u + xbuf[pl.ds(8 - (CONV_WIDTH - 1) + k, lc), :] * cw_ref[k:k + 1, :]
    u = _silu(u)
    xs = u[:, :D_SSM]
    bm = u[:, D_SSM:D_SSM + D_BC]
    cm = u[:, D_SSM + D_BC:]

    raw = misc_ref[0] + dtb_ref[...]
    dt_t = jnp.maximum(raw, 0.0) + jnp.log1p(jnp.exp(-jnp.abs(raw)))
    adt_t = dt_t * a_ref[...]
    row = lax.broadcasted_iota(jnp.int32, (lc, lc), 0)
    col = lax.broadcasted_iota(jnp.int32, (lc, lc), 1)
    causal = col <= row
    tril = causal.astype(F32)
    acs_t = jnp.dot(tril, adt_t, precision=HI, preferred_element_type=F32)
    expand = e_ref[...]
    dt_e = jnp.dot(dt_t, expand, precision=HI, preferred_element_type=F32)
    acs_e = jnp.dot(acs_t, expand, precision=HI, preferred_element_type=F32)
    acs_row = jnp.transpose(acs_t)

    xdt = xs * dt_e
    a_last = acs_e[lc - 1:lc, :]
    xdec = (xdt * jnp.exp(a_last - acs_e)).astype(BF16)
    xdt_b = xdt.astype(BF16)
    bm_b = bm.astype(BF16)
    cm_b = cm.astype(BF16)
    bm_t = jnp.transpose(bm).astype(BF16)

    heads_per_group = SSM_HEADS // SSM_GROUPS
    gw = heads_per_group * SSM_HEAD_DIM
    y_parts = []
    off_parts = []
    for g in range(SSM_GROUPS):
        bg = bm_b[:, g * SSM_STATE:(g + 1) * SSM_STATE]
        cg = cm_b[:, g * SSM_STATE:(g + 1) * SSM_STATE]
        cb = lax.dot_general(cg, bg, NT, preferred_element_type=F32)
        for r in range(heads_per_group):
            hd = g * heads_per_group + r
            a_col = acs_t[:, MISC_DT + hd:MISC_DT + hd + 1]
            a_row = acs_row[MISC_DT + hd:MISC_DT + hd + 1, :]
            lmat = jnp.exp(jnp.where(causal, a_col - a_row, -jnp.inf))
            mh = (cb * lmat).astype(BF16)
            y_parts.append(jnp.dot(mh, xdt_b[:, hd * SSM_HEAD_DIM:(hd + 1) * SSM_HEAD_DIM],
                                   preferred_element_type=F32))
        prev = state[g]
        off_parts.append(jnp.dot(cg, prev.astype(BF16), preferred_element_type=F32))
        st = jnp.dot(bm_t[g * SSM_STATE:(g + 1) * SSM_STATE, :], xdec[:, g * gw:(g + 1) * gw],
                     preferred_element_type=F32)
        state[g] = prev * jnp.exp(a_last[:, g * gw:(g + 1) * gw]) + st
    y = (jnp.concatenate(y_parts, axis=1)
         + jnp.concatenate(off_parts, axis=1) * jnp.exp(acs_e)
         + xs * dskip_ref[...])
    o_ref[0] = _rms(y * _silu(z_ref[0]), nw_ref[...]).astype(BF16)


def _ssd(xbc, z, misc, cw, cb, dtb_t, a_t, expand, dskip_e, nw):
    bsz, t, _ = xbc.shape
    lc = math.gcd(t, CHUNK)
    nc = t // lc
    gw = (SSM_HEADS // SSM_GROUPS) * SSM_HEAD_DIM

    def const(shape):
        return pl.BlockSpec(shape, lambda b, c: tuple(0 for _ in shape))

    return pl.pallas_call(
        _ssd_kernel,
        grid=(bsz, nc),
        in_specs=[pl.BlockSpec((1, lc, D_XBC), lambda b, c: (b, c, 0)),
                  pl.BlockSpec((1, lc, D_SSM), lambda b, c: (b, c, 0)),
                  pl.BlockSpec((1, lc, MISC_W), lambda b, c: (b, c, 0)),
                  const((CONV_WIDTH, D_XBC)), const((1, D_XBC)), const((1, MISC_W)), const((1, MISC_W)),
                  const((MISC_W, D_SSM)), const((1, D_SSM)), const((1, D_SSM))],
        out_specs=pl.BlockSpec((1, lc, D_SSM), lambda b, c: (b, c, 0)),
        out_shape=jax.ShapeDtypeStruct((bsz, t, D_SSM), BF16),
        scratch_shapes=[pltpu.VMEM((lc + 8, D_XBC), F32),
                        pltpu.VMEM((SSM_GROUPS, SSM_STATE, gw), F32)],
        compiler_params=pltpu.CompilerParams(
            dimension_semantics=("arbitrary", "arbitrary"), vmem_limit_bytes=48 * 2 ** 20),
        name="ssd",
    )(xbc, z, misc, cw, cb, dtb_t, a_t, expand, dskip_e, nw)


def _bucket_tiles():
    s = np.arange(BLK)[:, None]
    t = np.arange(BLK)[None, :]
    max_exact = N_BUCKETS // 2
    tiles = []
    for d0 in (0, BLK):
        n = np.maximum(d0 + t - s, 0)
        large = {}
        for dt_ in (np.float32, np.float64):
            nf = np.maximum(n, 1).astype(dt_)
            lg = max_exact + (np.log(nf / dt_(max_exact)) / dt_(math.log(MAX_DISTANCE / max_exact))
                              * dt_(N_BUCKETS - max_exact)).astype(np.int32)
            large[dt_] = np.minimum(lg, N_BUCKETS - 1)
        assert (large[np.float32] == large[np.float64]).all()
        tiles.append(np.where(n < max_exact, n, large[np.float64]).astype(np.int32))
    assert BLK + 1 >= MAX_DISTANCE
    return np.stack(tiles)


def _to_key(x):
    bits = lax.bitcast_convert_type(x, jnp.int32)
    key = bits ^ (lax.shift_right_arithmetic(bits, 31) & jnp.int32(0x7FFFFFFF))
    return jnp.where(key == -1, 0, key)


def _transpose32(rows):
    rows = list(rows)
    mask, j = 0x0000FFFF, 16
    while j:
        k = 0
        while k < 32:
            t = (rows[k] ^ lax.shift_right_logical(rows[k + j], jnp.int32(j))) & jnp.int32(mask)
            rows[k] = rows[k] ^ t
            rows[k + j] = rows[k + j] ^ lax.shift_left(t, jnp.int32(j))
            k = (k + j + 1) & ~j
        j >>= 1
        mask ^= (mask << j) & 0xFFFFFFFF
    return rows


def _dsa_kernel(relb_ref, bkt_ref, q_ref, qi_ref, miscq_ref, k_ref, vt_ref, kidx_ref, nw_ref,
                o_ref, keys, planes, live, bias_s, qcat, qicat, wcat, m_s, acc_s, p_s, *, topk, idx_bits):
    b = pl.program_id(0)
    qi = pl.program_id(1)
    tq = BLK
    i32 = jnp.int32
    rep = ATTN_HEADS // ATTN_KV_HEADS
    gw = rep * tq
    ch = DSA_CHUNK
    ch_blks = ch // BLK

    @pl.when((b == 0) & (qi == 0))
    def _init_bias():
        for d in range(2):
            bk = bkt_ref[d]
            for h in range(ATTN_HEADS):
                far = relb_ref[N_BUCKETS - 1, h]
                acc = jnp.zeros((BLK, tq), F32)
                for n in range(N_BUCKETS - 1):
                    acc = jnp.where(bk == n, relb_ref[n, h] - far, acc)
                bias_s[d, h // rep, :, (h % rep) * BLK:(h % rep + 1) * BLK] = acc * LOG2E
        bias_s[2] = jnp.zeros((ATTN_KV_HEADS, BLK, gw), F32)
        for w in range(32):
            planes[w] = jnp.zeros(planes.shape[1:], i32)

    q_t = jnp.transpose(q_ref[0].astype(F32))
    for h in range(ATTN_HEADS):
        qcat[h // rep, :, (h % rep) * BLK:(h % rep + 1) * BLK] = (
            q_t[h * ATTN_HEAD_DIM:(h + 1) * ATTN_HEAD_DIM, :].astype(BF16))
    qi_t = jnp.transpose(qi_ref[0].astype(F32))
    w_t = jnp.transpose(miscq_ref[0])[MISC_WIDX:MISC_WIDX + IDX_HEADS, :] * (
        IDX_HEADS ** -0.5 * IDX_DIM ** -0.5)
    for h in range(IDX_HEADS):
        qicat[:, h * BLK:(h + 1) * BLK] = qi_t[h * IDX_DIM:(h + 1) * IDX_DIM, :].astype(BF16)
        wcat[:, h * BLK:(h + 1) * BLK] = jnp.broadcast_to(w_t[h:h + 1, :], (8, BLK))

    nch = (qi + ch_blks) // ch_blks
    row_c = lax.broadcasted_iota(i32, (ch, tq), 0)
    t_idx = qi * BLK + lax.broadcasted_iota(i32, (ch, tq), 1)

    def chunk_start(i):
        return pl.multiple_of(i * ch, ch)

    def score_chunk(i):
        s0 = chunk_start(i)
        kb = kidx_ref[0, pl.ds(s0, ch), MISC_KIDX:MISC_KIDX + IDX_DIM]
        s = jnp.dot(kb, qicat[...], preferred_element_type=F32)
        r = jnp.maximum(s, 0.0) * wcat[0:1, :]
        acc = r[:, 0:BLK]
        for h in range(1, IDX_HEADS):
            acc = acc + r[:, h * BLK:(h + 1) * BLK]
        acc = jnp.where(s0 + row_c <= t_idx, acc, -jnp.inf)
        keys[pl.ds(s0, ch), :] = _to_key(acc)

    def plane_chunk(i):
        ku = (keys[pl.ds(chunk_start(i), ch), :] ^ i32(INT_MIN)).reshape(ch // (32 * 8), 32, 8, tq)
        words = _transpose32([ku[:, j] for j in range(32)])
        r0 = pl.multiple_of(i * (ch // 32), ch // 32)
        for w in range(32):
            planes[w, pl.ds(r0, ch // 32), :] = words[w].reshape(ch // 32, tq)

    def idx_chunk(i, carry):
        plane_chunk(i - 1)
        score_chunk(i)
        return carry

    score_chunk(0)
    lax.fori_loop(1, nch, idx_chunk, 0)
    plane_chunk(nch - 1)

    nrows = live.shape[0]
    live[...] = jnp.where(lax.broadcasted_iota(i32, (nrows, tq), 0) < nch * (ch // 32), i32(-1), i32(0))

    def ones_in(words):
        return jnp.sum(lax.population_count(words), axis=0, keepdims=True)

    def bit_body(w, st):
        thr_u, n_above, n_set = st
        take = n_above + n_set >= topk
        thr_u = jnp.where(take, thr_u | lax.shift_left(i32(1), i32(31) - w), thr_u)
        n_above = jnp.where(take, n_above, n_above + n_set)
        still = live[...] & (planes[w] ^ jnp.where(take, i32(0), i32(-1)))
        live[...] = still
        return thr_u, n_above, ones_in(still & planes[jnp.minimum(w + 1, 31)])

    zero = jnp.zeros((1, tq), i32)
    thr_u, n_gt, _ = lax.fori_loop(0, 32, bit_body, (zero, zero, ones_in(live[...] & planes[0])))
    thr = thr_u ^ i32(INT_MIN)
    n_eq = ones_in(live[...])
    need = topk - n_gt

    def count(pred):
        def body(i, cnt):
            s0 = chunk_start(i)
            hit = pred(keys[pl.ds(s0, ch), :], s0).astype(i32)
            n = ch
            while n > 8:
                n //= 2
                hit = hit[:n] + hit[n:2 * n]
            return cnt + hit
        cnt = lax.fori_loop(0, nch, body, jnp.zeros((8, tq), i32))
        return jnp.sum(cnt, axis=0, keepdims=True)

    @pl.when(jnp.max((n_eq > need).astype(i32)) > 0)
    def _ties():
        def jbody(it, p):
            cand = p + lax.shift_left(i32(1), i32(idx_bits - 1) - it)
            c = count(lambda kk, s0: (kk == thr) & (s0 + row_c < cand))
            return jnp.where(c < need, cand, p)
        last = lax.fori_loop(0, idx_bits, jbody, jnp.zeros((1, tq), i32))

        def demote(i, carry):
            s0 = chunk_start(i)
            kk = keys[pl.ds(s0, ch), :]
            keys[pl.ds(s0, ch), :] = jnp.where((kk == thr) & (s0 + row_c > last), INT_MIN, kk)
            return carry
        lax.fori_loop(0, nch, demote, 0)

    def pen_of(s0, causal):
        sel = keys[pl.ds(s0, ch), :] >= thr
        if causal:
            sel = sel & (s0 + row_c <= t_idx)
        pen = jnp.where(sel, 0.0, NEG)
        return jnp.concatenate([pen] * rep, axis=1)

    def near_bias(i, g):
        return jnp.concatenate(
            [bias_s[jnp.clip(qi - (i * ch_blks + u), 0, 2), g] for u in range(ch_blks)], axis=0)

    def vt_chunk(i, g):
        return jnp.concatenate([vt_ref[0, i * ch_blks + u, g] for u in range(ch_blks)], axis=1)

    s0d = pl.multiple_of(qi * BLK, BLK)
    kd = k_ref[0, pl.ds(s0d, BLK), :]
    tri = lax.broadcasted_iota(i32, (BLK, tq), 0) <= lax.broadcasted_iota(i32, (BLK, tq), 1)
    tri_pen = jnp.concatenate([jnp.where(tri, 0.0, NEG)] * rep, axis=1)
    for g in range(ATTN_KV_HEADS):
        sd = jnp.dot(kd[:, g * ATTN_HEAD_DIM:(g + 1) * ATTN_HEAD_DIM], qcat[g],
                     preferred_element_type=F32) + bias_s[0, g] + tri_pen
        m_s[g] = jnp.max(sd, axis=0, keepdims=True)
        acc_s[g] = jnp.zeros((V_ROWS, gw), F32)
        p_s[1, g] = jnp.zeros((ch, gw), BF16)

    def weigh_values(i, slot):
        for g in range(ATTN_KV_HEADS):
            acc_s[g] += jnp.dot(vt_chunk(i, g), p_s[slot, g], preferred_element_type=F32)

    def att_chunk(i, near):
        weigh_values(jnp.maximum(i - 1, 0), (i + 1) % 2)
        s0 = chunk_start(i)
        pen = pen_of(s0, near)
        kblk = k_ref[0, pl.ds(s0, ch), :]
        for g in range(ATTN_KV_HEADS):
            s = jnp.dot(kblk[:, g * ATTN_HEAD_DIM:(g + 1) * ATTN_HEAD_DIM], qcat[g],
                        preferred_element_type=F32)
            if near:
                s = s + near_bias(i, g)
            p_s[i % 2, g] = jnp.exp2(s - m_s[g] + pen).astype(BF16)

    def far_body(i, carry):
        att_chunk(i, False)
        return carry

    def near_body(i, carry):
        att_chunk(i, True)
        return carry

    n_far = jnp.maximum(qi - 1, 0) // ch_blks
    lax.fori_loop(0, n_far, far_body, 0)
    lax.fori_loop(n_far, nch, near_body, 0)
    weigh_values(nch - 1, (nch - 1) % 2)

    n_ok = jnp.zeros((1, gw), i32)
    for g in range(ATTN_KV_HEADS):
        a = acc_s[g]
        den = a[ATTN_HEAD_DIM:ATTN_HEAD_DIM + 1, :]
        top = jnp.max(jnp.abs(a[0:ATTN_HEAD_DIM, :]), axis=0, keepdims=True)
        n_ok = n_ok + ((den > DEN_LO) & (den < DEN_HI) & (top < F32_HUGE)).astype(i32)

    @pl.when(jnp.min(n_ok) < ATTN_KV_HEADS)
    def _exact():
        for g in range(ATTN_KV_HEADS):
            m_s[g] = jnp.full((1, gw), NEG, F32)
            acc_s[g] = jnp.zeros((V_ROWS, gw), F32)

        def body(i, carry):
            s0 = chunk_start(i)
            pen = pen_of(s0, True)
            kblk = k_ref[0, pl.ds(s0, ch), :]
            for g in range(ATTN_KV_HEADS):
                s = jnp.dot(kblk[:, g * ATTN_HEAD_DIM:(g + 1) * ATTN_HEAD_DIM], qcat[g],
                            preferred_element_type=F32) + near_bias(i, g) + pen
                m_old = m_s[g]
                m_new = jnp.maximum(m_old, jnp.max(s, axis=0, keepdims=True))
                p = jnp.exp2(s - m_new).astype(BF16)
                acc_s[g] = acc_s[g] * jnp.exp2(m_old - m_new) + jnp.dot(
                    vt_chunk(i, g), p, preferred_element_type=F32)
                m_s[g] = m_new
            return carry
        lax.fori_loop(0, nch, body, 0)

    parts = []
    for g in range(ATTN_KV_HEADS):
        a = acc_s[g]
        o = a[0:ATTN_HEAD_DIM, :] * (1.0 / a[ATTN_HEAD_DIM:ATTN_HEAD_DIM + 1, :])
        parts += [o[:, r * tq:(r + 1) * tq] for r in range(rep)]
    out_t = jnp.concatenate(parts, axis=0)
    o_ref[0] = _rms(jnp.transpose(out_t), nw_ref[...]).astype(BF16)


def _dsa(rel_bias, q, qi, misc, k, vt, miscb, nw):
    bsz, t, _ = q.shape
    nq = t // BLK
    topk = min(TOPK_MAX, t // 4)
    idx_bits = max(1, (t - 1).bit_length())
    gw = (ATTN_HEADS // ATTN_KV_HEADS) * BLK
    bkt = jnp.asarray(_bucket_tiles())
    kern = functools.partial(_dsa_kernel, topk=topk, idx_bits=idx_bits)
    return pl.pallas_call(
        kern,
        grid=(bsz, nq),
        in_specs=[pl.BlockSpec(memory_space=pltpu.SMEM),
                  pl.BlockSpec((2, BLK, BLK), lambda b, i: (0, 0, 0)),
                  pl.BlockSpec((1, BLK, D_ATTN), lambda b, i: (b, i, 0)),
                  pl.BlockSpec((1, BLK, IDX_HEADS * IDX_DIM), lambda b, i: (b, i, 0)),
                  pl.BlockSpec((1, BLK, MISC_W), lambda b, i: (b, i, 0)),
                  pl.BlockSpec((1, t, D_KV), lambda b, i: (b, 0, 0)),
                  pl.BlockSpec((1, nq, ATTN_KV_HEADS, V_ROWS, BLK), lambda b, i: (b, 0, 0, 0, 0)),
                  pl.BlockSpec((1, t, MISC_W), lambda b, i: (b, 0, 0)),
                  pl.BlockSpec((1, D_ATTN), lambda b, i: (0, 0))],
        out_specs=pl.BlockSpec((1, BLK, D_ATTN), lambda b, i: (b, i, 0)),
        out_shape=jax.ShapeDtypeStruct((bsz, t, D_ATTN), BF16),
        scratch_shapes=[pltpu.VMEM((t, BLK), jnp.int32),
                        pltpu.VMEM((32, t // 32, BLK), jnp.int32),
                        pltpu.VMEM((t // 32, BLK), jnp.int32),
                        pltpu.VMEM((3, ATTN_KV_HEADS, BLK, gw), F32),
                        pltpu.VMEM((ATTN_KV_HEADS, ATTN_HEAD_DIM, gw), BF16),
                        pltpu.VMEM((IDX_DIM, IDX_HEADS * BLK), BF16),
                        pltpu.VMEM((8, IDX_HEADS * BLK), F32),
                        pltpu.VMEM((ATTN_KV_HEADS, 1, gw), F32),
                        pltpu.VMEM((ATTN_KV_HEADS, V_ROWS, gw), F32),
                        pltpu.VMEM((2, ATTN_KV_HEADS, DSA_CHUNK, gw), BF16)],
        compiler_params=pltpu.CompilerParams(
            dimension_semantics=("arbitrary", "arbitrary"), vmem_limit_bytes=48 * 2 ** 20),
        name="dsa",
    )(rel_bias, bkt, q, qi, misc, k, vt, miscb, nw)


def _tail_kernel(x_ref, ys_ref, ya_ref, mod_ref, wo_ref, n1_ref, n2_ref, n3_ref, w1_ref, w2_ref, o_ref):
    x = x_ref[0]
    mix = (jnp.dot(ys_ref[0], wo_ref[0:D_SSM, :], preferred_element_type=F32)
           + jnp.dot(ya_ref[0], wo_ref[D_SSM:, :], preferred_element_type=F32))
    x1 = x + mod_ref[0, 2:3, :] * _rms(mix, n1_ref[...])
    h = (_rms(x1, n2_ref[...]) * (1.0 + mod_ref[0, 4:5, :]) + mod_ref[0, 3:4, :]).astype(BF16)
    f = jnp.zeros(x.shape, F32)
    step = D_MODEL
    for c0 in range(0, D_FF, step):
        a = jnp.maximum(jnp.dot(h, w1_ref[:, c0:c0 + step], preferred_element_type=F32), 0.0)
        f = f + jnp.dot((a * a).astype(BF16), w2_ref[c0:c0 + step, :], preferred_element_type=F32)
    o_ref[0] = x1 + mod_ref[0, 5:6, :] * _rms(f, n3_ref[...])


def _tail(x, ys, ya, mod, wo, n1, n2, n3, w1, w2):
    bsz, t, d = x.shape
    tm = min(ROW_TILE, t)

    def const(shape):
        return pl.BlockSpec(shape, lambda b, i: tuple(0 for _ in shape))

    return pl.pallas_call(
        _tail_kernel,
        grid=(bsz, t // tm),
        in_specs=[pl.BlockSpec((1, tm, d), lambda b, i: (b, i, 0)),
                  pl.BlockSpec((1, tm, D_SSM), lambda b, i: (b, i, 0)),
                  pl.BlockSpec((1, tm, D_ATTN), lambda b, i: (b, i, 0)),
                  pl.BlockSpec((1, 6, d), lambda b, i: (b, 0, 0)),
                  const((D_SSM + D_ATTN, d)), const((1, d)), const((1, d)), const((1, d)),
                  const((d, D_FF)), const((D_FF, d))],
        out_specs=pl.BlockSpec((1, tm, d), lambda b, i: (b, i, 0)),
        out_shape=jax.ShapeDtypeStruct((bsz, t, d), F32),
        compiler_params=pltpu.CompilerParams(
            dimension_semantics=("arbitrary", "arbitrary"), vmem_limit_bytes=60 * 2 ** 20),
        name="tail",
    )(x, ys, ya, mod, wo, n1, n2, n3, w1, w2)


def _permute_w_in(w):
    sizes = [D_SSM, D_XBC, SSM_HEADS, D_ATTN, D_KV, D_KV, IDX_HEADS * IDX_DIM, IDX_DIM, IDX_HEADS]
    pts = np.cumsum(sizes)[:-1].tolist()
    z, xbc, dt, q, k, v, qidx, kidx, widx = jnp.split(w, pts, axis=-1)
    pad = jnp.zeros((w.shape[0], MISC_W - IDX_DIM - IDX_HEADS - SSM_HEADS), w.dtype)
    return jnp.concatenate([z, xbc, q, k, v, qidx, kidx, widx, dt, pad], axis=-1).astype(BF16)


def _layer(x, c_pad, w_ada, b_ada, norm1_pre, norm1_post, w_in, conv_w, conv_b, dt_bias, a_log, d_skip,
           ssm_norm, rel_bias, attn_norm, w_out, norm2_pre, norm2_post, w_mlp_in, w_mlp_out):
    bsz, t, d = x.shape
    mod = _ada(c_pad, w_ada, b_ada[None, :])[:bsz].reshape(bsz, 6, d)
    z, xbc, q, k, v, qi, misc, miscb = _inproj(x, mod, norm1_pre[None, :], _permute_w_in(w_in))

    lane = jnp.arange(MISC_W)
    is_dt = (lane >= MISC_DT) & (lane < MISC_DT + SSM_HEADS)
    head_of = jnp.clip(lane - MISC_DT, 0, SSM_HEADS - 1)
    dtb_t = jnp.where(is_dt, dt_bias[head_of], 0.0)[None, :]
    a_t = jnp.where(is_dt, -jnp.exp(a_log.astype(F32))[head_of], 0.0)[None, :]
    expand = ((lane[:, None] - MISC_DT) == (jnp.arange(D_SSM)[None, :] // SSM_HEAD_DIM)).astype(F32)
    dskip_e = jnp.repeat(d_skip.astype(F32), SSM_HEAD_DIM)[None, :]
    y_ssd = _ssd(xbc, z, misc, conv_w, conv_b[None, :], dtb_t, a_t, expand, dskip_e, ssm_norm[None, :])

    nq = t // BLK
    v4 = v.reshape(bsz, t, ATTN_KV_HEADS, ATTN_HEAD_DIM)
    v4 = jnp.concatenate([v4, jnp.ones((bsz, t, ATTN_KV_HEADS, 1), BF16),
                          jnp.zeros((bsz, t, ATTN_KV_HEADS, V_ROWS - ATTN_HEAD_DIM - 1), BF16)], axis=-1)
    vt = jnp.transpose(v4.reshape(bsz, nq, BLK, ATTN_KV_HEADS, V_ROWS), (0, 1, 3, 4, 2))
    y_att = _dsa(rel_bias, q, qi, misc, k, vt, miscb, attn_norm[None, :])

    return _tail(x, y_ssd, y_att, mod, w_out.astype(BF16), norm1_post[None, :], norm2_pre[None, :],
                 norm2_post[None, :], w_mlp_in.astype(BF16), w_mlp_out.astype(BF16))


def kernel(x, c, w_ada, b_ada, norm1_pre, norm1_post, w_in, conv_w, conv_b, dt_bias, a_log, d_skip,
           ssm_norm, rel_bias, attn_norm, w_out, norm2_pre, norm2_post, w_mlp_in, w_mlp_out):
    bsz = x.shape[0]
    assert bsz <= 8 and x.shape[1] % DSA_CHUNK == 0 and x.shape[2] == D_MODEL
    c_pad = jnp.zeros((8, D_MODEL), F32).at[:bsz].set(c)
    for l in range(w_ada.shape[0]):
        x = _layer(x, c_pad, w_ada[l], b_ada[l], norm1_pre[l], norm1_post[l], w_in[l], conv_w[l], conv_b[l],
                   dt_bias[l], a_log[l], d_skip[l], ssm_norm[l], rel_bias, attn_norm[l], w_out[l],
                   norm2_pre[l], norm2_post[l], w_mlp_in[l], w_mlp_out[l])
    return x
```

```python
import functools
import math

import numpy as np
import jax
import jax.numpy as jnp
from jax import lax
from jax.experimental import pallas as pl
from jax.experimental.pallas import tpu as pltpu

D_MODEL = 1024
SSM_HEADS = 8
SSM_HEAD_DIM = 64
D_SSM = SSM_HEADS * SSM_HEAD_DIM
SSM_GROUPS = 2
SSM_STATE = 128
CONV_WIDTH = 4
CHUNK = 256
ATTN_HEADS = 8
ATTN_KV_HEADS = 2
ATTN_HEAD_DIM = 64
D_ATTN = ATTN_HEADS * ATTN_HEAD_DIM
D_KV = ATTN_KV_HEADS * ATTN_HEAD_DIM
IDX_HEADS = 8
IDX_DIM = 64
TOPK_MAX = 256
N_BUCKETS = 32
MAX_DISTANCE = 128
D_FF = 4 * D_MODEL
D_BC = SSM_GROUPS * SSM_STATE
D_XBC = D_SSM + 2 * D_BC
EPS = 1e-6

LANES = 128
MISC_W = LANES
MISC_KIDX = 0
MISC_WIDX = IDX_DIM
MISC_DT = IDX_DIM + IDX_HEADS
K_AUG = LANES
D_K_AUG = ATTN_KV_HEADS * K_AUG
SHIFT_ROWS = 16
OFF_Z = 0
OFF_XBC = OFF_Z + D_SSM
OFF_Q = OFF_XBC + D_XBC
OFF_K = OFF_Q + D_ATTN
OFF_V = OFF_K + D_K_AUG
OFF_QI = OFF_V + D_KV
OFF_MISC = OFF_QI + IDX_HEADS * IDX_DIM
D_IN_PAD = OFF_MISC + MISC_W

BLK = 128
DSA_CHUNK = 512
V_ROWS = 80
DEN_LO = 2.0 ** -80
DEN_HI = 2.0 ** 80
F32_HUGE = 3.0e38
LOG2E = math.log2(math.e)
ROW_TILE = 512
NEG = -1e30
INT_MIN = -2 ** 31

F32 = jnp.float32
BF16 = jnp.bfloat16
HI = lax.Precision.HIGHEST
NT = (((1,), (1,)), ((), ()))


def _silu(x):
    return x / (1.0 + jnp.exp(-x))


def _rms(x, w):
    return x * lax.rsqrt(jnp.mean(x * x, axis=-1, keepdims=True) + EPS) * w


def _ada_kernel(c_ref, w_ref, b_ref, o_ref):
    s = _silu(c_ref[...])
    o_ref[...] = jnp.dot(s, w_ref[...], precision=HI, preferred_element_type=F32) + b_ref[...]


def _ada(c_pad, w, b):
    n = w.shape[1]
    tn = D_MODEL
    return pl.pallas_call(
        _ada_kernel,
        grid=(n // tn,),
        in_specs=[pl.BlockSpec((8, D_MODEL), lambda j: (0, 0)),
                  pl.BlockSpec((D_MODEL, tn), lambda j: (0, j)),
                  pl.BlockSpec((1, tn), lambda j: (0, j))],
        out_specs=pl.BlockSpec((8, tn), lambda j: (0, j)),
        out_shape=jax.ShapeDtypeStruct((8, n), F32),
        name="ada",
    )(c_pad, w, b)


def _inproj_kernel(x_ref, mod_ref, nw_ref, w_ref,
                   z_ref, xbc_ref, q_ref, k_ref, v_ref, qi_ref, misc_ref, miscb_ref):
    x = x_ref[0]
    h = _rms(x, nw_ref[...]) * (1.0 + mod_ref[0, 1:2, :]) + mod_ref[0, 0:1, :]
    hb = h.astype(BF16)

    def seg(lo, width):
        return jnp.dot(hb, w_ref[:, lo:lo + width], preferred_element_type=F32)

    z_ref[0] = seg(OFF_Z, D_SSM)
    xbc_ref[0] = seg(OFF_XBC, D_XBC)
    q_ref[0] = (seg(OFF_Q, D_ATTN) * (ATTN_HEAD_DIM ** -0.5 * LOG2E)).astype(BF16)
    ones_col = lax.broadcasted_iota(jnp.int32, (1, D_K_AUG), 1) % K_AUG == ATTN_HEAD_DIM
    k_ref[0] = jnp.where(ones_col, 1.0, seg(OFF_K, D_K_AUG)).astype(BF16)
    v_ref[0] = seg(OFF_V, D_KV).astype(BF16)
    qi_ref[0] = seg(OFF_QI, IDX_HEADS * IDX_DIM).astype(BF16)
    misc = seg(OFF_MISC, MISC_W)
    misc_ref[0] = misc
    miscb_ref[0] = misc.astype(BF16)


def _inproj(x, mod, nw, w_perm):
    bsz, t, d = x.shape
    tm = min(ROW_TILE, t)

    def tok(width, dtype):
        return (pl.BlockSpec((1, tm, width), lambda b, i: (b, i, 0)),
                jax.ShapeDtypeStruct((bsz, t, width), dtype))

    outs = [tok(D_SSM, F32), tok(D_XBC, F32), tok(D_ATTN, BF16), tok(D_K_AUG, BF16), tok(D_KV, BF16),
            tok(IDX_HEADS * IDX_DIM, BF16), tok(MISC_W, F32), tok(MISC_W, BF16)]
    return pl.pallas_call(
        _inproj_kernel,
        grid=(bsz, t // tm),
        in_specs=[pl.BlockSpec((1, tm, d), lambda b, i: (b, i, 0)),
                  pl.BlockSpec((1, 6, d), lambda b, i: (b, 0, 0)),
                  pl.BlockSpec((1, d), lambda b, i: (0, 0)),
                  pl.BlockSpec((d, D_IN_PAD), lambda b, i: (0, 0))],
        out_specs=[o[0] for o in outs],
        out_shape=[o[1] for o in outs],
        compiler_params=pltpu.CompilerParams(
            dimension_semantics=("arbitrary", "arbitrary"), vmem_limit_bytes=48 * 2 ** 20),
        name="inproj",
    )(x, mod, nw, w_perm)


def _ssd_kernel(xbc_ref, z_ref, misc_ref, cw_ref, cb_ref, dtb_ref, a_ref, e_ref, dskip_ref, nw_ref,
                o_ref, xbuf, state):
    c = pl.program_id(1)
    lc = xbc_ref.shape[1]

    @pl.when(c == 0)
    def _():
        xbuf[0:8, :] = jnp.zeros((8, D_XBC), F32)
        state[...] = jnp.zeros(state.shape, F32)

    @pl.when(c > 0)
    def _():
        xbuf[0:8, :] = xbuf[lc:lc + 8, :]

    xbuf[8:lc + 8, :] = xbc_ref[0]
    u = cb_ref[...]
    for k in range(CONV_WIDTH):
        u = u + xbuf[pl.ds(8 - (CONV_WIDTH - 1) + k, lc), :] * cw_ref[k:k + 1, :]
    u = _silu(u)
    xs = u[:, :D_SSM]
    bm = u[:, D_SSM:D_SSM + D_BC]
    cm = u[:, D_SSM + D_BC:]

    raw = misc_ref[0] + dtb_ref[...]
    dt_t = jnp.maximum(raw, 0.0) + jnp.log1p(jnp.exp(-jnp.abs(raw)))
    adt_t = dt_t * a_ref[...]
    row = lax.broadcasted_iota(jnp.int32, (lc, lc), 0)
    col = lax.broadcasted_iota(jnp.int32, (lc, lc), 1)
    causal = col <= row
    tril = causal.astype(F32)
    acs_t = jnp.dot(tril, adt_t, precision=HI, preferred_element_type=F32)
    expand = e_ref[...]
    dt_e = jnp.dot(dt_t, expand, precision=HI, preferred_element_type=F32)
    acs_e = jnp.dot(acs_t, expand, precision=HI, preferred_element_type=F32)
    acs_row = jnp.transpose(acs_t)

    xdt = xs * dt_e
    a_last = acs_e[lc - 1:lc, :]
    xdec = (xdt * jnp.exp(a_last - acs_e)).astype(BF16)
    xdt_b = xdt.astype(BF16)
    bm_b = bm.astype(BF16)
    cm_b = cm.astype(BF16)
    bm_t = jnp.transpose(bm).astype(BF16)

    heads_per_group = SSM_HEADS // SSM_GROUPS
    gw = heads_per_group * SSM_HEAD_DIM
    y_parts = []
    off_parts = []
    for g in range(SSM_GROUPS):
        bg = bm_b[:, g * SSM_STATE:(g + 1) * SSM_STATE]
        cg = cm_b[:, g * SSM_STATE:(g + 1) * SSM_STATE]
        cb = lax.dot_general(cg, bg, NT, preferred_element_type=F32)
        for r in range(heads_per_group):
            hd = g * heads_per_group + r
            a_col = acs_t[:, MISC_DT + hd:MISC_DT + hd + 1]
            a_row = acs_row[MISC_DT + hd:MISC_DT + hd + 1, :]
            lmat = jnp.exp(jnp.where(causal, a_col - a_row, -jnp.inf))
            mh = (cb * lmat).astype(BF16)
            y_parts.append(jnp.dot(mh, xdt_b[:, hd * SSM_HEAD_DIM:(hd + 1) * SSM_HEAD_DIM],
                                   preferred_element_type=F32))
        prev = state[g]
        off_parts.append(jnp.dot(cg, prev.astype(BF16), preferred_element_type=F32))
        st = jnp.dot(bm_t[g * SSM_STATE:(g + 1) * SSM_STATE, :], xdec[:, g * gw:(g + 1) * gw],
                     preferred_element_type=F32)
        state[g] = prev * jnp.exp(a_last[:, g * gw:(g + 1) * gw]) + st
    y = (jnp.concatenate(y_parts, axis=1)
         + jnp.concatenate(off_parts, axis=1) * jnp.exp(acs_e)
         + xs * dskip_ref[...])
    o_ref[0] = _rms(y * _silu(z_ref[0]), nw_ref[...]).astype(BF16)


def _ssd(xbc, z, misc, cw, cb, dtb_t, a_t, expand, dskip_e, nw):
    bsz, t, _ = xbc.shape
    lc = math.gcd(t, CHUNK)
    nc = t // lc
    gw = (SSM_HEADS // SSM_GROUPS) * SSM_HEAD_DIM

    def const(shape):
        return pl.BlockSpec(shape, lambda b, c: tuple(0 for _ in shape))

    return pl.pallas_call(
        _ssd_kernel,
        grid=(bsz, nc),
        in_specs=[pl.BlockSpec((1, lc, D_XBC), lambda b, c: (b, c, 0)),
                  pl.BlockSpec((1, lc, D_SSM), lambda b, c: (b, c, 0)),
                  pl.BlockSpec((1, lc, MISC_W), lambda b, c: (b, c, 0)),
                  const((CONV_WIDTH, D_XBC)), const((1, D_XBC)), const((1, MISC_W)), const((1, MISC_W)),
                  const((MISC_W, D_SSM)), const((1, D_SSM)), const((1, D_SSM))],
        out_specs=pl.BlockSpec((1, lc, D_SSM), lambda b, c: (b, c, 0)),
        out_shape=jax.ShapeDtypeStruct((bsz, t, D_SSM), BF16),
        scratch_shapes=[pltpu.VMEM((lc + 8, D_XBC), F32),
                        pltpu.VMEM((SSM_GROUPS, SSM_STATE, gw), F32)],
        compiler_params=pltpu.CompilerParams(
            dimension_semantics=("arbitrary", "arbitrary"), vmem_limit_bytes=48 * 2 ** 20),
        name="ssd",
    )(xbc, z, misc, cw, cb, dtb_t, a_t, expand, dskip_e, nw)


def _bucket_tiles():
    s = np.arange(BLK)[:, None]
    t = np.arange(BLK)[None, :]
    max_exact = N_BUCKETS // 2
    tiles = []
    for d0 in (0, BLK):
        n = np.maximum(d0 + t - s, 0)
        large = {}
        for dt_ in (np.float32, np.float64):
            nf = np.maximum(n, 1).astype(dt_)
            lg = max_exact + (np.log(nf / dt_(max_exact)) / dt_(math.log(MAX_DISTANCE / max_exact))
                              * dt_(N_BUCKETS - max_exact)).astype(np.int32)
            large[dt_] = np.minimum(lg, N_BUCKETS - 1)
        assert (large[np.float32] == large[np.float64]).all()
        tiles.append(np.where(n < max_exact, n, large[np.float64]).astype(np.int32))
    assert BLK + 1 >= MAX_DISTANCE
    return np.stack(tiles)


def _to_key(x):
    bits = lax.bitcast_convert_type(x, jnp.int32)
    key = bits ^ (lax.shift_right_arithmetic(bits, 31) & jnp.int32(0x7FFFFFFF))
    return jnp.where(key == -1, 0, key)


def _transpose32(rows):
    rows = list(rows)
    mask, j = 0x0000FFFF, 16
    while j:
        k = 0
        while k < 32:
            t = (rows[k] ^ lax.shift_right_logical(rows[k + j], jnp.int32(j))) & jnp.int32(mask)
            rows[k] = rows[k] ^ t
            rows[k + j] = rows[k + j] ^ lax.shift_left(t, jnp.int32(j))
            k = (k + j + 1) & ~j
        j >>= 1
        mask ^= (mask << j) & 0xFFFFFFFF
    return rows


def _dsa_kernel(relb_ref, bkt_ref, q_ref, qic_ref, miscc_ref, qin_ref, miscn_ref, k_ref, vt_ref, kidx_ref,
                nw_ref, o_ref, keys, planes, live, thr_s, bias_s, qcat, qicat, wcat, m_s, acc_s, p_s,
                *, topk, idx_bits, nq):
    b = pl.program_id(0)
    qi = pl.program_id(1)
    tq = BLK
    i32 = jnp.int32
    rep = ATTN_HEADS // ATTN_KV_HEADS
    gw = rep * tq
    ch = DSA_CHUNK
    ch_blks = ch // BLK
    rows_per_chunk = ch // 32

    @pl.when((b == 0) & (qi == 0))
    def _init_bias():
        for d in range(2):
            bk = bkt_ref[d]
            for h in range(ATTN_HEADS):
                far = relb_ref[N_BUCKETS - 1, h]
                acc = jnp.zeros((BLK, tq), F32)
                for n in range(N_BUCKETS - 1):
                    acc = jnp.where(bk == n, relb_ref[n, h] - far, acc)
                bias_s[d, h // rep, :, (h % rep) * BLK:(h % rep + 1) * BLK] = acc * LOG2E
        bias_s[2] = jnp.zeros((ATTN_KV_HEADS, BLK, gw), F32)
        for w in range(32):
            planes[w] = jnp.zeros(planes.shape[1:], i32)

    row_c = lax.broadcasted_iota(i32, (ch, tq), 0)
    lane_c = lax.broadcasted_iota(i32, (ch, tq), 1)

    def chunk_start(c):
        return pl.multiple_of(c * ch, ch)

    def n_chunks(blk):
        return (blk + ch_blks) // ch_blks

    def load_indexer(qidx_ref, misc_ref):
        qi_t = jnp.transpose(qidx_ref[0].astype(F32))
        w_t = jnp.transpose(misc_ref[0])[MISC_WIDX:MISC_WIDX + IDX_HEADS, :] * (
            IDX_HEADS ** -0.5 * IDX_DIM ** -0.5)
        for h in range(IDX_HEADS):
            qicat[:, h * BLK:(h + 1) * BLK] = qi_t[h * IDX_DIM:(h + 1) * IDX_DIM, :].astype(BF16)
            wcat[:, h * BLK:(h + 1) * BLK] = jnp.broadcast_to(w_t[h:h + 1, :], (8, BLK))

    def score_chunk(c, blk, slot):
        s0 = chunk_start(c)
        kb = kidx_ref[0, pl.ds(s0, ch), MISC_KIDX:MISC_KIDX + IDX_DIM]
        s = jnp.dot(kb, qicat[...], preferred_element_type=F32)
        r = jnp.maximum(s, 0.0) * wcat[0:1, :]
        acc = r[:, 0:BLK]
        for h in range(1, IDX_HEADS):
            acc = acc + r[:, h * BLK:(h + 1) * BLK]
        acc = jnp.where(s0 + row_c <= blk * BLK + lane_c, acc, -jnp.inf)
        keys[slot, pl.ds(s0, ch), :] = _to_key(acc)

    def plane_chunk(c, slot):
        ku = (keys[slot, pl.ds(chunk_start(c), ch), :] ^ i32(INT_MIN)).reshape(ch // (32 * 8), 32, 8, tq)
        words = _transpose32([ku[:, j] for j in range(32)])
        r0 = pl.multiple_of(c * rows_per_chunk, rows_per_chunk)
        for w in range(32):
            planes[w, pl.ds(r0, rows_per_chunk), :] = words[w].reshape(rows_per_chunk, tq)

    def select(blk, slot):
        nch = n_chunks(blk)
        nrows = live.shape[0]
        live[...] = jnp.where(lax.broadcasted_iota(i32, (nrows, tq), 0) < nch * rows_per_chunk, i32(-1), i32(0))

        def ones_in(words):
            return jnp.sum(lax.population_count(words), axis=0, keepdims=True)

        def bit_body(w, st):
            thr_u, n_above, n_set = st
            take = n_above + n_set >= topk
            thr_u = jnp.where(take, thr_u | lax.shift_left(i32(1), i32(31) - w), thr_u)
            n_above = jnp.where(take, n_above, n_above + n_set)
            still = live[...] & (planes[w] ^ jnp.where(take, i32(0), i32(-1)))
            live[...] = still
            return thr_u, n_above, ones_in(still & planes[jnp.minimum(w + 1, 31)])

        zero = jnp.zeros((1, tq), i32)
        thr_u, n_gt, _ = lax.fori_loop(0, 32, bit_body, (zero, zero, ones_in(live[...] & planes[0])))
        thr = thr_u ^ i32(INT_MIN)
        thr_s[slot] = thr
        n_eq = ones_in(live[...])
        need = topk - n_gt

        @pl.when(jnp.max((n_eq > need).astype(i32)) > 0)
        def _ties():
            def count(pred):
                def body(c, cnt):
                    s0 = chunk_start(c)
                    hit = pred(keys[slot, pl.ds(s0, ch), :], s0).astype(i32)
                    n = ch
                    while n > 8:
                        n //= 2
                        hit = hit[:n] + hit[n:2 * n]
                    return cnt + hit
                cnt = lax.fori_loop(0, nch, body, jnp.zeros((8, tq), i32))
                return jnp.sum(cnt, axis=0, keepdims=True)

            def jbody(it, p):
                cand = p + lax.shift_left(i32(1), i32(idx_bits - 1) - it)
                c = count(lambda kk, s0: (kk == thr) & (s0 + row_c < cand))
                return jnp.where(c < need, cand, p)
            last = lax.fori_loop(0, idx_bits, jbody, jnp.zeros((1, tq), i32))

            def demote(c, carry):
                s0 = chunk_start(c)
                kk = keys[slot, pl.ds(s0, ch), :]
                keys[slot, pl.ds(s0, ch), :] = jnp.where((kk == thr) & (s0 + row_c > last), INT_MIN, kk)
                return carry
            lax.fori_loop(0, nch, demote, 0)

    @pl.when(qi == 0)
    def _first_block():
        load_indexer(qic_ref, miscc_ref)
        score_chunk(0, 0, 0)
        plane_chunk(0, 0)
        select(0, 0)

    slot_b = qi % 2
    slot_a = 1 - slot_b
    blk_a = jnp.minimum(qi + 1, nq - 1)
    nch_b = n_chunks(qi)
    nch_a = n_chunks(blk_a)
    load_indexer(qin_ref, miscn_ref)
    thr = thr_s[slot_b]

    q_t = jnp.transpose(q_ref[0].astype(F32))
    for h in range(ATTN_HEADS):
        qcat[h // rep, 0:ATTN_HEAD_DIM, (h % rep) * BLK:(h % rep + 1) * BLK] = (
            q_t[h * ATTN_HEAD_DIM:(h + 1) * ATTN_HEAD_DIM, :].astype(BF16))
    for g in range(ATTN_KV_HEADS):
        qcat[g, ATTN_HEAD_DIM:, :] = jnp.zeros((K_AUG - ATTN_HEAD_DIM, gw), BF16)

    def pen_of(s0, causal):
        sel = keys[slot_b, pl.ds(s0, ch), :] >= thr
        if causal:
            sel = sel & (s0 + row_c <= qi * BLK + lane_c)
        pen = jnp.where(sel, 0.0, NEG)
        return jnp.concatenate([pen] * rep, axis=1)

    def near_bias(c, g):
        return jnp.concatenate(
            [bias_s[jnp.clip(qi - (c * ch_blks + u), 0, 2), g] for u in range(ch_blks)], axis=0)

    def vt_chunk(c, g):
        return jnp.concatenate([vt_ref[0, c * ch_blks + u, g] for u in range(ch_blks)], axis=1)

    s0d = pl.multiple_of(qi * BLK, BLK)
    kd = k_ref[0, pl.ds(s0d, BLK), :]
    tri = lax.broadcasted_iota(i32, (BLK, tq), 0) <= lax.broadcasted_iota(i32, (BLK, tq), 1)
    tri_pen = jnp.concatenate([jnp.where(tri, 0.0, NEG)] * rep, axis=1)
    for g in range(ATTN_KV_HEADS):
        sd = jnp.dot(kd[:, g * K_AUG:(g + 1) * K_AUG], qcat[g],
                     preferred_element_type=F32) + bias_s[0, g] + tri_pen
        shift = jnp.max(sd, axis=0, keepdims=True)
        first = lax.broadcasted_iota(i32, (SHIFT_ROWS, gw), 0) == 0
        qcat[g, ATTN_HEAD_DIM:ATTN_HEAD_DIM + SHIFT_ROWS, :] = jnp.where(first, -shift, 0.0).astype(BF16)
        acc_s[g] = jnp.zeros((V_ROWS, gw), F32)
        p_s[1, g] = jnp.zeros((ch, gw), BF16)

    def weigh_values(c, slot):
        for g in range(ATTN_KV_HEADS):
            acc_s[g] += jnp.dot(vt_chunk(c, g), p_s[slot, g], preferred_element_type=F32)

    def att_chunk(c, near):
        weigh_values(jnp.maximum(c - 1, 0), (c + 1) % 2)
        s0 = chunk_start(c)
        pen = pen_of(s0, near)
        kblk = k_ref[0, pl.ds(s0, ch), :]
        for g in range(ATTN_KV_HEADS):
            s = jnp.dot(kblk[:, g * K_AUG:(g + 1) * K_AUG], qcat[g],
                        preferred_element_type=F32)
            if near:
                s = s + near_bias(c, g)
            p_s[c % 2, g] = jnp.exp2(s + pen).astype(BF16)

    def fused(c, near):
        plane_chunk(c, slot_a)
        att_chunk(c, near)
        score_chunk(c + 1, blk_a, slot_a)

    def far_body(c, carry):
        fused(c, False)
        return carry

    def near_body(c, carry):
        fused(c, True)
        return carry

    score_chunk(0, blk_a, slot_a)
    n_far = jnp.maximum(qi - 1, 0) // ch_blks
    lax.fori_loop(0, n_far, far_body, 0)
    lax.fori_loop(n_far, nch_b - 1, near_body, 0)
    plane_chunk(nch_b - 1, slot_a)
    att_chunk(nch_b - 1, True)
    weigh_values(nch_b - 1, (nch_b - 1) % 2)

    @pl.when(nch_a > nch_b)
    def _one_more_chunk():
        score_chunk(nch_b, blk_a, slot_a)
        plane_chunk(nch_b, slot_a)

    select(blk_a, slot_a)

    n_ok = jnp.zeros((1, gw), i32)
    for g in range(ATTN_KV_HEADS):
        a = acc_s[g]
        den = a[ATTN_HEAD_DIM:ATTN_HEAD_DIM + 1, :]
        top = jnp.max(jnp.abs(a[0:ATTN_HEAD_DIM, :]), axis=0, keepdims=True)
        n_ok = n_ok + ((den > DEN_LO) & (den < DEN_HI) & (top < F32_HUGE)).astype(i32)

    @pl.when(jnp.min(n_ok) < ATTN_KV_HEADS)
    def _exact():
        for g in range(ATTN_KV_HEADS):
            qcat[g, ATTN_HEAD_DIM:ATTN_HEAD_DIM + SHIFT_ROWS, :] = jnp.zeros((SHIFT_ROWS, gw), BF16)
            m_s[g] = jnp.full((1, gw), NEG, F32)
            acc_s[g] = jnp.zeros((V_ROWS, gw), F32)

        def body(c, carry):
            s0 = chunk_start(c)
            pen = pen_of(s0, True)
            kblk = k_ref[0, pl.ds(s0, ch), :]
            for g in range(ATTN_KV_HEADS):
                s = jnp.dot(kblk[:, g * K_AUG:(g + 1) * K_AUG], qcat[g],
                            preferred_element_type=F32) + near_bias(c, g) + pen
                m_old = m_s[g]
                m_new = jnp.maximum(m_old, jnp.max(s, axis=0, keepdims=True))
                p = jnp.exp2(s - m_new).astype(BF16)
                acc_s[g] = acc_s[g] * jnp.exp2(m_old - m_new) + jnp.dot(
                    vt_chunk(c, g), p, preferred_element_type=F32)
                m_s[g] = m_new
            return carry
        lax.fori_loop(0, nch_b, body, 0)

    parts = []
    for g in range(ATTN_KV_HEADS):
        a = acc_s[g]
        o = a[0:ATTN_HEAD_DIM, :] * (1.0 / a[ATTN_HEAD_DIM:ATTN_HEAD_DIM + 1, :])
        parts += [o[:, r * tq:(r + 1) * tq] for r in range(rep)]
    out_t = jnp.concatenate(parts, axis=0)
    o_ref[0] = _rms(jnp.transpose(out_t), nw_ref[...]).astype(BF16)


def _dsa(rel_bias, q, qi, misc, k, vt, miscb, nw):
    bsz, t, _ = q.shape
    nq = t // BLK
    topk = min(TOPK_MAX, t // 4)
    idx_bits = max(1, (t - 1).bit_length())
    gw = (ATTN_HEADS // ATTN_KV_HEADS) * BLK
    bkt = jnp.asarray(_bucket_tiles())
    kern = functools.partial(_dsa_kernel, topk=topk, idx_bits=idx_bits, nq=nq)

    def nxt(b, i):
        return (b, jnp.minimum(i + 1, nq - 1), 0)

    return pl.pallas_call(
        kern,
        grid=(bsz, nq),
        in_specs=[pl.BlockSpec(memory_space=pltpu.SMEM),
                  pl.BlockSpec((2, BLK, BLK), lambda b, i: (0, 0, 0)),
                  pl.BlockSpec((1, BLK, D_ATTN), lambda b, i: (b, i, 0)),
                  pl.BlockSpec((1, BLK, IDX_HEADS * IDX_DIM), lambda b, i: (b, i, 0)),
                  pl.BlockSpec((1, BLK, MISC_W), lambda b, i: (b, i, 0)),
                  pl.BlockSpec((1, BLK, IDX_HEADS * IDX_DIM), nxt),
                  pl.BlockSpec((1, BLK, MISC_W), nxt),
                  pl.BlockSpec((1, t, D_K_AUG), lambda b, i: (b, 0, 0)),
                  pl.BlockSpec((1, nq, ATTN_KV_HEADS, V_ROWS, BLK), lambda b, i: (b, 0, 0, 0, 0)),
                  pl.BlockSpec((1, t, MISC_W), lambda b, i: (b, 0, 0)),
                  pl.BlockSpec((1, D_ATTN), lambda b, i: (0, 0))],
        out_specs=pl.BlockSpec((1, BLK, D_ATTN), lambda b, i: (b, i, 0)),
        out_shape=jax.ShapeDtypeStruct((bsz, t, D_ATTN), BF16),
        scratch_shapes=[pltpu.VMEM((2, t, BLK), jnp.int32),
                        pltpu.VMEM((32, t // 32, BLK), jnp.int32),
                        pltpu.VMEM((t // 32, BLK), jnp.int32),
                        pltpu.VMEM((2, 1, BLK), jnp.int32),
                        pltpu.VMEM((3, ATTN_KV_HEADS, BLK, gw), F32),
                        pltpu.VMEM((ATTN_KV_HEADS, K_AUG, gw), BF16),
                        pltpu.VMEM((IDX_DIM, IDX_HEADS * BLK), BF16),
                        pltpu.VMEM((8, IDX_HEADS * BLK), F32),
                        pltpu.VMEM((ATTN_KV_HEADS, 1, gw), F32),
                        pltpu.VMEM((ATTN_KV_HEADS, V_ROWS, gw), F32),
                        pltpu.VMEM((2, ATTN_KV_HEADS, DSA_CHUNK, gw), BF16)],
        compiler_params=pltpu.CompilerParams(
            dimension_semantics=("arbitrary", "arbitrary"), vmem_limit_bytes=48 * 2 ** 20),
        name="dsa",
    )(rel_bias, bkt, q, qi, misc, qi, misc, k, vt, miscb, nw)


def _tail_kernel(x_ref, ys_ref, ya_ref, mod_ref, wo_ref, n1_ref, n2_ref, n3_ref, w1_ref, w2_ref, o_ref):
    x = x_ref[0]
    mix = (jnp.dot(ys_ref[0], wo_ref[0:D_SSM, :], preferred_element_type=F32)
           + jnp.dot(ya_ref[0], wo_ref[D_SSM:, :], preferred_element_type=F32))
    x1 = x + mod_ref[0, 2:3, :] * _rms(mix, n1_ref[...])
    h = (_rms(x1, n2_ref[...]) * (1.0 + mod_ref[0, 4:5, :]) + mod_ref[0, 3:4, :]).astype(BF16)
    f = jnp.zeros(x.shape, F32)
    step = D_MODEL
    for c0 in range(0, D_FF, step):
        a = jnp.maximum(jnp.dot(h, w1_ref[:, c0:c0 + step], preferred_element_type=F32), 0.0)
        f = f + jnp.dot((a * a).astype(BF16), w2_ref[c0:c0 + step, :], preferred_element_type=F32)
    o_ref[0] = x1 + mod_ref[0, 5:6, :] * _rms(f, n3_ref[...])


def _tail(x, ys, ya, mod, wo, n1, n2, n3, w1, w2):
    bsz, t, d = x.shape
    tm = min(ROW_TILE, t)

    def const(shape):
        return pl.BlockSpec(shape, lambda b, i: tuple(0 for _ in shape))

    return pl.pallas_call(
        _tail_kernel,
        grid=(bsz, t // tm),
        in_specs=[pl.BlockSpec((1, tm, d), lambda b, i: (b, i, 0)),
                  pl.BlockSpec((1, tm, D_SSM), lambda b, i: (b, i, 0)),
                  pl.BlockSpec((1, tm, D_ATTN), lambda b, i: (b, i, 0)),
                  pl.BlockSpec((1, 6, d), lambda b, i: (b, 0, 0)),
                  const((D_SSM + D_ATTN, d)), const((1, d)), const((1, d)), const((1, d)),
                  const((d, D_FF)), const((D_FF, d))],
        out_specs=pl.BlockSpec((1, tm, d), lambda b, i: (b, i, 0)),
        out_shape=jax.ShapeDtypeStruct((bsz, t, d), F32),
        compiler_params=pltpu.CompilerParams(
            dimension_semantics=("arbitrary", "arbitrary"), vmem_limit_bytes=60 * 2 ** 20),
        name="tail",
    )(x, ys, ya, mod, wo, n1, n2, n3, w1, w2)


def _permute_w_in(w):
    sizes = [D_SSM, D_XBC, SSM_HEADS, D_ATTN, D_KV, D_KV, IDX_HEADS * IDX_DIM, IDX_DIM, IDX_HEADS]
    pts = np.cumsum(sizes)[:-1].tolist()
    z, xbc, dt, q, k, v, qidx, kidx, widx = jnp.split(w, pts, axis=-1)
    pad = jnp.zeros((w.shape[0], MISC_W - IDX_DIM - IDX_HEADS - SSM_HEADS), w.dtype)
    kpad = jnp.zeros((w.shape[0], K_AUG - ATTN_HEAD_DIM), w.dtype)
    k_aug = []
    for g in range(ATTN_KV_HEADS):
        k_aug += [k[:, g * ATTN_HEAD_DIM:(g + 1) * ATTN_HEAD_DIM], kpad]
    return jnp.concatenate([z, xbc, q] + k_aug + [v, qidx, kidx, widx, dt, pad], axis=-1).astype(BF16)


def _layer(x, c_pad, w_ada, b_ada, norm1_pre, norm1_post, w_in, conv_w, conv_b, dt_bias, a_log, d_skip,
           ssm_norm, rel_bias, attn_norm, w_out, norm2_pre, norm2_post, w_mlp_in, w_mlp_out):
    bsz, t, d = x.shape
    mod = _ada(c_pad, w_ada, b_ada[None, :])[:bsz].reshape(bsz, 6, d)
    z, xbc, q, k, v, qi, misc, miscb = _inproj(x, mod, norm1_pre[None, :], _permute_w_in(w_in))

    lane = jnp.arange(MISC_W)
    is_dt = (lane >= MISC_DT) & (lane < MISC_DT + SSM_HEADS)
    head_of = jnp.clip(lane - MISC_DT, 0, SSM_HEADS - 1)
    dtb_t = jnp.where(is_dt, dt_bias[head_of], 0.0)[None, :]
    a_t = jnp.where(is_dt, -jnp.exp(a_log.astype(F32))[head_of], 0.0)[None, :]
    expand = ((lane[:, None] - MISC_DT) == (jnp.arange(D_SSM)[None, :] // SSM_HEAD_DIM)).astype(F32)
    dskip_e = jnp.repeat(d_skip.astype(F32), SSM_HEAD_DIM)[None, :]
    y_ssd = _ssd(xbc, z, misc, conv_w, conv_b[None, :], dtb_t, a_t, expand, dskip_e, ssm_norm[None, :])

    nq = t // BLK
    v4 = v.reshape(bsz, t, ATTN_KV_HEADS, ATTN_HEAD_DIM)
    v4 = jnp.concatenate([v4, jnp.ones((bsz, t, ATTN_KV_HEADS, 1), BF16),
                          jnp.zeros((bsz, t, ATTN_KV_HEADS, V_ROWS - ATTN_HEAD_DIM - 1), BF16)], axis=-1)
    vt = jnp.transpose(v4.reshape(bsz, nq, BLK, ATTN_KV_HEADS, V_ROWS), (0, 1, 3, 4, 2))
    y_att = _dsa(rel_bias, q, qi, misc, k, vt, miscb, attn_norm[None, :])

    return _tail(x, y_ssd, y_att, mod, w_out.astype(BF16), norm1_post[None, :], norm2_pre[None, :],
                 norm2_post[None, :], w_mlp_in.astype(BF16), w_mlp_out.astype(BF16))


def kernel(x, c, w_ada, b_ada, norm1_pre, norm1_post, w_in, conv_w, conv_b, dt_bias, a_log, d_skip,
           ssm_norm, rel_bias, attn_norm, w_out, norm2_pre, norm2_post, w_mlp_in, w_mlp_out):
    bsz = x.shape[0]
    assert bsz <= 8 and x.shape[1] % DSA_CHUNK == 0 and x.shape[2] == D_MODEL
    c_pad = jnp.zeros((8, D_MODEL), F32).at[:bsz].set(c)
    for l in range(w_ada.shape[0]):
        x = _layer(x, c_pad, w_ada[l], b_ada[l], norm1_pre[l], norm1_post[l], w_in[l], conv_w[l], conv_b[l],
                   dt_bias[l], a_log[l], d_skip[l], ssm_norm[l], rel_bias, attn_norm[l], w_out[l],
                   norm2_pre[l], norm2_post[l], w_mlp_in[l], w_mlp_out[l])
    return x
```

```python
import functools
import math

import numpy as np
import jax
import jax.numpy as jnp
from jax import lax
from jax.experimental import pallas as pl
from jax.experimental.pallas import tpu as pltpu

D_MODEL = 1024
SSM_HEADS = 8
SSM_HEAD_DIM = 64
D_SSM = SSM_HEADS * SSM_HEAD_DIM
SSM_GROUPS = 2
SSM_STATE = 128
CONV_WIDTH = 4
CHUNK = 256
ATTN_HEADS = 8
ATTN_KV_HEADS = 2
ATTN_HEAD_DIM = 64
D_ATTN = ATTN_HEADS * ATTN_HEAD_DIM
D_KV = ATTN_KV_HEADS * ATTN_HEAD_DIM
IDX_HEADS = 8
IDX_DIM = 64
TOPK_MAX = 256
N_BUCKETS = 32
MAX_DISTANCE = 128
D_FF = 4 * D_MODEL
D_BC = SSM_GROUPS * SSM_STATE
D_XBC = D_SSM + 2 * D_BC
EPS = 1e-6

LANES = 128
MISC_W = LANES
MISC_KIDX = 0
MISC_WIDX = IDX_DIM
MISC_DT = IDX_DIM + IDX_HEADS
K_AUG = LANES
D_K_AUG = ATTN_KV_HEADS * K_AUG
SHIFT_ROWS = 16
OFF_Z = 0
OFF_XBC = OFF_Z + D_SSM
OFF_Q = OFF_XBC + D_XBC
OFF_K = OFF_Q + D_ATTN
OFF_V = OFF_K + D_K_AUG
OFF_QI = OFF_V + D_KV
OFF_MISC = OFF_QI + IDX_HEADS * IDX_DIM
D_IN_PAD = OFF_MISC + MISC_W

BLK = 128
DSA_CHUNK = 512
V_ROWS = 80
DEN_LO = 2.0 ** -80
DEN_HI = 2.0 ** 80
F32_HUGE = 3.0e38
LOG2E = math.log2(math.e)
ROW_TILE = 512
NEG = -1e30
INT_MIN = -2 ** 31

F32 = jnp.float32
BF16 = jnp.bfloat16
HI = lax.Precision.HIGHEST
NT = (((1,), (1,)), ((), ()))


def _silu(x):
    return x / (1.0 + jnp.exp(-x))


def _rms(x, w):
    return x * lax.rsqrt(jnp.mean(x * x, axis=-1, keepdims=True) + EPS) * w


def _split3(x):
    hi = x.astype(BF16)
    rest = x - hi.astype(F32)
    mid = rest.astype(BF16)
    return hi, mid, (rest - mid.astype(F32)).astype(BF16)


def _ada_kernel(c_ref, w_ref, b_ref, o_ref):
    s = _silu(c_ref[...])
    o_ref[...] = jnp.dot(s, w_ref[...], precision=HI, preferred_element_type=F32) + b_ref[...]


def _ada(c_pad, w, b):
    n = w.shape[1]
    tn = D_MODEL
    return pl.pallas_call(
        _ada_kernel,
        grid=(n // tn,),
        in_specs=[pl.BlockSpec((8, D_MODEL), lambda j: (0, 0)),
                  pl.BlockSpec((D_MODEL, tn), lambda j: (0, j)),
                  pl.BlockSpec((1, tn), lambda j: (0, j))],
        out_specs=pl.BlockSpec((8, tn), lambda j: (0, j)),
        out_shape=jax.ShapeDtypeStruct((8, n), F32),
        name="ada",
    )(c_pad, w, b)


def _inproj_kernel(x_ref, mod_ref, nw_ref, w_ref,
                   z_ref, xbc_ref, q_ref, k_ref, v_ref, qi_ref, misc_ref, miscb_ref):
    x = x_ref[0]
    h = _rms(x, nw_ref[...]) * (1.0 + mod_ref[0, 1:2, :]) + mod_ref[0, 0:1, :]
    hb = h.astype(BF16)

    def seg(lo, width):
        return jnp.dot(hb, w_ref[:, lo:lo + width], preferred_element_type=F32)

    z_ref[0] = seg(OFF_Z, D_SSM)
    xbc_ref[0] = seg(OFF_XBC, D_XBC)
    q_ref[0] = (seg(OFF_Q, D_ATTN) * (ATTN_HEAD_DIM ** -0.5 * LOG2E)).astype(BF16)
    ones_col = lax.broadcasted_iota(jnp.int32, (1, D_K_AUG), 1) % K_AUG == ATTN_HEAD_DIM
    k_ref[0] = jnp.where(ones_col, 1.0, seg(OFF_K, D_K_AUG)).astype(BF16)
    v_ref[0] = seg(OFF_V, D_KV).astype(BF16)
    qi_ref[0] = seg(OFF_QI, IDX_HEADS * IDX_DIM).astype(BF16)
    misc = seg(OFF_MISC, MISC_W)
    misc_ref[0] = misc
    miscb_ref[0] = misc.astype(BF16)


def _inproj(x, mod, nw, w_perm):
    bsz, t, d = x.shape
    tm = min(ROW_TILE, t)

    def tok(width, dtype):
        return (pl.BlockSpec((1, tm, width), lambda b, i: (b, i, 0)),
                jax.ShapeDtypeStruct((bsz, t, width), dtype))

    outs = [tok(D_SSM, F32), tok(D_XBC, F32), tok(D_ATTN, BF16), tok(D_K_AUG, BF16), tok(D_KV, BF16),
            tok(IDX_HEADS * IDX_DIM, BF16), tok(MISC_W, F32), tok(MISC_W, BF16)]
    return pl.pallas_call(
        _inproj_kernel,
        grid=(bsz, t // tm),
        in_specs=[pl.BlockSpec((1, tm, d), lambda b, i: (b, i, 0)),
                  pl.BlockSpec((1, 6, d), lambda b, i: (b, 0, 0)),
                  pl.BlockSpec((1, d), lambda b, i: (0, 0)),
                  pl.BlockSpec((d, D_IN_PAD), lambda b, i: (0, 0))],
        out_specs=[o[0] for o in outs],
        out_shape=[o[1] for o in outs],
        compiler_params=pltpu.CompilerParams(
            dimension_semantics=("arbitrary", "arbitrary"), vmem_limit_bytes=48 * 2 ** 20),
        name="inproj",
    )(x, mod, nw, w_perm)


def _ssd_kernel(xbc_ref, z_ref, misc_ref, cw_ref, cb_ref, dtb_ref, a_ref, e_ref, dskip_ref, nw_ref,
                o_ref, xbuf, state):
    c = pl.program_id(1)
    lc = xbc_ref.shape[1]

    @pl.when(c == 0)
    def _():
        xbuf[0:8, :] = jnp.zeros((8, D_XBC), F32)
        state[...] = jnp.zeros(state.shape, F32)

    @pl.when(c > 0)
    def _():
        xbuf[0:8, :] = xbuf[lc:lc + 8, :]

    xbuf[8:lc + 8, :] = xbc_ref[0]
    u = cb_ref[...]
    for k in range(CONV_WIDTH):
        u = u + xbuf[pl.ds(8 - (CONV_WIDTH - 1) + k, lc), :] * cw_ref[k:k + 1, :]
    u = _silu(u)
    xs = u[:, :D_SSM]
    bm = u[:, D_SSM:D_SSM + D_BC]
    cm = u[:, D_SSM + D_BC:]

    raw = misc_ref[0] + dtb_ref[...]
    dt_t = jnp.maximum(raw, 0.0) + jnp.log1p(jnp.exp(-jnp.abs(raw)))
    adt_t = dt_t * a_ref[...]
    row = lax.broadcasted_iota(jnp.int32, (lc, lc), 0)
    col = lax.broadcasted_iota(jnp.int32, (lc, lc), 1)
    causal = col <= row
    tril = jnp.where(causal, 1.0, 0.0).astype(BF16)
    expand = e_ref[...]
    acs_t = sum(jnp.dot(tril, piece, preferred_element_type=F32) for piece in _split3(adt_t))
    dt_e = sum(jnp.dot(piece, expand, preferred_element_type=F32) for piece in _split3(dt_t))
    acs_e = sum(jnp.dot(piece, expand, preferred_element_type=F32) for piece in _split3(acs_t))
    acs_row = jnp.transpose(acs_t)

    xdt = xs * dt_e
    a_last = acs_e[lc - 1:lc, :]
    xdec = (xdt * jnp.exp(a_last - acs_e)).astype(BF16)
    xdt_b = xdt.astype(BF16)
    bm_b = bm.astype(BF16)
    cm_b = cm.astype(BF16)
    bm_t = jnp.transpose(bm).astype(BF16)

    heads_per_group = SSM_HEADS // SSM_GROUPS
    gw = heads_per_group * SSM_HEAD_DIM
    y_parts = []
    off_parts = []
    for g in range(SSM_GROUPS):
        bg = bm_b[:, g * SSM_STATE:(g + 1) * SSM_STATE]
        cg = cm_b[:, g * SSM_STATE:(g + 1) * SSM_STATE]
        cb = lax.dot_general(cg, bg, NT, preferred_element_type=F32)
        for r in range(heads_per_group):
            hd = g * heads_per_group + r
            a_col = acs_t[:, MISC_DT + hd:MISC_DT + hd + 1]
            a_row = acs_row[MISC_DT + hd:MISC_DT + hd + 1, :]
            lmat = jnp.exp(jnp.where(causal, a_col - a_row, -jnp.inf))
            mh = (cb * lmat).astype(BF16)
            y_parts.append(jnp.dot(mh, xdt_b[:, hd * SSM_HEAD_DIM:(hd + 1) * SSM_HEAD_DIM],
                                   preferred_element_type=F32))
        prev = state[g]
        off_parts.append(jnp.dot(cg, prev.astype(BF16), preferred_element_type=F32))
        st = jnp.dot(bm_t[g * SSM_STATE:(g + 1) * SSM_STATE, :], xdec[:, g * gw:(g + 1) * gw],
                     preferred_element_type=F32)
        state[g] = prev * jnp.exp(a_last[:, g * gw:(g + 1) * gw]) + st
    y = (jnp.concatenate(y_parts, axis=1)
         + jnp.concatenate(off_parts, axis=1) * jnp.exp(acs_e)
         + xs * dskip_ref[...])
    o_ref[0] = _rms(y * _silu(z_ref[0]), nw_ref[...]).astype(BF16)


def _ssd(xbc, z, misc, cw, cb, dtb_t, a_t, expand, dskip_e, nw):
    bsz, t, _ = xbc.shape
    lc = math.gcd(t, CHUNK)
    nc = t // lc
    gw = (SSM_HEADS // SSM_GROUPS) * SSM_HEAD_DIM

    def const(shape):
        return pl.BlockSpec(shape, lambda b, c: tuple(0 for _ in shape))

    return pl.pallas_call(
        _ssd_kernel,
        grid=(bsz, nc),
        in_specs=[pl.BlockSpec((1, lc, D_XBC), lambda b, c: (b, c, 0)),
                  pl.BlockSpec((1, lc, D_SSM), lambda b, c: (b, c, 0)),
                  pl.BlockSpec((1, lc, MISC_W), lambda b, c: (b, c, 0)),
                  const((CONV_WIDTH, D_XBC)), const((1, D_XBC)), const((1, MISC_W)), const((1, MISC_W)),
                  const((MISC_W, D_SSM)), const((1, D_SSM)), const((1, D_SSM))],
        out_specs=pl.BlockSpec((1, lc, D_SSM), lambda b, c: (b, c, 0)),
        out_shape=jax.ShapeDtypeStruct((bsz, t, D_SSM), BF16),
        scratch_shapes=[pltpu.VMEM((lc + 8, D_XBC), F32),
                        pltpu.VMEM((SSM_GROUPS, SSM_STATE, gw), F32)],
        compiler_params=pltpu.CompilerParams(
            dimension_semantics=("arbitrary", "arbitrary"), vmem_limit_bytes=48 * 2 ** 20),
        name="ssd",
    )(xbc, z, misc, cw, cb, dtb_t, a_t, expand, dskip_e, nw)


def _bucket_tiles():
    s = np.arange(BLK)[:, None]
    t = np.arange(BLK)[None, :]
    max_exact = N_BUCKETS // 2
    tiles = []
    for d0 in (0, BLK):
        n = np.maximum(d0 + t - s, 0)
        large = {}
        for dt_ in (np.float32, np.float64):
            nf = np.maximum(n, 1).astype(dt_)
            lg = max_exact + (np.log(nf / dt_(max_exact)) / dt_(math.log(MAX_DISTANCE / max_exact))
                              * dt_(N_BUCKETS - max_exact)).astype(np.int32)
            large[dt_] = np.minimum(lg, N_BUCKETS - 1)
        assert (large[np.float32] == large[np.float64]).all()
        tiles.append(np.where(n < max_exact, n, large[np.float64]).astype(np.int32))
    assert BLK + 1 >= MAX_DISTANCE
    return np.stack(tiles)


def _to_key(x):
    bits = lax.bitcast_convert_type(x, jnp.int32)
    key = bits ^ (lax.shift_right_arithmetic(bits, 31) & jnp.int32(0x7FFFFFFF))
    return jnp.where(key == -1, 0, key)


def _transpose32(rows):
    rows = list(rows)
    mask, j = 0x0000FFFF, 16
    while j:
        k = 0
        while k < 32:
            t = (rows[k] ^ lax.shift_right_logical(rows[k + j], jnp.int32(j))) & jnp.int32(mask)
            rows[k] = rows[k] ^ t
            rows[k + j] = rows[k + j] ^ lax.shift_left(t, jnp.int32(j))
            k = (k + j + 1) & ~j
        j >>= 1
        mask ^= (mask << j) & 0xFFFFFFFF
    return rows


def _dsa_kernel(relb_ref, bkt_ref, q_ref, qic_ref, miscc_ref, qin_ref, miscn_ref, k_ref, v_ref, kidx_ref,
                nw_ref, o_ref, keys, planes, live, thr_s, bias_s, qcat, qicat, wcat, m_s, acc_s, p_s, vt_s,
                *, topk, idx_bits, nq):
    b = pl.program_id(0)
    qi = pl.program_id(1)
    tq = BLK
    i32 = jnp.int32
    rep = ATTN_HEADS // ATTN_KV_HEADS
    gw = rep * tq
    ch = DSA_CHUNK
    ch_blks = ch // BLK
    rows_per_chunk = ch // 32

    @pl.when((b == 0) & (qi == 0))
    def _init_bias():
        for d in range(2):
            bk = bkt_ref[d]
            for h in range(ATTN_HEADS):
                far = relb_ref[N_BUCKETS - 1, h]
                acc = jnp.zeros((BLK, tq), F32)
                for n in range(N_BUCKETS - 1):
                    acc = jnp.where(bk == n, relb_ref[n, h] - far, acc)
                bias_s[d, h // rep, :, (h % rep) * BLK:(h % rep + 1) * BLK] = acc * LOG2E
        bias_s[2] = jnp.zeros((ATTN_KV_HEADS, BLK, gw), F32)
        for w in range(32):
            planes[w] = jnp.zeros(planes.shape[1:], i32)

    row_c = lax.broadcasted_iota(i32, (ch, tq), 0)
    lane_c = lax.broadcasted_iota(i32, (ch, tq), 1)

    def chunk_start(c):
        return pl.multiple_of(c * ch, ch)

    def n_chunks(blk):
        return (blk + ch_blks) // ch_blks

    def load_indexer(qidx_ref, misc_ref):
        qi_t = jnp.transpose(qidx_ref[0].astype(F32))
        w_t = jnp.transpose(misc_ref[0])[MISC_WIDX:MISC_WIDX + IDX_HEADS, :] * (
            IDX_HEADS ** -0.5 * IDX_DIM ** -0.5)
        for h in range(IDX_HEADS):
            qicat[:, h * BLK:(h + 1) * BLK] = qi_t[h * IDX_DIM:(h + 1) * IDX_DIM, :].astype(BF16)
            wcat[:, h * BLK:(h + 1) * BLK] = jnp.broadcast_to(w_t[h:h + 1, :], (8, BLK))

    def score_chunk(c, blk, slot):
        s0 = chunk_start(c)
        kb = kidx_ref[0, pl.ds(s0, ch), MISC_KIDX:MISC_KIDX + IDX_DIM]
        s = jnp.dot(kb, qicat[...], preferred_element_type=F32)
        r = jnp.maximum(s, 0.0) * wcat[0:1, :]
        acc = r[:, 0:BLK]
        for h in range(1, IDX_HEADS):
            acc = acc + r[:, h * BLK:(h + 1) * BLK]
        acc = jnp.where(s0 + row_c <= blk * BLK + lane_c, acc, -jnp.inf)
        keys[slot, pl.ds(s0, ch), :] = _to_key(acc)

    def plane_chunk(c, slot):
        ku = (keys[slot, pl.ds(chunk_start(c), ch), :] ^ i32(INT_MIN)).reshape(ch // (32 * 8), 32, 8, tq)
        words = _transpose32([ku[:, j] for j in range(32)])
        r0 = pl.multiple_of(c * rows_per_chunk, rows_per_chunk)
        for w in range(32):
            planes[w, pl.ds(r0, rows_per_chunk), :] = words[w].reshape(rows_per_chunk, tq)

    def select(blk, slot):
        nch = n_chunks(blk)
        nrows = live.shape[0]
        live[...] = jnp.where(lax.broadcasted_iota(i32, (nrows, tq), 0) < nch * rows_per_chunk, i32(-1), i32(0))

        def ones_in(words):
            return jnp.sum(lax.population_count(words), axis=0, keepdims=True)

        def bit_body(w, st):
            thr_u, n_above, n_set = st
            take = n_above + n_set >= topk
            thr_u = jnp.where(take, thr_u | lax.shift_left(i32(1), i32(31) - w), thr_u)
            n_above = jnp.where(take, n_above, n_above + n_set)
            still = live[...] & (planes[w] ^ jnp.where(take, i32(0), i32(-1)))
            live[...] = still
            return thr_u, n_above, ones_in(still & planes[jnp.minimum(w + 1, 31)])

        zero = jnp.zeros((1, tq), i32)
        thr_u, n_gt, _ = lax.fori_loop(0, 32, bit_body, (zero, zero, ones_in(live[...] & planes[0])))
        thr = thr_u ^ i32(INT_MIN)
        thr_s[slot] = thr
        n_eq = ones_in(live[...])
        need = topk - n_gt

        @pl.when(jnp.max((n_eq > need).astype(i32)) > 0)
        def _ties():
            def count(pred):
                def body(c, cnt):
                    s0 = chunk_start(c)
                    hit = pred(keys[slot, pl.ds(s0, ch), :], s0).astype(i32)
                    n = ch
                    while n > 8:
                        n //= 2
                        hit = hit[:n] + hit[n:2 * n]
                    return cnt + hit
                cnt = lax.fori_loop(0, nch, body, jnp.zeros((8, tq), i32))
                return jnp.sum(cnt, axis=0, keepdims=True)

            def jbody(it, p):
                cand = p + lax.shift_left(i32(1), i32(idx_bits - 1) - it)
                c = count(lambda kk, s0: (kk == thr) & (s0 + row_c < cand))
                return jnp.where(c < need, cand, p)
            last = lax.fori_loop(0, idx_bits, jbody, jnp.zeros((1, tq), i32))

            def demote(c, carry):
                s0 = chunk_start(c)
                kk = keys[slot, pl.ds(s0, ch), :]
                keys[slot, pl.ds(s0, ch), :] = jnp.where((kk == thr) & (s0 + row_c > last), INT_MIN, kk)
                return carry
            lax.fori_loop(0, nch, demote, 0)

    @pl.when(qi == 0)
    def _first_block():
        ones_tile = jnp.where(lax.broadcasted_iota(i32, (V_ROWS - ATTN_HEAD_DIM, BLK), 0) == 0, 1.0, 0.0)

        def value_block(j, carry):
            v_t = jnp.transpose(v_ref[0, pl.ds(pl.multiple_of(j * BLK, BLK), BLK), :].astype(F32))
            for g in range(ATTN_KV_HEADS):
                vt_s[j, g, 0:ATTN_HEAD_DIM, :] = v_t[g * ATTN_HEAD_DIM:(g + 1) * ATTN_HEAD_DIM, :].astype(BF16)
                vt_s[j, g, ATTN_HEAD_DIM:, :] = ones_tile.astype(BF16)
            return carry
        lax.fori_loop(0, nq, value_block, 0)

        load_indexer(qic_ref, miscc_ref)
        score_chunk(0, 0, 0)
        plane_chunk(0, 0)
        select(0, 0)

    slot_b = qi % 2
    slot_a = 1 - slot_b
    blk_a = jnp.minimum(qi + 1, nq - 1)
    nch_b = n_chunks(qi)
    nch_a = n_chunks(blk_a)
    load_indexer(qin_ref, miscn_ref)
    thr = thr_s[slot_b]

    q_t = jnp.transpose(q_ref[0].astype(F32))
    for h in range(ATTN_HEADS):
        qcat[h // rep, 0:ATTN_HEAD_DIM, (h % rep) * BLK:(h % rep + 1) * BLK] = (
            q_t[h * ATTN_HEAD_DIM:(h + 1) * ATTN_HEAD_DIM, :].astype(BF16))
    for g in range(ATTN_KV_HEADS):
        qcat[g, ATTN_HEAD_DIM:, :] = jnp.zeros((K_AUG - ATTN_HEAD_DIM, gw), BF16)

    def pen_of(s0, causal):
        sel = keys[slot_b, pl.ds(s0, ch), :] >= thr
        if causal:
            sel = sel & (s0 + row_c <= qi * BLK + lane_c)
        pen = jnp.where(sel, 0.0, NEG)
        return jnp.concatenate([pen] * rep, axis=1)

    def near_bias(c, g):
        return jnp.concatenate(
            [bias_s[jnp.clip(qi - (c * ch_blks + u), 0, 2), g] for u in range(ch_blks)], axis=0)

    def vt_chunk(c, g):
        return jnp.concatenate([vt_s[c * ch_blks + u, g] for u in range(ch_blks)], axis=1)

    s0d = pl.multiple_of(qi * BLK, BLK)
    kd = k_ref[0, pl.ds(s0d, BLK), :]
    tri = lax.broadcasted_iota(i32, (BLK, tq), 0) <= lax.broadcasted_iota(i32, (BLK, tq), 1)
    tri_pen = jnp.concatenate([jnp.where(tri, 0.0, NEG)] * rep, axis=1)
    for g in range(ATTN_KV_HEADS):
        sd = jnp.dot(kd[:, g * K_AUG:(g + 1) * K_AUG], qcat[g],
                     preferred_element_type=F32) + bias_s[0, g] + tri_pen
        shift = jnp.max(sd, axis=0, keepdims=True)
        first = lax.broadcasted_iota(i32, (SHIFT_ROWS, gw), 0) == 0
        qcat[g, ATTN_HEAD_DIM:ATTN_HEAD_DIM + SHIFT_ROWS, :] = jnp.where(first, -shift, 0.0).astype(BF16)
        acc_s[g] = jnp.zeros((V_ROWS, gw), F32)
        p_s[1, g] = jnp.zeros((ch, gw), BF16)

    def weigh_values(c, slot):
        for g in range(ATTN_KV_HEADS):
            acc_s[g] += jnp.dot(vt_chunk(c, g), p_s[slot, g], preferred_element_type=F32)

    def att_chunk(c, near):
        weigh_values(jnp.maximum(c - 1, 0), (c + 1) % 2)
        s0 = chunk_start(c)
        pen = pen_of(s0, near)
        kblk = k_ref[0, pl.ds(s0, ch), :]
        for g in range(ATTN_KV_HEADS):
            s = jnp.dot(kblk[:, g * K_AUG:(g + 1) * K_AUG], qcat[g],
                        preferred_element_type=F32)
            if near:
                s = s + near_bias(c, g)
            p_s[c % 2, g] = jnp.exp2(s + pen).astype(BF16)

    def fused(c, near):
        plane_chunk(c, slot_a)
        att_chunk(c, near)
        score_chunk(c + 1, blk_a, slot_a)

    def far_body(c, carry):
        fused(c, False)
        return carry

    def near_body(c, carry):
        fused(c, True)
        return carry

    score_chunk(0, blk_a, slot_a)
    n_far = jnp.maximum(qi - 1, 0) // ch_blks
    lax.fori_loop(0, n_far, far_body, 0)
    lax.fori_loop(n_far, nch_b - 1, near_body, 0)
    plane_chunk(nch_b - 1, slot_a)
    att_chunk(nch_b - 1, True)
    weigh_values(nch_b - 1, (nch_b - 1) % 2)

    @pl.when(nch_a > nch_b)
    def _one_more_chunk():
        score_chunk(nch_b, blk_a, slot_a)
        plane_chunk(nch_b, slot_a)

    select(blk_a, slot_a)

    n_ok = jnp.zeros((1, gw), i32)
    for g in range(ATTN_KV_HEADS):
        a = acc_s[g]
        den = a[ATTN_HEAD_DIM:ATTN_HEAD_DIM + 1, :]
        top = jnp.max(jnp.abs(a[0:ATTN_HEAD_DIM, :]), axis=0, keepdims=True)
        n_ok = n_ok + ((den > DEN_LO) & (den < DEN_HI) & (top < F32_HUGE)).astype(i32)

    @pl.when(jnp.min(n_ok) < ATTN_KV_HEADS)
    def _exact():
        for g in range(ATTN_KV_HEADS):
            qcat[g, ATTN_HEAD_DIM:ATTN_HEAD_DIM + SHIFT_ROWS, :] = jnp.zeros((SHIFT_ROWS, gw), BF16)
            m_s[g] = jnp.full((1, gw), NEG, F32)
            acc_s[g] = jnp.zeros((V_ROWS, gw), F32)

        def body(c, carry):
            s0 = chunk_start(c)
            pen = pen_of(s0, True)
            kblk = k_ref[0, pl.ds(s0, ch), :]
            for g in range(ATTN_KV_HEADS):
                s = jnp.dot(kblk[:, g * K_AUG:(g + 1) * K_AUG], qcat[g],
                            preferred_element_type=F32) + near_bias(c, g) + pen
                m_old = m_s[g]
                m_new = jnp.maximum(m_old, jnp.max(s, axis=0, keepdims=True))
                p = jnp.exp2(s - m_new).astype(BF16)
                acc_s[g] = acc_s[g] * jnp.exp2(m_old - m_new) + jnp.dot(
                    vt_chunk(c, g), p, preferred_element_type=F32)
                m_s[g] = m_new
            return carry
        lax.fori_loop(0, nch_b, body, 0)

    parts = []
    for g in range(ATTN_KV_HEADS):
        a = acc_s[g]
        o = a[0:ATTN_HEAD_DIM, :] * (1.0 / a[ATTN_HEAD_DIM:ATTN_HEAD_DIM + 1, :])
        parts += [o[:, r * tq:(r + 1) * tq] for r in range(rep)]
    out_t = jnp.concatenate(parts, axis=0)
    o_ref[0] = _rms(jnp.transpose(out_t), nw_ref[...]).astype(BF16)


def _dsa(rel_bias, q, qi, misc, k, v, miscb, nw):
    bsz, t, _ = q.shape
    nq = t // BLK
    topk = min(TOPK_MAX, t // 4)
    idx_bits = max(1, (t - 1).bit_length())
    gw = (ATTN_HEADS // ATTN_KV_HEADS) * BLK
    bkt = jnp.asarray(_bucket_tiles())
    kern = functools.partial(_dsa_kernel, topk=topk, idx_bits=idx_bits, nq=nq)

    def nxt(b, i):
        return (b, jnp.minimum(i + 1, nq - 1), 0)

    return pl.pallas_call(
        kern,
        grid=(bsz, nq),
        in_specs=[pl.BlockSpec(memory_space=pltpu.SMEM),
                  pl.BlockSpec((2, BLK, BLK), lambda b, i: (0, 0, 0)),
                  pl.BlockSpec((1, BLK, D_ATTN), lambda b, i: (b, i, 0)),
                  pl.BlockSpec((1, BLK, IDX_HEADS * IDX_DIM), lambda b, i: (b, i, 0)),
                  pl.BlockSpec((1, BLK, MISC_W), lambda b, i: (b, i, 0)),
                  pl.BlockSpec((1, BLK, IDX_HEADS * IDX_DIM), nxt),
                  pl.BlockSpec((1, BLK, MISC_W), nxt),
                  pl.BlockSpec((1, t, D_K_AUG), lambda b, i: (b, 0, 0)),
                  pl.BlockSpec((1, t, D_KV), lambda b, i: (b, 0, 0)),
                  pl.BlockSpec((1, t, MISC_W), lambda b, i: (b, 0, 0)),
                  pl.BlockSpec((1, D_ATTN), lambda b, i: (0, 0))],
        out_specs=pl.BlockSpec((1, BLK, D_ATTN), lambda b, i: (b, i, 0)),
        out_shape=jax.ShapeDtypeStruct((bsz, t, D_ATTN), BF16),
        scratch_shapes=[pltpu.VMEM((2, t, BLK), jnp.int32),
                        pltpu.VMEM((32, t // 32, BLK), jnp.int32),
                        pltpu.VMEM((t // 32, BLK), jnp.int32),
                        pltpu.VMEM((2, 1, BLK), jnp.int32),
                        pltpu.VMEM((3, ATTN_KV_HEADS, BLK, gw), F32),
                        pltpu.VMEM((ATTN_KV_HEADS, K_AUG, gw), BF16),
                        pltpu.VMEM((IDX_DIM, IDX_HEADS * BLK), BF16),
                        pltpu.VMEM((8, IDX_HEADS * BLK), F32),
                        pltpu.VMEM((ATTN_KV_HEADS, 1, gw), F32),
                        pltpu.VMEM((ATTN_KV_HEADS, V_ROWS, gw), F32),
                        pltpu.VMEM((2, ATTN_KV_HEADS, DSA_CHUNK, gw), BF16),
                        pltpu.VMEM((nq, ATTN_KV_HEADS, V_ROWS, BLK), BF16)],
        compiler_params=pltpu.CompilerParams(
            dimension_semantics=("arbitrary", "arbitrary"), vmem_limit_bytes=48 * 2 ** 20),
        name="dsa",
    )(rel_bias, bkt, q, qi, misc, qi, misc, k, v, miscb, nw)


def _tail_kernel(x_ref, ys_ref, ya_ref, mod_ref, wo_ref, n1_ref, n2_ref, n3_ref, w1_ref, w2_ref, o_ref):
    x = x_ref[0]
    mix = (jnp.dot(ys_ref[0], wo_ref[0:D_SSM, :], preferred_element_type=F32)
           + jnp.dot(ya_ref[0], wo_ref[D_SSM:, :], preferred_element_type=F32))
    x1 = x + mod_ref[0, 2:3, :] * _rms(mix, n1_ref[...])
    h = (_rms(x1, n2_ref[...]) * (1.0 + mod_ref[0, 4:5, :]) + mod_ref[0, 3:4, :]).astype(BF16)
    f = jnp.zeros(x.shape, F32)
    step = D_MODEL
    for c0 in range(0, D_FF, step):
        a = jnp.maximum(jnp.dot(h, w1_ref[:, c0:c0 + step], preferred_element_type=F32), 0.0)
        f = f + jnp.dot((a * a).astype(BF16), w2_ref[c0:c0 + step, :], preferred_element_type=F32)
    o_ref[0] = x1 + mod_ref[0, 5:6, :] * _rms(f, n3_ref[...])


def _tail(x, ys, ya, mod, wo, n1, n2, n3, w1, w2):
    bsz, t, d = x.shape
    tm = min(ROW_TILE, t)

    def const(shape):
        return pl.BlockSpec(shape, lambda b, i: tuple(0 for _ in shape))

    return pl.pallas_call(
        _tail_kernel,
        grid=(bsz, t // tm),
        in_specs=[pl.BlockSpec((1, tm, d), lambda b, i: (b, i, 0)),
                  pl.BlockSpec((1, tm, D_SSM), lambda b, i: (b, i, 0)),
                  pl.BlockSpec((1, tm, D_ATTN), lambda b, i: (b, i, 0)),
                  pl.BlockSpec((1, 6, d), lambda b, i: (b, 0, 0)),
                  const((D_SSM + D_ATTN, d)), const((1, d)), const((1, d)), const((1, d)),
                  const((d, D_FF)), const((D_FF, d))],
        out_specs=pl.BlockSpec((1, tm, d), lambda b, i: (b, i, 0)),
        out_shape=jax.ShapeDtypeStruct((bsz, t, d), F32),
        compiler_params=pltpu.CompilerParams(
            dimension_semantics=("arbitrary", "arbitrary"), vmem_limit_bytes=60 * 2 ** 20),
        name="tail",
    )(x, ys, ya, mod, wo, n1, n2, n3, w1, w2)


def _permute_w_in(w):
    sizes = [D_SSM, D_XBC, SSM_HEADS, D_ATTN, D_KV, D_KV, IDX_HEADS * IDX_DIM, IDX_DIM, IDX_HEADS]
    pts = np.cumsum(sizes)[:-1].tolist()
    z, xbc, dt, q, k, v, qidx, kidx, widx = jnp.split(w, pts, axis=-1)
    pad = jnp.zeros((w.shape[0], MISC_W - IDX_DIM - IDX_HEADS - SSM_HEADS), w.dtype)
    kpad = jnp.zeros((w.shape[0], K_AUG - ATTN_HEAD_DIM), w.dtype)
    k_aug = []
    for g in range(ATTN_KV_HEADS):
        k_aug += [k[:, g * ATTN_HEAD_DIM:(g + 1) * ATTN_HEAD_DIM], kpad]
    return jnp.concatenate([z, xbc, q] + k_aug + [v, qidx, kidx, widx, dt, pad], axis=-1).astype(BF16)


def _layer(x, c_pad, w_ada, b_ada, norm1_pre, norm1_post, w_in, conv_w, conv_b, dt_bias, a_log, d_skip,
           ssm_norm, rel_bias, attn_norm, w_out, norm2_pre, norm2_post, w_mlp_in, w_mlp_out):
    bsz, t, d = x.shape
    mod = _ada(c_pad, w_ada, b_ada[None, :])[:bsz].reshape(bsz, 6, d)
    z, xbc, q, k, v, qi, misc, miscb = _inproj(x, mod, norm1_pre[None, :], _permute_w_in(w_in))

    lane = jnp.arange(MISC_W)
    is_dt = (lane >= MISC_DT) & (lane < MISC_DT + SSM_HEADS)
    head_of = jnp.clip(lane - MISC_DT, 0, SSM_HEADS - 1)
    dtb_t = jnp.where(is_dt, dt_bias[head_of], 0.0)[None, :]
    a_t = jnp.where(is_dt, -jnp.exp(a_log.astype(F32))[head_of], 0.0)[None, :]
    expand = ((lane[:, None] - MISC_DT) == (jnp.arange(D_SSM)[None, :] // SSM_HEAD_DIM)).astype(BF16)
    dskip_e = jnp.repeat(d_skip.astype(F32), SSM_HEAD_DIM)[None, :]
    y_ssd = _ssd(xbc, z, misc, conv_w, conv_b[None, :], dtb_t, a_t, expand, dskip_e, ssm_norm[None, :])

    y_att = _dsa(rel_bias, q, qi, misc, k, v, miscb, attn_norm[None, :])

    return _tail(x, y_ssd, y_att, mod, w_out.astype(BF16), norm1_post[None, :], norm2_pre[None, :],
                 norm2_post[None, :], w_mlp_in.astype(BF16), w_mlp_out.astype(BF16))


def kernel(x, c, w_ada, b_ada, norm1_pre, norm1_post, w_in, conv_w, conv_b, dt_bias, a_log, d_skip,
           ssm_norm, rel_bias, attn_norm, w_out, norm2_pre, norm2_post, w_mlp_in, w_mlp_out):
    bsz = x.shape[0]
    assert bsz <= 8 and x.shape[1] % DSA_CHUNK == 0 and x.shape[2] == D_MODEL
    c_pad = jnp.zeros((8, D_MODEL), F32).at[:bsz].set(c)
    for l in range(w_ada.shape[0]):
        x = _layer(x, c_pad, w_ada[l], b_ada[l], norm1_pre[l], norm1_post[l], w_in[l], conv_w[l], conv_b[l],
                   dt_bias[l], a_log[l], d_skip[l], ssm_norm[l], rel_bias, attn_norm[l], w_out[l],
                   norm2_pre[l], norm2_post[l], w_mlp_in[l], w_mlp_out[l])
    return x
```

```python
import functools
import math

import numpy as np
import jax
import jax.numpy as jnp
from jax import lax
from jax.experimental import pallas as pl
from jax.experimental.pallas import tpu as pltpu

D_MODEL = 1024
SSM_HEADS = 8
SSM_HEAD_DIM = 64
D_SSM = SSM_HEADS * SSM_HEAD_DIM
SSM_GROUPS = 2
SSM_STATE = 128
CONV_WIDTH = 4
CHUNK = 256
ATTN_HEADS = 8
ATTN_KV_HEADS = 2
ATTN_HEAD_DIM = 64
D_ATTN = ATTN_HEADS * ATTN_HEAD_DIM
D_KV = ATTN_KV_HEADS * ATTN_HEAD_DIM
IDX_HEADS = 8
IDX_DIM = 64
TOPK_MAX = 256
N_BUCKETS = 32
MAX_DISTANCE = 128
D_FF = 4 * D_MODEL
D_BC = SSM_GROUPS * SSM_STATE
D_XBC = D_SSM + 2 * D_BC
EPS = 1e-6

LANES = 128
MISC_W = LANES
MISC_KIDX = 0
MISC_WIDX = IDX_DIM
MISC_DT = IDX_DIM + IDX_HEADS
K_AUG = LANES
D_K_AUG = ATTN_KV_HEADS * K_AUG
SHIFT_ROWS = 16
OFF_Z = 0
OFF_XBC = OFF_Z + D_SSM
OFF_Q = OFF_XBC + D_XBC
OFF_K = OFF_Q + D_ATTN
OFF_V = OFF_K + D_K_AUG
OFF_QI = OFF_V + D_KV
OFF_MISC = OFF_QI + IDX_HEADS * IDX_DIM
D_IN_PAD = OFF_MISC + MISC_W

BLK = 128
DSA_CHUNK = 512
V_ROWS = 80
DEN_LO = 2.0 ** -80
DEN_HI = 2.0 ** 80
F32_HUGE = 3.0e38
LOG2E = math.log2(math.e)
ROW_TILE = 512
NEG = -1e30
INT_MIN = -2 ** 31

F32 = jnp.float32
BF16 = jnp.bfloat16
HI = lax.Precision.HIGHEST
NT = (((1,), (1,)), ((), ()))


def _silu(x):
    return x / (1.0 + jnp.exp(-x))


def _rms(x, w):
    return x * lax.rsqrt(jnp.mean(x * x, axis=-1, keepdims=True) + EPS) * w


def _split3(x):
    hi = x.astype(BF16)
    rest = x - hi.astype(F32)
    mid = rest.astype(BF16)
    return hi, mid, (rest - mid.astype(F32)).astype(BF16)


def _ada_kernel(c_ref, w_ref, b_ref, o_ref):
    s = _silu(c_ref[...])
    o_ref[...] = jnp.dot(s, w_ref[...], precision=HI, preferred_element_type=F32) + b_ref[...]


def _ada(c_pad, w, b):
    n = w.shape[1]
    tn = D_MODEL
    return pl.pallas_call(
        _ada_kernel,
        grid=(n // tn,),
        in_specs=[pl.BlockSpec((8, D_MODEL), lambda j: (0, 0)),
                  pl.BlockSpec((D_MODEL, tn), lambda j: (0, j)),
                  pl.BlockSpec((1, tn), lambda j: (0, j))],
        out_specs=pl.BlockSpec((8, tn), lambda j: (0, j)),
        out_shape=jax.ShapeDtypeStruct((8, n), F32),
        name="ada",
    )(c_pad, w, b)


def _inproj_kernel(x_ref, mod_ref, nw_ref, w_ref,
                   z_ref, xbc_ref, q_ref, k_ref, v_ref, qi_ref, misc_ref, miscb_ref):
    x = x_ref[0]
    h = _rms(x, nw_ref[...]) * (1.0 + mod_ref[0, 1:2, :]) + mod_ref[0, 0:1, :]
    hb = h.astype(BF16)

    def seg(lo, width):
        return jnp.dot(hb, w_ref[:, lo:lo + width], preferred_element_type=F32)

    z_ref[0] = seg(OFF_Z, D_SSM)
    xbc_ref[0] = seg(OFF_XBC, D_XBC)
    q_ref[0] = (seg(OFF_Q, D_ATTN) * (ATTN_HEAD_DIM ** -0.5 * LOG2E)).astype(BF16)
    ones_col = lax.broadcasted_iota(jnp.int32, (1, D_K_AUG), 1) % K_AUG == ATTN_HEAD_DIM
    k_ref[0] = jnp.where(ones_col, 1.0, seg(OFF_K, D_K_AUG)).astype(BF16)
    v_ref[0] = seg(OFF_V, D_KV).astype(BF16)
    qi_ref[0] = seg(OFF_QI, IDX_HEADS * IDX_DIM).astype(BF16)
    misc = seg(OFF_MISC, MISC_W)
    misc_ref[0] = misc
    miscb_ref[0] = misc.astype(BF16)


def _inproj(x, mod, nw, w_perm):
    bsz, t, d = x.shape
    tm = min(ROW_TILE, t)

    def tok(width, dtype):
        return (pl.BlockSpec((1, tm, width), lambda b, i: (b, i, 0)),
                jax.ShapeDtypeStruct((bsz, t, width), dtype))

    outs = [tok(D_SSM, F32), tok(D_XBC, F32), tok(D_ATTN, BF16), tok(D_K_AUG, BF16), tok(D_KV, BF16),
            tok(IDX_HEADS * IDX_DIM, BF16), tok(MISC_W, F32), tok(MISC_W, BF16)]
    return pl.pallas_call(
        _inproj_kernel,
        grid=(bsz, t // tm),
        in_specs=[pl.BlockSpec((1, tm, d), lambda b, i: (b, i, 0)),
                  pl.BlockSpec((1, 6, d), lambda b, i: (b, 0, 0)),
                  pl.BlockSpec((1, d), lambda b, i: (0, 0)),
                  pl.BlockSpec((d, D_IN_PAD), lambda b, i: (0, 0))],
        out_specs=[o[0] for o in outs],
        out_shape=[o[1] for o in outs],
        compiler_params=pltpu.CompilerParams(
            dimension_semantics=("arbitrary", "arbitrary"), vmem_limit_bytes=48 * 2 ** 20),
        name="inproj",
    )(x, mod, nw, w_perm)


def _ssd_kernel(xbc_ref, z_ref, misc_ref, cw_ref, cb_ref, dtb_ref, a_ref, e_ref, dskip_ref, nw_ref,
                o_ref, xbuf, state):
    c = pl.program_id(1)
    lc = xbc_ref.shape[1]

    @pl.when(c == 0)
    def _():
        xbuf[0:8, :] = jnp.zeros((8, D_XBC), F32)
        state[...] = jnp.zeros(state.shape, F32)

    @pl.when(c > 0)
    def _():
        xbuf[0:8, :] = xbuf[lc:lc + 8, :]

    xbuf[8:lc + 8, :] = xbc_ref[0]
    u = cb_ref[...]
    for k in range(CONV_WIDTH):
        u = u + xbuf[pl.ds(8 - (CONV_WIDTH - 1) + k, lc), :] * cw_ref[k:k + 1, :]
    u = _silu(u)
    xs = u[:, :D_SSM]
    bm = u[:, D_SSM:D_SSM + D_BC]
    cm = u[:, D_SSM + D_BC:]

    raw = misc_ref[0] + dtb_ref[...]
    dt_t = jnp.maximum(raw, 0.0) + jnp.log1p(jnp.exp(-jnp.abs(raw)))
    adt_t = dt_t * a_ref[...]
    row = lax.broadcasted_iota(jnp.int32, (lc, lc), 0)
    col = lax.broadcasted_iota(jnp.int32, (lc, lc), 1)
    causal = col <= row
    tril = jnp.where(causal, 1.0, 0.0).astype(BF16)
    expand = e_ref[...]
    acs_t = sum(jnp.dot(tril, piece, preferred_element_type=F32) for piece in _split3(adt_t))
    dt_e = sum(jnp.dot(piece, expand, preferred_element_type=F32) for piece in _split3(dt_t))
    acs_e = sum(jnp.dot(piece, expand, preferred_element_type=F32) for piece in _split3(acs_t))
    acs_row = jnp.transpose(acs_t)

    xdt = xs * dt_e
    a_last = acs_e[lc - 1:lc, :]
    xdec = (xdt * jnp.exp(a_last - acs_e)).astype(BF16)
    xdt_b = xdt.astype(BF16)
    bm_b = bm.astype(BF16)
    cm_b = cm.astype(BF16)
    bm_t = jnp.transpose(bm).astype(BF16)

    heads_per_group = SSM_HEADS // SSM_GROUPS
    gw = heads_per_group * SSM_HEAD_DIM
    y_parts = []
    off_parts = []
    for g in range(SSM_GROUPS):
        bg = bm_b[:, g * SSM_STATE:(g + 1) * SSM_STATE]
        cg = cm_b[:, g * SSM_STATE:(g + 1) * SSM_STATE]
        cb = lax.dot_general(cg, bg, NT, preferred_element_type=F32)
        for r in range(heads_per_group):
            hd = g * heads_per_group + r
            a_col = acs_t[:, MISC_DT + hd:MISC_DT + hd + 1]
            a_row = acs_row[MISC_DT + hd:MISC_DT + hd + 1, :]
            lmat = jnp.exp(jnp.where(causal, a_col - a_row, -jnp.inf))
            mh = (cb * lmat).astype(BF16)
            y_parts.append(jnp.dot(mh, xdt_b[:, hd * SSM_HEAD_DIM:(hd + 1) * SSM_HEAD_DIM],
                                   preferred_element_type=F32))
        prev = state[g]
        off_parts.append(jnp.dot(cg, prev.astype(BF16), preferred_element_type=F32))
        st = jnp.dot(bm_t[g * SSM_STATE:(g + 1) * SSM_STATE, :], xdec[:, g * gw:(g + 1) * gw],
                     preferred_element_type=F32)
        state[g] = prev * jnp.exp(a_last[:, g * gw:(g + 1) * gw]) + st
    y = (jnp.concatenate(y_parts, axis=1)
         + jnp.concatenate(off_parts, axis=1) * jnp.exp(acs_e)
         + xs * dskip_ref[...])
    o_ref[0] = _rms(y * _silu(z_ref[0]), nw_ref[...]).astype(BF16)


def _ssd(xbc, z, misc, cw, cb, dtb_t, a_t, expand, dskip_e, nw):
    bsz, t, _ = xbc.shape
    lc = math.gcd(t, CHUNK)
    nc = t // lc
    gw = (SSM_HEADS // SSM_GROUPS) * SSM_HEAD_DIM

    def const(shape):
        return pl.BlockSpec(shape, lambda b, c: tuple(0 for _ in shape))

    return pl.pallas_call(
        _ssd_kernel,
        grid=(bsz, nc),
        in_specs=[pl.BlockSpec((1, lc, D_XBC), lambda b, c: (b, c, 0)),
                  pl.BlockSpec((1, lc, D_SSM), lambda b, c: (b, c, 0)),
                  pl.BlockSpec((1, lc, MISC_W), lambda b, c: (b, c, 0)),
                  const((CONV_WIDTH, D_XBC)), const((1, D_XBC)), const((1, MISC_W)), const((1, MISC_W)),
                  const((MISC_W, D_SSM)), const((1, D_SSM)), const((1, D_SSM))],
        out_specs=pl.BlockSpec((1, lc, D_SSM), lambda b, c: (b, c, 0)),
        out_shape=jax.ShapeDtypeStruct((bsz, t, D_SSM), BF16),
        scratch_shapes=[pltpu.VMEM((lc + 8, D_XBC), F32),
                        pltpu.VMEM((SSM_GROUPS, SSM_STATE, gw), F32)],
        compiler_params=pltpu.CompilerParams(
            dimension_semantics=("arbitrary", "arbitrary"), vmem_limit_bytes=48 * 2 ** 20),
        name="ssd",
    )(xbc, z, misc, cw, cb, dtb_t, a_t, expand, dskip_e, nw)


def _bucket_tiles():
    s = np.arange(BLK)[:, None]
    t = np.arange(BLK)[None, :]
    max_exact = N_BUCKETS // 2
    tiles = []
    for d0 in (0, BLK):
        n = np.maximum(d0 + t - s, 0)
        large = {}
        for dt_ in (np.float32, np.float64):
            nf = np.maximum(n, 1).astype(dt_)
            lg = max_exact + (np.log(nf / dt_(max_exact)) / dt_(math.log(MAX_DISTANCE / max_exact))
                              * dt_(N_BUCKETS - max_exact)).astype(np.int32)
            large[dt_] = np.minimum(lg, N_BUCKETS - 1)
        assert (large[np.float32] == large[np.float64]).all()
        tiles.append(np.where(n < max_exact, n, large[np.float64]).astype(np.int32))
    assert BLK + 1 >= MAX_DISTANCE
    return np.stack(tiles)


def _to_key(x):
    bits = lax.bitcast_convert_type(x, jnp.int32)
    key = bits ^ (lax.shift_right_arithmetic(bits, 31) & jnp.int32(0x7FFFFFFF))
    return jnp.where(key == -1, 0, key)


def _transpose32(rows):
    rows = list(rows)
    mask, j = 0x0000FFFF, 16
    while j:
        k = 0
        while k < 32:
            t = (rows[k] ^ lax.shift_right_logical(rows[k + j], jnp.int32(j))) & jnp.int32(mask)
            rows[k] = rows[k] ^ t
            rows[k + j] = rows[k + j] ^ lax.shift_left(t, jnp.int32(j))
            k = (k + j + 1) & ~j
        j >>= 1
        mask ^= (mask << j) & 0xFFFFFFFF
    return rows


def _dsa_kernel(relb_ref, bkt_ref, q_ref, qic_ref, miscc_ref, qin_ref, miscn_ref, k_ref, v_ref, kidx_ref,
                nw_ref, o_ref, keys, planes, live, thr_s, bias_s, qcat, qicat, wcat, m_s, acc_s, p_s, vt_s,
                *, topk, idx_bits, nq):
    b = pl.program_id(0)
    qi = pl.program_id(1)
    tq = BLK
    i32 = jnp.int32
    rep = ATTN_HEADS // ATTN_KV_HEADS
    gw = rep * tq
    ch = DSA_CHUNK
    ch_blks = ch // BLK
    rows_per_chunk = ch // 32

    @pl.when((b == 0) & (qi == 0))
    def _init_bias():
        for d in range(2):
            bk = bkt_ref[d]
            for h in range(ATTN_HEADS):
                far = relb_ref[N_BUCKETS - 1, h]
                acc = jnp.zeros((BLK, tq), F32)
                for n in range(N_BUCKETS - 1):
                    acc = jnp.where(bk == n, relb_ref[n, h] - far, acc)
                bias_s[d, h // rep, :, (h % rep) * BLK:(h % rep + 1) * BLK] = acc * LOG2E
        bias_s[2] = jnp.zeros((ATTN_KV_HEADS, BLK, gw), F32)
        for w in range(32):
            planes[w] = jnp.zeros(planes.shape[1:], i32)

    row_c = lax.broadcasted_iota(i32, (ch, tq), 0)
    lane_c = lax.broadcasted_iota(i32, (ch, tq), 1)

    def chunk_start(c):
        return pl.multiple_of(c * ch, ch)

    def n_chunks(blk):
        return (blk + ch_blks) // ch_blks

    def load_indexer(qidx_ref, misc_ref):
        qi_t = jnp.transpose(qidx_ref[0].astype(F32))
        w_t = jnp.transpose(misc_ref[0])[MISC_WIDX:MISC_WIDX + IDX_HEADS, :] * (
            IDX_HEADS ** -0.5 * IDX_DIM ** -0.5)
        for h in range(IDX_HEADS):
            qicat[:, h * BLK:(h + 1) * BLK] = qi_t[h * IDX_DIM:(h + 1) * IDX_DIM, :].astype(BF16)
            wcat[:, h * BLK:(h + 1) * BLK] = jnp.broadcast_to(w_t[h:h + 1, :], (8, BLK))

    def score_chunk(c, blk, slot):
        s0 = chunk_start(c)
        kb = kidx_ref[0, pl.ds(s0, ch), MISC_KIDX:MISC_KIDX + IDX_DIM]
        s = jnp.dot(kb, qicat[...], preferred_element_type=F32)
        r = jnp.maximum(s, 0.0) * wcat[0:1, :]
        acc = r[:, 0:BLK]
        for h in range(1, IDX_HEADS):
            acc = acc + r[:, h * BLK:(h + 1) * BLK]
        acc = jnp.where(s0 + row_c <= blk * BLK + lane_c, acc, -jnp.inf)
        keys[slot, pl.ds(s0, ch), :] = _to_key(acc)

    def plane_chunk(c, slot):
        ku = (keys[slot, pl.ds(chunk_start(c), ch), :] ^ i32(INT_MIN)).reshape(ch // (32 * 8), 32, 8, tq)
        words = _transpose32([ku[:, j] for j in range(32)])
        r0 = pl.multiple_of(c * rows_per_chunk, rows_per_chunk)
        for w in range(32):
            planes[w, pl.ds(r0, rows_per_chunk), :] = words[w].reshape(rows_per_chunk, tq)

    def select(blk, slot):
        nch = n_chunks(blk)
        nrows = live.shape[0]
        live[...] = jnp.where(lax.broadcasted_iota(i32, (nrows, tq), 0) < nch * rows_per_chunk, i32(-1), i32(0))

        def ones_in(words):
            return jnp.sum(lax.population_count(words), axis=0, keepdims=True)

        def bit_body(w, st):
            thr_u, n_above, n_set = st
            take = n_above + n_set >= topk
            thr_u = jnp.where(take, thr_u | lax.shift_left(i32(1), i32(31) - w), thr_u)
            n_above = jnp.where(take, n_above, n_above + n_set)
            still = live[...] & (planes[w] ^ jnp.where(take, i32(0), i32(-1)))
            live[...] = still
            return thr_u, n_above, ones_in(still & planes[jnp.minimum(w + 1, 31)])

        zero = jnp.zeros((1, tq), i32)
        thr_u, n_gt, _ = lax.fori_loop(0, 32, bit_body, (zero, zero, ones_in(live[...] & planes[0])))
        thr = thr_u ^ i32(INT_MIN)
        thr_s[slot] = thr
        n_eq = ones_in(live[...])
        need = topk - n_gt

        @pl.when(jnp.max((n_eq > need).astype(i32)) > 0)
        def _ties():
            def count(pred):
                def body(c, cnt):
                    s0 = chunk_start(c)
                    hit = pred(keys[slot, pl.ds(s0, ch), :], s0).astype(i32)
                    n = ch
                    while n > 8:
                        n //= 2
                        hit = hit[:n] + hit[n:2 * n]
                    return cnt + hit
                cnt = lax.fori_loop(0, nch, body, jnp.zeros((8, tq), i32))
                return jnp.sum(cnt, axis=0, keepdims=True)

            def jbody(it, p):
                cand = p + lax.shift_left(i32(1), i32(idx_bits - 1) - it)
                c = count(lambda kk, s0: (kk == thr) & (s0 + row_c < cand))
                return jnp.where(c < need, cand, p)
            last = lax.fori_loop(0, idx_bits, jbody, jnp.zeros((1, tq), i32))

            def demote(c, carry):
                s0 = chunk_start(c)
                kk = keys[slot, pl.ds(s0, ch), :]
                keys[slot, pl.ds(s0, ch), :] = jnp.where((kk == thr) & (s0 + row_c > last), INT_MIN, kk)
                return carry
            lax.fori_loop(0, nch, demote, 0)

    @pl.when(qi == 0)
    def _first_block():
        ones_tile = jnp.where(lax.broadcasted_iota(i32, (V_ROWS - ATTN_HEAD_DIM, BLK), 0) == 0, 1.0, 0.0)

        def value_block(j, carry):
            v_t = jnp.transpose(v_ref[0, pl.ds(pl.multiple_of(j * BLK, BLK), BLK), :].astype(F32))
            for g in range(ATTN_KV_HEADS):
                vt_s[j, g, 0:ATTN_HEAD_DIM, :] = v_t[g * ATTN_HEAD_DIM:(g + 1) * ATTN_HEAD_DIM, :].astype(BF16)
                vt_s[j, g, ATTN_HEAD_DIM:, :] = ones_tile.astype(BF16)
            return carry
        lax.fori_loop(0, nq, value_block, 0)

        load_indexer(qic_ref, miscc_ref)
        score_chunk(0, 0, 0)
        plane_chunk(0, 0)
        select(0, 0)

    slot_b = qi % 2
    slot_a = 1 - slot_b
    blk_a = jnp.minimum(qi + 1, nq - 1)
    nch_b = n_chunks(qi)
    nch_a = n_chunks(blk_a)
    load_indexer(qin_ref, miscn_ref)
    thr = thr_s[slot_b]

    q_t = jnp.transpose(q_ref[0].astype(F32))
    for h in range(ATTN_HEADS):
        qcat[h // rep, 0:ATTN_HEAD_DIM, (h % rep) * BLK:(h % rep + 1) * BLK] = (
            q_t[h * ATTN_HEAD_DIM:(h + 1) * ATTN_HEAD_DIM, :].astype(BF16))
    for g in range(ATTN_KV_HEADS):
        qcat[g, ATTN_HEAD_DIM:, :] = jnp.zeros((K_AUG - ATTN_HEAD_DIM, gw), BF16)

    def pen_of(s0, causal):
        sel = keys[slot_b, pl.ds(s0, ch), :] >= thr
        if causal:
            sel = sel & (s0 + row_c <= qi * BLK + lane_c)
        pen = jnp.where(sel, 0.0, NEG)
        return jnp.concatenate([pen] * rep, axis=1)

    def near_bias(c, g):
        return jnp.concatenate(
            [bias_s[jnp.clip(qi - (c * ch_blks + u), 0, 2), g] for u in range(ch_blks)], axis=0)

    def vt_chunk(c, g):
        return jnp.concatenate([vt_s[c * ch_blks + u, g] for u in range(ch_blks)], axis=1)

    s0d = pl.multiple_of(qi * BLK, BLK)
    kd = k_ref[0, pl.ds(s0d, BLK), :]
    tri = lax.broadcasted_iota(i32, (BLK, tq), 0) <= lax.broadcasted_iota(i32, (BLK, tq), 1)
    tri_pen = jnp.concatenate([jnp.where(tri, 0.0, NEG)] * rep, axis=1)
    for g in range(ATTN_KV_HEADS):
        sd = jnp.dot(kd[:, g * K_AUG:(g + 1) * K_AUG], qcat[g],
                     preferred_element_type=F32) + bias_s[0, g] + tri_pen
        shift = jnp.max(sd, axis=0, keepdims=True)
        first = lax.broadcasted_iota(i32, (SHIFT_ROWS, gw), 0) == 0
        qcat[g, ATTN_HEAD_DIM:ATTN_HEAD_DIM + SHIFT_ROWS, :] = jnp.where(first, -shift, 0.0).astype(BF16)
        acc_s[g] = jnp.zeros((V_ROWS, gw), F32)
        p_s[1, g] = jnp.zeros((ch, gw), BF16)

    def weigh_values(c, slot):
        for g in range(ATTN_KV_HEADS):
            acc_s[g] += jnp.dot(vt_chunk(c, g), p_s[slot, g], preferred_element_type=F32)

    def att_chunk(c, near, slots=None):
        prev_slot, cur_slot = slots if slots is not None else ((c + 1) % 2, c % 2)
        weigh_values(jnp.maximum(c - 1, 0), prev_slot)
        s0 = chunk_start(c)
        pen = pen_of(s0, near)
        kblk = k_ref[0, pl.ds(s0, ch), :]
        for g in range(ATTN_KV_HEADS):
            s = jnp.dot(kblk[:, g * K_AUG:(g + 1) * K_AUG], qcat[g],
                        preferred_element_type=F32)
            if near:
                s = s + near_bias(c, g)
            p_s[cur_slot, g] = jnp.exp2(s + pen).astype(BF16)

    def fused(c, near, slots=None):
        plane_chunk(c, slot_a)
        att_chunk(c, near, slots)
        score_chunk(c + 1, blk_a, slot_a)

    def far_pair(j, carry):
        fused(2 * j, False, (1, 0))
        fused(2 * j + 1, False, (0, 1))
        return carry

    def far_body(c, carry):
        fused(c, False, (1, 0))
        return carry

    def near_body(c, carry):
        fused(c, True)
        return carry

    score_chunk(0, blk_a, slot_a)
    n_far = jnp.maximum(qi - 1, 0) // ch_blks
    lax.fori_loop(0, n_far // 2, far_pair, 0)
    lax.fori_loop((n_far // 2) * 2, n_far, far_body, 0)
    lax.fori_loop(n_far, nch_b - 1, near_body, 0)
    plane_chunk(nch_b - 1, slot_a)
    att_chunk(nch_b - 1, True)
    weigh_values(nch_b - 1, (nch_b - 1) % 2)

    @pl.when(nch_a > nch_b)
    def _one_more_chunk():
        score_chunk(nch_b, blk_a, slot_a)
        plane_chunk(nch_b, slot_a)

    select(blk_a, slot_a)

    n_ok = jnp.zeros((1, gw), i32)
    for g in range(ATTN_KV_HEADS):
        a = acc_s[g]
        den = a[ATTN_HEAD_DIM:ATTN_HEAD_DIM + 1, :]
        top = jnp.max(jnp.abs(a[0:ATTN_HEAD_DIM, :]), axis=0, keepdims=True)
        n_ok = n_ok + ((den > DEN_LO) & (den < DEN_HI) & (top < F32_HUGE)).astype(i32)

    @pl.when(jnp.min(n_ok) < ATTN_KV_HEADS)
    def _exact():
        for g in range(ATTN_KV_HEADS):
            qcat[g, ATTN_HEAD_DIM:ATTN_HEAD_DIM + SHIFT_ROWS, :] = jnp.zeros((SHIFT_ROWS, gw), BF16)
            m_s[g] = jnp.full((1, gw), NEG, F32)
            acc_s[g] = jnp.zeros((V_ROWS, gw), F32)

        def body(c, carry):
            s0 = chunk_start(c)
            pen = pen_of(s0, True)
            kblk = k_ref[0, pl.ds(s0, ch), :]
            for g in range(ATTN_KV_HEADS):
                s = jnp.dot(kblk[:, g * K_AUG:(g + 1) * K_AUG], qcat[g],
                            preferred_element_type=F32) + near_bias(c, g) + pen
                m_old = m_s[g]
                m_new = jnp.maximum(m_old, jnp.max(s, axis=0, keepdims=True))
                p = jnp.exp2(s - m_new).astype(BF16)
                acc_s[g] = acc_s[g] * jnp.exp2(m_old - m_new) + jnp.dot(
                    vt_chunk(c, g), p, preferred_element_type=F32)
                m_s[g] = m_new
            return carry
        lax.fori_loop(0, nch_b, body, 0)

    parts = []
    for g in range(ATTN_KV_HEADS):
        a = acc_s[g]
        o = a[0:ATTN_HEAD_DIM, :] * (1.0 / a[ATTN_HEAD_DIM:ATTN_HEAD_DIM + 1, :])
        parts += [o[:, r * tq:(r + 1) * tq] for r in range(rep)]
    out_t = jnp.concatenate(parts, axis=0)
    o_ref[0] = _rms(jnp.transpose(out_t), nw_ref[...]).astype(BF16)


def _dsa(rel_bias, q, qi, misc, k, v, miscb, nw):
    bsz, t, _ = q.shape
    nq = t // BLK
    topk = min(TOPK_MAX, t // 4)
    idx_bits = max(1, (t - 1).bit_length())
    gw = (ATTN_HEADS // ATTN_KV_HEADS) * BLK
    bkt = jnp.asarray(_bucket_tiles())
    kern = functools.partial(_dsa_kernel, topk=topk, idx_bits=idx_bits, nq=nq)

    def nxt(b, i):
        return (b, jnp.minimum(i + 1, nq - 1), 0)

    return pl.pallas_call(
        kern,
        grid=(bsz, nq),
        in_specs=[pl.BlockSpec(memory_space=pltpu.SMEM),
                  pl.BlockSpec((2, BLK, BLK), lambda b, i: (0, 0, 0)),
                  pl.BlockSpec((1, BLK, D_ATTN), lambda b, i: (b, i, 0)),
                  pl.BlockSpec((1, BLK, IDX_HEADS * IDX_DIM), lambda b, i: (b, i, 0)),
                  pl.BlockSpec((1, BLK, MISC_W), lambda b, i: (b, i, 0)),
                  pl.BlockSpec((1, BLK, IDX_HEADS * IDX_DIM), nxt),
                  pl.BlockSpec((1, BLK, MISC_W), nxt),
                  pl.BlockSpec((1, t, D_K_AUG), lambda b, i: (b, 0, 0)),
                  pl.BlockSpec((1, t, D_KV), lambda b, i: (b, 0, 0)),
                  pl.BlockSpec((1, t, MISC_W), lambda b, i: (b, 0, 0)),
                  pl.BlockSpec((1, D_ATTN), lambda b, i: (0, 0))],
        out_specs=pl.BlockSpec((1, BLK, D_ATTN), lambda b, i: (b, i, 0)),
        out_shape=jax.ShapeDtypeStruct((bsz, t, D_ATTN), BF16),
        scratch_shapes=[pltpu.VMEM((2, t, BLK), jnp.int32),
                        pltpu.VMEM((32, t // 32, BLK), jnp.int32),
                        pltpu.VMEM((t // 32, BLK), jnp.int32),
                        pltpu.VMEM((2, 1, BLK), jnp.int32),
                        pltpu.VMEM((3, ATTN_KV_HEADS, BLK, gw), F32),
                        pltpu.VMEM((ATTN_KV_HEADS, K_AUG, gw), BF16),
                        pltpu.VMEM((IDX_DIM, IDX_HEADS * BLK), BF16),
                        pltpu.VMEM((8, IDX_HEADS * BLK), F32),
                        pltpu.VMEM((ATTN_KV_HEADS, 1, gw), F32),
                        pltpu.VMEM((ATTN_KV_HEADS, V_ROWS, gw), F32),
                        pltpu.VMEM((2, ATTN_KV_HEADS, DSA_CHUNK, gw), BF16),
                        pltpu.VMEM((nq, ATTN_KV_HEADS, V_ROWS, BLK), BF16)],
        compiler_params=pltpu.CompilerParams(
            dimension_semantics=("arbitrary", "arbitrary"), vmem_limit_bytes=48 * 2 ** 20),
        name="dsa",
    )(rel_bias, bkt, q, qi, misc, qi, misc, k, v, miscb, nw)


def _tail_kernel(x_ref, ys_ref, ya_ref, mod_ref, wo_ref, n1_ref, n2_ref, n3_ref, w1_ref, w2_ref, o_ref):
    x = x_ref[0]
    mix = (jnp.dot(ys_ref[0], wo_ref[0:D_SSM, :], preferred_element_type=F32)
           + jnp.dot(ya_ref[0], wo_ref[D_SSM:, :], preferred_element_type=F32))
    x1 = x + mod_ref[0, 2:3, :] * _rms(mix, n1_ref[...])
    h = (_rms(x1, n2_ref[...]) * (1.0 + mod_ref[0, 4:5, :]) + mod_ref[0, 3:4, :]).astype(BF16)
    f = jnp.zeros(x.shape, F32)
    step = D_MODEL
    for c0 in range(0, D_FF, step):
        a = jnp.maximum(jnp.dot(h, w1_ref[:, c0:c0 + step], preferred_element_type=F32), 0.0)
        f = f + jnp.dot((a * a).astype(BF16), w2_ref[c0:c0 + step, :], preferred_element_type=F32)
    o_ref[0] = x1 + mod_ref[0, 5:6, :] * _rms(f, n3_ref[...])


def _tail(x, ys, ya, mod, wo, n1, n2, n3, w1, w2):
    bsz, t, d = x.shape
    tm = min(ROW_TILE, t)

    def const(shape):
        return pl.BlockSpec(shape, lambda b, i: tuple(0 for _ in shape))

    return pl.pallas_call(
        _tail_kernel,
        grid=(bsz, t // tm),
        in_specs=[pl.BlockSpec((1, tm, d), lambda b, i: (b, i, 0)),
                  pl.BlockSpec((1, tm, D_SSM), lambda b, i: (b, i, 0)),
                  pl.BlockSpec((1, tm, D_ATTN), lambda b, i: (b, i, 0)),
                  pl.BlockSpec((1, 6, d), lambda b, i: (b, 0, 0)),
                  const((D_SSM + D_ATTN, d)), const((1, d)), const((1, d)), const((1, d)),
                  const((d, D_FF)), const((D_FF, d))],
        out_specs=pl.BlockSpec((1, tm, d), lambda b, i: (b, i, 0)),
        out_shape=jax.ShapeDtypeStruct((bsz, t, d), F32),
        compiler_params=pltpu.CompilerParams(
            dimension_semantics=("arbitrary", "arbitrary"), vmem_limit_bytes=60 * 2 ** 20),
        name="tail",
    )(x, ys, ya, mod, wo, n1, n2, n3, w1, w2)


def _permute_w_in(w):
    sizes = [D_SSM, D_XBC, SSM_HEADS, D_ATTN, D_KV, D_KV, IDX_HEADS * IDX_DIM, IDX_DIM, IDX_HEADS]
    pts = np.cumsum(sizes)[:-1].tolist()
    z, xbc, dt, q, k, v, qidx, kidx, widx = jnp.split(w, pts, axis=-1)
    pad = jnp.zeros((w.shape[0], MISC_W - IDX_DIM - IDX_HEADS - SSM_HEADS), w.dtype)
    kpad = jnp.zeros((w.shape[0], K_AUG - ATTN_HEAD_DIM), w.dtype)
    k_aug = []
    for g in range(ATTN_KV_HEADS):
        k_aug += [k[:, g * ATTN_HEAD_DIM:(g + 1) * ATTN_HEAD_DIM], kpad]
    return jnp.concatenate([z, xbc, q] + k_aug + [v, qidx, kidx, widx, dt, pad], axis=-1).astype(BF16)


def _layer(x, c_pad, w_ada, b_ada, norm1_pre, norm1_post, w_in, conv_w, conv_b, dt_bias, a_log, d_skip,
           ssm_norm, rel_bias, attn_norm, w_out, norm2_pre, norm2_post, w_mlp_in, w_mlp_out):
    bsz, t, d = x.shape
    mod = _ada(c_pad, w_ada, b_ada[None, :])[:bsz].reshape(bsz, 6, d)
    z, xbc, q, k, v, qi, misc, miscb = _inproj(x, mod, norm1_pre[None, :], _permute_w_in(w_in))

    lane = jnp.arange(MISC_W)
    is_dt = (lane >= MISC_DT) & (lane < MISC_DT + SSM_HEADS)
    head_of = jnp.clip(lane - MISC_DT, 0, SSM_HEADS - 1)
    dtb_t = jnp.where(is_dt, dt_bias[head_of], 0.0)[None, :]
    a_t = jnp.where(is_dt, -jnp.exp(a_log.astype(F32))[head_of], 0.0)[None, :]
    expand = ((lane[:, None] - MISC_DT) == (jnp.arange(D_SSM)[None, :] // SSM_HEAD_DIM)).astype(BF16)
    dskip_e = jnp.repeat(d_skip.astype(F32), SSM_HEAD_DIM)[None, :]
    y_ssd = _ssd(xbc, z, misc, conv_w, conv_b[None, :], dtb_t, a_t, expand, dskip_e, ssm_norm[None, :])

    y_att = _dsa(rel_bias, q, qi, misc, k, v, miscb, attn_norm[None, :])

    return _tail(x, y_ssd, y_att, mod, w_out.astype(BF16), norm1_post[None, :], norm2_pre[None, :],
                 norm2_post[None, :], w_mlp_in.astype(BF16), w_mlp_out.astype(BF16))


def kernel(x, c, w_ada, b_ada, norm1_pre, norm1_post, w_in, conv_w, conv_b, dt_bias, a_log, d_skip,
           ssm_norm, rel_bias, attn_norm, w_out, norm2_pre, norm2_post, w_mlp_in, w_mlp_out):
    bsz = x.shape[0]
    assert bsz <= 8 and x.shape[1] % DSA_CHUNK == 0 and x.shape[2] == D_MODEL
    c_pad = jnp.zeros((8, D_MODEL), F32).at[:bsz].set(c)
    for l in range(w_ada.shape[0]):
        x = _layer(x, c_pad, w_ada[l], b_ada[l], norm1_pre[l], norm1_post[l], w_in[l], conv_w[l], conv_b[l],
                   dt_bias[l], a_log[l], d_skip[l], ssm_norm[l], rel_bias, attn_norm[l], w_out[l],
                   norm2_pre[l], norm2_post[l], w_mlp_in[l], w_mlp_out[l])
    return x
```

```python
import functools
import math

import numpy as np
import jax
import jax.numpy as jnp
from jax import lax
from jax.experimental import pallas as pl
from jax.experimental.pallas import tpu as pltpu

D_MODEL = 1024
SSM_HEADS = 8
SSM_HEAD_DIM = 64
D_SSM = SSM_HEADS * SSM_HEAD_DIM
SSM_GROUPS = 2
SSM_STATE = 128
CONV_WIDTH = 4
CHUNK = 256
ATTN_HEADS = 8
ATTN_KV_HEADS = 2
ATTN_HEAD_DIM = 64
D_ATTN = ATTN_HEADS * ATTN_HEAD_DIM
D_KV = ATTN_KV_HEADS * ATTN_HEAD_DIM
IDX_HEADS = 8
IDX_DIM = 64
TOPK_MAX = 256
N_BUCKETS = 32
MAX_DISTANCE = 128
D_FF = 4 * D_MODEL
D_BC = SSM_GROUPS * SSM_STATE
D_XBC = D_SSM + 2 * D_BC
EPS = 1e-6

LANES = 128
MISC_W = LANES
MISC_KIDX = 0
MISC_WIDX = IDX_DIM
MISC_DT = IDX_DIM + IDX_HEADS
K_AUG = LANES
D_K_AUG = ATTN_KV_HEADS * K_AUG
SHIFT_ROWS = 16
OFF_Z = 0
OFF_XBC = OFF_Z + D_SSM
OFF_Q = OFF_XBC + D_XBC
OFF_K = OFF_Q + D_ATTN
OFF_V = OFF_K + D_K_AUG
OFF_QI = OFF_V + D_KV
OFF_MISC = OFF_QI + IDX_HEADS * IDX_DIM
D_IN_PAD = OFF_MISC + MISC_W

BLK = 128
DSA_CHUNK = 512
V_ROWS = 80
SEL_SIZES = 4
DEN_LO = 2.0 ** -80
DEN_HI = 2.0 ** 80
F32_HUGE = 3.0e38
LOG2E = math.log2(math.e)
ROW_TILE = 512
NEG = -1e30
INT_MIN = -2 ** 31

F32 = jnp.float32
BF16 = jnp.bfloat16
HI = lax.Precision.HIGHEST
NT = (((1,), (1,)), ((), ()))


def _silu(x):
    return x / (1.0 + jnp.exp(-x))


def _rms(x, w):
    return x * lax.rsqrt(jnp.mean(x * x, axis=-1, keepdims=True) + EPS) * w


def _split3(x):
    hi = x.astype(BF16)
    rest = x - hi.astype(F32)
    mid = rest.astype(BF16)
    return hi, mid, (rest - mid.astype(F32)).astype(BF16)


def _ada_kernel(c_ref, w_ref, b_ref, o_ref):
    s = _silu(c_ref[...])
    o_ref[...] = jnp.dot(s, w_ref[...], precision=HI, preferred_element_type=F32) + b_ref[...]


def _ada(c_pad, w, b):
    n = w.shape[1]
    tn = D_MODEL
    return pl.pallas_call(
        _ada_kernel,
        grid=(n // tn,),
        in_specs=[pl.BlockSpec((8, D_MODEL), lambda j: (0, 0)),
                  pl.BlockSpec((D_MODEL, tn), lambda j: (0, j)),
                  pl.BlockSpec((1, tn), lambda j: (0, j))],
        out_specs=pl.BlockSpec((8, tn), lambda j: (0, j)),
        out_shape=jax.ShapeDtypeStruct((8, n), F32),
        name="ada",
    )(c_pad, w, b)


def _inproj_kernel(x_ref, mod_ref, nw_ref, w_ref,
                   z_ref, xbc_ref, q_ref, k_ref, v_ref, qi_ref, misc_ref, miscb_ref):
    x = x_ref[0]
    h = _rms(x, nw_ref[...]) * (1.0 + mod_ref[0, 1:2, :]) + mod_ref[0, 0:1, :]
    hb = h.astype(BF16)

    def seg(lo, width):
        return jnp.dot(hb, w_ref[:, lo:lo + width], preferred_element_type=F32)

    z_ref[0] = seg(OFF_Z, D_SSM)
    xbc_ref[0] = seg(OFF_XBC, D_XBC)
    q_ref[0] = (seg(OFF_Q, D_ATTN) * (ATTN_HEAD_DIM ** -0.5 * LOG2E)).astype(BF16)
    ones_col = lax.broadcasted_iota(jnp.int32, (1, D_K_AUG), 1) % K_AUG == ATTN_HEAD_DIM
    k_ref[0] = jnp.where(ones_col, 1.0, seg(OFF_K, D_K_AUG)).astype(BF16)
    v_ref[0] = seg(OFF_V, D_KV).astype(BF16)
    qi_ref[0] = seg(OFF_QI, IDX_HEADS * IDX_DIM).astype(BF16)
    misc = seg(OFF_MISC, MISC_W)
    misc_ref[0] = misc
    miscb_ref[0] = misc.astype(BF16)


def _inproj(x, mod, nw, w_perm):
    bsz, t, d = x.shape
    tm = min(ROW_TILE, t)

    def tok(width, dtype):
        return (pl.BlockSpec((1, tm, width), lambda b, i: (b, i, 0)),
                jax.ShapeDtypeStruct((bsz, t, width), dtype))

    outs = [tok(D_SSM, F32), tok(D_XBC, F32), tok(D_ATTN, BF16), tok(D_K_AUG, BF16), tok(D_KV, BF16),
            tok(IDX_HEADS * IDX_DIM, BF16), tok(MISC_W, F32), tok(MISC_W, BF16)]
    return pl.pallas_call(
        _inproj_kernel,
        grid=(bsz, t // tm),
        in_specs=[pl.BlockSpec((1, tm, d), lambda b, i: (b, i, 0)),
                  pl.BlockSpec((1, 6, d), lambda b, i: (b, 0, 0)),
                  pl.BlockSpec((1, d), lambda b, i: (0, 0)),
                  pl.BlockSpec((d, D_IN_PAD), lambda b, i: (0, 0))],
        out_specs=[o[0] for o in outs],
        out_shape=[o[1] for o in outs],
        compiler_params=pltpu.CompilerParams(
            dimension_semantics=("arbitrary", "arbitrary"), vmem_limit_bytes=48 * 2 ** 20),
        name="inproj",
    )(x, mod, nw, w_perm)


def _ssd_kernel(xbc_ref, z_ref, misc_ref, cw_ref, cb_ref, dtb_ref, a_ref, e_ref, dskip_ref, nw_ref,
                o_ref, xbuf, state):
    c = pl.program_id(1)
    lc = xbc_ref.shape[1]

    @pl.when(c == 0)
    def _():
        xbuf[0:8, :] = jnp.zeros((8, D_XBC), F32)
        state[...] = jnp.zeros(state.shape, F32)

    @pl.when(c > 0)
    def _():
        xbuf[0:8, :] = xbuf[lc:lc + 8, :]

    xbuf[8:lc + 8, :] = xbc_ref[0]
    u = cb_ref[...]
    for k in range(CONV_WIDTH):
        u = u + xbuf[pl.ds(8 - (CONV_WIDTH - 1) + k, lc), :] * cw_ref[k:k + 1, :]
    u = _silu(u)
    xs = u[:, :D_SSM]
    bm = u[:, D_SSM:D_SSM + D_BC]
    cm = u[:, D_SSM + D_BC:]

    raw = misc_ref[0] + dtb_ref[...]
    dt_t = jnp.maximum(raw, 0.0) + jnp.log1p(jnp.exp(-jnp.abs(raw)))
    adt_t = dt_t * a_ref[...]
    row = lax.broadcasted_iota(jnp.int32, (lc, lc), 0)
    col = lax.broadcasted_iota(jnp.int32, (lc, lc), 1)
    causal = col <= row
    tril = jnp.where(causal, 1.0, 0.0).astype(BF16)
    expand = e_ref[...]
    acs_t = sum(jnp.dot(tril, piece, preferred_element_type=F32) for piece in _split3(adt_t))
    dt_e = sum(jnp.dot(piece, expand, preferred_element_type=F32) for piece in _split3(dt_t))
    acs_e = sum(jnp.dot(piece, expand, preferred_element_type=F32) for piece in _split3(acs_t))
    acs_row = jnp.transpose(acs_t)

    xdt = xs * dt_e
    a_last = acs_e[lc - 1:lc, :]
    xdec = (xdt * jnp.exp(a_last - acs_e)).astype(BF16)
    xdt_b = xdt.astype(BF16)
    bm_b = bm.astype(BF16)
    cm_b = cm.astype(BF16)
    bm_t = jnp.transpose(bm).astype(BF16)

    heads_per_group = SSM_HEADS // SSM_GROUPS
    gw = heads_per_group * SSM_HEAD_DIM
    y_parts = []
    off_parts = []
    for g in range(SSM_GROUPS):
        bg = bm_b[:, g * SSM_STATE:(g + 1) * SSM_STATE]
        cg = cm_b[:, g * SSM_STATE:(g + 1) * SSM_STATE]
        cb = lax.dot_general(cg, bg, NT, preferred_element_type=F32)
        for r in range(heads_per_group):
            hd = g * heads_per_group + r
            a_col = acs_t[:, MISC_DT + hd:MISC_DT + hd + 1]
            a_row = acs_row[MISC_DT + hd:MISC_DT + hd + 1, :]
            lmat = jnp.exp(jnp.where(causal, a_col - a_row, -jnp.inf))
            mh = (cb * lmat).astype(BF16)
            y_parts.append(jnp.dot(mh, xdt_b[:, hd * SSM_HEAD_DIM:(hd + 1) * SSM_HEAD_DIM],
                                   preferred_element_type=F32))
        prev = state[g]
        off_parts.append(jnp.dot(cg, prev.astype(BF16), preferred_element_type=F32))
        st = jnp.dot(bm_t[g * SSM_STATE:(g + 1) * SSM_STATE, :], xdec[:, g * gw:(g + 1) * gw],
                     preferred_element_type=F32)
        state[g] = prev * jnp.exp(a_last[:, g * gw:(g + 1) * gw]) + st
    y = (jnp.concatenate(y_parts, axis=1)
         + jnp.concatenate(off_parts, axis=1) * jnp.exp(acs_e)
         + xs * dskip_ref[...])
    o_ref[0] = _rms(y * _silu(z_ref[0]), nw_ref[...]).astype(BF16)


def _ssd(xbc, z, misc, cw, cb, dtb_t, a_t, expand, dskip_e, nw):
    bsz, t, _ = xbc.shape
    lc = math.gcd(t, CHUNK)
    nc = t // lc
    gw = (SSM_HEADS // SSM_GROUPS) * SSM_HEAD_DIM

    def const(shape):
        return pl.BlockSpec(shape, lambda b, c: tuple(0 for _ in shape))

    return pl.pallas_call(
        _ssd_kernel,
        grid=(bsz, nc),
        in_specs=[pl.BlockSpec((1, lc, D_XBC), lambda b, c: (b, c, 0)),
                  pl.BlockSpec((1, lc, D_SSM), lambda b, c: (b, c, 0)),
                  pl.BlockSpec((1, lc, MISC_W), lambda b, c: (b, c, 0)),
                  const((CONV_WIDTH, D_XBC)), const((1, D_XBC)), const((1, MISC_W)), const((1, MISC_W)),
                  const((MISC_W, D_SSM)), const((1, D_SSM)), const((1, D_SSM))],
        out_specs=pl.BlockSpec((1, lc, D_SSM), lambda b, c: (b, c, 0)),
        out_shape=jax.ShapeDtypeStruct((bsz, t, D_SSM), BF16),
        scratch_shapes=[pltpu.VMEM((lc + 8, D_XBC), F32),
                        pltpu.VMEM((SSM_GROUPS, SSM_STATE, gw), F32)],
        compiler_params=pltpu.CompilerParams(
            dimension_semantics=("arbitrary", "arbitrary"), vmem_limit_bytes=48 * 2 ** 20),
        name="ssd",
    )(xbc, z, misc, cw, cb, dtb_t, a_t, expand, dskip_e, nw)


def _bucket_tiles():
    s = np.arange(BLK)[:, None]
    t = np.arange(BLK)[None, :]
    max_exact = N_BUCKETS // 2
    tiles = []
    for d0 in (0, BLK):
        n = np.maximum(d0 + t - s, 0)
        large = {}
        for dt_ in (np.float32, np.float64):
            nf = np.maximum(n, 1).astype(dt_)
            lg = max_exact + (np.log(nf / dt_(max_exact)) / dt_(math.log(MAX_DISTANCE / max_exact))
                              * dt_(N_BUCKETS - max_exact)).astype(np.int32)
            large[dt_] = np.minimum(lg, N_BUCKETS - 1)
        assert (large[np.float32] == large[np.float64]).all()
        tiles.append(np.where(n < max_exact, n, large[np.float64]).astype(np.int32))
    assert BLK + 1 >= MAX_DISTANCE
    return np.stack(tiles)


def _to_key(x):
    bits = lax.bitcast_convert_type(x, jnp.int32)
    key = bits ^ (lax.shift_right_arithmetic(bits, 31) & jnp.int32(0x7FFFFFFF))
    return jnp.where(key == -1, 0, key)


def _transpose32(rows):
    rows = list(rows)
    mask, j = 0x0000FFFF, 16
    while j:
        k = 0
        while k < 32:
            t = (rows[k] ^ lax.shift_right_logical(rows[k + j], jnp.int32(j))) & jnp.int32(mask)
            rows[k] = rows[k] ^ t
            rows[k + j] = rows[k + j] ^ lax.shift_left(t, jnp.int32(j))
            k = (k + j + 1) & ~j
        j >>= 1
        mask ^= (mask << j) & 0xFFFFFFFF
    return rows


def _dsa_kernel(relb_ref, bkt_ref, q_ref, qic_ref, miscc_ref, qin_ref, miscn_ref, k_ref, v_ref, kidx_ref,
                nw_ref, o_ref, keys, planes, live, thr_s, cnt_s, bias_s, qcat, qicat, wcat, m_s, acc_s, p_s, vt_s,
                *, topk, idx_bits, nq):
    b = pl.program_id(0)
    qi = pl.program_id(1)
    tq = BLK
    i32 = jnp.int32
    rep = ATTN_HEADS // ATTN_KV_HEADS
    gw = rep * tq
    ch = DSA_CHUNK
    ch_blks = ch // BLK
    rows_per_chunk = ch // 32

    @pl.when((b == 0) & (qi == 0))
    def _init_bias():
        for d in range(2):
            bk = bkt_ref[d]
            for h in range(ATTN_HEADS):
                far = relb_ref[N_BUCKETS - 1, h]
                acc = jnp.zeros((BLK, tq), F32)
                for n in range(N_BUCKETS - 1):
                    acc = jnp.where(bk == n, relb_ref[n, h] - far, acc)
                bias_s[d, h // rep, :, (h % rep) * BLK:(h % rep + 1) * BLK] = acc * LOG2E
        bias_s[2] = jnp.zeros((ATTN_KV_HEADS, BLK, gw), F32)
        for w in range(32):
            planes[w] = jnp.zeros(planes.shape[1:], i32)

    row_c = lax.broadcasted_iota(i32, (ch, tq), 0)
    lane_c = lax.broadcasted_iota(i32, (ch, tq), 1)

    def chunk_start(c):
        return pl.multiple_of(c * ch, ch)

    def n_chunks(blk):
        return (blk + ch_blks) // ch_blks

    def load_indexer(qidx_ref, misc_ref):
        qi_t = jnp.transpose(qidx_ref[0].astype(F32))
        w_t = jnp.transpose(misc_ref[0])[MISC_WIDX:MISC_WIDX + IDX_HEADS, :] * (
            IDX_HEADS ** -0.5 * IDX_DIM ** -0.5)
        for h in range(IDX_HEADS):
            qicat[:, h * BLK:(h + 1) * BLK] = qi_t[h * IDX_DIM:(h + 1) * IDX_DIM, :].astype(BF16)
            wcat[:, h * BLK:(h + 1) * BLK] = jnp.broadcast_to(w_t[h:h + 1, :], (8, BLK))

    def score_chunk(c, blk, slot):
        s0 = chunk_start(c)
        kb = kidx_ref[0, pl.ds(s0, ch), MISC_KIDX:MISC_KIDX + IDX_DIM]
        s = jnp.dot(kb, qicat[...], preferred_element_type=F32)
        r = jnp.maximum(s, 0.0) * wcat[0:1, :]
        acc = r[:, 0:BLK]
        for h in range(1, IDX_HEADS):
            acc = acc + r[:, h * BLK:(h + 1) * BLK]
        acc = jnp.where(s0 + row_c <= blk * BLK + lane_c, acc, -jnp.inf)
        keys[slot, pl.ds(s0, ch), :] = _to_key(acc)

    def plane_chunk(c, slot):
        ku = (keys[slot, pl.ds(chunk_start(c), ch), :] ^ i32(INT_MIN)).reshape(ch // (32 * 8), 32, 8, tq)
        words = _transpose32([ku[:, j] for j in range(32)])
        r0 = pl.multiple_of(c * rows_per_chunk, rows_per_chunk)
        for w in range(32):
            planes[w, pl.ds(r0, rows_per_chunk), :] = words[w].reshape(rows_per_chunk, tq)

    def select(blk, slot):
        nch = n_chunks(blk)
        used = nch * rows_per_chunk

        def ones_in(words):
            return jnp.sum(lax.population_count(words), axis=0, keepdims=True)

        def passes(nr):
            live[0:nr] = jnp.where(lax.broadcasted_iota(i32, (nr, tq), 0) < used, i32(-1), i32(0))

            def bit_body(w, st):
                thr_u, n_above, n_set = st
                take = n_above + n_set >= topk
                thr_u = jnp.where(take, thr_u | lax.shift_left(i32(1), i32(31) - w), thr_u)
                n_above = jnp.where(take, n_above, n_above + n_set)
                still = live[0:nr] & (planes[w, 0:nr] ^ jnp.where(take, i32(0), i32(-1)))
                live[0:nr] = still
                return thr_u, n_above, ones_in(still & planes[jnp.minimum(w + 1, 31), 0:nr])

            zero = jnp.zeros((1, tq), i32)
            thr_u, n_above, _ = lax.fori_loop(
                0, 32, bit_body, (zero, zero, ones_in(live[0:nr] & planes[0, 0:nr])))
            thr_s[slot] = thr_u ^ i32(INT_MIN)
            cnt_s[0] = n_above
            cnt_s[1] = ones_in(live[0:nr])

        size_step = live.shape[0] // SEL_SIZES
        for v in range(SEL_SIZES):
            pl.when((used > v * size_step) & (used <= (v + 1) * size_step))(
                functools.partial(passes, (v + 1) * size_step))
        thr = thr_s[slot]
        n_gt = cnt_s[0]
        n_eq = cnt_s[1]
        need = topk - n_gt

        @pl.when(jnp.max((n_eq > need).astype(i32)) > 0)
        def _ties():
            def count(pred):
                def body(c, cnt):
                    s0 = chunk_start(c)
                    hit = pred(keys[slot, pl.ds(s0, ch), :], s0).astype(i32)
                    n = ch
                    while n > 8:
                        n //= 2
                        hit = hit[:n] + hit[n:2 * n]
                    return cnt + hit
                cnt = lax.fori_loop(0, nch, body, jnp.zeros((8, tq), i32))
                return jnp.sum(cnt, axis=0, keepdims=True)

            def jbody(it, p):
                cand = p + lax.shift_left(i32(1), i32(idx_bits - 1) - it)
                c = count(lambda kk, s0: (kk == thr) & (s0 + row_c < cand))
                return jnp.where(c < need, cand, p)
            last = lax.fori_loop(0, idx_bits, jbody, jnp.zeros((1, tq), i32))

            def demote(c, carry):
                s0 = chunk_start(c)
                kk = keys[slot, pl.ds(s0, ch), :]
                keys[slot, pl.ds(s0, ch), :] = jnp.where((kk == thr) & (s0 + row_c > last), INT_MIN, kk)
                return carry
            lax.fori_loop(0, nch, demote, 0)

    @pl.when(qi == 0)
    def _first_block():
        ones_tile = jnp.where(lax.broadcasted_iota(i32, (V_ROWS - ATTN_HEAD_DIM, BLK), 0) == 0, 1.0, 0.0)

        def value_block(j, carry):
            v_t = jnp.transpose(v_ref[0, pl.ds(pl.multiple_of(j * BLK, BLK), BLK), :].astype(F32))
            for g in range(ATTN_KV_HEADS):
                vt_s[j, g, 0:ATTN_HEAD_DIM, :] = v_t[g * ATTN_HEAD_DIM:(g + 1) * ATTN_HEAD_DIM, :].astype(BF16)
                vt_s[j, g, ATTN_HEAD_DIM:, :] = ones_tile.astype(BF16)
            return carry
        lax.fori_loop(0, nq, value_block, 0)

        load_indexer(qic_ref, miscc_ref)
        score_chunk(0, 0, 0)
        plane_chunk(0, 0)
        select(0, 0)

    slot_b = qi % 2
    slot_a = 1 - slot_b
    blk_a = jnp.minimum(qi + 1, nq - 1)
    nch_b = n_chunks(qi)
    nch_a = n_chunks(blk_a)
    load_indexer(qin_ref, miscn_ref)
    thr = thr_s[slot_b]

    q_t = jnp.transpose(q_ref[0].astype(F32))
    for h in range(ATTN_HEADS):
        qcat[h // rep, 0:ATTN_HEAD_DIM, (h % rep) * BLK:(h % rep + 1) * BLK] = (
            q_t[h * ATTN_HEAD_DIM:(h + 1) * ATTN_HEAD_DIM, :].astype(BF16))
    for g in range(ATTN_KV_HEADS):
        qcat[g, ATTN_HEAD_DIM:, :] = jnp.zeros((K_AUG - ATTN_HEAD_DIM, gw), BF16)

    def pen_of(s0, causal):
        sel = keys[slot_b, pl.ds(s0, ch), :] >= thr
        if causal:
            sel = sel & (s0 + row_c <= qi * BLK + lane_c)
        pen = jnp.where(sel, 0.0, NEG)
        return jnp.concatenate([pen] * rep, axis=1)

    def near_bias(c, g):
        return jnp.concatenate(
            [bias_s[jnp.clip(qi - (c * ch_blks + u), 0, 2), g] for u in range(ch_blks)], axis=0)

    def vt_chunk(c, g):
        return jnp.concatenate([vt_s[c * ch_blks + u, g] for u in range(ch_blks)], axis=1)

    s0d = pl.multiple_of(qi * BLK, BLK)
    kd = k_ref[0, pl.ds(s0d, BLK), :]
    tri = lax.broadcasted_iota(i32, (BLK, tq), 0) <= lax.broadcasted_iota(i32, (BLK, tq), 1)
    tri_pen = jnp.concatenate([jnp.where(tri, 0.0, NEG)] * rep, axis=1)
    for g in range(ATTN_KV_HEADS):
        sd = jnp.dot(kd[:, g * K_AUG:(g + 1) * K_AUG], qcat[g],
                     preferred_element_type=F32) + bias_s[0, g] + tri_pen
        shift = jnp.max(sd, axis=0, keepdims=True)
        first = lax.broadcasted_iota(i32, (SHIFT_ROWS, gw), 0) == 0
        qcat[g, ATTN_HEAD_DIM:ATTN_HEAD_DIM + SHIFT_ROWS, :] = jnp.where(first, -shift, 0.0).astype(BF16)
        acc_s[g] = jnp.zeros((V_ROWS, gw), F32)
        p_s[1, g] = jnp.zeros((ch, gw), BF16)

    def weigh_values(c, slot):
        for g in range(ATTN_KV_HEADS):
            acc_s[g] += jnp.dot(vt_chunk(c, g), p_s[slot, g], preferred_element_type=F32)

    def att_chunk(c, near, slots=None):
        prev_slot, cur_slot = slots if slots is not None else ((c + 1) % 2, c % 2)
        weigh_values(jnp.maximum(c - 1, 0), prev_slot)
        s0 = chunk_start(c)
        pen = pen_of(s0, near)
        kblk = k_ref[0, pl.ds(s0, ch), :]
        for g in range(ATTN_KV_HEADS):
            s = jnp.dot(kblk[:, g * K_AUG:(g + 1) * K_AUG], qcat[g],
                        preferred_element_type=F32)
            if near:
                s = s + near_bias(c, g)
            p_s[cur_slot, g] = jnp.exp2(s + pen).astype(BF16)

    def fused(c, near, slots=None):
        plane_chunk(c, slot_a)
        att_chunk(c, near, slots)
        score_chunk(c + 1, blk_a, slot_a)

    def far_pair(j, carry):
        fused(2 * j, False, (1, 0))
        fused(2 * j + 1, False, (0, 1))
        return carry

    def far_quad(j, carry):
        far_pair(2 * j, carry)
        return far_pair(2 * j + 1, carry)

    def far_body(c, carry):
        fused(c, False, (1, 0))
        return carry

    def near_body(c, carry):
        fused(c, True)
        return carry

    score_chunk(0, blk_a, slot_a)
    n_far = jnp.maximum(qi - 1, 0) // ch_blks
    lax.fori_loop(0, n_far // 4, far_quad, 0)
    lax.fori_loop((n_far // 4) * 2, n_far // 2, far_pair, 0)
    lax.fori_loop((n_far // 2) * 2, n_far, far_body, 0)
    lax.fori_loop(n_far, nch_b - 1, near_body, 0)
    plane_chunk(nch_b - 1, slot_a)
    att_chunk(nch_b - 1, True)
    weigh_values(nch_b - 1, (nch_b - 1) % 2)

    @pl.when(nch_a > nch_b)
    def _one_more_chunk():
        score_chunk(nch_b, blk_a, slot_a)
        plane_chunk(nch_b, slot_a)

    select(blk_a, slot_a)

    n_ok = jnp.zeros((1, gw), i32)
    for g in range(ATTN_KV_HEADS):
        a = acc_s[g]
        den = a[ATTN_HEAD_DIM:ATTN_HEAD_DIM + 1, :]
        top = jnp.max(jnp.abs(a[0:ATTN_HEAD_DIM, :]), axis=0, keepdims=True)
        n_ok = n_ok + ((den > DEN_LO) & (den < DEN_HI) & (top < F32_HUGE)).astype(i32)

    @pl.when(jnp.min(n_ok) < ATTN_KV_HEADS)
    def _exact():
        for g in range(ATTN_KV_HEADS):
            qcat[g, ATTN_HEAD_DIM:ATTN_HEAD_DIM + SHIFT_ROWS, :] = jnp.zeros((SHIFT_ROWS, gw), BF16)
            m_s[g] = jnp.full((1, gw), NEG, F32)
            acc_s[g] = jnp.zeros((V_ROWS, gw), F32)

        def body(c, carry):
            s0 = chunk_start(c)
            pen = pen_of(s0, True)
            kblk = k_ref[0, pl.ds(s0, ch), :]
            for g in range(ATTN_KV_HEADS):
                s = jnp.dot(kblk[:, g * K_AUG:(g + 1) * K_AUG], qcat[g],
                            preferred_element_type=F32) + near_bias(c, g) + pen
                m_old = m_s[g]
                m_new = jnp.maximum(m_old, jnp.max(s, axis=0, keepdims=True))
                p = jnp.exp2(s - m_new).astype(BF16)
                acc_s[g] = acc_s[g] * jnp.exp2(m_old - m_new) + jnp.dot(
                    vt_chunk(c, g), p, preferred_element_type=F32)
                m_s[g] = m_new
            return carry
        lax.fori_loop(0, nch_b, body, 0)

    parts = []
    for g in range(ATTN_KV_HEADS):
        a = acc_s[g]
        o = a[0:ATTN_HEAD_DIM, :] * (1.0 / a[ATTN_HEAD_DIM:ATTN_HEAD_DIM + 1, :])
        parts += [o[:, r * tq:(r + 1) * tq] for r in range(rep)]
    out_t = jnp.concatenate(parts, axis=0)
    o_ref[0] = _rms(jnp.transpose(out_t), nw_ref[...]).astype(BF16)


def _dsa(rel_bias, q, qi, misc, k, v, miscb, nw):
    bsz, t, _ = q.shape
    nq = t // BLK
    topk = min(TOPK_MAX, t // 4)
    idx_bits = max(1, (t - 1).bit_length())
    gw = (ATTN_HEADS // ATTN_KV_HEADS) * BLK
    bkt = jnp.asarray(_bucket_tiles())
    kern = functools.partial(_dsa_kernel, topk=topk, idx_bits=idx_bits, nq=nq)

    def nxt(b, i):
        return (b, jnp.minimum(i + 1, nq - 1), 0)

    return pl.pallas_call(
        kern,
        grid=(bsz, nq),
        in_specs=[pl.BlockSpec(memory_space=pltpu.SMEM),
                  pl.BlockSpec((2, BLK, BLK), lambda b, i: (0, 0, 0)),
                  pl.BlockSpec((1, BLK, D_ATTN), lambda b, i: (b, i, 0)),
                  pl.BlockSpec((1, BLK, IDX_HEADS * IDX_DIM), lambda b, i: (b, i, 0)),
                  pl.BlockSpec((1, BLK, MISC_W), lambda b, i: (b, i, 0)),
                  pl.BlockSpec((1, BLK, IDX_HEADS * IDX_DIM), nxt),
                  pl.BlockSpec((1, BLK, MISC_W), nxt),
                  pl.BlockSpec((1, t, D_K_AUG), lambda b, i: (b, 0, 0)),
                  pl.BlockSpec((1, t, D_KV), lambda b, i: (b, 0, 0)),
                  pl.BlockSpec((1, t, MISC_W), lambda b, i: (b, 0, 0)),
                  pl.BlockSpec((1, D_ATTN), lambda b, i: (0, 0))],
        out_specs=pl.BlockSpec((1, BLK, D_ATTN), lambda b, i: (b, i, 0)),
        out_shape=jax.ShapeDtypeStruct((bsz, t, D_ATTN), BF16),
        scratch_shapes=[pltpu.VMEM((2, t, BLK), jnp.int32),
                        pltpu.VMEM((32, t // 32, BLK), jnp.int32),
                        pltpu.VMEM((t // 32, BLK), jnp.int32),
                        pltpu.VMEM((2, 1, BLK), jnp.int32),
                        pltpu.VMEM((2, 1, BLK), jnp.int32),
                        pltpu.VMEM((3, ATTN_KV_HEADS, BLK, gw), F32),
                        pltpu.VMEM((ATTN_KV_HEADS, K_AUG, gw), BF16),
                        pltpu.VMEM((IDX_DIM, IDX_HEADS * BLK), BF16),
                        pltpu.VMEM((8, IDX_HEADS * BLK), F32),
                        pltpu.VMEM((ATTN_KV_HEADS, 1, gw), F32),
                        pltpu.VMEM((ATTN_KV_HEADS, V_ROWS, gw), F32),
                        pltpu.VMEM((2, ATTN_KV_HEADS, DSA_CHUNK, gw), BF16),
                        pltpu.VMEM((nq, ATTN_KV_HEADS, V_ROWS, BLK), BF16)],
        compiler_params=pltpu.CompilerParams(
            dimension_semantics=("arbitrary", "arbitrary"), vmem_limit_bytes=48 * 2 ** 20),
        name="dsa",
    )(rel_bias, bkt, q, qi, misc, qi, misc, k, v, miscb, nw)


def _tail_kernel(x_ref, ys_ref, ya_ref, mod_ref, wo_ref, n1_ref, n2_ref, n3_ref, w1_ref, w2_ref, o_ref):
    x = x_ref[0]
    mix = (jnp.dot(ys_ref[0], wo_ref[0:D_SSM, :], preferred_element_type=F32)
           + jnp.dot(ya_ref[0], wo_ref[D_SSM:, :], preferred_element_type=F32))
    x1 = x + mod_ref[0, 2:3, :] * _rms(mix, n1_ref[...])
    h = (_rms(x1, n2_ref[...]) * (1.0 + mod_ref[0, 4:5, :]) + mod_ref[0, 3:4, :]).astype(BF16)
    f = jnp.zeros(x.shape, F32)
    step = D_MODEL
    for c0 in range(0, D_FF, step):
        a = jnp.maximum(jnp.dot(h, w1_ref[:, c0:c0 + step], preferred_element_type=F32), 0.0)
        f = f + jnp.dot((a * a).astype(BF16), w2_ref[c0:c0 + step, :], preferred_element_type=F32)
    o_ref[0] = x1 + mod_ref[0, 5:6, :] * _rms(f, n3_ref[...])


def _tail(x, ys, ya, mod, wo, n1, n2, n3, w1, w2):
    bsz, t, d = x.shape
    tm = min(ROW_TILE, t)

    def const(shape):
        return pl.BlockSpec(shape, lambda b, i: tuple(0 for _ in shape))

    return pl.pallas_call(
        _tail_kernel,
        grid=(bsz, t // tm),
        in_specs=[pl.BlockSpec((1, tm, d), lambda b, i: (b, i, 0)),
                  pl.BlockSpec((1, tm, D_SSM), lambda b, i: (b, i, 0)),
                  pl.BlockSpec((1, tm, D_ATTN), lambda b, i: (b, i, 0)),
                  pl.BlockSpec((1, 6, d), lambda b, i: (b, 0, 0)),
                  const((D_SSM + D_ATTN, d)), const((1, d)), const((1, d)), const((1, d)),
                  const((d, D_FF)), const((D_FF, d))],
        out_specs=pl.BlockSpec((1, tm, d), lambda b, i: (b, i, 0)),
        out_shape=jax.ShapeDtypeStruct((bsz, t, d), F32),
        compiler_params=pltpu.CompilerParams(
            dimension_semantics=("arbitrary", "arbitrary"), vmem_limit_bytes=60 * 2 ** 20),
        name="tail",
    )(x, ys, ya, mod, wo, n1, n2, n3, w1, w2)


def _permute_w_in(w):
    sizes = [D_SSM, D_XBC, SSM_HEADS, D_ATTN, D_KV, D_KV, IDX_HEADS * IDX_DIM, IDX_DIM, IDX_HEADS]
    pts = np.cumsum(sizes)[:-1].tolist()
    z, xbc, dt, q, k, v, qidx, kidx, widx = jnp.split(w, pts, axis=-1)
    pad = jnp.zeros((w.shape[0], MISC_W - IDX_DIM - IDX_HEADS - SSM_HEADS), w.dtype)
    kpad = jnp.zeros((w.shape[0], K_AUG - ATTN_HEAD_DIM), w.dtype)
    k_aug = []
    for g in range(ATTN_KV_HEADS):
        k_aug += [k[:, g * ATTN_HEAD_DIM:(g + 1) * ATTN_HEAD_DIM], kpad]
    return jnp.concatenate([z, xbc, q] + k_aug + [v, qidx, kidx, widx, dt, pad], axis=-1).astype(BF16)


def _layer(x, c_pad, w_ada, b_ada, norm1_pre, norm1_post, w_in, conv_w, conv_b, dt_bias, a_log, d_skip,
           ssm_norm, rel_bias, attn_norm, w_out, norm2_pre, norm2_post, w_mlp_in, w_mlp_out):
    bsz, t, d = x.shape
    mod = _ada(c_pad, w_ada, b_ada[None, :])[:bsz].reshape(bsz, 6, d)
    z, xbc, q, k, v, qi, misc, miscb = _inproj(x, mod, norm1_pre[None, :], _permute_w_in(w_in))

    lane = jnp.arange(MISC_W)
    is_dt = (lane >= MISC_DT) & (lane < MISC_DT + SSM_HEADS)
    head_of = jnp.clip(lane - MISC_DT, 0, SSM_HEADS - 1)
    dtb_t = jnp.where(is_dt, dt_bias[head_of], 0.0)[None, :]
    a_t = jnp.where(is_dt, -jnp.exp(a_log.astype(F32))[head_of], 0.0)[None, :]
    expand = ((lane[:, None] - MISC_DT) == (jnp.arange(D_SSM)[None, :] // SSM_HEAD_DIM)).astype(BF16)
    dskip_e = jnp.repeat(d_skip.astype(F32), SSM_HEAD_DIM)[None, :]
    y_ssd = _ssd(xbc, z, misc, conv_w, conv_b[None, :], dtb_t, a_t, expand, dskip_e, ssm_norm[None, :])

    y_att = _dsa(rel_bias, q, qi, misc, k, v, miscb, attn_norm[None, :])

    return _tail(x, y_ssd, y_att, mod, w_out.astype(BF16), norm1_post[None, :], norm2_pre[None, :],
                 norm2_post[None, :], w_mlp_in.astype(BF16), w_mlp_out.astype(BF16))


def kernel(x, c, w_ada, b_ada, norm1_pre, norm1_post, w_in, conv_w, conv_b, dt_bias, a_log, d_skip,
           ssm_norm, rel_bias, attn_norm, w_out, norm2_pre, norm2_post, w_mlp_in, w_mlp_out):
    bsz = x.shape[0]
    assert bsz <= 8 and x.shape[1] % DSA_CHUNK == 0 and x.shape[2] == D_MODEL
    c_pad = jnp.zeros((8, D_MODEL), F32).at[:bsz].set(c)
    for l in range(w_ada.shape[0]):
        x = _layer(x, c_pad, w_ada[l], b_ada[l], norm1_pre[l], norm1_post[l], w_in[l], conv_w[l], conv_b[l],
                   dt_bias[l], a_log[l], d_skip[l], ssm_norm[l], rel_bias, attn_norm[l], w_out[l],
                   norm2_pre[l], norm2_post[l], w_mlp_in[l], w_mlp_out[l])
    return x
```

```python
import functools
import math

import numpy as np
import jax
import jax.numpy as jnp
from jax import lax
from jax.experimental import pallas as pl
from jax.experimental.pallas import tpu as pltpu

D_MODEL = 1024
SSM_HEADS = 8
SSM_HEAD_DIM = 64
D_SSM = SSM_HEADS * SSM_HEAD_DIM
SSM_GROUPS = 2
SSM_STATE = 128
CONV_WIDTH = 4
CHUNK = 256
ATTN_HEADS = 8
ATTN_KV_HEADS = 2
ATTN_HEAD_DIM = 64
D_ATTN = ATTN_HEADS * ATTN_HEAD_DIM
D_KV = ATTN_KV_HEADS * ATTN_HEAD_DIM
IDX_HEADS = 8
IDX_DIM = 64
TOPK_MAX = 256
N_BUCKETS = 32
MAX_DISTANCE = 128
D_FF = 4 * D_MODEL
D_BC = SSM_GROUPS * SSM_STATE
D_XBC = D_SSM + 2 * D_BC
EPS = 1e-6

LANES = 128
MISC_W = LANES
MISC_KIDX = 0
MISC_WIDX = IDX_DIM
MISC_DT = IDX_DIM + IDX_HEADS
K_AUG = LANES
D_K_AUG = ATTN_KV_HEADS * K_AUG
SHIFT_ROWS = 16
OFF_Z = 0
OFF_XBC = OFF_Z + D_SSM
OFF_Q = OFF_XBC + D_XBC
OFF_K = OFF_Q + D_ATTN
OFF_V = OFF_K + D_K_AUG
OFF_QI = OFF_V + D_KV
OFF_MISC = OFF_QI + IDX_HEADS * IDX_DIM
D_IN_PAD = OFF_MISC + MISC_W

BLK = 128
DSA_CHUNK = 512
V_ROWS = 80
SEL_SIZES = 4
DEN_LO = 2.0 ** -80
DEN_HI = 2.0 ** 80
F32_HUGE = 3.0e38
LOG2E = math.log2(math.e)
ROW_TILE = 512
NEG = -1e30
INT_MIN = -2 ** 31

F32 = jnp.float32
BF16 = jnp.bfloat16
HI = lax.Precision.HIGHEST
NT = (((1,), (1,)), ((), ()))


def _silu(x):
    return x / (1.0 + jnp.exp(-x))


def _rms(x, w):
    return x * lax.rsqrt(jnp.mean(x * x, axis=-1, keepdims=True) + EPS) * w


def _split3(x):
    hi = x.astype(BF16)
    rest = x - hi.astype(F32)
    mid = rest.astype(BF16)
    return hi, mid, (rest - mid.astype(F32)).astype(BF16)


def _ada_kernel(c_ref, w_ref, b_ref, o_ref):
    s = _silu(c_ref[...])
    o_ref[...] = jnp.dot(s, w_ref[...], precision=HI, preferred_element_type=F32) + b_ref[...]


def _ada(c_pad, w, b):
    n = w.shape[1]
    tn = D_MODEL
    return pl.pallas_call(
        _ada_kernel,
        grid=(n // tn,),
        in_specs=[pl.BlockSpec((8, D_MODEL), lambda j: (0, 0)),
                  pl.BlockSpec((D_MODEL, tn), lambda j: (0, j)),
                  pl.BlockSpec((1, tn), lambda j: (0, j))],
        out_specs=pl.BlockSpec((8, tn), lambda j: (0, j)),
        out_shape=jax.ShapeDtypeStruct((8, n), F32),
        name="ada",
    )(c_pad, w, b)


def _inproj_kernel(x_ref, mod_ref, nw_ref, w_ref,
                   z_ref, xbc_ref, q_ref, k_ref, v_ref, qi_ref, misc_ref, miscb_ref):
    x = x_ref[0]
    h = _rms(x, nw_ref[...]) * (1.0 + mod_ref[0, 1:2, :]) + mod_ref[0, 0:1, :]
    hb = h.astype(BF16)

    def seg(lo, width):
        return jnp.dot(hb, w_ref[:, lo:lo + width], preferred_element_type=F32)

    z_ref[0] = seg(OFF_Z, D_SSM)
    xbc_ref[0] = seg(OFF_XBC, D_XBC)
    q_ref[0] = (seg(OFF_Q, D_ATTN) * (ATTN_HEAD_DIM ** -0.5 * LOG2E)).astype(BF16)
    ones_col = lax.broadcasted_iota(jnp.int32, (1, D_K_AUG), 1) % K_AUG == ATTN_HEAD_DIM
    k_ref[0] = jnp.where(ones_col, 1.0, seg(OFF_K, D_K_AUG)).astype(BF16)
    v_ref[0] = seg(OFF_V, D_KV).astype(BF16)
    qi_ref[0] = seg(OFF_QI, IDX_HEADS * IDX_DIM).astype(BF16)
    misc = seg(OFF_MISC, MISC_W)
    misc_ref[0] = misc
    miscb_ref[0] = misc.astype(BF16)


def _inproj(x, mod, nw, w_perm):
    bsz, t, d = x.shape
    tm = min(ROW_TILE, t)

    def tok(width, dtype):
        return (pl.BlockSpec((1, tm, width), lambda b, i: (b, i, 0)),
                jax.ShapeDtypeStruct((bsz, t, width), dtype))

    outs = [tok(D_SSM, F32), tok(D_XBC, F32), tok(D_ATTN, BF16), tok(D_K_AUG, BF16), tok(D_KV, BF16),
            tok(IDX_HEADS * IDX_DIM, BF16), tok(MISC_W, F32), tok(MISC_W, BF16)]
    return pl.pallas_call(
        _inproj_kernel,
        grid=(bsz, t // tm),
        in_specs=[pl.BlockSpec((1, tm, d), lambda b, i: (b, i, 0)),
                  pl.BlockSpec((1, 6, d), lambda b, i: (b, 0, 0)),
                  pl.BlockSpec((1, d), lambda b, i: (0, 0)),
                  pl.BlockSpec((d, D_IN_PAD), lambda b, i: (0, 0))],
        out_specs=[o[0] for o in outs],
        out_shape=[o[1] for o in outs],
        compiler_params=pltpu.CompilerParams(
            dimension_semantics=("arbitrary", "arbitrary"), vmem_limit_bytes=48 * 2 ** 20),
        name="inproj",
    )(x, mod, nw, w_perm)


def _ssd_kernel(xbc_ref, z_ref, misc_ref, cw_ref, cb_ref, dtb_ref, a_ref, e_ref, dskip_ref, nw_ref,
                o_ref, xbuf, state):
    c = pl.program_id(1)
    lc = xbc_ref.shape[1]

    @pl.when(c == 0)
    def _():
        xbuf[0:8, :] = jnp.zeros((8, D_XBC), F32)
        state[...] = jnp.zeros(state.shape, F32)

    @pl.when(c > 0)
    def _():
        xbuf[0:8, :] = xbuf[lc:lc + 8, :]

    xbuf[8:lc + 8, :] = xbc_ref[0]
    u = cb_ref[...]
    xall = xbuf[...]
    for k in range(CONV_WIDTH):
        back = CONV_WIDTH - 1 - k
        xk = xall if back == 0 else pltpu.roll(xall, back, 0)
        u = u + xk[8:8 + lc] * cw_ref[k:k + 1, :]
    u = _silu(u)
    xs = u[:, :D_SSM]
    bm = u[:, D_SSM:D_SSM + D_BC]
    cm = u[:, D_SSM + D_BC:]

    raw = misc_ref[0] + dtb_ref[...]
    dt_t = jnp.maximum(raw, 0.0) + jnp.log1p(jnp.exp(-jnp.abs(raw)))
    adt_t = dt_t * a_ref[...]
    row = lax.broadcasted_iota(jnp.int32, (lc, lc), 0)
    col = lax.broadcasted_iota(jnp.int32, (lc, lc), 1)
    causal = col <= row
    tril = jnp.where(causal, 1.0, 0.0).astype(BF16)
    expand = e_ref[...]
    acs_t = sum(jnp.dot(tril, piece, preferred_element_type=F32) for piece in _split3(adt_t))
    dt_e = sum(jnp.dot(piece, expand, preferred_element_type=F32) for piece in _split3(dt_t))
    acs_e = sum(jnp.dot(piece, expand, preferred_element_type=F32) for piece in _split3(acs_t))
    acs_row = jnp.transpose(acs_t)

    xdt = xs * dt_e
    a_last = acs_e[lc - 1:lc, :]
    xdec = (xdt * jnp.exp2(a_last - acs_e)).astype(BF16)
    xdt_b = xdt.astype(BF16)
    bm_b = bm.astype(BF16)
    cm_b = cm.astype(BF16)
    bm_t = jnp.transpose(bm).astype(BF16)

    heads_per_group = SSM_HEADS // SSM_GROUPS
    gw = heads_per_group * SSM_HEAD_DIM
    y_parts = []
    off_parts = []
    for g in range(SSM_GROUPS):
        bg = bm_b[:, g * SSM_STATE:(g + 1) * SSM_STATE]
        cg = cm_b[:, g * SSM_STATE:(g + 1) * SSM_STATE]
        cb = lax.dot_general(cg, bg, NT, preferred_element_type=F32)
        for r in range(heads_per_group):
            hd = g * heads_per_group + r
            a_col = acs_t[:, MISC_DT + hd:MISC_DT + hd + 1]
            a_row = acs_row[MISC_DT + hd:MISC_DT + hd + 1, :]
            lmat = jnp.exp2(jnp.where(causal, a_col - a_row, -jnp.inf))
            mh = (cb * lmat).astype(BF16)
            y_parts.append(jnp.dot(mh, xdt_b[:, hd * SSM_HEAD_DIM:(hd + 1) * SSM_HEAD_DIM],
                                   preferred_element_type=F32))
        prev = state[g]
        off_parts.append(jnp.dot(cg, prev.astype(BF16), preferred_element_type=F32))
        st = jnp.dot(bm_t[g * SSM_STATE:(g + 1) * SSM_STATE, :], xdec[:, g * gw:(g + 1) * gw],
                     preferred_element_type=F32)
        state[g] = prev * jnp.exp2(a_last[:, g * gw:(g + 1) * gw]) + st
    y = (jnp.concatenate(y_parts, axis=1)
         + jnp.concatenate(off_parts, axis=1) * jnp.exp2(acs_e)
         + xs * dskip_ref[...])
    o_ref[0] = _rms(y * _silu(z_ref[0]), nw_ref[...]).astype(BF16)


def _ssd(xbc, z, misc, cw, cb, dtb_t, a_t, expand, dskip_e, nw):
    bsz, t, _ = xbc.shape
    lc = math.gcd(t, CHUNK)
    nc = t // lc
    gw = (SSM_HEADS // SSM_GROUPS) * SSM_HEAD_DIM

    def const(shape):
        return pl.BlockSpec(shape, lambda b, c: tuple(0 for _ in shape))

    return pl.pallas_call(
        _ssd_kernel,
        grid=(bsz, nc),
        in_specs=[pl.BlockSpec((1, lc, D_XBC), lambda b, c: (b, c, 0)),
                  pl.BlockSpec((1, lc, D_SSM), lambda b, c: (b, c, 0)),
                  pl.BlockSpec((1, lc, MISC_W), lambda b, c: (b, c, 0)),
                  const((CONV_WIDTH, D_XBC)), const((1, D_XBC)), const((1, MISC_W)), const((1, MISC_W)),
                  const((MISC_W, D_SSM)), const((1, D_SSM)), const((1, D_SSM))],
        out_specs=pl.BlockSpec((1, lc, D_SSM), lambda b, c: (b, c, 0)),
        out_shape=jax.ShapeDtypeStruct((bsz, t, D_SSM), BF16),
        scratch_shapes=[pltpu.VMEM((lc + 8, D_XBC), F32),
                        pltpu.VMEM((SSM_GROUPS, SSM_STATE, gw), F32)],
        compiler_params=pltpu.CompilerParams(
            dimension_semantics=("arbitrary", "arbitrary"), vmem_limit_bytes=48 * 2 ** 20),
        name="ssd",
    )(xbc, z, misc, cw, cb, dtb_t, a_t, expand, dskip_e, nw)


def _bucket_tiles():
    s = np.arange(BLK)[:, None]
    t = np.arange(BLK)[None, :]
    max_exact = N_BUCKETS // 2
    tiles = []
    for d0 in (0, BLK):
        n = np.maximum(d0 + t - s, 0)
        large = {}
        for dt_ in (np.float32, np.float64):
            nf = np.maximum(n, 1).astype(dt_)
            lg = max_exact + (np.log(nf / dt_(max_exact)) / dt_(math.log(MAX_DISTANCE / max_exact))
                              * dt_(N_BUCKETS - max_exact)).astype(np.int32)
            large[dt_] = np.minimum(lg, N_BUCKETS - 1)
        assert (large[np.float32] == large[np.float64]).all()
        tiles.append(np.where(n < max_exact, n, large[np.float64]).astype(np.int32))
    assert BLK + 1 >= MAX_DISTANCE
    return np.stack(tiles)


def _to_key(x):
    bits = lax.bitcast_convert_type(x, jnp.int32)
    key = bits ^ (lax.shift_right_arithmetic(bits, 31) & jnp.int32(0x7FFFFFFF))
    return jnp.where(key == -1, 0, key)


def _transpose32(rows):
    rows = list(rows)
    mask, j = 0x0000FFFF, 16
    while j:
        k = 0
        while k < 32:
            t = (rows[k] ^ lax.shift_right_logical(rows[k + j], jnp.int32(j))) & jnp.int32(mask)
            rows[k] = rows[k] ^ t
            rows[k + j] = rows[k + j] ^ lax.shift_left(t, jnp.int32(j))
            k = (k + j + 1) & ~j
        j >>= 1
        mask ^= (mask << j) & 0xFFFFFFFF
    return rows


def _dsa_kernel(relb_ref, bkt_ref, q_ref, qic_ref, miscc_ref, qin_ref, miscn_ref, k_ref, v_ref, kidx_ref,
                nw_ref, o_ref, keys, planes, live, thr_s, cnt_s, bias_s, qcat, qicat, wcat, m_s, acc_s, p_s, vt_s,
                *, topk, idx_bits, nq):
    b = pl.program_id(0)
    qi = pl.program_id(1)
    tq = BLK
    i32 = jnp.int32
    rep = ATTN_HEADS // ATTN_KV_HEADS
    gw = rep * tq
    ch = DSA_CHUNK
    ch_blks = ch // BLK
    rows_per_chunk = ch // 32

    @pl.when((b == 0) & (qi == 0))
    def _init_bias():
        for d in range(2):
            bk = bkt_ref[d]
            for h in range(ATTN_HEADS):
                far = relb_ref[N_BUCKETS - 1, h]
                acc = jnp.zeros((BLK, tq), F32)
                for n in range(N_BUCKETS - 1):
                    acc = jnp.where(bk == n, relb_ref[n, h] - far, acc)
                bias_s[d, h // rep, :, (h % rep) * BLK:(h % rep + 1) * BLK] = acc * LOG2E
        bias_s[2] = jnp.zeros((ATTN_KV_HEADS, BLK, gw), F32)
        for w in range(32):
            planes[w] = jnp.zeros(planes.shape[1:], i32)

    row_c = lax.broadcasted_iota(i32, (ch, tq), 0)
    lane_c = lax.broadcasted_iota(i32, (ch, tq), 1)

    def chunk_start(c):
        return pl.multiple_of(c * ch, ch)

    def n_chunks(blk):
        return (blk + ch_blks) // ch_blks

    def load_indexer(qidx_ref, misc_ref):
        qi_t = jnp.transpose(qidx_ref[0].astype(F32))
        w_t = jnp.transpose(misc_ref[0])[MISC_WIDX:MISC_WIDX + IDX_HEADS, :] * (
            IDX_HEADS ** -0.5 * IDX_DIM ** -0.5)
        for h in range(IDX_HEADS):
            qicat[:, h * BLK:(h + 1) * BLK] = qi_t[h * IDX_DIM:(h + 1) * IDX_DIM, :].astype(BF16)
            wcat[:, h * BLK:(h + 1) * BLK] = jnp.broadcast_to(w_t[h:h + 1, :], (8, BLK))

    def score_chunk(c, blk, slot):
        s0 = chunk_start(c)
        kb = kidx_ref[0, pl.ds(s0, ch), MISC_KIDX:MISC_KIDX + IDX_DIM]
        s = jnp.dot(kb, qicat[...], preferred_element_type=F32)
        r = jnp.maximum(s, 0.0) * wcat[0:1, :]
        acc = r[:, 0:BLK]
        for h in range(1, IDX_HEADS):
            acc = acc + r[:, h * BLK:(h + 1) * BLK]
        acc = jnp.where(s0 + row_c <= blk * BLK + lane_c, acc, -jnp.inf)
        keys[slot, pl.ds(s0, ch), :] = _to_key(acc)

    def plane_chunk(c, slot):
        ku = (keys[slot, pl.ds(chunk_start(c), ch), :] ^ i32(INT_MIN)).reshape(ch // (32 * 8), 32, 8, tq)
        words = _transpose32([ku[:, j] for j in range(32)])
        r0 = pl.multiple_of(c * rows_per_chunk, rows_per_chunk)
        for w in range(32):
            planes[w, pl.ds(r0, rows_per_chunk), :] = words[w].reshape(rows_per_chunk, tq)

    def select(blk, slot):
        nch = n_chunks(blk)
        used = nch * rows_per_chunk

        def ones_in(words):
            return jnp.sum(lax.population_count(words), axis=0, keepdims=True)

        def passes(nr):
            live[0:nr] = jnp.where(lax.broadcasted_iota(i32, (nr, tq), 0) < used, i32(-1), i32(0))

            def bit_body(w, st):
                thr_u, n_above, n_set = st
                take = n_above + n_set >= topk
                thr_u = jnp.where(take, thr_u | lax.shift_left(i32(1), i32(31) - w), thr_u)
                n_above = jnp.where(take, n_above, n_above + n_set)
                still = live[0:nr] & (planes[w, 0:nr] ^ jnp.where(take, i32(0), i32(-1)))
                live[0:nr] = still
                return thr_u, n_above, ones_in(still & planes[jnp.minimum(w + 1, 31), 0:nr])

            zero = jnp.zeros((1, tq), i32)
            thr_u, n_above, _ = lax.fori_loop(
                0, 32, bit_body, (zero, zero, ones_in(live[0:nr] & planes[0, 0:nr])))
            thr_s[slot] = thr_u ^ i32(INT_MIN)
            cnt_s[0] = n_above
            cnt_s[1] = ones_in(live[0:nr])

        size_step = live.shape[0] // SEL_SIZES
        for v in range(SEL_SIZES):
            pl.when((used > v * size_step) & (used <= (v + 1) * size_step))(
                functools.partial(passes, (v + 1) * size_step))
        thr = thr_s[slot]
        n_gt = cnt_s[0]
        n_eq = cnt_s[1]
        need = topk - n_gt

        def break_ties():
            def count(pred):
                def body(c, cnt):
                    s0 = chunk_start(c)
                    hit = pred(keys[slot, pl.ds(s0, ch), :], s0).astype(i32)
                    n = ch
                    while n > 8:
                        n //= 2
                        hit = hit[:n] + hit[n:2 * n]
                    return cnt + hit
                cnt = lax.fori_loop(0, nch, body, jnp.zeros((8, tq), i32))
                return jnp.sum(cnt, axis=0, keepdims=True)

            def jbody(it, p):
                cand = p + lax.shift_left(i32(1), i32(idx_bits - 1) - it)
                c = count(lambda kk, s0: (kk == thr) & (s0 + row_c < cand))
                return jnp.where(c < need, cand, p)
            last = lax.fori_loop(0, idx_bits, jbody, jnp.zeros((1, tq), i32))

            def demote(c, carry):
                s0 = chunk_start(c)
                kk = keys[slot, pl.ds(s0, ch), :]
                keys[slot, pl.ds(s0, ch), :] = jnp.where((kk == thr) & (s0 + row_c > last), INT_MIN, kk)
                return carry
            lax.fori_loop(0, nch, demote, 0)

        return (n_eq > need).astype(i32), break_ties

    def any_lane(flags):
        return jnp.max(flags, axis=1, keepdims=True)

    @pl.when(qi == 0)
    def _first_block():
        ones_tile = jnp.where(lax.broadcasted_iota(i32, (V_ROWS - ATTN_HEAD_DIM, BLK), 0) == 0, 1.0, 0.0)

        def value_block(j, carry):
            v_t = jnp.transpose(v_ref[0, pl.ds(pl.multiple_of(j * BLK, BLK), BLK), :].astype(F32))
            for g in range(ATTN_KV_HEADS):
                vt_s[j, g, 0:ATTN_HEAD_DIM, :] = v_t[g * ATTN_HEAD_DIM:(g + 1) * ATTN_HEAD_DIM, :].astype(BF16)
                vt_s[j, g, ATTN_HEAD_DIM:, :] = ones_tile.astype(BF16)
            return carry
        lax.fori_loop(0, nq, value_block, 0)

        load_indexer(qic_ref, miscc_ref)
        score_chunk(0, 0, 0)
        plane_chunk(0, 0)
        tied, break_ties = select(0, 0)
        pl.when(any_lane(tied)[0, 0] > 0)(break_ties)

    slot_b = qi % 2
    slot_a = 1 - slot_b
    blk_a = jnp.minimum(qi + 1, nq - 1)
    nch_b = n_chunks(qi)
    nch_a = n_chunks(blk_a)
    load_indexer(qin_ref, miscn_ref)
    thr = thr_s[slot_b]

    q_t = jnp.transpose(q_ref[0].astype(F32))
    for h in range(ATTN_HEADS):
        qcat[h // rep, 0:ATTN_HEAD_DIM, (h % rep) * BLK:(h % rep + 1) * BLK] = (
            q_t[h * ATTN_HEAD_DIM:(h + 1) * ATTN_HEAD_DIM, :].astype(BF16))
    for g in range(ATTN_KV_HEADS):
        qcat[g, ATTN_HEAD_DIM:, :] = jnp.zeros((K_AUG - ATTN_HEAD_DIM, gw), BF16)

    def pen_of(s0, causal):
        sel = keys[slot_b, pl.ds(s0, ch), :] >= thr
        if causal:
            sel = sel & (s0 + row_c <= qi * BLK + lane_c)
        pen = jnp.where(sel, 0.0, NEG)
        return jnp.concatenate([pen] * rep, axis=1)

    def near_bias(c, g):
        return jnp.concatenate(
            [bias_s[jnp.clip(qi - (c * ch_blks + u), 0, 2), g] for u in range(ch_blks)], axis=0)

    def vt_chunk(c, g):
        return jnp.concatenate([vt_s[c * ch_blks + u, g] for u in range(ch_blks)], axis=1)

    s0d = pl.multiple_of(qi * BLK, BLK)
    kd = k_ref[0, pl.ds(s0d, BLK), :]
    tri = lax.broadcasted_iota(i32, (BLK, tq), 0) <= lax.broadcasted_iota(i32, (BLK, tq), 1)
    tri_pen = jnp.concatenate([jnp.where(tri, 0.0, NEG)] * rep, axis=1)
    for g in range(ATTN_KV_HEADS):
        sd = jnp.dot(kd[:, g * K_AUG:(g + 1) * K_AUG], qcat[g],
                     preferred_element_type=F32) + bias_s[0, g] + tri_pen
        shift = jnp.max(sd, axis=0, keepdims=True)
        first = lax.broadcasted_iota(i32, (SHIFT_ROWS, gw), 0) == 0
        qcat[g, ATTN_HEAD_DIM:ATTN_HEAD_DIM + SHIFT_ROWS, :] = jnp.where(first, -shift, 0.0).astype(BF16)
        acc_s[g] = jnp.zeros((V_ROWS, gw), F32)
        p_s[1, g] = jnp.zeros((ch, gw), BF16)

    def weigh_values(c, slot):
        for g in range(ATTN_KV_HEADS):
            acc_s[g] += jnp.dot(vt_chunk(c, g), p_s[slot, g], preferred_element_type=F32)

    def att_chunk(c, near, slots=None):
        prev_slot, cur_slot = slots if slots is not None else ((c + 1) % 2, c % 2)
        weigh_values(jnp.maximum(c - 1, 0), prev_slot)
        s0 = chunk_start(c)
        pen = pen_of(s0, near)
        kblk = k_ref[0, pl.ds(s0, ch), :]
        for g in range(ATTN_KV_HEADS):
            s = jnp.dot(kblk[:, g * K_AUG:(g + 1) * K_AUG], qcat[g],
                        preferred_element_type=F32)
            if near:
                s = s + near_bias(c, g)
            p_s[cur_slot, g] = jnp.exp2(s + pen).astype(BF16)

    def fused(c, near, slots=None):
        plane_chunk(c, slot_a)
        att_chunk(c, near, slots)
        score_chunk(c + 1, blk_a, slot_a)

    def far_pair(j, carry):
        fused(2 * j, False, (1, 0))
        fused(2 * j + 1, False, (0, 1))
        return carry

    def far_quad(j, carry):
        far_pair(2 * j, carry)
        return far_pair(2 * j + 1, carry)

    def far_body(c, carry):
        fused(c, False, (1, 0))
        return carry

    def near_body(c, carry):
        fused(c, True)
        return carry

    score_chunk(0, blk_a, slot_a)
    n_far = jnp.maximum(qi - 1, 0) // ch_blks
    lax.fori_loop(0, n_far // 4, far_quad, 0)
    lax.fori_loop((n_far // 4) * 2, n_far // 2, far_pair, 0)
    lax.fori_loop((n_far // 2) * 2, n_far, far_body, 0)
    lax.fori_loop(n_far, nch_b - 1, near_body, 0)
    plane_chunk(nch_b - 1, slot_a)
    att_chunk(nch_b - 1, True)
    weigh_values(nch_b - 1, (nch_b - 1) % 2)

    @pl.when(nch_a > nch_b)
    def _one_more_chunk():
        score_chunk(nch_b, blk_a, slot_a)
        plane_chunk(nch_b, slot_a)

    tied, break_ties = select(blk_a, slot_a)

    n_ok = jnp.zeros((1, gw), i32)
    for g in range(ATTN_KV_HEADS):
        a = acc_s[g]
        den = a[ATTN_HEAD_DIM:ATTN_HEAD_DIM + 1, :]
        top = jnp.max(jnp.abs(a[0:ATTN_HEAD_DIM, :]), axis=0, keepdims=True)
        n_ok = n_ok + ((den > DEN_LO) & (den < DEN_HI) & (top < F32_HUGE)).astype(i32)

    rare = (any_lane(tied) + 2 * any_lane((n_ok < ATTN_KV_HEADS).astype(i32)))[0, 0]
    pl.when(rare % 2 == 1)(break_ties)

    @pl.when(rare >= 2)
    def _exact():
        for g in range(ATTN_KV_HEADS):
            qcat[g, ATTN_HEAD_DIM:ATTN_HEAD_DIM + SHIFT_ROWS, :] = jnp.zeros((SHIFT_ROWS, gw), BF16)
            m_s[g] = jnp.full((1, gw), NEG, F32)
            acc_s[g] = jnp.zeros((V_ROWS, gw), F32)

        def body(c, carry):
            s0 = chunk_start(c)
            pen = pen_of(s0, True)
            kblk = k_ref[0, pl.ds(s0, ch), :]
            for g in range(ATTN_KV_HEADS):
                s = jnp.dot(kblk[:, g * K_AUG:(g + 1) * K_AUG], qcat[g],
                            preferred_element_type=F32) + near_bias(c, g) + pen
                m_old = m_s[g]
                m_new = jnp.maximum(m_old, jnp.max(s, axis=0, keepdims=True))
                p = jnp.exp2(s - m_new).astype(BF16)
                acc_s[g] = acc_s[g] * jnp.exp2(m_old - m_new) + jnp.dot(
                    vt_chunk(c, g), p, preferred_element_type=F32)
                m_s[g] = m_new
            return carry
        lax.fori_loop(0, nch_b, body, 0)

    parts = []
    for g in range(ATTN_KV_HEADS):
        a = acc_s[g]
        o = a[0:ATTN_HEAD_DIM, :] * (1.0 / a[ATTN_HEAD_DIM:ATTN_HEAD_DIM + 1, :])
        parts += [o[:, r * tq:(r + 1) * tq] for r in range(rep)]
    out_t = jnp.concatenate(parts, axis=0)
    o_ref[0] = _rms(jnp.transpose(out_t), nw_ref[...]).astype(BF16)


def _dsa(rel_bias, q, qi, misc, k, v, miscb, nw):
    bsz, t, _ = q.shape
    nq = t // BLK
    topk = min(TOPK_MAX, t // 4)
    idx_bits = max(1, (t - 1).bit_length())
    gw = (ATTN_HEADS // ATTN_KV_HEADS) * BLK
    bkt = jnp.asarray(_bucket_tiles())
    kern = functools.partial(_dsa_kernel, topk=topk, idx_bits=idx_bits, nq=nq)

    def nxt(b, i):
        return (b, jnp.minimum(i + 1, nq - 1), 0)

    return pl.pallas_call(
        kern,
        grid=(bsz, nq),
        in_specs=[pl.BlockSpec(memory_space=pltpu.SMEM),
                  pl.BlockSpec((2, BLK, BLK), lambda b, i: (0, 0, 0)),
                  pl.BlockSpec((1, BLK, D_ATTN), lambda b, i: (b, i, 0)),
                  pl.BlockSpec((1, BLK, IDX_HEADS * IDX_DIM), lambda b, i: (b, i, 0)),
                  pl.BlockSpec((1, BLK, MISC_W), lambda b, i: (b, i, 0)),
                  pl.BlockSpec((1, BLK, IDX_HEADS * IDX_DIM), nxt),
                  pl.BlockSpec((1, BLK, MISC_W), nxt),
                  pl.BlockSpec((1, t, D_K_AUG), lambda b, i: (b, 0, 0)),
                  pl.BlockSpec((1, t, D_KV), lambda b, i: (b, 0, 0)),
                  pl.BlockSpec((1, t, MISC_W), lambda b, i: (b, 0, 0)),
                  pl.BlockSpec((1, D_ATTN), lambda b, i: (0, 0))],
        out_specs=pl.BlockSpec((1, BLK, D_ATTN), lambda b, i: (b, i, 0)),
        out_shape=jax.ShapeDtypeStruct((bsz, t, D_ATTN), BF16),
        scratch_shapes=[pltpu.VMEM((2, t, BLK), jnp.int32),
                        pltpu.VMEM((32, t // 32, BLK), jnp.int32),
                        pltpu.VMEM((t // 32, BLK), jnp.int32),
                        pltpu.VMEM((2, 1, BLK), jnp.int32),
                        pltpu.VMEM((2, 1, BLK), jnp.int32),
                        pltpu.VMEM((3, ATTN_KV_HEADS, BLK, gw), F32),
                        pltpu.VMEM((ATTN_KV_HEADS, K_AUG, gw), BF16),
                        pltpu.VMEM((IDX_DIM, IDX_HEADS * BLK), BF16),
                        pltpu.VMEM((8, IDX_HEADS * BLK), F32),
                        pltpu.VMEM((ATTN_KV_HEADS, 1, gw), F32),
                        pltpu.VMEM((ATTN_KV_HEADS, V_ROWS, gw), F32),
                        pltpu.VMEM((2, ATTN_KV_HEADS, DSA_CHUNK, gw), BF16),
                        pltpu.VMEM((nq, ATTN_KV_HEADS, V_ROWS, BLK), BF16)],
        compiler_params=pltpu.CompilerParams(
            dimension_semantics=("arbitrary", "arbitrary"), vmem_limit_bytes=48 * 2 ** 20),
        name="dsa",
    )(rel_bias, bkt, q, qi, misc, qi, misc, k, v, miscb, nw)


def _tail_kernel(x_ref, ys_ref, ya_ref, mod_ref, wo_ref, n1_ref, n2_ref, n3_ref, w1_ref, w2_ref, o_ref):
    x = x_ref[0]
    mix = (jnp.dot(ys_ref[0], wo_ref[0:D_SSM, :], preferred_element_type=F32)
           + jnp.dot(ya_ref[0], wo_ref[D_SSM:, :], preferred_element_type=F32))
    x1 = x + mod_ref[0, 2:3, :] * _rms(mix, n1_ref[...])
    h = (_rms(x1, n2_ref[...]) * (1.0 + mod_ref[0, 4:5, :]) + mod_ref[0, 3:4, :]).astype(BF16)
    f = jnp.zeros(x.shape, F32)
    step = D_MODEL
    for c0 in range(0, D_FF, step):
        a = jnp.maximum(jnp.dot(h, w1_ref[:, c0:c0 + step], preferred_element_type=F32), 0.0)
        f = f + jnp.dot((a * a).astype(BF16), w2_ref[c0:c0 + step, :], preferred_element_type=F32)
    o_ref[0] = x1 + mod_ref[0, 5:6, :] * _rms(f, n3_ref[...])


def _tail(x, ys, ya, mod, wo, n1, n2, n3, w1, w2):
    bsz, t, d = x.shape
    tm = min(ROW_TILE, t)

    def const(shape):
        return pl.BlockSpec(shape, lambda b, i: tuple(0 for _ in shape))

    return pl.pallas_call(
        _tail_kernel,
        grid=(bsz, t // tm),
        in_specs=[pl.BlockSpec((1, tm, d), lambda b, i: (b, i, 0)),
                  pl.BlockSpec((1, tm, D_SSM), lambda b, i: (b, i, 0)),
                  pl.BlockSpec((1, tm, D_ATTN), lambda b, i: (b, i, 0)),
                  pl.BlockSpec((1, 6, d), lambda b, i: (b, 0, 0)),
                  const((D_SSM + D_ATTN, d)), const((1, d)), const((1, d)), const((1, d)),
                  const((d, D_FF)), const((D_FF, d))],
        out_specs=pl.BlockSpec((1, tm, d), lambda b, i: (b, i, 0)),
        out_shape=jax.ShapeDtypeStruct((bsz, t, d), F32),
        compiler_params=pltpu.CompilerParams(
            dimension_semantics=("arbitrary", "arbitrary"), vmem_limit_bytes=60 * 2 ** 20),
        name="tail",
    )(x, ys, ya, mod, wo, n1, n2, n3, w1, w2)


def _permute_w_in(w):
    sizes = [D_SSM, D_XBC, SSM_HEADS, D_ATTN, D_KV, D_KV, IDX_HEADS * IDX_DIM, IDX_DIM, IDX_HEADS]
    pts = np.cumsum(sizes)[:-1].tolist()
    z, xbc, dt, q, k, v, qidx, kidx, widx = jnp.split(w, pts, axis=-1)
    pad = jnp.zeros((w.shape[0], MISC_W - IDX_DIM - IDX_HEADS - SSM_HEADS), w.dtype)
    kpad = jnp.zeros((w.shape[0], K_AUG - ATTN_HEAD_DIM), w.dtype)
    k_aug = []
    for g in range(ATTN_KV_HEADS):
        k_aug += [k[:, g * ATTN_HEAD_DIM:(g + 1) * ATTN_HEAD_DIM], kpad]
    return jnp.concatenate([z, xbc, q] + k_aug + [v, qidx, kidx, widx, dt, pad], axis=-1).astype(BF16)


def _layer(x, c_pad, w_ada, b_ada, norm1_pre, norm1_post, w_in, conv_w, conv_b, dt_bias, a_log, d_skip,
           ssm_norm, rel_bias, attn_norm, w_out, norm2_pre, norm2_post, w_mlp_in, w_mlp_out):
    bsz, t, d = x.shape
    mod = _ada(c_pad, w_ada, b_ada[None, :])[:bsz].reshape(bsz, 6, d)
    z, xbc, q, k, v, qi, misc, miscb = _inproj(x, mod, norm1_pre[None, :], _permute_w_in(w_in))

    lane = jnp.arange(MISC_W)
    is_dt = (lane >= MISC_DT) & (lane < MISC_DT + SSM_HEADS)
    head_of = jnp.clip(lane - MISC_DT, 0, SSM_HEADS - 1)
    dtb_t = jnp.where(is_dt, dt_bias[head_of], 0.0)[None, :]
    a_t = jnp.where(is_dt, (-jnp.exp(a_log.astype(F32)) * LOG2E)[head_of], 0.0)[None, :]
    expand = ((lane[:, None] - MISC_DT) == (jnp.arange(D_SSM)[None, :] // SSM_HEAD_DIM)).astype(BF16)
    dskip_e = jnp.repeat(d_skip.astype(F32), SSM_HEAD_DIM)[None, :]
    y_ssd = _ssd(xbc, z, misc, conv_w, conv_b[None, :], dtb_t, a_t, expand, dskip_e, ssm_norm[None, :])

    y_att = _dsa(rel_bias, q, qi, misc, k, v, miscb, attn_norm[None, :])

    return _tail(x, y_ssd, y_att, mod, w_out.astype(BF16), norm1_post[None, :], norm2_pre[None, :],
                 norm2_post[None, :], w_mlp_in.astype(BF16), w_mlp_out.astype(BF16))


def kernel(x, c, w_ada, b_ada, norm1_pre, norm1_post, w_in, conv_w, conv_b, dt_bias, a_log, d_skip,
           ssm_norm, rel_bias, attn_norm, w_out, norm2_pre, norm2_post, w_mlp_in, w_mlp_out):
    bsz = x.shape[0]
    assert bsz <= 8 and x.shape[1] % DSA_CHUNK == 0 and x.shape[2] == D_MODEL
    c_pad = jnp.zeros((8, D_MODEL), F32).at[:bsz].set(c)
    for l in range(w_ada.shape[0]):
        x = _layer(x, c_pad, w_ada[l], b_ada[l], norm1_pre[l], norm1_post[l], w_in[l], conv_w[l], conv_b[l],
                   dt_bias[l], a_log[l], d_skip[l], ssm_norm[l], rel_bias, attn_norm[l], w_out[l],
                   norm2_pre[l], norm2_post[l], w_mlp_in[l], w_mlp_out[l])
    return x
```

```python
import functools
import math

import numpy as np
import jax
import jax.numpy as jnp
from jax import lax
from jax.experimental import pallas as pl
from jax.experimental.pallas import tpu as pltpu

D_MODEL = 1024
SSM_HEADS = 8
SSM_HEAD_DIM = 64
D_SSM = SSM_HEADS * SSM_HEAD_DIM
SSM_GROUPS = 2
SSM_STATE = 128
CONV_WIDTH = 4
CHUNK = 256
ATTN_HEADS = 8
ATTN_KV_HEADS = 2
ATTN_HEAD_DIM = 64
D_ATTN = ATTN_HEADS * ATTN_HEAD_DIM
D_KV = ATTN_KV_HEADS * ATTN_HEAD_DIM
IDX_HEADS = 8
IDX_DIM = 64
TOPK_MAX = 256
N_BUCKETS = 32
MAX_DISTANCE = 128
D_FF = 4 * D_MODEL
D_BC = SSM_GROUPS * SSM_STATE
D_XBC = D_SSM + 2 * D_BC
EPS = 1e-6

LANES = 128
MISC_W = LANES
MISC_KIDX = 0
MISC_WIDX = IDX_DIM
MISC_DT = IDX_DIM + IDX_HEADS
K_AUG = LANES
D_K_AUG = ATTN_KV_HEADS * K_AUG
SHIFT_ROWS = 16
OFF_Z = 0
OFF_XBC = OFF_Z + D_SSM
OFF_Q = OFF_XBC + D_XBC
OFF_K = OFF_Q + D_ATTN
OFF_QI = OFF_K + D_K_AUG
OFF_V = OFF_QI + IDX_HEADS * IDX_DIM
OFF_MISC = OFF_V + D_KV
D_IN_PAD = OFF_MISC + MISC_W

BLK = 128
DSA_CHUNK = 512
V_ROWS = 80
SEL_SIZES = 4
DEN_LO = 2.0 ** -80
DEN_HI = 2.0 ** 80
F32_HUGE = 3.0e38
LOG2E = math.log2(math.e)
ROW_TILE = 512
NEG = -1e30
INT_MIN = -2 ** 31

F32 = jnp.float32
BF16 = jnp.bfloat16
HI = lax.Precision.HIGHEST
NT = (((1,), (1,)), ((), ()))


def _silu(x):
    return x / (1.0 + jnp.exp(-x))


def _rms(x, w):
    return x * lax.rsqrt(jnp.mean(x * x, axis=-1, keepdims=True) + EPS) * w


def _split3(x):
    hi = x.astype(BF16)
    rest = x - hi.astype(F32)
    mid = rest.astype(BF16)
    return hi, mid, (rest - mid.astype(F32)).astype(BF16)


def _ada_kernel(c_ref, w_ref, b_ref, o_ref):
    s = _silu(c_ref[...])
    o_ref[...] = jnp.dot(s, w_ref[...], precision=HI, preferred_element_type=F32) + b_ref[...]


def _ada(c_pad, w, b):
    n = w.shape[1]
    tn = D_MODEL
    return pl.pallas_call(
        _ada_kernel,
        grid=(n // tn,),
        in_specs=[pl.BlockSpec((8, D_MODEL), lambda j: (0, 0)),
                  pl.BlockSpec((D_MODEL, tn), lambda j: (0, j)),
                  pl.BlockSpec((1, tn), lambda j: (0, j))],
        out_specs=pl.BlockSpec((8, tn), lambda j: (0, j)),
        out_shape=jax.ShapeDtypeStruct((8, n), F32),
        name="ada",
    )(c_pad, w, b)


def _inproj_kernel(x_ref, mod_ref, nw_ref, w_ref,
                   z_ref, xbc_ref, q_ref, k_ref, v_ref, qi_ref, misc_ref, miscb_ref):
    x = x_ref[0]
    h = _rms(x, nw_ref[...]) * (1.0 + mod_ref[0, 1:2, :]) + mod_ref[0, 0:1, :]
    hb = h.astype(BF16)

    def seg(lo, width):
        return jnp.dot(hb, w_ref[:, lo:lo + width], preferred_element_type=F32)

    z_ref[0] = seg(OFF_Z, D_SSM)
    xbc_ref[0] = seg(OFF_XBC, D_XBC)
    q_ref[0] = (seg(OFF_Q, D_ATTN) * (ATTN_HEAD_DIM ** -0.5 * LOG2E)).astype(BF16)
    ones_col = lax.broadcasted_iota(jnp.int32, (1, D_K_AUG), 1) % K_AUG == ATTN_HEAD_DIM
    k_ref[0] = jnp.where(ones_col, 1.0, seg(OFF_K, D_K_AUG)).astype(BF16)
    qi_ref[0] = seg(OFF_QI, IDX_HEADS * IDX_DIM).astype(BF16)
    v_misc = seg(OFF_V, D_KV + MISC_W)
    v_ref[0] = v_misc[:, :D_KV].astype(BF16)
    misc = v_misc[:, D_KV:]
    misc_ref[0] = misc
    miscb_ref[0] = misc.astype(BF16)


def _inproj(x, mod, nw, w_perm):
    bsz, t, d = x.shape
    tm = min(ROW_TILE, t)

    def tok(width, dtype):
        return (pl.BlockSpec((1, tm, width), lambda b, i: (b, i, 0)),
                jax.ShapeDtypeStruct((bsz, t, width), dtype))

    outs = [tok(D_SSM, F32), tok(D_XBC, F32), tok(D_ATTN, BF16), tok(D_K_AUG, BF16), tok(D_KV, BF16),
            tok(IDX_HEADS * IDX_DIM, BF16), tok(MISC_W, F32), tok(MISC_W, BF16)]
    return pl.pallas_call(
        _inproj_kernel,
        grid=(bsz, t // tm),
        in_specs=[pl.BlockSpec((1, tm, d), lambda b, i: (b, i, 0)),
                  pl.BlockSpec((1, 6, d), lambda b, i: (b, 0, 0)),
                  pl.BlockSpec((1, d), lambda b, i: (0, 0)),
                  pl.BlockSpec((d, D_IN_PAD), lambda b, i: (0, 0))],
        out_specs=[o[0] for o in outs],
        out_shape=[o[1] for o in outs],
        compiler_params=pltpu.CompilerParams(
            dimension_semantics=("arbitrary", "arbitrary"), vmem_limit_bytes=48 * 2 ** 20),
        name="inproj",
    )(x, mod, nw, w_perm)


def _ssd_kernel(xbc_ref, z_ref, misc_ref, cw_ref, cb_ref, dtb_ref, a_ref, e_ref, dskip_ref, nw_ref,
                o_ref, xbuf, state):
    c = pl.program_id(1)
    lc = xbc_ref.shape[1]

    @pl.when(c == 0)
    def _():
        xbuf[0:8, :] = jnp.zeros((8, D_XBC), F32)
        state[...] = jnp.zeros(state.shape, F32)

    @pl.when(c > 0)
    def _():
        xbuf[0:8, :] = xbuf[lc:lc + 8, :]

    xbuf[8:lc + 8, :] = xbc_ref[0]
    u = cb_ref[...]
    xall = xbuf[...]
    for k in range(CONV_WIDTH):
        back = CONV_WIDTH - 1 - k
        xk = xall if back == 0 else pltpu.roll(xall, back, 0)
        u = u + xk[8:8 + lc] * cw_ref[k:k + 1, :]
    u = _silu(u)
    xs = u[:, :D_SSM]
    bm = u[:, D_SSM:D_SSM + D_BC]
    cm = u[:, D_SSM + D_BC:]

    raw = misc_ref[0] + dtb_ref[...]
    dt_t = jnp.maximum(raw, 0.0) + jnp.log1p(jnp.exp(-jnp.abs(raw)))
    adt_t = dt_t * a_ref[...]
    row = lax.broadcasted_iota(jnp.int32, (lc, lc), 0)
    col = lax.broadcasted_iota(jnp.int32, (lc, lc), 1)
    causal = col <= row
    tril = jnp.where(causal, 1.0, 0.0).astype(BF16)
    expand = e_ref[...]
    acs_t = sum(jnp.dot(tril, piece, preferred_element_type=F32) for piece in _split3(adt_t))
    dt_e = sum(jnp.dot(piece, expand, preferred_element_type=F32) for piece in _split3(dt_t))
    acs_e = sum(jnp.dot(piece, expand, preferred_element_type=F32) for piece in _split3(acs_t))
    acs_row = jnp.transpose(acs_t)

    xdt = xs * dt_e
    a_last = acs_e[lc - 1:lc, :]
    xdec = (xdt * jnp.exp2(a_last - acs_e)).astype(BF16)
    xdt_b = xdt.astype(BF16)
    bm_b = bm.astype(BF16)
    cm_b = cm.astype(BF16)
    bm_t = jnp.transpose(bm).astype(BF16)

    heads_per_group = SSM_HEADS // SSM_GROUPS
    gw = heads_per_group * SSM_HEAD_DIM
    y_parts = []
    off_parts = []
    for g in range(SSM_GROUPS):
        bg = bm_b[:, g * SSM_STATE:(g + 1) * SSM_STATE]
        cg = cm_b[:, g * SSM_STATE:(g + 1) * SSM_STATE]
        cb = lax.dot_general(cg, bg, NT, preferred_element_type=F32)
        for r in range(heads_per_group):
            hd = g * heads_per_group + r
            a_col = acs_t[:, MISC_DT + hd:MISC_DT + hd + 1]
            a_row = acs_row[MISC_DT + hd:MISC_DT + hd + 1, :]
            lmat = jnp.exp2(jnp.where(causal, a_col - a_row, -jnp.inf))
            mh = (cb * lmat).astype(BF16)
            y_parts.append(jnp.dot(mh, xdt_b[:, hd * SSM_HEAD_DIM:(hd + 1) * SSM_HEAD_DIM],
                                   preferred_element_type=F32))
        prev = state[g]
        off_parts.append(jnp.dot(cg, prev.astype(BF16), preferred_element_type=F32))
        st = jnp.dot(bm_t[g * SSM_STATE:(g + 1) * SSM_STATE, :], xdec[:, g * gw:(g + 1) * gw],
                     preferred_element_type=F32)
        state[g] = prev * jnp.exp2(a_last[:, g * gw:(g + 1) * gw]) + st
    y = (jnp.concatenate(y_parts, axis=1)
         + jnp.concatenate(off_parts, axis=1) * jnp.exp2(acs_e)
         + xs * dskip_ref[...])
    o_ref[0] = _rms(y * _silu(z_ref[0]), nw_ref[...]).astype(BF16)


def _ssd(xbc, z, misc, cw, cb, dtb_t, a_t, expand, dskip_e, nw):
    bsz, t, _ = xbc.shape
    lc = math.gcd(t, CHUNK)
    nc = t // lc
    gw = (SSM_HEADS // SSM_GROUPS) * SSM_HEAD_DIM

    def const(shape):
        return pl.BlockSpec(shape, lambda b, c: tuple(0 for _ in shape))

    return pl.pallas_call(
        _ssd_kernel,
        grid=(bsz, nc),
        in_specs=[pl.BlockSpec((1, lc, D_XBC), lambda b, c: (b, c, 0)),
                  pl.BlockSpec((1, lc, D_SSM), lambda b, c: (b, c, 0)),
                  pl.BlockSpec((1, lc, MISC_W), lambda b, c: (b, c, 0)),
                  const((CONV_WIDTH, D_XBC)), const((1, D_XBC)), const((1, MISC_W)), const((1, MISC_W)),
                  const((MISC_W, D_SSM)), const((1, D_SSM)), const((1, D_SSM))],
        out_specs=pl.BlockSpec((1, lc, D_SSM), lambda b, c: (b, c, 0)),
        out_shape=jax.ShapeDtypeStruct((bsz, t, D_SSM), BF16),
        scratch_shapes=[pltpu.VMEM((lc + 8, D_XBC), F32),
                        pltpu.VMEM((SSM_GROUPS, SSM_STATE, gw), F32)],
        compiler_params=pltpu.CompilerParams(
            dimension_semantics=("arbitrary", "arbitrary"), vmem_limit_bytes=48 * 2 ** 20),
        name="ssd",
    )(xbc, z, misc, cw, cb, dtb_t, a_t, expand, dskip_e, nw)


def _bucket_tiles():
    s = np.arange(BLK)[:, None]
    t = np.arange(BLK)[None, :]
    max_exact = N_BUCKETS // 2
    tiles = []
    for d0 in (0, BLK):
        n = np.maximum(d0 + t - s, 0)
        large = {}
        for dt_ in (np.float32, np.float64):
            nf = np.maximum(n, 1).astype(dt_)
            lg = max_exact + (np.log(nf / dt_(max_exact)) / dt_(math.log(MAX_DISTANCE / max_exact))
                              * dt_(N_BUCKETS - max_exact)).astype(np.int32)
            large[dt_] = np.minimum(lg, N_BUCKETS - 1)
        assert (large[np.float32] == large[np.float64]).all()
        tiles.append(np.where(n < max_exact, n, large[np.float64]).astype(np.int32))
    assert BLK + 1 >= MAX_DISTANCE
    return np.stack(tiles)


def _to_key(x):
    bits = lax.bitcast_convert_type(x, jnp.int32)
    key = bits ^ (lax.shift_right_arithmetic(bits, 31) & jnp.int32(0x7FFFFFFF))
    return jnp.where(key == -1, 0, key)


def _transpose32(rows):
    rows = list(rows)
    mask, j = 0x0000FFFF, 16
    while j:
        k = 0
        while k < 32:
            t = (rows[k] ^ lax.shift_right_logical(rows[k + j], jnp.int32(j))) & jnp.int32(mask)
            rows[k] = rows[k] ^ t
            rows[k + j] = rows[k + j] ^ lax.shift_left(t, jnp.int32(j))
            k = (k + j + 1) & ~j
        j >>= 1
        mask ^= (mask << j) & 0xFFFFFFFF
    return rows


def _dsa_kernel(relb_ref, bkt_ref, q_ref, qic_ref, miscc_ref, qin_ref, miscn_ref, k_ref, v_ref, kidx_ref,
                nw_ref, o_ref, keys, planes, live, thr_s, cnt_s, bias_s, qcat, qicat, wcat, m_s, acc_s, p_s, vt_s,
                *, topk, idx_bits, nq):
    b = pl.program_id(0)
    qi = pl.program_id(1)
    tq = BLK
    i32 = jnp.int32
    rep = ATTN_HEADS // ATTN_KV_HEADS
    gw = rep * tq
    ch = DSA_CHUNK
    ch_blks = ch // BLK
    rows_per_chunk = ch // 32

    @pl.when((b == 0) & (qi == 0))
    def _init_bias():
        for d in range(2):
            bk = bkt_ref[d]
            for h in range(ATTN_HEADS):
                far = relb_ref[N_BUCKETS - 1, h]
                acc = jnp.zeros((BLK, tq), F32)
                for n in range(N_BUCKETS - 1):
                    acc = jnp.where(bk == n, relb_ref[n, h] - far, acc)
                bias_s[d, h // rep, :, (h % rep) * BLK:(h % rep + 1) * BLK] = acc * LOG2E
        bias_s[2] = jnp.zeros((ATTN_KV_HEADS, BLK, gw), F32)
        for w in range(32):
            planes[w] = jnp.zeros(planes.shape[1:], i32)

    row_c = lax.broadcasted_iota(i32, (ch, tq), 0)
    lane_c = lax.broadcasted_iota(i32, (ch, tq), 1)

    def chunk_start(c):
        return pl.multiple_of(c * ch, ch)

    def n_chunks(blk):
        return (blk + ch_blks) // ch_blks

    def load_indexer(qidx_ref, misc_ref):
        qi_t = jnp.transpose(qidx_ref[0].astype(F32))
        w_t = jnp.transpose(misc_ref[0])[MISC_WIDX:MISC_WIDX + IDX_HEADS, :] * (
            IDX_HEADS ** -0.5 * IDX_DIM ** -0.5)
        for h in range(IDX_HEADS):
            qicat[:, h * BLK:(h + 1) * BLK] = qi_t[h * IDX_DIM:(h + 1) * IDX_DIM, :].astype(BF16)
            wcat[:, h * BLK:(h + 1) * BLK] = jnp.broadcast_to(w_t[h:h + 1, :], (8, BLK))

    def score_chunk(c, blk, slot):
        s0 = chunk_start(c)
        kb = kidx_ref[0, pl.ds(s0, ch), MISC_KIDX:MISC_KIDX + IDX_DIM]
        s = jnp.dot(kb, qicat[...], preferred_element_type=F32)
        r = jnp.maximum(s, 0.0) * wcat[0:1, :]
        acc = r[:, 0:BLK]
        for h in range(1, IDX_HEADS):
            acc = acc + r[:, h * BLK:(h + 1) * BLK]
        acc = jnp.where(s0 + row_c <= blk * BLK + lane_c, acc, -jnp.inf)
        keys[slot, pl.ds(s0, ch), :] = _to_key(acc)

    def plane_chunk(c, slot):
        ku = (keys[slot, pl.ds(chunk_start(c), ch), :] ^ i32(INT_MIN)).reshape(ch // (32 * 8), 32, 8, tq)
        words = _transpose32([ku[:, j] for j in range(32)])
        r0 = pl.multiple_of(c * rows_per_chunk, rows_per_chunk)
        for w in range(32):
            planes[w, pl.ds(r0, rows_per_chunk), :] = words[w].reshape(rows_per_chunk, tq)

    def select(blk, slot):
        nch = n_chunks(blk)
        used = nch * rows_per_chunk

        def ones_in(words):
            return jnp.sum(lax.population_count(words), axis=0, keepdims=True)

        def passes(nr):
            live[0:nr] = jnp.where(lax.broadcasted_iota(i32, (nr, tq), 0) < used, i32(-1), i32(0))

            def bit_body(w, st):
                thr_u, n_above, n_set = st
                take = n_above + n_set >= topk
                thr_u = jnp.where(take, thr_u | lax.shift_left(i32(1), i32(31) - w), thr_u)
                n_above = jnp.where(take, n_above, n_above + n_set)
                still = live[0:nr] & (planes[w, 0:nr] ^ jnp.where(take, i32(0), i32(-1)))
                live[0:nr] = still
                return thr_u, n_above, ones_in(still & planes[jnp.minimum(w + 1, 31), 0:nr])

            zero = jnp.zeros((1, tq), i32)
            thr_u, n_above, _ = lax.fori_loop(
                0, 32, bit_body, (zero, zero, ones_in(live[0:nr] & planes[0, 0:nr])))
            thr_s[slot] = thr_u ^ i32(INT_MIN)
            cnt_s[0] = n_above
            cnt_s[1] = ones_in(live[0:nr])

        size_step = live.shape[0] // SEL_SIZES
        for v in range(SEL_SIZES):
            pl.when((used > v * size_step) & (used <= (v + 1) * size_step))(
                functools.partial(passes, (v + 1) * size_step))
        thr = thr_s[slot]
        n_gt = cnt_s[0]
        n_eq = cnt_s[1]
        need = topk - n_gt

        def break_ties():
            def count(pred):
                def body(c, cnt):
                    s0 = chunk_start(c)
                    hit = pred(keys[slot, pl.ds(s0, ch), :], s0).astype(i32)
                    n = ch
                    while n > 8:
                        n //= 2
                        hit = hit[:n] + hit[n:2 * n]
                    return cnt + hit
                cnt = lax.fori_loop(0, nch, body, jnp.zeros((8, tq), i32))
                return jnp.sum(cnt, axis=0, keepdims=True)

            def jbody(it, p):
                cand = p + lax.shift_left(i32(1), i32(idx_bits - 1) - it)
                c = count(lambda kk, s0: (kk == thr) & (s0 + row_c < cand))
                return jnp.where(c < need, cand, p)
            last = lax.fori_loop(0, idx_bits, jbody, jnp.zeros((1, tq), i32))

            def demote(c, carry):
                s0 = chunk_start(c)
                kk = keys[slot, pl.ds(s0, ch), :]
                keys[slot, pl.ds(s0, ch), :] = jnp.where((kk == thr) & (s0 + row_c > last), INT_MIN, kk)
                return carry
            lax.fori_loop(0, nch, demote, 0)

        return (n_eq > need).astype(i32), break_ties

    def any_lane(flags):
        return jnp.max(flags, axis=1, keepdims=True)

    @pl.when(qi == 0)
    def _first_block():
        ones_tile = jnp.where(lax.broadcasted_iota(i32, (V_ROWS - ATTN_HEAD_DIM, BLK), 0) == 0, 1.0, 0.0)

        def value_block(j, carry):
            v_t = jnp.transpose(v_ref[0, pl.ds(pl.multiple_of(j * BLK, BLK), BLK), :].astype(F32))
            for g in range(ATTN_KV_HEADS):
                vt_s[j, g, 0:ATTN_HEAD_DIM, :] = v_t[g * ATTN_HEAD_DIM:(g + 1) * ATTN_HEAD_DIM, :].astype(BF16)
                vt_s[j, g, ATTN_HEAD_DIM:, :] = ones_tile.astype(BF16)
            return carry
        lax.fori_loop(0, nq, value_block, 0)

        load_indexer(qic_ref, miscc_ref)
        score_chunk(0, 0, 0)
        plane_chunk(0, 0)
        tied, break_ties = select(0, 0)
        pl.when(any_lane(tied)[0, 0] > 0)(break_ties)

    slot_b = qi % 2
    slot_a = 1 - slot_b
    blk_a = jnp.minimum(qi + 1, nq - 1)
    nch_b = n_chunks(qi)
    nch_a = n_chunks(blk_a)
    load_indexer(qin_ref, miscn_ref)
    thr = thr_s[slot_b]

    q_t = jnp.transpose(q_ref[0].astype(F32))
    for h in range(ATTN_HEADS):
        qcat[h // rep, 0:ATTN_HEAD_DIM, (h % rep) * BLK:(h % rep + 1) * BLK] = (
            q_t[h * ATTN_HEAD_DIM:(h + 1) * ATTN_HEAD_DIM, :].astype(BF16))
    for g in range(ATTN_KV_HEADS):
        qcat[g, ATTN_HEAD_DIM:, :] = jnp.zeros((K_AUG - ATTN_HEAD_DIM, gw), BF16)

    def pen_of(s0, causal):
        sel = keys[slot_b, pl.ds(s0, ch), :] >= thr
        if causal:
            sel = sel & (s0 + row_c <= qi * BLK + lane_c)
        pen = jnp.where(sel, 0.0, NEG)
        return jnp.concatenate([pen] * rep, axis=1)

    def near_bias(c, g):
        return jnp.concatenate(
            [bias_s[jnp.clip(qi - (c * ch_blks + u), 0, 2), g] for u in range(ch_blks)], axis=0)

    def vt_chunk(c, g):
        return jnp.concatenate([vt_s[c * ch_blks + u, g] for u in range(ch_blks)], axis=1)

    s0d = pl.multiple_of(qi * BLK, BLK)
    kd = k_ref[0, pl.ds(s0d, BLK), :]
    tri = lax.broadcasted_iota(i32, (BLK, tq), 0) <= lax.broadcasted_iota(i32, (BLK, tq), 1)
    tri_pen = jnp.concatenate([jnp.where(tri, 0.0, NEG)] * rep, axis=1)
    for g in range(ATTN_KV_HEADS):
        sd = jnp.dot(kd[:, g * K_AUG:(g + 1) * K_AUG], qcat[g],
                     preferred_element_type=F32) + bias_s[0, g] + tri_pen
        shift = jnp.max(sd, axis=0, keepdims=True)
        first = lax.broadcasted_iota(i32, (SHIFT_ROWS, gw), 0) == 0
        qcat[g, ATTN_HEAD_DIM:ATTN_HEAD_DIM + SHIFT_ROWS, :] = jnp.where(first, -shift, 0.0).astype(BF16)
        acc_s[g] = jnp.zeros((V_ROWS, gw), F32)
        p_s[1, g] = jnp.zeros((ch, gw), BF16)

    def weigh_values(c, slot):
        for g in range(ATTN_KV_HEADS):
            acc_s[g] += jnp.dot(vt_chunk(c, g), p_s[slot, g], preferred_element_type=F32)

    def att_chunk(c, near, slots=None):
        prev_slot, cur_slot = slots if slots is not None else ((c + 1) % 2, c % 2)
        weigh_values(jnp.maximum(c - 1, 0), prev_slot)
        s0 = chunk_start(c)
        pen = pen_of(s0, near)
        kblk = k_ref[0, pl.ds(s0, ch), :]
        for g in range(ATTN_KV_HEADS):
            s = jnp.dot(kblk[:, g * K_AUG:(g + 1) * K_AUG], qcat[g],
                        preferred_element_type=F32)
            if near:
                s = s + near_bias(c, g)
            p_s[cur_slot, g] = jnp.exp2(s + pen).astype(BF16)

    def fused(c, near, slots=None):
        plane_chunk(c, slot_a)
        att_chunk(c, near, slots)
        score_chunk(c + 1, blk_a, slot_a)

    def far_pair(j, carry):
        fused(2 * j, False, (1, 0))
        fused(2 * j + 1, False, (0, 1))
        return carry

    def far_quad(j, carry):
        far_pair(2 * j, carry)
        return far_pair(2 * j + 1, carry)

    def far_body(c, carry):
        fused(c, False, (1, 0))
        return carry

    def near_body(c, carry):
        fused(c, True)
        return carry

    score_chunk(0, blk_a, slot_a)
    n_far = jnp.maximum(qi - 1, 0) // ch_blks
    lax.fori_loop(0, n_far // 4, far_quad, 0)
    lax.fori_loop((n_far // 4) * 2, n_far // 2, far_pair, 0)
    lax.fori_loop((n_far // 2) * 2, n_far, far_body, 0)
    lax.fori_loop(n_far, nch_b - 1, near_body, 0)
    @pl.when(nch_a > nch_b)
    def _one_more_chunk():
        score_chunk(nch_b, blk_a, slot_a)
        plane_chunk(nch_b, slot_a)

    plane_chunk(nch_b - 1, slot_a)
    att_chunk(nch_b - 1, True)
    weigh_values(nch_b - 1, (nch_b - 1) % 2)
    tied, break_ties = select(blk_a, slot_a)

    n_ok = jnp.zeros((1, gw), i32)
    for g in range(ATTN_KV_HEADS):
        a = acc_s[g]
        den = a[ATTN_HEAD_DIM:ATTN_HEAD_DIM + 1, :]
        top = jnp.max(jnp.abs(a[0:ATTN_HEAD_DIM, :]), axis=0, keepdims=True)
        n_ok = n_ok + ((den > DEN_LO) & (den < DEN_HI) & (top < F32_HUGE)).astype(i32)

    rare = (any_lane(tied) + 2 * any_lane((n_ok < ATTN_KV_HEADS).astype(i32)))[0, 0]
    pl.when(rare % 2 == 1)(break_ties)

    @pl.when(rare >= 2)
    def _exact():
        for g in range(ATTN_KV_HEADS):
            qcat[g, ATTN_HEAD_DIM:ATTN_HEAD_DIM + SHIFT_ROWS, :] = jnp.zeros((SHIFT_ROWS, gw), BF16)
            m_s[g] = jnp.full((1, gw), NEG, F32)
            acc_s[g] = jnp.zeros((V_ROWS, gw), F32)

        def body(c, carry):
            s0 = chunk_start(c)
            pen = pen_of(s0, True)
            kblk = k_ref[0, pl.ds(s0, ch), :]
            for g in range(ATTN_KV_HEADS):
                s = jnp.dot(kblk[:, g * K_AUG:(g + 1) * K_AUG], qcat[g],
                            preferred_element_type=F32) + near_bias(c, g) + pen
                m_old = m_s[g]
                m_new = jnp.maximum(m_old, jnp.max(s, axis=0, keepdims=True))
                p = jnp.exp2(s - m_new).astype(BF16)
                acc_s[g] = acc_s[g] * jnp.exp2(m_old - m_new) + jnp.dot(
                    vt_chunk(c, g), p, preferred_element_type=F32)
                m_s[g] = m_new
            return carry
        lax.fori_loop(0, nch_b, body, 0)

    parts = []
    for g in range(ATTN_KV_HEADS):
        a = acc_s[g]
        o = a[0:ATTN_HEAD_DIM, :] * (1.0 / a[ATTN_HEAD_DIM:ATTN_HEAD_DIM + 1, :])
        parts += [o[:, r * tq:(r + 1) * tq] for r in range(rep)]
    out_t = jnp.concatenate(parts, axis=0)
    o_ref[0] = _rms(jnp.transpose(out_t), nw_ref[...]).astype(BF16)


def _dsa(rel_bias, q, qi, misc, k, v, miscb, nw):
    bsz, t, _ = q.shape
    nq = t // BLK
    topk = min(TOPK_MAX, t // 4)
    idx_bits = max(1, (t - 1).bit_length())
    gw = (ATTN_HEADS // ATTN_KV_HEADS) * BLK
    bkt = jnp.asarray(_bucket_tiles())
    kern = functools.partial(_dsa_kernel, topk=topk, idx_bits=idx_bits, nq=nq)

    def nxt(b, i):
        return (b, jnp.minimum(i + 1, nq - 1), 0)

    return pl.pallas_call(
        kern,
        grid=(bsz, nq),
        in_specs=[pl.BlockSpec(memory_space=pltpu.SMEM),
                  pl.BlockSpec((2, BLK, BLK), lambda b, i: (0, 0, 0)),
                  pl.BlockSpec((1, BLK, D_ATTN), lambda b, i: (b, i, 0)),
                  pl.BlockSpec((1, BLK, IDX_HEADS * IDX_DIM), lambda b, i: (b, i, 0)),
                  pl.BlockSpec((1, BLK, MISC_W), lambda b, i: (b, i, 0)),
                  pl.BlockSpec((1, BLK, IDX_HEADS * IDX_DIM), nxt),
                  pl.BlockSpec((1, BLK, MISC_W), nxt),
                  pl.BlockSpec((1, t, D_K_AUG), lambda b, i: (b, 0, 0)),
                  pl.BlockSpec((1, t, D_KV), lambda b, i: (b, 0, 0)),
                  pl.BlockSpec((1, t, MISC_W), lambda b, i: (b, 0, 0)),
                  pl.BlockSpec((1, D_ATTN), lambda b, i: (0, 0))],
        out_specs=pl.BlockSpec((1, BLK, D_ATTN), lambda b, i: (b, i, 0)),
        out_shape=jax.ShapeDtypeStruct((bsz, t, D_ATTN), BF16),
        scratch_shapes=[pltpu.VMEM((2, t, BLK), jnp.int32),
                        pltpu.VMEM((32, t // 32, BLK), jnp.int32),
                        pltpu.VMEM((t // 32, BLK), jnp.int32),
                        pltpu.VMEM((2, 1, BLK), jnp.int32),
                        pltpu.VMEM((2, 1, BLK), jnp.int32),
                        pltpu.VMEM((3, ATTN_KV_HEADS, BLK, gw), F32),
                        pltpu.VMEM((ATTN_KV_HEADS, K_AUG, gw), BF16),
                        pltpu.VMEM((IDX_DIM, IDX_HEADS * BLK), BF16),
                        pltpu.VMEM((8, IDX_HEADS * BLK), F32),
                        pltpu.VMEM((ATTN_KV_HEADS, 1, gw), F32),
                        pltpu.VMEM((ATTN_KV_HEADS, V_ROWS, gw), F32),
                        pltpu.VMEM((2, ATTN_KV_HEADS, DSA_CHUNK, gw), BF16),
                        pltpu.VMEM((nq, ATTN_KV_HEADS, V_ROWS, BLK), BF16)],
        compiler_params=pltpu.CompilerParams(
            dimension_semantics=("arbitrary", "arbitrary"), vmem_limit_bytes=48 * 2 ** 20),
        name="dsa",
    )(rel_bias, bkt, q, qi, misc, qi, misc, k, v, miscb, nw)


def _tail_kernel(x_ref, ys_ref, ya_ref, mod_ref, wo_ref, n1_ref, n2_ref, n3_ref, w1_ref, w2_ref, o_ref):
    x = x_ref[0]
    mix = (jnp.dot(ys_ref[0], wo_ref[0:D_SSM, :], preferred_element_type=F32)
           + jnp.dot(ya_ref[0], wo_ref[D_SSM:, :], preferred_element_type=F32))
    x1 = x + mod_ref[0, 2:3, :] * _rms(mix, n1_ref[...])
    h = (_rms(x1, n2_ref[...]) * (1.0 + mod_ref[0, 4:5, :]) + mod_ref[0, 3:4, :]).astype(BF16)
    f = jnp.zeros(x.shape, F32)
    step = D_MODEL
    for c0 in range(0, D_FF, step):
        a = jnp.maximum(jnp.dot(h, w1_ref[:, c0:c0 + step], preferred_element_type=F32), 0.0)
        f = f + jnp.dot((a * a).astype(BF16), w2_ref[c0:c0 + step, :], preferred_element_type=F32)
    o_ref[0] = x1 + mod_ref[0, 5:6, :] * _rms(f, n3_ref[...])


def _tail(x, ys, ya, mod, wo, n1, n2, n3, w1, w2):
    bsz, t, d = x.shape
    tm = min(ROW_TILE, t)

    def const(shape):
        return pl.BlockSpec(shape, lambda b, i: tuple(0 for _ in shape))

    return pl.pallas_call(
        _tail_kernel,
        grid=(bsz, t // tm),
        in_specs=[pl.BlockSpec((1, tm, d), lambda b, i: (b, i, 0)),
                  pl.BlockSpec((1, tm, D_SSM), lambda b, i: (b, i, 0)),
                  pl.BlockSpec((1, tm, D_ATTN), lambda b, i: (b, i, 0)),
                  pl.BlockSpec((1, 6, d), lambda b, i: (b, 0, 0)),
                  const((D_SSM + D_ATTN, d)), const((1, d)), const((1, d)), const((1, d)),
                  const((d, D_FF)), const((D_FF, d))],
        out_specs=pl.BlockSpec((1, tm, d), lambda b, i: (b, i, 0)),
        out_shape=jax.ShapeDtypeStruct((bsz, t, d), F32),
        compiler_params=pltpu.CompilerParams(
            dimension_semantics=("arbitrary", "arbitrary"), vmem_limit_bytes=60 * 2 ** 20),
        name="tail",
    )(x, ys, ya, mod, wo, n1, n2, n3, w1, w2)


def _permute_w_in(w):
    sizes = [D_SSM, D_XBC, SSM_HEADS, D_ATTN, D_KV, D_KV, IDX_HEADS * IDX_DIM, IDX_DIM, IDX_HEADS]
    pts = np.cumsum(sizes)[:-1].tolist()
    z, xbc, dt, q, k, v, qidx, kidx, widx = jnp.split(w, pts, axis=-1)
    pad = jnp.zeros((w.shape[0], MISC_W - IDX_DIM - IDX_HEADS - SSM_HEADS), w.dtype)
    kpad = jnp.zeros((w.shape[0], K_AUG - ATTN_HEAD_DIM), w.dtype)
    k_aug = []
    for g in range(ATTN_KV_HEADS):
        k_aug += [k[:, g * ATTN_HEAD_DIM:(g + 1) * ATTN_HEAD_DIM], kpad]
    return jnp.concatenate([z, xbc, q] + k_aug + [qidx, v, kidx, widx, dt, pad], axis=-1).astype(BF16)


def _layer(x, c_pad, w_ada, b_ada, norm1_pre, norm1_post, w_in, conv_w, conv_b, dt_bias, a_log, d_skip,
           ssm_norm, rel_bias, attn_norm, w_out, norm2_pre, norm2_post, w_mlp_in, w_mlp_out):
    bsz, t, d = x.shape
    mod = _ada(c_pad, w_ada, b_ada[None, :])[:bsz].reshape(bsz, 6, d)
    z, xbc, q, k, v, qi, misc, miscb = _inproj(x, mod, norm1_pre[None, :], _permute_w_in(w_in))

    lane = jnp.arange(MISC_W)
    is_dt = (lane >= MISC_DT) & (lane < MISC_DT + SSM_HEADS)
    head_of = jnp.clip(lane - MISC_DT, 0, SSM_HEADS - 1)
    dtb_t = jnp.where(is_dt, dt_bias[head_of], 0.0)[None, :]
    a_t = jnp.where(is_dt, (-jnp.exp(a_log.astype(F32)) * LOG2E)[head_of], 0.0)[None, :]
    expand = ((lane[:, None] - MISC_DT) == (jnp.arange(D_SSM)[None, :] // SSM_HEAD_DIM)).astype(BF16)
    dskip_e = jnp.repeat(d_skip.astype(F32), SSM_HEAD_DIM)[None, :]
    y_ssd = _ssd(xbc, z, misc, conv_w, conv_b[None, :], dtb_t, a_t, expand, dskip_e, ssm_norm[None, :])

    y_att = _dsa(rel_bias, q, qi, misc, k, v, miscb, attn_norm[None, :])

    return _tail(x, y_ssd, y_att, mod, w_out.astype(BF16), norm1_post[None, :], norm2_pre[None, :],
                 norm2_post[None, :], w_mlp_in.astype(BF16), w_mlp_out.astype(BF16))


def kernel(x, c, w_ada, b_ada, norm1_pre, norm1_post, w_in, conv_w, conv_b, dt_bias, a_log, d_skip,
           ssm_norm, rel_bias, attn_norm, w_out, norm2_pre, norm2_post, w_mlp_in, w_mlp_out):
    bsz = x.shape[0]
    assert bsz <= 8 and x.shape[1] % DSA_CHUNK == 0 and x.shape[2] == D_MODEL
    c_pad = jnp.zeros((8, D_MODEL), F32).at[:bsz].set(c)
    for l in range(w_ada.shape[0]):
        x = _layer(x, c_pad, w_ada[l], b_ada[l], norm1_pre[l], norm1_post[l], w_in[l], conv_w[l], conv_b[l],
                   dt_bias[l], a_log[l], d_skip[l], ssm_norm[l], rel_bias, attn_norm[l], w_out[l],
                   norm2_pre[l], norm2_post[l], w_mlp_in[l], w_mlp_out[l])
    return x
```

```python
import functools
import math

import numpy as np
import jax
import jax.numpy as jnp
from jax import lax
from jax.experimental import pallas as pl
from jax.experimental.pallas import tpu as pltpu

D_MODEL = 1024
SSM_HEADS = 8
SSM_HEAD_DIM = 64
D_SSM = SSM_HEADS * SSM_HEAD_DIM
SSM_GROUPS = 2
SSM_STATE = 128
CONV_WIDTH = 4
CHUNK = 256
ATTN_HEADS = 8
ATTN_KV_HEADS = 2
ATTN_HEAD_DIM = 64
D_ATTN = ATTN_HEADS * ATTN_HEAD_DIM
D_KV = ATTN_KV_HEADS * ATTN_HEAD_DIM
IDX_HEADS = 8
IDX_DIM = 64
TOPK_MAX = 256
N_BUCKETS = 32
MAX_DISTANCE = 128
D_FF = 4 * D_MODEL
D_BC = SSM_GROUPS * SSM_STATE
D_XBC = D_SSM + 2 * D_BC
EPS = 1e-6

LANES = 128
V7X_VMEM_BYTES = 64 * 2 ** 20
VMEM_LIMIT = V7X_VMEM_BYTES * 3 // 4
VMEM_LIMIT_TAIL = V7X_VMEM_BYTES * 15 // 16
MISC_W = LANES
MISC_KIDX = 0
MISC_WIDX = IDX_DIM
MISC_DT = IDX_DIM + IDX_HEADS
K_AUG = LANES
D_K_AUG = ATTN_KV_HEADS * K_AUG
SHIFT_ROWS = 16
OFF_Z = 0
OFF_XBC = OFF_Z + D_SSM
OFF_Q = OFF_XBC + D_XBC
OFF_K = OFF_Q + D_ATTN
OFF_QI = OFF_K + D_K_AUG
OFF_V = OFF_QI + IDX_HEADS * IDX_DIM
OFF_MISC = OFF_V + D_KV
D_IN_PAD = OFF_MISC + MISC_W

BLK = 128
DSA_CHUNK = 512
V_ROWS = 80
SEL_SIZES = 4
DEN_LO = 2.0 ** -80
DEN_HI = 2.0 ** 80
F32_HUGE = 3.0e38
LOG2E = math.log2(math.e)
ROW_TILE = 512
NEG = -1e30
INT_MIN = -2 ** 31

F32 = jnp.float32
BF16 = jnp.bfloat16
HI = lax.Precision.HIGHEST
NT = (((1,), (1,)), ((), ()))


def _silu(x):
    return x / (1.0 + jnp.exp(-x))


def _rms(x, w):
    return x * lax.rsqrt(jnp.mean(x * x, axis=-1, keepdims=True) + EPS) * w


def _split3(x):
    hi = x.astype(BF16)
    rest = x - hi.astype(F32)
    mid = rest.astype(BF16)
    return hi, mid, (rest - mid.astype(F32)).astype(BF16)


def _ada_kernel(c_ref, w_ref, b_ref, o_ref):
    s = _silu(c_ref[...])
    o_ref[...] = jnp.dot(s, w_ref[...], precision=HI, preferred_element_type=F32) + b_ref[...]


def _ada(c_pad, w, b):
    n = w.shape[1]
    tn = D_MODEL
    return pl.pallas_call(
        _ada_kernel,
        grid=(n // tn,),
        in_specs=[pl.BlockSpec((8, D_MODEL), lambda j: (0, 0)),
                  pl.BlockSpec((D_MODEL, tn), lambda j: (0, j)),
                  pl.BlockSpec((1, tn), lambda j: (0, j))],
        out_specs=pl.BlockSpec((8, tn), lambda j: (0, j)),
        out_shape=jax.ShapeDtypeStruct((8, n), F32),
        name="ada",
    )(c_pad, w, b)


def _inproj_kernel(x_ref, mod_ref, nw_ref, w_ref,
                   z_ref, xbc_ref, q_ref, k_ref, vt_ref, qi_ref, misc_ref, miscb_ref):
    x = x_ref[0]
    h = _rms(x, nw_ref[...]) * (1.0 + mod_ref[0, 1:2, :]) + mod_ref[0, 0:1, :]
    hb = h.astype(BF16)

    def seg(lo, width):
        return jnp.dot(hb, w_ref[:, lo:lo + width], preferred_element_type=F32)

    z_ref[0] = seg(OFF_Z, D_SSM)
    xbc_ref[0] = seg(OFF_XBC, D_XBC)
    q_ref[0] = (seg(OFF_Q, D_ATTN) * (ATTN_HEAD_DIM ** -0.5 * LOG2E)).astype(BF16)
    ones_col = lax.broadcasted_iota(jnp.int32, (1, D_K_AUG), 1) % K_AUG == ATTN_HEAD_DIM
    k_ref[0] = jnp.where(ones_col, 1.0, seg(OFF_K, D_K_AUG)).astype(BF16)
    qi_ref[0] = seg(OFF_QI, IDX_HEADS * IDX_DIM).astype(BF16)
    v_misc = seg(OFF_V, D_KV + MISC_W)
    ones_tile = jnp.where(lax.broadcasted_iota(jnp.int32, (V_ROWS - ATTN_HEAD_DIM, BLK), 0) == 0, 1.0, 0.0)
    for j in range(x.shape[0] // BLK):
        v_t = jnp.transpose(v_misc[j * BLK:(j + 1) * BLK, :D_KV])
        for g in range(ATTN_KV_HEADS):
            vt_ref[0, j, g, 0:ATTN_HEAD_DIM, :] = v_t[g * ATTN_HEAD_DIM:(g + 1) * ATTN_HEAD_DIM, :].astype(BF16)
            vt_ref[0, j, g, ATTN_HEAD_DIM:, :] = ones_tile.astype(BF16)
    misc = v_misc[:, D_KV:]
    misc_ref[0] = misc
    miscb_ref[0] = misc.astype(BF16)


def _inproj(x, mod, nw, w_perm):
    bsz, t, d = x.shape
    tm = min(ROW_TILE, t)

    def tok(width, dtype):
        return (pl.BlockSpec((1, tm, width), lambda b, i: (b, i, 0)),
                jax.ShapeDtypeStruct((bsz, t, width), dtype))

    vt = (pl.BlockSpec((1, tm // BLK, ATTN_KV_HEADS, V_ROWS, BLK), lambda b, i: (b, i, 0, 0, 0)),
          jax.ShapeDtypeStruct((bsz, t // BLK, ATTN_KV_HEADS, V_ROWS, BLK), BF16))
    outs = [tok(D_SSM, F32), tok(D_XBC, F32), tok(D_ATTN, BF16), tok(D_K_AUG, BF16), vt,
            tok(IDX_HEADS * IDX_DIM, BF16), tok(MISC_W, F32), tok(MISC_W, BF16)]
    return pl.pallas_call(
        _inproj_kernel,
        grid=(bsz, t // tm),
        in_specs=[pl.BlockSpec((1, tm, d), lambda b, i: (b, i, 0)),
                  pl.BlockSpec((1, 6, d), lambda b, i: (b, 0, 0)),
                  pl.BlockSpec((1, d), lambda b, i: (0, 0)),
                  pl.BlockSpec((d, D_IN_PAD), lambda b, i: (0, 0))],
        out_specs=[o[0] for o in outs],
        out_shape=[o[1] for o in outs],
        compiler_params=pltpu.CompilerParams(
            dimension_semantics=("arbitrary", "arbitrary"), vmem_limit_bytes=VMEM_LIMIT),
        name="inproj",
    )(x, mod, nw, w_perm)


def _ssd_kernel(xbc_ref, z_ref, misc_ref, cw_ref, cb_ref, dtb_ref, a_ref, e_ref, dskip_ref, nw_ref,
                o_ref, xbuf, state):
    c = pl.program_id(1)
    lc = xbc_ref.shape[1]

    @pl.when(c == 0)
    def _():
        xbuf[0:8, :] = jnp.zeros((8, D_XBC), F32)
        state[...] = jnp.zeros(state.shape, F32)

    @pl.when(c > 0)
    def _():
        xbuf[0:8, :] = xbuf[lc:lc + 8, :]

    xbuf[8:lc + 8, :] = xbc_ref[0]
    u = cb_ref[...]
    xall = xbuf[...]
    for k in range(CONV_WIDTH):
        back = CONV_WIDTH - 1 - k
        xk = xall if back == 0 else pltpu.roll(xall, back, 0)
        u = u + xk[8:8 + lc] * cw_ref[k:k + 1, :]
    u = _silu(u)
    xs = u[:, :D_SSM]
    bm = u[:, D_SSM:D_SSM + D_BC]
    cm = u[:, D_SSM + D_BC:]

    raw = misc_ref[0] + dtb_ref[...]
    dt_t = jnp.maximum(raw, 0.0) + jnp.log1p(jnp.exp(-jnp.abs(raw)))
    adt_t = dt_t * a_ref[...]
    row = lax.broadcasted_iota(jnp.int32, (lc, lc), 0)
    col = lax.broadcasted_iota(jnp.int32, (lc, lc), 1)
    causal = col <= row
    tril = jnp.where(causal, 1.0, 0.0).astype(BF16)
    expand = e_ref[...]
    acs_t = sum(jnp.dot(tril, piece, preferred_element_type=F32) for piece in _split3(adt_t))
    dt_e = sum(jnp.dot(piece, expand, preferred_element_type=F32) for piece in _split3(dt_t))
    acs_e = sum(jnp.dot(piece, expand, preferred_element_type=F32) for piece in _split3(acs_t))
    acs_row = jnp.transpose(acs_t)

    xdt = xs * dt_e
    a_last = acs_e[lc - 1:lc, :]
    xdec = (xdt * jnp.exp2(a_last - acs_e)).astype(BF16)
    xdt_b = xdt.astype(BF16)
    bm_b = bm.astype(BF16)
    cm_b = cm.astype(BF16)
    bm_t = jnp.transpose(bm).astype(BF16)

    heads_per_group = SSM_HEADS // SSM_GROUPS
    gw = heads_per_group * SSM_HEAD_DIM
    y_parts = []
    off_parts = []
    for g in range(SSM_GROUPS):
        bg = bm_b[:, g * SSM_STATE:(g + 1) * SSM_STATE]
        cg = cm_b[:, g * SSM_STATE:(g + 1) * SSM_STATE]
        cb = lax.dot_general(cg, bg, NT, preferred_element_type=F32)
        for r in range(heads_per_group):
            hd = g * heads_per_group + r
            a_col = acs_t[:, MISC_DT + hd:MISC_DT + hd + 1]
            a_row = acs_row[MISC_DT + hd:MISC_DT + hd + 1, :]
            lmat = jnp.exp2(jnp.where(causal, a_col - a_row, -jnp.inf))
            mh = (cb * lmat).astype(BF16)
            y_parts.append(jnp.dot(mh, xdt_b[:, hd * SSM_HEAD_DIM:(hd + 1) * SSM_HEAD_DIM],
                                   preferred_element_type=F32))
        prev = state[g]
        off_parts.append(jnp.dot(cg, prev.astype(BF16), preferred_element_type=F32))
        st = jnp.dot(bm_t[g * SSM_STATE:(g + 1) * SSM_STATE, :], xdec[:, g * gw:(g + 1) * gw],
                     preferred_element_type=F32)
        state[g] = prev * jnp.exp2(a_last[:, g * gw:(g + 1) * gw]) + st
    y = (jnp.concatenate(y_parts, axis=1)
         + jnp.concatenate(off_parts, axis=1) * jnp.exp2(acs_e)
         + xs * dskip_ref[...])
    o_ref[0] = _rms(y * _silu(z_ref[0]), nw_ref[...]).astype(BF16)


def _ssd(xbc, z, misc, cw, cb, dtb_t, a_t, expand, dskip_e, nw):
    bsz, t, _ = xbc.shape
    lc = math.gcd(t, CHUNK)
    nc = t // lc
    gw = (SSM_HEADS // SSM_GROUPS) * SSM_HEAD_DIM

    def const(shape):
        return pl.BlockSpec(shape, lambda b, c: tuple(0 for _ in shape))

    return pl.pallas_call(
        _ssd_kernel,
        grid=(bsz, nc),
        in_specs=[pl.BlockSpec((1, lc, D_XBC), lambda b, c: (b, c, 0)),
                  pl.BlockSpec((1, lc, D_SSM), lambda b, c: (b, c, 0)),
                  pl.BlockSpec((1, lc, MISC_W), lambda b, c: (b, c, 0)),
                  const((CONV_WIDTH, D_XBC)), const((1, D_XBC)), const((1, MISC_W)), const((1, MISC_W)),
                  const((MISC_W, D_SSM)), const((1, D_SSM)), const((1, D_SSM))],
        out_specs=pl.BlockSpec((1, lc, D_SSM), lambda b, c: (b, c, 0)),
        out_shape=jax.ShapeDtypeStruct((bsz, t, D_SSM), BF16),
        scratch_shapes=[pltpu.VMEM((lc + 8, D_XBC), F32),
                        pltpu.VMEM((SSM_GROUPS, SSM_STATE, gw), F32)],
        compiler_params=pltpu.CompilerParams(
            dimension_semantics=("arbitrary", "arbitrary"), vmem_limit_bytes=VMEM_LIMIT),
        name="ssd",
    )(xbc, z, misc, cw, cb, dtb_t, a_t, expand, dskip_e, nw)


def _bucket_tiles():
    s = np.arange(BLK)[:, None]
    t = np.arange(BLK)[None, :]
    max_exact = N_BUCKETS // 2
    tiles = []
    for d0 in (0, BLK):
        n = np.maximum(d0 + t - s, 0)
        large = {}
        for dt_ in (np.float32, np.float64):
            nf = np.maximum(n, 1).astype(dt_)
            lg = max_exact + (np.log(nf / dt_(max_exact)) / dt_(math.log(MAX_DISTANCE / max_exact))
                              * dt_(N_BUCKETS - max_exact)).astype(np.int32)
            large[dt_] = np.minimum(lg, N_BUCKETS - 1)
        assert (large[np.float32] == large[np.float64]).all()
        tiles.append(np.where(n < max_exact, n, large[np.float64]).astype(np.int32))
    assert BLK + 1 >= MAX_DISTANCE
    return np.stack(tiles)


def _to_key(x):
    bits = lax.bitcast_convert_type(x, jnp.int32)
    key = bits ^ (lax.shift_right_arithmetic(bits, 31) & jnp.int32(0x7FFFFFFF))
    return jnp.where(key == -1, 0, key)


def _transpose32(rows):
    rows = list(rows)
    mask, j = 0x0000FFFF, 16
    while j:
        k = 0
        while k < 32:
            t = (rows[k] ^ lax.shift_right_logical(rows[k + j], jnp.int32(j))) & jnp.int32(mask)
            rows[k] = rows[k] ^ t
            rows[k + j] = rows[k + j] ^ lax.shift_left(t, jnp.int32(j))
            k = (k + j + 1) & ~j
        j >>= 1
        mask ^= (mask << j) & 0xFFFFFFFF
    return rows


def _dsa_kernel(relb_ref, bkt_ref, q_ref, qic_ref, miscc_ref, qin_ref, miscn_ref, k_ref, vt_ref, kidx_ref,
                nw_ref, o_ref, keys, planes, live, thr_s, cnt_s, bias_s, qcat, qicat, wcat, m_s, acc_s, p_s,
                *, topk, idx_bits, nq):
    b = pl.program_id(0)
    qi = pl.program_id(1)
    tq = BLK
    i32 = jnp.int32
    rep = ATTN_HEADS // ATTN_KV_HEADS
    gw = rep * tq
    ch = DSA_CHUNK
    ch_blks = ch // BLK
    rows_per_chunk = ch // 32

    @pl.when((b == 0) & (qi == 0))
    def _init_bias():
        for d in range(2):
            bk = bkt_ref[d]
            for h in range(ATTN_HEADS):
                far = relb_ref[N_BUCKETS - 1, h]
                acc = jnp.zeros((BLK, tq), F32)
                for n in range(N_BUCKETS - 1):
                    acc = jnp.where(bk == n, relb_ref[n, h] - far, acc)
                bias_s[d, h // rep, :, (h % rep) * BLK:(h % rep + 1) * BLK] = acc * LOG2E
        bias_s[2] = jnp.zeros((ATTN_KV_HEADS, BLK, gw), F32)
        for w in range(32):
            planes[w] = jnp.zeros(planes.shape[1:], i32)

    row_c = lax.broadcasted_iota(i32, (ch, tq), 0)
    lane_c = lax.broadcasted_iota(i32, (ch, tq), 1)

    def chunk_start(c):
        return pl.multiple_of(c * ch, ch)

    def n_chunks(blk):
        return (blk + ch_blks) // ch_blks

    def load_indexer(qidx_ref, misc_ref):
        qi_t = jnp.transpose(qidx_ref[0].astype(F32))
        w_t = jnp.transpose(misc_ref[0])[MISC_WIDX:MISC_WIDX + IDX_HEADS, :] * (
            IDX_HEADS ** -0.5 * IDX_DIM ** -0.5)
        for h in range(IDX_HEADS):
            qicat[:, h * BLK:(h + 1) * BLK] = qi_t[h * IDX_DIM:(h + 1) * IDX_DIM, :].astype(BF16)
            wcat[:, h * BLK:(h + 1) * BLK] = jnp.broadcast_to(w_t[h:h + 1, :], (8, BLK))

    def score_chunk(c, blk, slot):
        s0 = chunk_start(c)
        kb = kidx_ref[0, pl.ds(s0, ch), MISC_KIDX:MISC_KIDX + IDX_DIM]
        s = jnp.dot(kb, qicat[...], preferred_element_type=F32)
        r = jnp.maximum(s, 0.0) * wcat[0:1, :]
        acc = r[:, 0:BLK]
        for h in range(1, IDX_HEADS):
            acc = acc + r[:, h * BLK:(h + 1) * BLK]
        acc = jnp.where(s0 + row_c <= blk * BLK + lane_c, acc, -jnp.inf)
        keys[slot, pl.ds(s0, ch), :] = _to_key(acc)

    def plane_chunk(c, slot):
        ku = (keys[slot, pl.ds(chunk_start(c), ch), :] ^ i32(INT_MIN)).reshape(ch // (32 * 8), 32, 8, tq)
        words = _transpose32([ku[:, j] for j in range(32)])
        r0 = pl.multiple_of(c * rows_per_chunk, rows_per_chunk)
        for w in range(32):
            planes[w, pl.ds(r0, rows_per_chunk), :] = words[w].reshape(rows_per_chunk, tq)

    def select(blk, slot):
        nch = n_chunks(blk)
        used = nch * rows_per_chunk

        def ones_in(words):
            return jnp.sum(lax.population_count(words), axis=0, keepdims=True)

        def passes(nr):
            live[0:nr] = jnp.where(lax.broadcasted_iota(i32, (nr, tq), 0) < used, i32(-1), i32(0))

            def bit_body(w, st):
                thr_u, n_above, n_set = st
                take = n_above + n_set >= topk
                thr_u = jnp.where(take, thr_u | lax.shift_left(i32(1), i32(31) - w), thr_u)
                n_above = jnp.where(take, n_above, n_above + n_set)
                still = live[0:nr] & (planes[w, 0:nr] ^ jnp.where(take, i32(0), i32(-1)))
                live[0:nr] = still
                return thr_u, n_above, ones_in(still & planes[jnp.minimum(w + 1, 31), 0:nr])

            zero = jnp.zeros((1, tq), i32)
            thr_u, n_above, _ = lax.fori_loop(
                0, 32, bit_body, (zero, zero, ones_in(live[0:nr] & planes[0, 0:nr])))
            thr_s[slot] = thr_u ^ i32(INT_MIN)
            cnt_s[0] = n_above
            cnt_s[1] = ones_in(live[0:nr])

        size_step = live.shape[0] // SEL_SIZES
        for v in range(SEL_SIZES):
            pl.when((used > v * size_step) & (used <= (v + 1) * size_step))(
                functools.partial(passes, (v + 1) * size_step))
        thr = thr_s[slot]
        n_gt = cnt_s[0]
        n_eq = cnt_s[1]
        need = topk - n_gt

        def break_ties():
            def count(pred):
                def body(c, cnt):
                    s0 = chunk_start(c)
                    hit = pred(keys[slot, pl.ds(s0, ch), :], s0).astype(i32)
                    n = ch
                    while n > 8:
                        n //= 2
                        hit = hit[:n] + hit[n:2 * n]
                    return cnt + hit
                cnt = lax.fori_loop(0, nch, body, jnp.zeros((8, tq), i32))
                return jnp.sum(cnt, axis=0, keepdims=True)

            def jbody(it, p):
                cand = p + lax.shift_left(i32(1), i32(idx_bits - 1) - it)
                c = count(lambda kk, s0: (kk == thr) & (s0 + row_c < cand))
                return jnp.where(c < need, cand, p)
            last = lax.fori_loop(0, idx_bits, jbody, jnp.zeros((1, tq), i32))

            def demote(c, carry):
                s0 = chunk_start(c)
                kk = keys[slot, pl.ds(s0, ch), :]
                keys[slot, pl.ds(s0, ch), :] = jnp.where((kk == thr) & (s0 + row_c > last), INT_MIN, kk)
                return carry
            lax.fori_loop(0, nch, demote, 0)

        return (n_eq > need).astype(i32), break_ties

    def any_lane(flags):
        return jnp.max(flags, axis=1, keepdims=True)

    @pl.when(qi == 0)
    def _first_block():
        load_indexer(qic_ref, miscc_ref)
        score_chunk(0, 0, 0)
        plane_chunk(0, 0)
        tied, break_ties = select(0, 0)
        pl.when(any_lane(tied)[0, 0] > 0)(break_ties)

    slot_b = qi % 2
    slot_a = 1 - slot_b
    blk_a = jnp.minimum(qi + 1, nq - 1)
    nch_b = n_chunks(qi)
    nch_a = n_chunks(blk_a)
    load_indexer(qin_ref, miscn_ref)
    thr = thr_s[slot_b]

    q_t = jnp.transpose(q_ref[0].astype(F32))
    for h in range(ATTN_HEADS):
        qcat[h // rep, 0:ATTN_HEAD_DIM, (h % rep) * BLK:(h % rep + 1) * BLK] = (
            q_t[h * ATTN_HEAD_DIM:(h + 1) * ATTN_HEAD_DIM, :].astype(BF16))
    for g in range(ATTN_KV_HEADS):
        qcat[g, ATTN_HEAD_DIM:, :] = jnp.zeros((K_AUG - ATTN_HEAD_DIM, gw), BF16)

    def pen_of(s0, causal):
        sel = keys[slot_b, pl.ds(s0, ch), :] >= thr
        if causal:
            sel = sel & (s0 + row_c <= qi * BLK + lane_c)
        pen = jnp.where(sel, 0.0, NEG)
        return jnp.concatenate([pen] * rep, axis=1)

    def near_bias(c, g):
        return jnp.concatenate(
            [bias_s[jnp.clip(qi - (c * ch_blks + u), 0, 2), g] for u in range(ch_blks)], axis=0)

    def vt_chunk(c, g):
        return jnp.concatenate([vt_ref[0, c * ch_blks + u, g] for u in range(ch_blks)], axis=1)

    s0d = pl.multiple_of(qi * BLK, BLK)
    kd = k_ref[0, pl.ds(s0d, BLK), :]
    tri = lax.broadcasted_iota(i32, (BLK, tq), 0) <= lax.broadcasted_iota(i32, (BLK, tq), 1)
    tri_pen = jnp.concatenate([jnp.where(tri, 0.0, NEG)] * rep, axis=1)
    for g in range(ATTN_KV_HEADS):
        sd = jnp.dot(kd[:, g * K_AUG:(g + 1) * K_AUG], qcat[g],
                     preferred_element_type=F32) + bias_s[0, g] + tri_pen
        shift = jnp.max(sd, axis=0, keepdims=True)
        first = lax.broadcasted_iota(i32, (SHIFT_ROWS, gw), 0) == 0
        qcat[g, ATTN_HEAD_DIM:ATTN_HEAD_DIM + SHIFT_ROWS, :] = jnp.where(first, -shift, 0.0).astype(BF16)
        acc_s[g] = jnp.zeros((V_ROWS, gw), F32)
        p_s[1, g] = jnp.zeros((ch, gw), BF16)

    def weigh_values(c, slot):
        for g in range(ATTN_KV_HEADS):
            acc_s[g] += jnp.dot(vt_chunk(c, g), p_s[slot, g], preferred_element_type=F32)

    def att_chunk(c, near, slots=None):
        prev_slot, cur_slot = slots if slots is not None else ((c + 1) % 2, c % 2)
        weigh_values(jnp.maximum(c - 1, 0), prev_slot)
        s0 = chunk_start(c)
        pen = pen_of(s0, near)
        kblk = k_ref[0, pl.ds(s0, ch), :]
        for g in range(ATTN_KV_HEADS):
            s = jnp.dot(kblk[:, g * K_AUG:(g + 1) * K_AUG], qcat[g],
                        preferred_element_type=F32)
            if near:
                s = s + near_bias(c, g)
            p_s[cur_slot, g] = jnp.exp2(s + pen).astype(BF16)

    def fused(c, near, slots=None):
        plane_chunk(c, slot_a)
        att_chunk(c, near, slots)
        score_chunk(c + 1, blk_a, slot_a)

    def far_pair(j, carry):
        fused(2 * j, False, (1, 0))
        fused(2 * j + 1, False, (0, 1))
        return carry

    def far_quad(j, carry):
        far_pair(2 * j, carry)
        return far_pair(2 * j + 1, carry)

    def far_body(c, carry):
        fused(c, False, (1, 0))
        return carry

    def near_body(c, carry):
        fused(c, True)
        return carry

    score_chunk(0, blk_a, slot_a)
    n_far = jnp.maximum(qi - 1, 0) // ch_blks
    lax.fori_loop(0, n_far // 4, far_quad, 0)
    lax.fori_loop((n_far // 4) * 2, n_far // 2, far_pair, 0)
    lax.fori_loop((n_far // 2) * 2, n_far, far_body, 0)
    lax.fori_loop(n_far, nch_b - 1, near_body, 0)
    @pl.when(nch_a > nch_b)
    def _one_more_chunk():
        score_chunk(nch_b, blk_a, slot_a)
        plane_chunk(nch_b, slot_a)

    plane_chunk(nch_b - 1, slot_a)
    att_chunk(nch_b - 1, True)
    weigh_values(nch_b - 1, (nch_b - 1) % 2)
    tied, break_ties = select(blk_a, slot_a)

    n_ok = jnp.zeros((1, gw), i32)
    for g in range(ATTN_KV_HEADS):
        a = acc_s[g]
        den = a[ATTN_HEAD_DIM:ATTN_HEAD_DIM + 1, :]
        top = jnp.max(jnp.abs(a[0:ATTN_HEAD_DIM, :]), axis=0, keepdims=True)
        n_ok = n_ok + ((den > DEN_LO) & (den < DEN_HI) & (top < F32_HUGE)).astype(i32)

    rare = (any_lane(tied) + 2 * any_lane((n_ok < ATTN_KV_HEADS).astype(i32)))[0, 0]
    pl.when(rare % 2 == 1)(break_ties)

    @pl.when(rare >= 2)
    def _exact():
        for g in range(ATTN_KV_HEADS):
            qcat[g, ATTN_HEAD_DIM:ATTN_HEAD_DIM + SHIFT_ROWS, :] = jnp.zeros((SHIFT_ROWS, gw), BF16)
            m_s[g] = jnp.full((1, gw), NEG, F32)
            acc_s[g] = jnp.zeros((V_ROWS, gw), F32)

        def body(c, carry):
            s0 = chunk_start(c)
            pen = pen_of(s0, True)
            kblk = k_ref[0, pl.ds(s0, ch), :]
            for g in range(ATTN_KV_HEADS):
                s = jnp.dot(kblk[:, g * K_AUG:(g + 1) * K_AUG], qcat[g],
                            preferred_element_type=F32) + near_bias(c, g) + pen
                m_old = m_s[g]
                m_new = jnp.maximum(m_old, jnp.max(s, axis=0, keepdims=True))
                p = jnp.exp2(s - m_new).astype(BF16)
                acc_s[g] = acc_s[g] * jnp.exp2(m_old - m_new) + jnp.dot(
                    vt_chunk(c, g), p, preferred_element_type=F32)
                m_s[g] = m_new
            return carry
        lax.fori_loop(0, nch_b, body, 0)

    parts = []
    for g in range(ATTN_KV_HEADS):
        a = acc_s[g]
        o = a[0:ATTN_HEAD_DIM, :] * (1.0 / a[ATTN_HEAD_DIM:ATTN_HEAD_DIM + 1, :])
        parts += [o[:, r * tq:(r + 1) * tq] for r in range(rep)]
    out_t = jnp.concatenate(parts, axis=0)
    o_ref[0] = _rms(jnp.transpose(out_t), nw_ref[...]).astype(BF16)


def _dsa(rel_bias, q, qi, misc, k, vt, miscb, nw):
    bsz, t, _ = q.shape
    nq = t // BLK
    topk = min(TOPK_MAX, t // 4)
    idx_bits = max(1, (t - 1).bit_length())
    gw = (ATTN_HEADS // ATTN_KV_HEADS) * BLK
    bkt = jnp.asarray(_bucket_tiles())
    kern = functools.partial(_dsa_kernel, topk=topk, idx_bits=idx_bits, nq=nq)

    def nxt(b, i):
        return (b, jnp.minimum(i + 1, nq - 1), 0)

    return pl.pallas_call(
        kern,
        grid=(bsz, nq),
        in_specs=[pl.BlockSpec(memory_space=pltpu.SMEM),
                  pl.BlockSpec((2, BLK, BLK), lambda b, i: (0, 0, 0)),
                  pl.BlockSpec((1, BLK, D_ATTN), lambda b, i: (b, i, 0)),
                  pl.BlockSpec((1, BLK, IDX_HEADS * IDX_DIM), lambda b, i: (b, i, 0)),
                  pl.BlockSpec((1, BLK, MISC_W), lambda b, i: (b, i, 0)),
                  pl.BlockSpec((1, BLK, IDX_HEADS * IDX_DIM), nxt),
                  pl.BlockSpec((1, BLK, MISC_W), nxt),
                  pl.BlockSpec((1, t, D_K_AUG), lambda b, i: (b, 0, 0)),
                  pl.BlockSpec((1, nq, ATTN_KV_HEADS, V_ROWS, BLK), lambda b, i: (b, 0, 0, 0, 0)),
                  pl.BlockSpec((1, t, MISC_W), lambda b, i: (b, 0, 0)),
                  pl.BlockSpec((1, D_ATTN), lambda b, i: (0, 0))],
        out_specs=pl.BlockSpec((1, BLK, D_ATTN), lambda b, i: (b, i, 0)),
        out_shape=jax.ShapeDtypeStruct((bsz, t, D_ATTN), BF16),
        scratch_shapes=[pltpu.VMEM((2, t, BLK), jnp.int32),
                        pltpu.VMEM((32, t // 32, BLK), jnp.int32),
                        pltpu.VMEM((t // 32, BLK), jnp.int32),
                        pltpu.VMEM((2, 1, BLK), jnp.int32),
                        pltpu.VMEM((2, 1, BLK), jnp.int32),
                        pltpu.VMEM((3, ATTN_KV_HEADS, BLK, gw), F32),
                        pltpu.VMEM((ATTN_KV_HEADS, K_AUG, gw), BF16),
                        pltpu.VMEM((IDX_DIM, IDX_HEADS * BLK), BF16),
                        pltpu.VMEM((8, IDX_HEADS * BLK), F32),
                        pltpu.VMEM((ATTN_KV_HEADS, 1, gw), F32),
                        pltpu.VMEM((ATTN_KV_HEADS, V_ROWS, gw), F32),
                        pltpu.VMEM((2, ATTN_KV_HEADS, DSA_CHUNK, gw), BF16)],
        compiler_params=pltpu.CompilerParams(
            dimension_semantics=("arbitrary", "arbitrary"), vmem_limit_bytes=VMEM_LIMIT),
        name="dsa",
    )(rel_bias, bkt, q, qi, misc, qi, misc, k, vt, miscb, nw)


def _tail_kernel(x_ref, ys_ref, ya_ref, mod_ref, wo_ref, n1_ref, n2_ref, n3_ref, w1_ref, w2_ref, o_ref):
    x = x_ref[0]
    mix = (jnp.dot(ys_ref[0], wo_ref[0:D_SSM, :], preferred_element_type=F32)
           + jnp.dot(ya_ref[0], wo_ref[D_SSM:, :], preferred_element_type=F32))
    x1 = x + mod_ref[0, 2:3, :] * _rms(mix, n1_ref[...])
    h = (_rms(x1, n2_ref[...]) * (1.0 + mod_ref[0, 4:5, :]) + mod_ref[0, 3:4, :]).astype(BF16)
    f = jnp.zeros(x.shape, F32)
    step = D_MODEL
    for c0 in range(0, D_FF, step):
        a = jnp.maximum(jnp.dot(h, w1_ref[:, c0:c0 + step], preferred_element_type=F32), 0.0)
        f = f + jnp.dot((a * a).astype(BF16), w2_ref[c0:c0 + step, :], preferred_element_type=F32)
    o_ref[0] = x1 + mod_ref[0, 5:6, :] * _rms(f, n3_ref[...])


def _tail(x, ys, ya, mod, wo, n1, n2, n3, w1, w2):
    bsz, t, d = x.shape
    tm = min(ROW_TILE, t)

    def const(shape):
        return pl.BlockSpec(shape, lambda b, i: tuple(0 for _ in shape))

    return pl.pallas_call(
        _tail_kernel,
        grid=(bsz, t // tm),
        in_specs=[pl.BlockSpec((1, tm, d), lambda b, i: (b, i, 0)),
                  pl.BlockSpec((1, tm, D_SSM), lambda b, i: (b, i, 0)),
                  pl.BlockSpec((1, tm, D_ATTN), lambda b, i: (b, i, 0)),
                  pl.BlockSpec((1, 6, d), lambda b, i: (b, 0, 0)),
                  const((D_SSM + D_ATTN, d)), const((1, d)), const((1, d)), const((1, d)),
                  const((d, D_FF)), const((D_FF, d))],
        out_specs=pl.BlockSpec((1, tm, d), lambda b, i: (b, i, 0)),
        out_shape=jax.ShapeDtypeStruct((bsz, t, d), F32),
        compiler_params=pltpu.CompilerParams(
            dimension_semantics=("arbitrary", "arbitrary"), vmem_limit_bytes=VMEM_LIMIT_TAIL),
        name="tail",
    )(x, ys, ya, mod, wo, n1, n2, n3, w1, w2)


def _permute_w_in(w):
    sizes = [D_SSM, D_XBC, SSM_HEADS, D_ATTN, D_KV, D_KV, IDX_HEADS * IDX_DIM, IDX_DIM, IDX_HEADS]
    pts = np.cumsum(sizes)[:-1].tolist()
    z, xbc, dt, q, k, v, qidx, kidx, widx = jnp.split(w, pts, axis=-1)
    pad = jnp.zeros((w.shape[0], MISC_W - IDX_DIM - IDX_HEADS - SSM_HEADS), w.dtype)
    kpad = jnp.zeros((w.shape[0], K_AUG - ATTN_HEAD_DIM), w.dtype)
    k_aug = []
    for g in range(ATTN_KV_HEADS):
        k_aug += [k[:, g * ATTN_HEAD_DIM:(g + 1) * ATTN_HEAD_DIM], kpad]
    return jnp.concatenate([z, xbc, q] + k_aug + [qidx, v, kidx, widx, dt, pad], axis=-1).astype(BF16)


def _layer(x, c_pad, w_ada, b_ada, norm1_pre, norm1_post, w_in, conv_w, conv_b, dt_bias, a_log, d_skip,
           ssm_norm, rel_bias, attn_norm, w_out, norm2_pre, norm2_post, w_mlp_in, w_mlp_out):
    bsz, t, d = x.shape
    mod = _ada(c_pad, w_ada, b_ada[None, :])[:bsz].reshape(bsz, 6, d)
    z, xbc, q, k, vt, qi, misc, miscb = _inproj(x, mod, norm1_pre[None, :], _permute_w_in(w_in))

    lane = jnp.arange(MISC_W)
    is_dt = (lane >= MISC_DT) & (lane < MISC_DT + SSM_HEADS)
    head_of = jnp.clip(lane - MISC_DT, 0, SSM_HEADS - 1)
    dtb_t = jnp.where(is_dt, dt_bias[head_of], 0.0)[None, :]
    a_t = jnp.where(is_dt, (-jnp.exp(a_log.astype(F32)) * LOG2E)[head_of], 0.0)[None, :]
    expand = ((lane[:, None] - MISC_DT) == (jnp.arange(D_SSM)[None, :] // SSM_HEAD_DIM)).astype(BF16)
    dskip_e = jnp.repeat(d_skip.astype(F32), SSM_HEAD_DIM)[None, :]
    y_ssd = _ssd(xbc, z, misc, conv_w, conv_b[None, :], dtb_t, a_t, expand, dskip_e, ssm_norm[None, :])

    y_att = _dsa(rel_bias, q, qi, misc, k, vt, miscb, attn_norm[None, :])

    return _tail(x, y_ssd, y_att, mod, w_out.astype(BF16), norm1_post[None, :], norm2_pre[None, :],
                 norm2_post[None, :], w_mlp_in.astype(BF16), w_mlp_out.astype(BF16))


def kernel(x, c, w_ada, b_ada, norm1_pre, norm1_post, w_in, conv_w, conv_b, dt_bias, a_log, d_skip,
           ssm_norm, rel_bias, attn_norm, w_out, norm2_pre, norm2_post, w_mlp_in, w_mlp_out):
    bsz = x.shape[0]
    assert bsz <= 8 and x.shape[1] % DSA_CHUNK == 0 and x.shape[2] == D_MODEL
    c_pad = jnp.zeros((8, D_MODEL), F32).at[:bsz].set(c)
    for l in range(w_ada.shape[0]):
        x = _layer(x, c_pad, w_ada[l], b_ada[l], norm1_pre[l], norm1_post[l], w_in[l], conv_w[l], conv_b[l],
                   dt_bias[l], a_log[l], d_skip[l], ssm_norm[l], rel_bias, attn_norm[l], w_out[l],
                   norm2_pre[l], norm2_post[l], w_mlp_in[l], w_mlp_out[l])
    return x
```

```python
import functools
import math

import numpy as np
import jax
import jax.numpy as jnp
from jax import lax
from jax.experimental import pallas as pl
from jax.experimental.pallas import tpu as pltpu

D_MODEL = 1024
SSM_HEADS = 8
SSM_HEAD_DIM = 64
D_SSM = SSM_HEADS * SSM_HEAD_DIM
SSM_GROUPS = 2
SSM_STATE = 128
CONV_WIDTH = 4
CHUNK = 256
ATTN_HEADS = 8
ATTN_KV_HEADS = 2
ATTN_HEAD_DIM = 64
D_ATTN = ATTN_HEADS * ATTN_HEAD_DIM
D_KV = ATTN_KV_HEADS * ATTN_HEAD_DIM
IDX_HEADS = 8
IDX_DIM = 64
TOPK_MAX = 256
N_BUCKETS = 32
MAX_DISTANCE = 128
D_FF = 4 * D_MODEL
D_BC = SSM_GROUPS * SSM_STATE
D_XBC = D_SSM + 2 * D_BC
EPS = 1e-6

LANES = 128
V7X_VMEM_BYTES = 64 * 2 ** 20
VMEM_LIMIT = V7X_VMEM_BYTES * 3 // 4
VMEM_LIMIT_TAIL = V7X_VMEM_BYTES * 15 // 16
MISC_W = LANES
MISC_KIDX = 0
MISC_WIDX = IDX_DIM
MISC_DT = IDX_DIM + IDX_HEADS
K_AUG = LANES
D_K_AUG = ATTN_KV_HEADS * K_AUG
SHIFT_ROWS = 16
OFF_Z = 0
OFF_XBC = OFF_Z + D_SSM
OFF_Q = OFF_XBC + D_XBC
OFF_K = OFF_Q + D_ATTN
OFF_QI = OFF_K + D_K_AUG
OFF_V = OFF_QI + IDX_HEADS * IDX_DIM
OFF_MISC = OFF_V + D_KV
D_IN_PAD = OFF_MISC + MISC_W

BLK = 128
DSA_CHUNK = 512
V_ROWS = 80
SEL_SIZES = 4
DEN_LO = 2.0 ** -80
DEN_HI = 2.0 ** 80
F32_HUGE = 3.0e38
LOG2E = math.log2(math.e)
ROW_TILE = 512
NEG = -1e30
INT_MIN = -2 ** 31

F32 = jnp.float32
BF16 = jnp.bfloat16
HI = lax.Precision.HIGHEST
NT = (((1,), (1,)), ((), ()))


def _silu(x):
    return x / (1.0 + jnp.exp(-x))


def _rms(x, w):
    return x * lax.rsqrt(jnp.mean(x * x, axis=-1, keepdims=True) + EPS) * w


def _split3(x):
    hi = x.astype(BF16)
    rest = x - hi.astype(F32)
    mid = rest.astype(BF16)
    return hi, mid, (rest - mid.astype(F32)).astype(BF16)


def _ada_kernel(c_ref, w_ref, b_ref, o_ref):
    s = _silu(c_ref[...])
    o_ref[...] = jnp.dot(s, w_ref[...], precision=HI, preferred_element_type=F32) + b_ref[...]


def _ada(c_pad, w, b):
    n = w.shape[1]
    tn = D_MODEL
    return pl.pallas_call(
        _ada_kernel,
        grid=(n // tn,),
        in_specs=[pl.BlockSpec((8, D_MODEL), lambda j: (0, 0)),
                  pl.BlockSpec((D_MODEL, tn), lambda j: (0, j)),
                  pl.BlockSpec((1, tn), lambda j: (0, j))],
        out_specs=pl.BlockSpec((8, tn), lambda j: (0, j)),
        out_shape=jax.ShapeDtypeStruct((8, n), F32),
        name="ada",
    )(c_pad, w, b)


def _inproj_kernel(x_ref, mod_ref, nw_ref, w_ref,
                   z_ref, xbc_ref, q_ref, k_ref, vt_ref, qi_ref, misc_ref, miscb_ref):
    x = x_ref[0]
    h = _rms(x, nw_ref[...]) * (1.0 + mod_ref[0, 1:2, :]) + mod_ref[0, 0:1, :]
    hb = h.astype(BF16)

    def seg(lo, width):
        return jnp.dot(hb, w_ref[:, lo:lo + width], preferred_element_type=F32)

    z_ref[0] = seg(OFF_Z, D_SSM)
    xbc_ref[0] = seg(OFF_XBC, D_XBC)
    rep = ATTN_HEADS // ATTN_KV_HEADS
    q_all = seg(OFF_Q, D_ATTN) * (ATTN_HEAD_DIM ** -0.5 * LOG2E)
    qi_all = seg(OFF_QI, IDX_HEADS * IDX_DIM)
    for j in range(x.shape[0] // BLK):
        q_t = jnp.transpose(q_all[j * BLK:(j + 1) * BLK, :])
        qi_t = jnp.transpose(qi_all[j * BLK:(j + 1) * BLK, :])
        for h in range(ATTN_HEADS):
            q_ref[0, j, h // rep, :, (h % rep) * BLK:(h % rep + 1) * BLK] = (
                q_t[h * ATTN_HEAD_DIM:(h + 1) * ATTN_HEAD_DIM, :].astype(BF16))
        for h in range(IDX_HEADS):
            qi_ref[0, j, :, h * BLK:(h + 1) * BLK] = qi_t[h * IDX_DIM:(h + 1) * IDX_DIM, :].astype(BF16)
    ones_col = lax.broadcasted_iota(jnp.int32, (1, D_K_AUG), 1) % K_AUG == ATTN_HEAD_DIM
    k_ref[0] = jnp.where(ones_col, 1.0, seg(OFF_K, D_K_AUG)).astype(BF16)
    v_misc = seg(OFF_V, D_KV + MISC_W)
    ones_tile = jnp.where(lax.broadcasted_iota(jnp.int32, (V_ROWS - ATTN_HEAD_DIM, BLK), 0) == 0, 1.0, 0.0)
    for j in range(x.shape[0] // BLK):
        v_t = jnp.transpose(v_misc[j * BLK:(j + 1) * BLK, :D_KV])
        for g in range(ATTN_KV_HEADS):
            vt_ref[0, j, g, 0:ATTN_HEAD_DIM, :] = v_t[g * ATTN_HEAD_DIM:(g + 1) * ATTN_HEAD_DIM, :].astype(BF16)
            vt_ref[0, j, g, ATTN_HEAD_DIM:, :] = ones_tile.astype(BF16)
    misc = v_misc[:, D_KV:]
    misc_ref[0] = misc
    miscb_ref[0] = misc.astype(BF16)


def _inproj(x, mod, nw, w_perm):
    bsz, t, d = x.shape
    tm = min(ROW_TILE, t)

    def tok(width, dtype):
        return (pl.BlockSpec((1, tm, width), lambda b, i: (b, i, 0)),
                jax.ShapeDtypeStruct((bsz, t, width), dtype))

    vt = (pl.BlockSpec((1, tm // BLK, ATTN_KV_HEADS, V_ROWS, BLK), lambda b, i: (b, i, 0, 0, 0)),
          jax.ShapeDtypeStruct((bsz, t // BLK, ATTN_KV_HEADS, V_ROWS, BLK), BF16))
    gw = (ATTN_HEADS // ATTN_KV_HEADS) * BLK
    qt = (pl.BlockSpec((1, tm // BLK, ATTN_KV_HEADS, ATTN_HEAD_DIM, gw), lambda b, i: (b, i, 0, 0, 0)),
          jax.ShapeDtypeStruct((bsz, t // BLK, ATTN_KV_HEADS, ATTN_HEAD_DIM, gw), BF16))
    qit = (pl.BlockSpec((1, tm // BLK, IDX_DIM, IDX_HEADS * BLK), lambda b, i: (b, i, 0, 0)),
           jax.ShapeDtypeStruct((bsz, t // BLK, IDX_DIM, IDX_HEADS * BLK), BF16))
    outs = [tok(D_SSM, F32), tok(D_XBC, F32), qt, tok(D_K_AUG, BF16), vt, qit,
            tok(MISC_W, F32), tok(MISC_W, BF16)]
    return pl.pallas_call(
        _inproj_kernel,
        grid=(bsz, t // tm),
        in_specs=[pl.BlockSpec((1, tm, d), lambda b, i: (b, i, 0)),
                  pl.BlockSpec((1, 6, d), lambda b, i: (b, 0, 0)),
                  pl.BlockSpec((1, d), lambda b, i: (0, 0)),
                  pl.BlockSpec((d, D_IN_PAD), lambda b, i: (0, 0))],
        out_specs=[o[0] for o in outs],
        out_shape=[o[1] for o in outs],
        compiler_params=pltpu.CompilerParams(
            dimension_semantics=("arbitrary", "arbitrary"), vmem_limit_bytes=VMEM_LIMIT),
        name="inproj",
    )(x, mod, nw, w_perm)


def _ssd_kernel(xbc_ref, z_ref, misc_ref, cw_ref, cb_ref, dtb_ref, a_ref, e_ref, dskip_ref, nw_ref,
                o_ref, xbuf, state):
    c = pl.program_id(1)
    lc = xbc_ref.shape[1]

    @pl.when(c == 0)
    def _():
        xbuf[0:8, :] = jnp.zeros((8, D_XBC), F32)
        state[...] = jnp.zeros(state.shape, F32)

    @pl.when(c > 0)
    def _():
        xbuf[0:8, :] = xbuf[lc:lc + 8, :]

    xbuf[8:lc + 8, :] = xbc_ref[0]
    u = cb_ref[...]
    xall = xbuf[...]
    for k in range(CONV_WIDTH):
        back = CONV_WIDTH - 1 - k
        xk = xall if back == 0 else pltpu.roll(xall, back, 0)
        u = u + xk[8:8 + lc] * cw_ref[k:k + 1, :]
    u = _silu(u)
    xs = u[:, :D_SSM]
    bm = u[:, D_SSM:D_SSM + D_BC]
    cm = u[:, D_SSM + D_BC:]

    raw = misc_ref[0] + dtb_ref[...]
    dt_t = jnp.maximum(raw, 0.0) + jnp.log1p(jnp.exp(-jnp.abs(raw)))
    adt_t = dt_t * a_ref[...]
    row = lax.broadcasted_iota(jnp.int32, (lc, lc), 0)
    col = lax.broadcasted_iota(jnp.int32, (lc, lc), 1)
    causal = col <= row
    tril = jnp.where(causal, 1.0, 0.0).astype(BF16)
    expand = e_ref[...]
    acs_t = sum(jnp.dot(tril, piece, preferred_element_type=F32) for piece in _split3(adt_t))
    dt_e = sum(jnp.dot(piece, expand, preferred_element_type=F32) for piece in _split3(dt_t))
    acs_e = sum(jnp.dot(piece, expand, preferred_element_type=F32) for piece in _split3(acs_t))
    acs_row = jnp.transpose(acs_t)

    xdt = xs * dt_e
    a_last = acs_e[lc - 1:lc, :]
    xdec = (xdt * jnp.exp2(a_last - acs_e)).astype(BF16)
    xdt_b = xdt.astype(BF16)
    bm_b = bm.astype(BF16)
    cm_b = cm.astype(BF16)
    bm_t = jnp.transpose(bm).astype(BF16)

    heads_per_group = SSM_HEADS // SSM_GROUPS
    gw = heads_per_group * SSM_HEAD_DIM
    y_parts = []
    off_parts = []
    for g in range(SSM_GROUPS):
        bg = bm_b[:, g * SSM_STATE:(g + 1) * SSM_STATE]
        cg = cm_b[:, g * SSM_STATE:(g + 1) * SSM_STATE]
        cb = lax.dot_general(cg, bg, NT, preferred_element_type=F32)
        for r in range(heads_per_group):
            hd = g * heads_per_group + r
            a_col = acs_t[:, MISC_DT + hd:MISC_DT + hd + 1]
            a_row = acs_row[MISC_DT + hd:MISC_DT + hd + 1, :]
            lmat = jnp.exp2(jnp.where(causal, a_col - a_row, -jnp.inf))
            mh = (cb * lmat).astype(BF16)
            y_parts.append(jnp.dot(mh, xdt_b[:, hd * SSM_HEAD_DIM:(hd + 1) * SSM_HEAD_DIM],
                                   preferred_element_type=F32))
        prev = state[g]
        off_parts.append(jnp.dot(cg, prev.astype(BF16), preferred_element_type=F32))
        st = jnp.dot(bm_t[g * SSM_STATE:(g + 1) * SSM_STATE, :], xdec[:, g * gw:(g + 1) * gw],
                     preferred_element_type=F32)
        state[g] = prev * jnp.exp2(a_last[:, g * gw:(g + 1) * gw]) + st
    y = (jnp.concatenate(y_parts, axis=1)
         + jnp.concatenate(off_parts, axis=1) * jnp.exp2(acs_e)
         + xs * dskip_ref[...])
    o_ref[0] = _rms(y * _silu(z_ref[0]), nw_ref[...]).astype(BF16)


def _ssd(xbc, z, misc, cw, cb, dtb_t, a_t, expand, dskip_e, nw):
    bsz, t, _ = xbc.shape
    lc = math.gcd(t, CHUNK)
    nc = t // lc
    gw = (SSM_HEADS // SSM_GROUPS) * SSM_HEAD_DIM

    def const(shape):
        return pl.BlockSpec(shape, lambda b, c: tuple(0 for _ in shape))

    return pl.pallas_call(
        _ssd_kernel,
        grid=(bsz, nc),
        in_specs=[pl.BlockSpec((1, lc, D_XBC), lambda b, c: (b, c, 0)),
                  pl.BlockSpec((1, lc, D_SSM), lambda b, c: (b, c, 0)),
                  pl.BlockSpec((1, lc, MISC_W), lambda b, c: (b, c, 0)),
                  const((CONV_WIDTH, D_XBC)), const((1, D_XBC)), const((1, MISC_W)), const((1, MISC_W)),
                  const((MISC_W, D_SSM)), const((1, D_SSM)), const((1, D_SSM))],
        out_specs=pl.BlockSpec((1, lc, D_SSM), lambda b, c: (b, c, 0)),
        out_shape=jax.ShapeDtypeStruct((bsz, t, D_SSM), BF16),
        scratch_shapes=[pltpu.VMEM((lc + 8, D_XBC), F32),
                        pltpu.VMEM((SSM_GROUPS, SSM_STATE, gw), F32)],
        compiler_params=pltpu.CompilerParams(
            dimension_semantics=("arbitrary", "arbitrary"), vmem_limit_bytes=VMEM_LIMIT),
        name="ssd",
    )(xbc, z, misc, cw, cb, dtb_t, a_t, expand, dskip_e, nw)


def _bucket_tiles():
    s = np.arange(BLK)[:, None]
    t = np.arange(BLK)[None, :]
    max_exact = N_BUCKETS // 2
    tiles = []
    for d0 in (0, BLK):
        n = np.maximum(d0 + t - s, 0)
        large = {}
        for dt_ in (np.float32, np.float64):
            nf = np.maximum(n, 1).astype(dt_)
            lg = max_exact + (np.log(nf / dt_(max_exact)) / dt_(math.log(MAX_DISTANCE / max_exact))
                              * dt_(N_BUCKETS - max_exact)).astype(np.int32)
            large[dt_] = np.minimum(lg, N_BUCKETS - 1)
        assert (large[np.float32] == large[np.float64]).all()
        tiles.append(np.where(n < max_exact, n, large[np.float64]).astype(np.int32))
    assert BLK + 1 >= MAX_DISTANCE
    return np.stack(tiles)


def _to_key(x):
    bits = lax.bitcast_convert_type(x, jnp.int32)
    key = bits ^ (lax.shift_right_arithmetic(bits, 31) & jnp.int32(0x7FFFFFFF))
    return jnp.where(key == -1, 0, key)


def _transpose32(rows):
    rows = list(rows)
    mask, j = 0x0000FFFF, 16
    while j:
        k = 0
        while k < 32:
            t = (rows[k] ^ lax.shift_right_logical(rows[k + j], jnp.int32(j))) & jnp.int32(mask)
            rows[k] = rows[k] ^ t
            rows[k + j] = rows[k + j] ^ lax.shift_left(t, jnp.int32(j))
            k = (k + j + 1) & ~j
        j >>= 1
        mask ^= (mask << j) & 0xFFFFFFFF
    return rows


def _dsa_kernel(relb_ref, bkt_ref, q_ref, qic_ref, miscc_ref, qin_ref, miscn_ref, k_ref, vt_ref, kidx_ref,
                nw_ref, o_ref, keys, planes, live, thr_s, cnt_s, bias_s, qcat, qicat, wcat, m_s, acc_s, p_s,
                *, topk, idx_bits, nq):
    b = pl.program_id(0)
    qi = pl.program_id(1)
    tq = BLK
    i32 = jnp.int32
    rep = ATTN_HEADS // ATTN_KV_HEADS
    gw = rep * tq
    ch = DSA_CHUNK
    ch_blks = ch // BLK
    rows_per_chunk = ch // 32

    @pl.when((b == 0) & (qi == 0))
    def _init_bias():
        for d in range(2):
            bk = bkt_ref[d]
            for h in range(ATTN_HEADS):
                far = relb_ref[N_BUCKETS - 1, h]
                acc = jnp.zeros((BLK, tq), F32)
                for n in range(N_BUCKETS - 1):
                    acc = jnp.where(bk == n, relb_ref[n, h] - far, acc)
                bias_s[d, h // rep, :, (h % rep) * BLK:(h % rep + 1) * BLK] = acc * LOG2E
        bias_s[2] = jnp.zeros((ATTN_KV_HEADS, BLK, gw), F32)
        for w in range(32):
            planes[w] = jnp.zeros(planes.shape[1:], i32)

    row_c = lax.broadcasted_iota(i32, (ch, tq), 0)
    lane_c = lax.broadcasted_iota(i32, (ch, tq), 1)

    def chunk_start(c):
        return pl.multiple_of(c * ch, ch)

    def n_chunks(blk):
        return (blk + ch_blks) // ch_blks

    def load_indexer(qidx_ref, misc_ref):
        qicat[...] = qidx_ref[0, 0]
        w_t = jnp.transpose(misc_ref[0])[MISC_WIDX:MISC_WIDX + IDX_HEADS, :] * (
            IDX_HEADS ** -0.5 * IDX_DIM ** -0.5)
        for h in range(IDX_HEADS):
            wcat[:, h * BLK:(h + 1) * BLK] = jnp.broadcast_to(w_t[h:h + 1, :], (8, BLK))

    def score_chunk(c, blk, slot):
        s0 = chunk_start(c)
        kb = kidx_ref[0, pl.ds(s0, ch), MISC_KIDX:MISC_KIDX + IDX_DIM]
        s = jnp.dot(kb, qicat[...], preferred_element_type=F32)
        r = jnp.maximum(s, 0.0) * wcat[0:1, :]
        acc = r[:, 0:BLK]
        for h in range(1, IDX_HEADS):
            acc = acc + r[:, h * BLK:(h + 1) * BLK]
        acc = jnp.where(s0 + row_c <= blk * BLK + lane_c, acc, -jnp.inf)
        keys[slot, pl.ds(s0, ch), :] = _to_key(acc)

    def plane_chunk(c, slot):
        ku = (keys[slot, pl.ds(chunk_start(c), ch), :] ^ i32(INT_MIN)).reshape(ch // (32 * 8), 32, 8, tq)
        words = _transpose32([ku[:, j] for j in range(32)])
        r0 = pl.multiple_of(c * rows_per_chunk, rows_per_chunk)
        for w in range(32):
            planes[w, pl.ds(r0, rows_per_chunk), :] = words[w].reshape(rows_per_chunk, tq)

    def select(blk, slot):
        nch = n_chunks(blk)
        used = nch * rows_per_chunk

        def ones_in(words):
            return jnp.sum(lax.population_count(words), axis=0, keepdims=True)

        def passes(nr):
            live[0:nr] = jnp.where(lax.broadcasted_iota(i32, (nr, tq), 0) < used, i32(-1), i32(0))

            def bit_body(w, st):
                thr_u, n_above, n_set = st
                take = n_above + n_set >= topk
                thr_u = jnp.where(take, thr_u | lax.shift_left(i32(1), i32(31) - w), thr_u)
                n_above = jnp.where(take, n_above, n_above + n_set)
                still = live[0:nr] & (planes[w, 0:nr] ^ jnp.where(take, i32(0), i32(-1)))
                live[0:nr] = still
                return thr_u, n_above, ones_in(still & planes[jnp.minimum(w + 1, 31), 0:nr])

            zero = jnp.zeros((1, tq), i32)
            thr_u, n_above, _ = lax.fori_loop(
                0, 32, bit_body, (zero, zero, ones_in(live[0:nr] & planes[0, 0:nr])))
            thr_s[slot] = thr_u ^ i32(INT_MIN)
            cnt_s[0] = n_above
            cnt_s[1] = ones_in(live[0:nr])

        size_step = live.shape[0] // SEL_SIZES
        for v in range(SEL_SIZES):
            pl.when((used > v * size_step) & (used <= (v + 1) * size_step))(
                functools.partial(passes, (v + 1) * size_step))
        thr = thr_s[slot]
        n_gt = cnt_s[0]
        n_eq = cnt_s[1]
        need = topk - n_gt

        def break_ties():
            def count(pred):
                def body(c, cnt):
                    s0 = chunk_start(c)
                    hit = pred(keys[slot, pl.ds(s0, ch), :], s0).astype(i32)
                    n = ch
                    while n > 8:
                        n //= 2
                        hit = hit[:n] + hit[n:2 * n]
                    return cnt + hit
                cnt = lax.fori_loop(0, nch, body, jnp.zeros((8, tq), i32))
                return jnp.sum(cnt, axis=0, keepdims=True)

            def jbody(it, p):
                cand = p + lax.shift_left(i32(1), i32(idx_bits - 1) - it)
                c = count(lambda kk, s0: (kk == thr) & (s0 + row_c < cand))
                return jnp.where(c < need, cand, p)
            last = lax.fori_loop(0, idx_bits, jbody, jnp.zeros((1, tq), i32))

            def demote(c, carry):
                s0 = chunk_start(c)
                kk = keys[slot, pl.ds(s0, ch), :]
                keys[slot, pl.ds(s0, ch), :] = jnp.where((kk == thr) & (s0 + row_c > last), INT_MIN, kk)
                return carry
            lax.fori_loop(0, nch, demote, 0)

        return (n_eq > need).astype(i32), break_ties

    def any_lane(flags):
        return jnp.max(flags, axis=1, keepdims=True)

    @pl.when(qi == 0)
    def _first_block():
        load_indexer(qic_ref, miscc_ref)
        score_chunk(0, 0, 0)
        plane_chunk(0, 0)
        tied, break_ties = select(0, 0)
        pl.when(any_lane(tied)[0, 0] > 0)(break_ties)

    slot_b = qi % 2
    slot_a = 1 - slot_b
    blk_a = jnp.minimum(qi + 1, nq - 1)
    nch_b = n_chunks(qi)
    nch_a = n_chunks(blk_a)
    load_indexer(qin_ref, miscn_ref)
    thr = thr_s[slot_b]

    for g in range(ATTN_KV_HEADS):
        qcat[g, 0:ATTN_HEAD_DIM, :] = q_ref[0, 0, g]
    for g in range(ATTN_KV_HEADS):
        qcat[g, ATTN_HEAD_DIM:, :] = jnp.zeros((K_AUG - ATTN_HEAD_DIM, gw), BF16)

    def pen_of(s0, causal):
        sel = keys[slot_b, pl.ds(s0, ch), :] >= thr
        if causal:
            sel = sel & (s0 + row_c <= qi * BLK + lane_c)
        pen = jnp.where(sel, 0.0, NEG)
        return jnp.concatenate([pen] * rep, axis=1)

    def near_bias(c, g):
        return jnp.concatenate(
            [bias_s[jnp.clip(qi - (c * ch_blks + u), 0, 2), g] for u in range(ch_blks)], axis=0)

    def vt_chunk(c, g):
        return jnp.concatenate([vt_ref[0, c * ch_blks + u, g] for u in range(ch_blks)], axis=1)

    s0d = pl.multiple_of(qi * BLK, BLK)
    kd = k_ref[0, pl.ds(s0d, BLK), :]
    tri = lax.broadcasted_iota(i32, (BLK, tq), 0) <= lax.broadcasted_iota(i32, (BLK, tq), 1)
    tri_pen = jnp.concatenate([jnp.where(tri, 0.0, NEG)] * rep, axis=1)
    for g in range(ATTN_KV_HEADS):
        sd = jnp.dot(kd[:, g * K_AUG:(g + 1) * K_AUG], qcat[g],
                     preferred_element_type=F32) + bias_s[0, g] + tri_pen
        shift = jnp.max(sd, axis=0, keepdims=True)
        first = lax.broadcasted_iota(i32, (SHIFT_ROWS, gw), 0) == 0
        qcat[g, ATTN_HEAD_DIM:ATTN_HEAD_DIM + SHIFT_ROWS, :] = jnp.where(first, -shift, 0.0).astype(BF16)
        acc_s[g] = jnp.zeros((V_ROWS, gw), F32)
        p_s[1, g] = jnp.zeros((ch, gw), BF16)

    def weigh_values(c, slot):
        for g in range(ATTN_KV_HEADS):
            acc_s[g] += jnp.dot(vt_chunk(c, g), p_s[slot, g], preferred_element_type=F32)

    def att_chunk(c, near, slots=None):
        prev_slot, cur_slot = slots if slots is not None else ((c + 1) % 2, c % 2)
        weigh_values(jnp.maximum(c - 1, 0), prev_slot)
        s0 = chunk_start(c)
        pen = pen_of(s0, near)
        kblk = k_ref[0, pl.ds(s0, ch), :]
        for g in range(ATTN_KV_HEADS):
            s = jnp.dot(kblk[:, g * K_AUG:(g + 1) * K_AUG], qcat[g],
                        preferred_element_type=F32)
            if near:
                s = s + near_bias(c, g)
            p_s[cur_slot, g] = jnp.exp2(s + pen).astype(BF16)

    def fused(c, near, slots=None):
        plane_chunk(c, slot_a)
        att_chunk(c, near, slots)
        score_chunk(c + 1, blk_a, slot_a)

    def far_pair(j, carry):
        fused(2 * j, False, (1, 0))
        fused(2 * j + 1, False, (0, 1))
        return carry

    def far_quad(j, carry):
        far_pair(2 * j, carry)
        return far_pair(2 * j + 1, carry)

    def far_body(c, carry):
        fused(c, False, (1, 0))
        return carry

    def near_body(c, carry):
        fused(c, True)
        return carry

    score_chunk(0, blk_a, slot_a)
    n_far = jnp.maximum(qi - 1, 0) // ch_blks
    lax.fori_loop(0, n_far // 4, far_quad, 0)
    lax.fori_loop((n_far // 4) * 2, n_far // 2, far_pair, 0)
    lax.fori_loop((n_far // 2) * 2, n_far, far_body, 0)
    lax.fori_loop(n_far, nch_b - 1, near_body, 0)
    @pl.when(nch_a > nch_b)
    def _one_more_chunk():
        score_chunk(nch_b, blk_a, slot_a)
        plane_chunk(nch_b, slot_a)

    plane_chunk(nch_b - 1, slot_a)
    att_chunk(nch_b - 1, True)
    weigh_values(nch_b - 1, (nch_b - 1) % 2)
    tied, break_ties = select(blk_a, slot_a)

    n_ok = jnp.zeros((1, gw), i32)
    for g in range(ATTN_KV_HEADS):
        a = acc_s[g]
        den = a[ATTN_HEAD_DIM:ATTN_HEAD_DIM + 1, :]
        top = jnp.max(jnp.abs(a[0:ATTN_HEAD_DIM, :]), axis=0, keepdims=True)
        n_ok = n_ok + ((den > DEN_LO) & (den < DEN_HI) & (top < F32_HUGE)).astype(i32)

    rare = (any_lane(tied) + 2 * any_lane((n_ok < ATTN_KV_HEADS).astype(i32)))[0, 0]
    pl.when(rare % 2 == 1)(break_ties)

    @pl.when(rare >= 2)
    def _exact():
        for g in range(ATTN_KV_HEADS):
            qcat[g, ATTN_HEAD_DIM:ATTN_HEAD_DIM + SHIFT_ROWS, :] = jnp.zeros((SHIFT_ROWS, gw), BF16)
            m_s[g] = jnp.full((1, gw), NEG, F32)
            acc_s[g] = jnp.zeros((V_ROWS, gw), F32)

        def body(c, carry):
            s0 = chunk_start(c)
            pen = pen_of(s0, True)
            kblk = k_ref[0, pl.ds(s0, ch), :]
            for g in range(ATTN_KV_HEADS):
                s = jnp.dot(kblk[:, g * K_AUG:(g + 1) * K_AUG], qcat[g],
                            preferred_element_type=F32) + near_bias(c, g) + pen
                m_old = m_s[g]
                m_new = jnp.maximum(m_old, jnp.max(s, axis=0, keepdims=True))
                p = jnp.exp2(s - m_new).astype(BF16)
                acc_s[g] = acc_s[g] * jnp.exp2(m_old - m_new) + jnp.dot(
                    vt_chunk(c, g), p, preferred_element_type=F32)
                m_s[g] = m_new
            return carry
        lax.fori_loop(0, nch_b, body, 0)

    parts = []
    for g in range(ATTN_KV_HEADS):
        a = acc_s[g]
        o = a[0:ATTN_HEAD_DIM, :] * (1.0 / a[ATTN_HEAD_DIM:ATTN_HEAD_DIM + 1, :])
        parts += [o[:, r * tq:(r + 1) * tq] for r in range(rep)]
    out_t = jnp.concatenate(parts, axis=0)
    o_ref[0] = _rms(jnp.transpose(out_t), nw_ref[...]).astype(BF16)


def _dsa(rel_bias, q, qi, misc, k, vt, miscb, nw):
    bsz, t, _ = k.shape
    nq = t // BLK
    topk = min(TOPK_MAX, t // 4)
    idx_bits = max(1, (t - 1).bit_length())
    gw = (ATTN_HEADS // ATTN_KV_HEADS) * BLK
    bkt = jnp.asarray(_bucket_tiles())
    kern = functools.partial(_dsa_kernel, topk=topk, idx_bits=idx_bits, nq=nq)

    def nxt(b, i):
        return (b, jnp.minimum(i + 1, nq - 1), 0)

    return pl.pallas_call(
        kern,
        grid=(bsz, nq),
        in_specs=[pl.BlockSpec(memory_space=pltpu.SMEM),
                  pl.BlockSpec((2, BLK, BLK), lambda b, i: (0, 0, 0)),
                  pl.BlockSpec((1, 1, ATTN_KV_HEADS, ATTN_HEAD_DIM, gw), lambda b, i: (b, i, 0, 0, 0)),
                  pl.BlockSpec((1, 1, IDX_DIM, IDX_HEADS * BLK), lambda b, i: (b, i, 0, 0)),
                  pl.BlockSpec((1, BLK, MISC_W), lambda b, i: (b, i, 0)),
                  pl.BlockSpec((1, 1, IDX_DIM, IDX_HEADS * BLK), lambda b, i: nxt(b, i) + (0,)),
                  pl.BlockSpec((1, BLK, MISC_W), nxt),
                  pl.BlockSpec((1, t, D_K_AUG), lambda b, i: (b, 0, 0)),
                  pl.BlockSpec((1, nq, ATTN_KV_HEADS, V_ROWS, BLK), lambda b, i: (b, 0, 0, 0, 0)),
                  pl.BlockSpec((1, t, MISC_W), lambda b, i: (b, 0, 0)),
                  pl.BlockSpec((1, D_ATTN), lambda b, i: (0, 0))],
        out_specs=pl.BlockSpec((1, BLK, D_ATTN), lambda b, i: (b, i, 0)),
        out_shape=jax.ShapeDtypeStruct((bsz, t, D_ATTN), BF16),
        scratch_shapes=[pltpu.VMEM((2, t, BLK), jnp.int32),
                        pltpu.VMEM((32, t // 32, BLK), jnp.int32),
                        pltpu.VMEM((t // 32, BLK), jnp.int32),
                        pltpu.VMEM((2, 1, BLK), jnp.int32),
                        pltpu.VMEM((2, 1, BLK), jnp.int32),
                        pltpu.VMEM((3, ATTN_KV_HEADS, BLK, gw), F32),
                        pltpu.VMEM((ATTN_KV_HEADS, K_AUG, gw), BF16),
                        pltpu.VMEM((IDX_DIM, IDX_HEADS * BLK), BF16),
                        pltpu.VMEM((8, IDX_HEADS * BLK), F32),
                        pltpu.VMEM((ATTN_KV_HEADS, 1, gw), F32),
                        pltpu.VMEM((ATTN_KV_HEADS, V_ROWS, gw), F32),
                        pltpu.VMEM((2, ATTN_KV_HEADS, DSA_CHUNK, gw), BF16)],
        compiler_params=pltpu.CompilerParams(
            dimension_semantics=("arbitrary", "arbitrary"), vmem_limit_bytes=VMEM_LIMIT),
        name="dsa",
    )(rel_bias, bkt, q, qi, misc, qi, misc, k, vt, miscb, nw)


def _tail_kernel(x_ref, ys_ref, ya_ref, mod_ref, wo_ref, n1_ref, n2_ref, n3_ref, w1_ref, w2_ref, o_ref):
    x = x_ref[0]
    mix = (jnp.dot(ys_ref[0], wo_ref[0:D_SSM, :], preferred_element_type=F32)
           + jnp.dot(ya_ref[0], wo_ref[D_SSM:, :], preferred_element_type=F32))
    x1 = x + mod_ref[0, 2:3, :] * _rms(mix, n1_ref[...])
    h = (_rms(x1, n2_ref[...]) * (1.0 + mod_ref[0, 4:5, :]) + mod_ref[0, 3:4, :]).astype(BF16)
    f = jnp.zeros(x.shape, F32)
    step = D_MODEL
    for c0 in range(0, D_FF, step):
        a = jnp.maximum(jnp.dot(h, w1_ref[:, c0:c0 + step], preferred_element_type=F32), 0.0)
        f = f + jnp.dot((a * a).astype(BF16), w2_ref[c0:c0 + step, :], preferred_element_type=F32)
    o_ref[0] = x1 + mod_ref[0, 5:6, :] * _rms(f, n3_ref[...])


def _tail(x, ys, ya, mod, wo, n1, n2, n3, w1, w2):
    bsz, t, d = x.shape
    tm = min(ROW_TILE, t)

    def const(shape):
        return pl.BlockSpec(shape, lambda b, i: tuple(0 for _ in shape))

    return pl.pallas_call(
        _tail_kernel,
        grid=(bsz, t // tm),
        in_specs=[pl.BlockSpec((1, tm, d), lambda b, i: (b, i, 0)),
                  pl.BlockSpec((1, tm, D_SSM), lambda b, i: (b, i, 0)),
                  pl.BlockSpec((1, tm, D_ATTN), lambda b, i: (b, i, 0)),
                  pl.BlockSpec((1, 6, d), lambda b, i: (b, 0, 0)),
                  const((D_SSM + D_ATTN, d)), const((1, d)), const((1, d)), const((1, d)),
                  const((d, D_FF)), const((D_FF, d))],
        out_specs=pl.BlockSpec((1, tm, d), lambda b, i: (b, i, 0)),
        out_shape=jax.ShapeDtypeStruct((bsz, t, d), F32),
        compiler_params=pltpu.CompilerParams(
            dimension_semantics=("arbitrary", "arbitrary"), vmem_limit_bytes=VMEM_LIMIT_TAIL),
        name="tail",
    )(x, ys, ya, mod, wo, n1, n2, n3, w1, w2)


def _permute_w_in(w):
    sizes = [D_SSM, D_XBC, SSM_HEADS, D_ATTN, D_KV, D_KV, IDX_HEADS * IDX_DIM, IDX_DIM, IDX_HEADS]
    pts = np.cumsum(sizes)[:-1].tolist()
    z, xbc, dt, q, k, v, qidx, kidx, widx = jnp.split(w, pts, axis=-1)
    pad = jnp.zeros((w.shape[0], MISC_W - IDX_DIM - IDX_HEADS - SSM_HEADS), w.dtype)
    kpad = jnp.zeros((w.shape[0], K_AUG - ATTN_HEAD_DIM), w.dtype)
    k_aug = []
    for g in range(ATTN_KV_HEADS):
        k_aug += [k[:, g * ATTN_HEAD_DIM:(g + 1) * ATTN_HEAD_DIM], kpad]
    return jnp.concatenate([z, xbc, q] + k_aug + [qidx, v, kidx, widx, dt, pad], axis=-1).astype(BF16)


def _layer(x, c_pad, w_ada, b_ada, norm1_pre, norm1_post, w_in, conv_w, conv_b, dt_bias, a_log, d_skip,
           ssm_norm, rel_bias, attn_norm, w_out, norm2_pre, norm2_post, w_mlp_in, w_mlp_out):
    bsz, t, d = x.shape
    mod = _ada(c_pad, w_ada, b_ada[None, :])[:bsz].reshape(bsz, 6, d)
    z, xbc, q, k, vt, qi, misc, miscb = _inproj(x, mod, norm1_pre[None, :], _permute_w_in(w_in))

    lane = jnp.arange(MISC_W)
    is_dt = (lane >= MISC_DT) & (lane < MISC_DT + SSM_HEADS)
    head_of = jnp.clip(lane - MISC_DT, 0, SSM_HEADS - 1)
    dtb_t = jnp.where(is_dt, dt_bias[head_of], 0.0)[None, :]
    a_t = jnp.where(is_dt, (-jnp.exp(a_log.astype(F32)) * LOG2E)[head_of], 0.0)[None, :]
    expand = ((lane[:, None] - MISC_DT) == (jnp.arange(D_SSM)[None, :] // SSM_HEAD_DIM)).astype(BF16)
    dskip_e = jnp.repeat(d_skip.astype(F32), SSM_HEAD_DIM)[None, :]
    y_ssd = _ssd(xbc, z, misc, conv_w, conv_b[None, :], dtb_t, a_t, expand, dskip_e, ssm_norm[None, :])

    y_att = _dsa(rel_bias, q, qi, misc, k, vt, miscb, attn_norm[None, :])

    return _tail(x, y_ssd, y_att, mod, w_out.astype(BF16), norm1_post[None, :], norm2_pre[None, :],
                 norm2_post[None, :], w_mlp_in.astype(BF16), w_mlp_out.astype(BF16))


def kernel(x, c, w_ada, b_ada, norm1_pre, norm1_post, w_in, conv_w, conv_b, dt_bias, a_log, d_skip,
           ssm_norm, rel_bias, attn_norm, w_out, norm2_pre, norm2_post, w_mlp_in, w_mlp_out):
    bsz = x.shape[0]
    assert bsz <= 8 and x.shape[1] % DSA_CHUNK == 0 and x.shape[2] == D_MODEL
    c_pad = jnp.zeros((8, D_MODEL), F32).at[:bsz].set(c)
    for l in range(w_ada.shape[0]):
        x = _layer(x, c_pad, w_ada[l], b_ada[l], norm1_pre[l], norm1_post[l], w_in[l], conv_w[l], conv_b[l],
                   dt_bias[l], a_log[l], d_skip[l], ssm_norm[l], rel_bias, attn_norm[l], w_out[l],
                   norm2_pre[l], norm2_post[l], w_mlp_in[l], w_mlp_out[l])
    return x
```

```python
import functools
import math

import numpy as np
import jax
import jax.numpy as jnp
from jax import lax
from jax.experimental import pallas as pl
from jax.experimental.pallas import tpu as pltpu

D_MODEL = 1024
SSM_HEADS = 8
SSM_HEAD_DIM = 64
D_SSM = SSM_HEADS * SSM_HEAD_DIM
SSM_GROUPS = 2
SSM_STATE = 128
CONV_WIDTH = 4
CHUNK = 256
ATTN_HEADS = 8
ATTN_KV_HEADS = 2
ATTN_HEAD_DIM = 64
D_ATTN = ATTN_HEADS * ATTN_HEAD_DIM
D_KV = ATTN_KV_HEADS * ATTN_HEAD_DIM
IDX_HEADS = 8
IDX_DIM = 64
TOPK_MAX = 256
N_BUCKETS = 32
MAX_DISTANCE = 128
D_FF = 4 * D_MODEL
D_BC = SSM_GROUPS * SSM_STATE
D_XBC = D_SSM + 2 * D_BC
EPS = 1e-6

LANES = 128
V7X_VMEM_BYTES = 64 * 2 ** 20
VMEM_LIMIT = V7X_VMEM_BYTES * 3 // 4
VMEM_LIMIT_TAIL = V7X_VMEM_BYTES * 15 // 16
MISC_W = LANES
MISC_KIDX = 0
MISC_WIDX = IDX_DIM
MISC_DT = IDX_DIM + IDX_HEADS
OFF_Z = 0
OFF_XBC = OFF_Z + D_SSM
OFF_Q = OFF_XBC + D_XBC
OFF_QI = OFF_Q + D_ATTN
OFF_K = OFF_QI + IDX_HEADS * IDX_DIM
OFF_V = OFF_K + D_KV
OFF_MISC = OFF_V + D_KV
D_IN_PAD = OFF_MISC + MISC_W

BLK = 128
DSA_CHUNK = 512
V_ROWS = 80
ZERO_SLOT = 2
SEL_SIZES = 4
DEN_LO = 2.0 ** -80
DEN_HI = 2.0 ** 80
F32_HUGE = 3.0e38
LOG2E = math.log2(math.e)
ROW_TILE = 512
NEG = -1e30
INT_MIN = -2 ** 31

F32 = jnp.float32
BF16 = jnp.bfloat16
HI = lax.Precision.HIGHEST
NT = (((1,), (1,)), ((), ()))


def _silu(x):
    return x / (1.0 + jnp.exp(-x))


def _rms(x, w):
    return x * lax.rsqrt(jnp.mean(x * x, axis=-1, keepdims=True) + EPS) * w


def _split3(x):
    hi = x.astype(BF16)
    rest = x - hi.astype(F32)
    mid = rest.astype(BF16)
    return hi, mid, (rest - mid.astype(F32)).astype(BF16)


def _ada_kernel(c_ref, w_ref, b_ref, o_ref):
    s = _silu(c_ref[...])
    o_ref[...] = jnp.dot(s, w_ref[...], precision=HI, preferred_element_type=F32) + b_ref[...]


def _ada(c_pad, w, b):
    n = w.shape[1]
    tn = D_MODEL
    return pl.pallas_call(
        _ada_kernel,
        grid=(n // tn,),
        in_specs=[pl.BlockSpec((8, D_MODEL), lambda j: (0, 0)),
                  pl.BlockSpec((D_MODEL, tn), lambda j: (0, j)),
                  pl.BlockSpec((1, tn), lambda j: (0, j))],
        out_specs=pl.BlockSpec((8, tn), lambda j: (0, j)),
        out_shape=jax.ShapeDtypeStruct((8, n), F32),
        name="ada",
    )(c_pad, w, b)


def _inproj_kernel(x_ref, mod_ref, nw_ref, w_ref,
                   z_ref, xbc_ref, q_ref, k_ref, vt_ref, qi_ref, misc_ref, miscb_ref):
    x = x_ref[0]
    h = _rms(x, nw_ref[...]) * (1.0 + mod_ref[0, 1:2, :]) + mod_ref[0, 0:1, :]
    hb = h.astype(BF16)

    def seg(lo, width):
        return jnp.dot(hb, w_ref[:, lo:lo + width], preferred_element_type=F32)

    z_ref[0] = seg(OFF_Z, D_SSM)
    xbc_ref[0] = seg(OFF_XBC, D_XBC)
    rep = ATTN_HEADS // ATTN_KV_HEADS
    q_all = seg(OFF_Q, D_ATTN) * (ATTN_HEAD_DIM ** -0.5 * LOG2E)
    qi_all = seg(OFF_QI, IDX_HEADS * IDX_DIM)
    for j in range(x.shape[0] // BLK):
        q_t = jnp.transpose(q_all[j * BLK:(j + 1) * BLK, :])
        qi_t = jnp.transpose(qi_all[j * BLK:(j + 1) * BLK, :])
        for h in range(ATTN_HEADS):
            q_ref[0, j, h // rep, :, (h % rep) * BLK:(h % rep + 1) * BLK] = (
                q_t[h * ATTN_HEAD_DIM:(h + 1) * ATTN_HEAD_DIM, :].astype(BF16))
        for h in range(IDX_HEADS):
            qi_ref[0, j, :, h * BLK:(h + 1) * BLK] = qi_t[h * IDX_DIM:(h + 1) * IDX_DIM, :].astype(BF16)
    k_v_misc = seg(OFF_K, 2 * D_KV + MISC_W)
    k_ref[0] = k_v_misc[:, :D_KV].astype(BF16)
    v_misc = k_v_misc[:, D_KV:]
    ones_tile = jnp.where(lax.broadcasted_iota(jnp.int32, (V_ROWS - ATTN_HEAD_DIM, BLK), 0) == 0, 1.0, 0.0)
    for j in range(x.shape[0] // BLK):
        v_t = jnp.transpose(v_misc[j * BLK:(j + 1) * BLK, :D_KV])
        for g in range(ATTN_KV_HEADS):
            vt_ref[0, j, g, 0:ATTN_HEAD_DIM, :] = v_t[g * ATTN_HEAD_DIM:(g + 1) * ATTN_HEAD_DIM, :].astype(BF16)
            vt_ref[0, j, g, ATTN_HEAD_DIM:, :] = ones_tile.astype(BF16)
    misc = v_misc[:, D_KV:]
    misc_ref[0] = misc
    miscb_ref[0] = misc.astype(BF16)


def _inproj(x, mod, nw, w_perm):
    bsz, t, d = x.shape
    tm = min(ROW_TILE, t)

    def tok(width, dtype):
        return (pl.BlockSpec((1, tm, width), lambda b, i: (b, i, 0)),
                jax.ShapeDtypeStruct((bsz, t, width), dtype))

    vt = (pl.BlockSpec((1, tm // BLK, ATTN_KV_HEADS, V_ROWS, BLK), lambda b, i: (b, i, 0, 0, 0)),
          jax.ShapeDtypeStruct((bsz, t // BLK, ATTN_KV_HEADS, V_ROWS, BLK), BF16))
    gw = (ATTN_HEADS // ATTN_KV_HEADS) * BLK
    qt = (pl.BlockSpec((1, tm // BLK, ATTN_KV_HEADS, ATTN_HEAD_DIM, gw), lambda b, i: (b, i, 0, 0, 0)),
          jax.ShapeDtypeStruct((bsz, t // BLK, ATTN_KV_HEADS, ATTN_HEAD_DIM, gw), BF16))
    qit = (pl.BlockSpec((1, tm // BLK, IDX_DIM, IDX_HEADS * BLK), lambda b, i: (b, i, 0, 0)),
           jax.ShapeDtypeStruct((bsz, t // BLK, IDX_DIM, IDX_HEADS * BLK), BF16))
    outs = [tok(D_SSM, F32), tok(D_XBC, F32), qt, tok(D_KV, BF16), vt, qit,
            tok(MISC_W, F32), tok(MISC_W, BF16)]
    return pl.pallas_call(
        _inproj_kernel,
        grid=(bsz, t // tm),
        in_specs=[pl.BlockSpec((1, tm, d), lambda b, i: (b, i, 0)),
                  pl.BlockSpec((1, 6, d), lambda b, i: (b, 0, 0)),
                  pl.BlockSpec((1, d), lambda b, i: (0, 0)),
                  pl.BlockSpec((d, D_IN_PAD), lambda b, i: (0, 0))],
        out_specs=[o[0] for o in outs],
        out_shape=[o[1] for o in outs],
        compiler_params=pltpu.CompilerParams(
            dimension_semantics=("arbitrary", "arbitrary"), vmem_limit_bytes=VMEM_LIMIT),
        name="inproj",
    )(x, mod, nw, w_perm)


def _ssd_kernel(xbc_ref, z_ref, misc_ref, cw_ref, cb_ref, dtb_ref, a_ref, e_ref, dskip_ref, nw_ref,
                o_ref, xbuf, state):
    c = pl.program_id(1)
    lc = xbc_ref.shape[1]

    @pl.when(c == 0)
    def _():
        xbuf[0:8, :] = jnp.zeros((8, D_XBC), F32)
        state[...] = jnp.zeros(state.shape, F32)

    @pl.when(c > 0)
    def _():
        xbuf[0:8, :] = xbuf[lc:lc + 8, :]

    xbuf[8:lc + 8, :] = xbc_ref[0]
    u = cb_ref[...]
    xall = xbuf[...]
    for k in range(CONV_WIDTH):
        back = CONV_WIDTH - 1 - k
        xk = xall if back == 0 else pltpu.roll(xall, back, 0)
        u = u + xk[8:8 + lc] * cw_ref[k:k + 1, :]
    u = _silu(u)
    xs = u[:, :D_SSM]
    bm = u[:, D_SSM:D_SSM + D_BC]
    cm = u[:, D_SSM + D_BC:]

    raw = misc_ref[0] + dtb_ref[...]
    dt_t = jnp.maximum(raw, 0.0) + jnp.log1p(jnp.exp(-jnp.abs(raw)))
    adt_t = dt_t * a_ref[...]
    row = lax.broadcasted_iota(jnp.int32, (lc, lc), 0)
    col = lax.broadcasted_iota(jnp.int32, (lc, lc), 1)
    causal = col <= row
    tril = jnp.where(causal, 1.0, 0.0).astype(BF16)
    expand = e_ref[...]
    acs_t = sum(jnp.dot(tril, piece, preferred_element_type=F32) for piece in _split3(adt_t))
    dt_e = sum(jnp.dot(piece, expand, preferred_element_type=F32) for piece in _split3(dt_t))
    acs_e = sum(jnp.dot(piece, expand, preferred_element_type=F32) for piece in _split3(acs_t))
    acs_row = jnp.transpose(acs_t)

    xdt = xs * dt_e
    a_last = acs_e[lc - 1:lc, :]
    xdec = (xdt * jnp.exp2(a_last - acs_e)).astype(BF16)
    xdt_b = xdt.astype(BF16)
    bm_b = bm.astype(BF16)
    cm_b = cm.astype(BF16)
    bm_t = jnp.transpose(bm).astype(BF16)

    heads_per_group = SSM_HEADS // SSM_GROUPS
    gw = heads_per_group * SSM_HEAD_DIM
    y_parts = []
    off_parts = []
    for g in range(SSM_GROUPS):
        bg = bm_b[:, g * SSM_STATE:(g + 1) * SSM_STATE]
        cg = cm_b[:, g * SSM_STATE:(g + 1) * SSM_STATE]
        cb = lax.dot_general(cg, bg, NT, preferred_element_type=F32)
        for r in range(heads_per_group):
            hd = g * heads_per_group + r
            a_col = acs_t[:, MISC_DT + hd:MISC_DT + hd + 1]
            a_row = acs_row[MISC_DT + hd:MISC_DT + hd + 1, :]
            lmat = jnp.exp2(jnp.where(causal, a_col - a_row, -jnp.inf))
            mh = (cb * lmat).astype(BF16)
            y_parts.append(jnp.dot(mh, xdt_b[:, hd * SSM_HEAD_DIM:(hd + 1) * SSM_HEAD_DIM],
                                   preferred_element_type=F32))
        prev = state[g]
        off_parts.append(jnp.dot(cg, prev.astype(BF16), preferred_element_type=F32))
        st = jnp.dot(bm_t[g * SSM_STATE:(g + 1) * SSM_STATE, :], xdec[:, g * gw:(g + 1) * gw],
                     preferred_element_type=F32)
        state[g] = prev * jnp.exp2(a_last[:, g * gw:(g + 1) * gw]) + st
    y = (jnp.concatenate(y_parts, axis=1)
         + jnp.concatenate(off_parts, axis=1) * jnp.exp2(acs_e)
         + xs * dskip_ref[...])
    o_ref[0] = _rms(y * _silu(z_ref[0]), nw_ref[...]).astype(BF16)


def _ssd(xbc, z, misc, cw, cb, dtb_t, a_t, expand, dskip_e, nw):
    bsz, t, _ = xbc.shape
    lc = math.gcd(t, CHUNK)
    nc = t // lc
    gw = (SSM_HEADS // SSM_GROUPS) * SSM_HEAD_DIM

    def const(shape):
        return pl.BlockSpec(shape, lambda b, c: tuple(0 for _ in shape))

    return pl.pallas_call(
        _ssd_kernel,
        grid=(bsz, nc),
        in_specs=[pl.BlockSpec((1, lc, D_XBC), lambda b, c: (b, c, 0)),
                  pl.BlockSpec((1, lc, D_SSM), lambda b, c: (b, c, 0)),
                  pl.BlockSpec((1, lc, MISC_W), lambda b, c: (b, c, 0)),
                  const((CONV_WIDTH, D_XBC)), const((1, D_XBC)), const((1, MISC_W)), const((1, MISC_W)),
                  const((MISC_W, D_SSM)), const((1, D_SSM)), const((1, D_SSM))],
        out_specs=pl.BlockSpec((1, lc, D_SSM), lambda b, c: (b, c, 0)),
        out_shape=jax.ShapeDtypeStruct((bsz, t, D_SSM), BF16),
        scratch_shapes=[pltpu.VMEM((lc + 8, D_XBC), F32),
                        pltpu.VMEM((SSM_GROUPS, SSM_STATE, gw), F32)],
        compiler_params=pltpu.CompilerParams(
            dimension_semantics=("arbitrary", "arbitrary"), vmem_limit_bytes=VMEM_LIMIT),
        name="ssd",
    )(xbc, z, misc, cw, cb, dtb_t, a_t, expand, dskip_e, nw)


def _bucket_tiles():
    s = np.arange(BLK)[:, None]
    t = np.arange(BLK)[None, :]
    max_exact = N_BUCKETS // 2
    tiles = []
    for d0 in (0, BLK):
        n = np.maximum(d0 + t - s, 0)
        large = {}
        for dt_ in (np.float32, np.float64):
            nf = np.maximum(n, 1).astype(dt_)
            lg = max_exact + (np.log(nf / dt_(max_exact)) / dt_(math.log(MAX_DISTANCE / max_exact))
                              * dt_(N_BUCKETS - max_exact)).astype(np.int32)
            large[dt_] = np.minimum(lg, N_BUCKETS - 1)
        assert (large[np.float32] == large[np.float64]).all()
        tiles.append(np.where(n < max_exact, n, large[np.float64]).astype(np.int32))
    assert BLK + 1 >= MAX_DISTANCE
    return np.stack(tiles)


def _to_key(x):
    bits = lax.bitcast_convert_type(x, jnp.int32)
    key = bits ^ (lax.shift_right_arithmetic(bits, 31) & jnp.int32(0x7FFFFFFF))
    return jnp.where(key == -1, 0, key)


def _transpose32(rows):
    rows = list(rows)
    mask, j = 0x0000FFFF, 16
    while j:
        k = 0
        while k < 32:
            t = (rows[k] ^ lax.shift_right_logical(rows[k + j], jnp.int32(j))) & jnp.int32(mask)
            rows[k] = rows[k] ^ t
            rows[k + j] = rows[k + j] ^ lax.shift_left(t, jnp.int32(j))
            k = (k + j + 1) & ~j
        j >>= 1
        mask ^= (mask << j) & 0xFFFFFFFF
    return rows


def _dsa_kernel(relb_ref, bkt_ref, q_ref, qic_ref, miscc_ref, qin_ref, miscn_ref, k_ref, vt_ref, kidx_ref,
                nw_ref, o_ref, keys, planes, live, thr_s, cnt_s, bias_s, qcat, qicat, wcat, m_s, acc_s, p_s,
                *, topk, idx_bits, nq):
    b = pl.program_id(0)
    qi = pl.program_id(1)
    tq = BLK
    i32 = jnp.int32
    rep = ATTN_HEADS // ATTN_KV_HEADS
    gw = rep * tq
    ch = DSA_CHUNK
    ch_blks = ch // BLK
    rows_per_chunk = ch // 32

    @pl.when((b == 0) & (qi == 0))
    def _init_bias():
        for d in range(2):
            bk = bkt_ref[d]
            for h in range(ATTN_HEADS):
                far = relb_ref[N_BUCKETS - 1, h]
                acc = jnp.zeros((BLK, tq), F32)
                for n in range(N_BUCKETS - 1):
                    acc = jnp.where(bk == n, relb_ref[n, h] - far, acc)
                bias_s[d, h // rep, :, (h % rep) * BLK:(h % rep + 1) * BLK] = acc * LOG2E
        bias_s[2] = jnp.zeros((ATTN_KV_HEADS, BLK, gw), F32)
        for w in range(32):
            planes[w] = jnp.zeros(planes.shape[1:], i32)
        p_s[ZERO_SLOT] = jnp.zeros(p_s.shape[1:], BF16)

    row_c = lax.broadcasted_iota(i32, (ch, tq), 0)
    lane_c = lax.broadcasted_iota(i32, (ch, tq), 1)

    def chunk_start(c):
        return pl.multiple_of(c * ch, ch)

    def n_chunks(blk):
        return (blk + ch_blks) // ch_blks

    def load_indexer(qidx_ref, misc_ref):
        qicat[...] = qidx_ref[0, 0]
        w_t = jnp.transpose(misc_ref[0])[MISC_WIDX:MISC_WIDX + IDX_HEADS, :] * (
            IDX_HEADS ** -0.5 * IDX_DIM ** -0.5)
        for h in range(IDX_HEADS):
            wcat[:, h * BLK:(h + 1) * BLK] = jnp.broadcast_to(w_t[h:h + 1, :], (8, BLK))

    def score_chunk(c, blk, slot):
        s0 = chunk_start(c)
        kb = kidx_ref[0, pl.ds(s0, ch), MISC_KIDX:MISC_KIDX + IDX_DIM]
        s = jnp.dot(kb, qicat[...], preferred_element_type=F32)
        r = jnp.maximum(s, 0.0) * wcat[0:1, :]
        acc = r[:, 0:BLK]
        for h in range(1, IDX_HEADS):
            acc = acc + r[:, h * BLK:(h + 1) * BLK]
        acc = jnp.where(s0 + row_c <= blk * BLK + lane_c, acc, -jnp.inf)
        keys[slot, pl.ds(s0, ch), :] = _to_key(acc)

    def plane_chunk(c, slot):
        ku = (keys[slot, pl.ds(chunk_start(c), ch), :] ^ i32(INT_MIN)).reshape(ch // (32 * 8), 32, 8, tq)
        words = _transpose32([ku[:, j] for j in range(32)])
        r0 = pl.multiple_of(c * rows_per_chunk, rows_per_chunk)
        for w in range(32):
            planes[w, pl.ds(r0, rows_per_chunk), :] = words[w].reshape(rows_per_chunk, tq)

    def select(blk, slot):
        nch = n_chunks(blk)
        used = nch * rows_per_chunk

        def ones_in(words):
            return jnp.sum(lax.population_count(words), axis=0, keepdims=True)

        def passes(nr):
            live[0:nr] = jnp.where(lax.broadcasted_iota(i32, (nr, tq), 0) < used, i32(-1), i32(0))

            def bit_body(w, st):
                thr_u, n_above, n_set = st
                take = n_above + n_set >= topk
                thr_u = jnp.where(take, thr_u | lax.shift_left(i32(1), i32(31) - w), thr_u)
                n_above = jnp.where(take, n_above, n_above + n_set)
                still = live[0:nr] & (planes[w, 0:nr] ^ jnp.where(take, i32(0), i32(-1)))
                live[0:nr] = still
                return thr_u, n_above, ones_in(still & planes[jnp.minimum(w + 1, 31), 0:nr])

            zero = jnp.zeros((1, tq), i32)
            thr_u, n_above, _ = lax.fori_loop(
                0, 32, bit_body, (zero, zero, ones_in(live[0:nr] & planes[0, 0:nr])))
            thr_s[slot] = thr_u ^ i32(INT_MIN)
            cnt_s[0] = n_above
            cnt_s[1] = ones_in(live[0:nr])

        size_step = live.shape[0] // SEL_SIZES
        for v in range(SEL_SIZES):
            pl.when((used > v * size_step) & (used <= (v + 1) * size_step))(
                functools.partial(passes, (v + 1) * size_step))
        thr = thr_s[slot]
        n_gt = cnt_s[0]
        n_eq = cnt_s[1]
        need = topk - n_gt

        def break_ties():
            def count(pred):
                def body(c, cnt):
                    s0 = chunk_start(c)
                    hit = pred(keys[slot, pl.ds(s0, ch), :], s0).astype(i32)
                    n = ch
                    while n > 8:
                        n //= 2
                        hit = hit[:n] + hit[n:2 * n]
                    return cnt + hit
                cnt = lax.fori_loop(0, nch, body, jnp.zeros((8, tq), i32))
                return jnp.sum(cnt, axis=0, keepdims=True)

            def jbody(it, p):
                cand = p + lax.shift_left(i32(1), i32(idx_bits - 1) - it)
                c = count(lambda kk, s0: (kk == thr) & (s0 + row_c < cand))
                return jnp.where(c < need, cand, p)
            last = lax.fori_loop(0, idx_bits, jbody, jnp.zeros((1, tq), i32))

            def demote(c, carry):
                s0 = chunk_start(c)
                kk = keys[slot, pl.ds(s0, ch), :]
                keys[slot, pl.ds(s0, ch), :] = jnp.where((kk == thr) & (s0 + row_c > last), INT_MIN, kk)
                return carry
            lax.fori_loop(0, nch, demote, 0)

        return (n_eq > need).astype(i32), break_ties

    def any_lane(flags):
        return jnp.max(flags, axis=1, keepdims=True)

    @pl.when(qi == 0)
    def _first_block():
        load_indexer(qic_ref, miscc_ref)
        score_chunk(0, 0, 0)
        plane_chunk(0, 0)
        tied, break_ties = select(0, 0)
        pl.when(any_lane(tied)[0, 0] > 0)(break_ties)

    slot_b = qi % 2
    slot_a = 1 - slot_b
    blk_a = jnp.minimum(qi + 1, nq - 1)
    nch_b = n_chunks(qi)
    nch_a = n_chunks(blk_a)
    load_indexer(qin_ref, miscn_ref)
    thr = thr_s[slot_b]

    for g in range(ATTN_KV_HEADS):
        qcat[g] = q_ref[0, 0, g]

    def pen_of(s0, causal):
        sel = keys[slot_b, pl.ds(s0, ch), :] >= thr
        if causal:
            sel = sel & (s0 + row_c <= qi * BLK + lane_c)
        pen = jnp.where(sel, 0.0, NEG)
        return jnp.concatenate([pen] * rep, axis=1)

    def near_bias(c, g):
        return jnp.concatenate(
            [bias_s[jnp.clip(qi - (c * ch_blks + u), 0, 2), g] for u in range(ch_blks)], axis=0)

    def vt_chunk(c, g):
        return jnp.concatenate([vt_ref[0, c * ch_blks + u, g] for u in range(ch_blks)], axis=1)

    for g in range(ATTN_KV_HEADS):
        acc_s[g] = jnp.zeros((V_ROWS, gw), F32)

    def weigh_values(c, slot):
        for g in range(ATTN_KV_HEADS):
            acc_s[g] += jnp.dot(vt_chunk(c, g), p_s[slot, g], preferred_element_type=F32)

    def att_chunk(c, near, slots=None):
        prev_slot, cur_slot = slots if slots is not None else ((c + 1) % 2, c % 2)
        if not (isinstance(cur_slot, int) and cur_slot == 1):
            prev_slot = jnp.where(c == 0, ZERO_SLOT, prev_slot)
        weigh_values(jnp.maximum(c - 1, 0), prev_slot)
        s0 = chunk_start(c)
        pen = pen_of(s0, near)
        kblk = k_ref[0, pl.ds(s0, ch), :]
        for g in range(ATTN_KV_HEADS):
            s = jnp.dot(kblk[:, g * ATTN_HEAD_DIM:(g + 1) * ATTN_HEAD_DIM], qcat[g],
                        preferred_element_type=F32)
            if near:
                s = s + near_bias(c, g)
            p_s[cur_slot, g] = jnp.exp2(s + pen).astype(BF16)

    def fused(c, near, slots=None):
        plane_chunk(c, slot_a)
        att_chunk(c, near, slots)
        score_chunk(c + 1, blk_a, slot_a)

    def far_pair(j, carry):
        fused(2 * j, False, (1, 0))
        fused(2 * j + 1, False, (0, 1))
        return carry

    def far_quad(j, carry):
        far_pair(2 * j, carry)
        return far_pair(2 * j + 1, carry)

    def far_body(c, carry):
        fused(c, False, (1, 0))
        return carry

    def near_body(c, carry):
        fused(c, True)
        return carry

    score_chunk(0, blk_a, slot_a)
    n_far = jnp.maximum(qi - 1, 0) // ch_blks
    lax.fori_loop(0, n_far // 4, far_quad, 0)
    lax.fori_loop((n_far // 4) * 2, n_far // 2, far_pair, 0)
    lax.fori_loop((n_far // 2) * 2, n_far, far_body, 0)
    lax.fori_loop(n_far, nch_b - 1, near_body, 0)
    @pl.when(nch_a > nch_b)
    def _one_more_chunk():
        score_chunk(nch_b, blk_a, slot_a)
        plane_chunk(nch_b, slot_a)

    plane_chunk(nch_b - 1, slot_a)
    att_chunk(nch_b - 1, True)
    weigh_values(nch_b - 1, (nch_b - 1) % 2)
    tied, break_ties = select(blk_a, slot_a)

    n_ok = jnp.zeros((1, gw), i32)
    for g in range(ATTN_KV_HEADS):
        a = acc_s[g]
        den = a[ATTN_HEAD_DIM:ATTN_HEAD_DIM + 1, :]
        top = jnp.max(jnp.abs(a[0:ATTN_HEAD_DIM, :]), axis=0, keepdims=True)
        n_ok = n_ok + ((den > DEN_LO) & (den < DEN_HI) & (top < F32_HUGE)).astype(i32)

    rare = (any_lane(tied) + 2 * any_lane((n_ok < ATTN_KV_HEADS).astype(i32)))[0, 0]
    pl.when(rare % 2 == 1)(break_ties)

    @pl.when(rare >= 2)
    def _exact():
        for g in range(ATTN_KV_HEADS):
            m_s[g] = jnp.full((1, gw), NEG, F32)
            acc_s[g] = jnp.zeros((V_ROWS, gw), F32)

        def body(c, carry):
            s0 = chunk_start(c)
            pen = pen_of(s0, True)
            kblk = k_ref[0, pl.ds(s0, ch), :]
            for g in range(ATTN_KV_HEADS):
                s = jnp.dot(kblk[:, g * ATTN_HEAD_DIM:(g + 1) * ATTN_HEAD_DIM], qcat[g],
                            preferred_element_type=F32) + near_bias(c, g) + pen
                m_old = m_s[g]
                m_new = jnp.maximum(m_old, jnp.max(s, axis=0, keepdims=True))
                p = jnp.exp2(s - m_new).astype(BF16)
                acc_s[g] = acc_s[g] * jnp.exp2(m_old - m_new) + jnp.dot(
                    vt_chunk(c, g), p, preferred_element_type=F32)
                m_s[g] = m_new
            return carry
        lax.fori_loop(0, nch_b, body, 0)

    parts = []
    for g in range(ATTN_KV_HEADS):
        a = acc_s[g]
        o = a[0:ATTN_HEAD_DIM, :] * (1.0 / a[ATTN_HEAD_DIM:ATTN_HEAD_DIM + 1, :])
        parts += [o[:, r * tq:(r + 1) * tq] for r in range(rep)]
    out_t = jnp.concatenate(parts, axis=0)
    o_ref[0] = _rms(jnp.transpose(out_t), nw_ref[...]).astype(BF16)


def _dsa(rel_bias, q, qi, misc, k, vt, miscb, nw):
    bsz, t, _ = k.shape
    nq = t // BLK
    topk = min(TOPK_MAX, t // 4)
    idx_bits = max(1, (t - 1).bit_length())
    gw = (ATTN_HEADS // ATTN_KV_HEADS) * BLK
    bkt = jnp.asarray(_bucket_tiles())
    kern = functools.partial(_dsa_kernel, topk=topk, idx_bits=idx_bits, nq=nq)

    def nxt(b, i):
        return (b, jnp.minimum(i + 1, nq - 1), 0)

    return pl.pallas_call(
        kern,
        grid=(bsz, nq),
        in_specs=[pl.BlockSpec(memory_space=pltpu.SMEM),
                  pl.BlockSpec((2, BLK, BLK), lambda b, i: (0, 0, 0)),
                  pl.BlockSpec((1, 1, ATTN_KV_HEADS, ATTN_HEAD_DIM, gw), lambda b, i: (b, i, 0, 0, 0)),
                  pl.BlockSpec((1, 1, IDX_DIM, IDX_HEADS * BLK), lambda b, i: (b, i, 0, 0)),
                  pl.BlockSpec((1, BLK, MISC_W), lambda b, i: (b, i, 0)),
                  pl.BlockSpec((1, 1, IDX_DIM, IDX_HEADS * BLK), lambda b, i: nxt(b, i) + (0,)),
                  pl.BlockSpec((1, BLK, MISC_W), nxt),
                  pl.BlockSpec((1, t, D_KV), lambda b, i: (b, 0, 0)),
                  pl.BlockSpec((1, nq, ATTN_KV_HEADS, V_ROWS, BLK), lambda b, i: (b, 0, 0, 0, 0)),
                  pl.BlockSpec((1, t, MISC_W), lambda b, i: (b, 0, 0)),
                  pl.BlockSpec((1, D_ATTN), lambda b, i: (0, 0))],
        out_specs=pl.BlockSpec((1, BLK, D_ATTN), lambda b, i: (b, i, 0)),
        out_shape=jax.ShapeDtypeStruct((bsz, t, D_ATTN), BF16),
        scratch_shapes=[pltpu.VMEM((2, t, BLK), jnp.int32),
                        pltpu.VMEM((32, t // 32, BLK), jnp.int32),
                        pltpu.VMEM((t // 32, BLK), jnp.int32),
                        pltpu.VMEM((2, 1, BLK), jnp.int32),
                        pltpu.VMEM((2, 1, BLK), jnp.int32),
                        pltpu.VMEM((3, ATTN_KV_HEADS, BLK, gw), F32),
                        pltpu.VMEM((ATTN_KV_HEADS, ATTN_HEAD_DIM, gw), BF16),
                        pltpu.VMEM((IDX_DIM, IDX_HEADS * BLK), BF16),
                        pltpu.VMEM((8, IDX_HEADS * BLK), F32),
                        pltpu.VMEM((ATTN_KV_HEADS, 1, gw), F32),
                        pltpu.VMEM((ATTN_KV_HEADS, V_ROWS, gw), F32),
                        pltpu.VMEM((ZERO_SLOT + 1, ATTN_KV_HEADS, DSA_CHUNK, gw), BF16)],
        compiler_params=pltpu.CompilerParams(
            dimension_semantics=("arbitrary", "arbitrary"), vmem_limit_bytes=VMEM_LIMIT),
        name="dsa",
    )(rel_bias, bkt, q, qi, misc, qi, misc, k, vt, miscb, nw)


def _tail_kernel(x_ref, ys_ref, ya_ref, mod_ref, wo_ref, n1_ref, n2_ref, n3_ref, w1_ref, w2_ref, o_ref):
    x = x_ref[0]
    mix = (jnp.dot(ys_ref[0], wo_ref[0:D_SSM, :], preferred_element_type=F32)
           + jnp.dot(ya_ref[0], wo_ref[D_SSM:, :], preferred_element_type=F32))
    x1 = x + mod_ref[0, 2:3, :] * _rms(mix, n1_ref[...])
    h = (_rms(x1, n2_ref[...]) * (1.0 + mod_ref[0, 4:5, :]) + mod_ref[0, 3:4, :]).astype(BF16)
    f = jnp.zeros(x.shape, F32)
    step = D_MODEL
    for c0 in range(0, D_FF, step):
        a = jnp.maximum(jnp.dot(h, w1_ref[:, c0:c0 + step], preferred_element_type=F32), 0.0)
        f = f + jnp.dot((a * a).astype(BF16), w2_ref[c0:c0 + step, :], preferred_element_type=F32)
    o_ref[0] = x1 + mod_ref[0, 5:6, :] * _rms(f, n3_ref[...])


def _tail(x, ys, ya, mod, wo, n1, n2, n3, w1, w2):
    bsz, t, d = x.shape
    tm = min(ROW_TILE, t)

    def const(shape):
        return pl.BlockSpec(shape, lambda b, i: tuple(0 for _ in shape))

    return pl.pallas_call(
        _tail_kernel,
        grid=(bsz, t // tm),
        in_specs=[pl.BlockSpec((1, tm, d), lambda b, i: (b, i, 0)),
                  pl.BlockSpec((1, tm, D_SSM), lambda b, i: (b, i, 0)),
                  pl.BlockSpec((1, tm, D_ATTN), lambda b, i: (b, i, 0)),
                  pl.BlockSpec((1, 6, d), lambda b, i: (b, 0, 0)),
                  const((D_SSM + D_ATTN, d)), const((1, d)), const((1, d)), const((1, d)),
                  const((d, D_FF)), const((D_FF, d))],
        out_specs=pl.BlockSpec((1, tm, d), lambda b, i: (b, i, 0)),
        out_shape=jax.ShapeDtypeStruct((bsz, t, d), F32),
        compiler_params=pltpu.CompilerParams(
            dimension_semantics=("arbitrary", "arbitrary"), vmem_limit_bytes=VMEM_LIMIT_TAIL),
        name="tail",
    )(x, ys, ya, mod, wo, n1, n2, n3, w1, w2)


def _permute_w_in(w):
    sizes = [D_SSM, D_XBC, SSM_HEADS, D_ATTN, D_KV, D_KV, IDX_HEADS * IDX_DIM, IDX_DIM, IDX_HEADS]
    pts = np.cumsum(sizes)[:-1].tolist()
    z, xbc, dt, q, k, v, qidx, kidx, widx = jnp.split(w, pts, axis=-1)
    pad = jnp.zeros((w.shape[0], MISC_W - IDX_DIM - IDX_HEADS - SSM_HEADS), w.dtype)
    return jnp.concatenate([z, xbc, q, qidx, k, v, kidx, widx, dt, pad], axis=-1).astype(BF16)


def _layer(x, c_pad, w_ada, b_ada, norm1_pre, norm1_post, w_in, conv_w, conv_b, dt_bias, a_log, d_skip,
           ssm_norm, rel_bias, attn_norm, w_out, norm2_pre, norm2_post, w_mlp_in, w_mlp_out):
    bsz, t, d = x.shape
    mod = _ada(c_pad, w_ada, b_ada[None, :])[:bsz].reshape(bsz, 6, d)
    z, xbc, q, k, vt, qi, misc, miscb = _inproj(x, mod, norm1_pre[None, :], _permute_w_in(w_in))

    lane = jnp.arange(MISC_W)
    is_dt = (lane >= MISC_DT) & (lane < MISC_DT + SSM_HEADS)
    head_of = jnp.clip(lane - MISC_DT, 0, SSM_HEADS - 1)
    dtb_t = jnp.where(is_dt, dt_bias[head_of], 0.0)[None, :]
    a_t = jnp.where(is_dt, (-jnp.exp(a_log.astype(F32)) * LOG2E)[head_of], 0.0)[None, :]
    expand = ((lane[:, None] - MISC_DT) == (jnp.arange(D_SSM)[None, :] // SSM_HEAD_DIM)).astype(BF16)
    dskip_e = jnp.repeat(d_skip.astype(F32), SSM_HEAD_DIM)[None, :]
    y_ssd = _ssd(xbc, z, misc, conv_w, conv_b[None, :], dtb_t, a_t, expand, dskip_e, ssm_norm[None, :])

    y_att = _dsa(rel_bias, q, qi, misc, k, vt, miscb, attn_norm[None, :])

    return _tail(x, y_ssd, y_att, mod, w_out.astype(BF16), norm1_post[None, :], norm2_pre[None, :],
                 norm2_post[None, :], w_mlp_in.astype(BF16), w_mlp_out.astype(BF16))


def kernel(x, c, w_ada, b_ada, norm1_pre, norm1_post, w_in, conv_w, conv_b, dt_bias, a_log, d_skip,
           ssm_norm, rel_bias, attn_norm, w_out, norm2_pre, norm2_post, w_mlp_in, w_mlp_out):
    bsz = x.shape[0]
    assert bsz <= 8 and x.shape[1] % DSA_CHUNK == 0 and x.shape[2] == D_MODEL
    c_pad = jnp.zeros((8, D_MODEL), F32).at[:bsz].set(c)
    for l in range(w_ada.shape[0]):
        x = _layer(x, c_pad, w_ada[l], b_ada[l], norm1_pre[l], norm1_post[l], w_in[l], conv_w[l], conv_b[l],
                   dt_bias[l], a_log[l], d_skip[l], ssm_norm[l], rel_bias, attn_norm[l], w_out[l],
                   norm2_pre[l], norm2_post[l], w_mlp_in[l], w_mlp_out[l])
    return x
```

```python
import functools
import math

import numpy as np
import jax
import jax.numpy as jnp
from jax import lax
from jax.experimental import pallas as pl
from jax.experimental.pallas import tpu as pltpu

D_MODEL = 1024
SSM_HEADS = 8
SSM_HEAD_DIM = 64
D_SSM = SSM_HEADS * SSM_HEAD_DIM
SSM_GROUPS = 2
SSM_STATE = 128
CONV_WIDTH = 4
CHUNK = 256
ATTN_HEADS = 8
ATTN_KV_HEADS = 2
ATTN_HEAD_DIM = 64
D_ATTN = ATTN_HEADS * ATTN_HEAD_DIM
D_KV = ATTN_KV_HEADS * ATTN_HEAD_DIM
IDX_HEADS = 8
IDX_DIM = 64
TOPK_MAX = 256
N_BUCKETS = 32
MAX_DISTANCE = 128
D_FF = 4 * D_MODEL
D_BC = SSM_GROUPS * SSM_STATE
D_XBC = D_SSM + 2 * D_BC
EPS = 1e-6

LANES = 128
V7X_VMEM_BYTES = 64 * 2 ** 20
VMEM_LIMIT = V7X_VMEM_BYTES * 3 // 4
VMEM_LIMIT_TAIL = V7X_VMEM_BYTES * 15 // 16
MISC_W = LANES
MISC_KIDX = 0
MISC_WIDX = IDX_DIM
MISC_DT = IDX_DIM + IDX_HEADS
OFF_Z = 0
OFF_XBC = OFF_Z + D_SSM
OFF_Q = OFF_XBC + D_XBC
OFF_QI = OFF_Q + D_ATTN
OFF_K = OFF_QI + IDX_HEADS * IDX_DIM
OFF_V = OFF_K + D_KV
OFF_MISC = OFF_V + D_KV
D_IN_PAD = OFF_MISC + MISC_W

BLK = 128
DSA_CHUNK = 512
V_ROWS = 80
ZERO_SLOT = 2
SEL_SIZES = 4
DEN_LO = 2.0 ** -80
DEN_HI = 2.0 ** 80
F32_HUGE = 3.0e38
LOG2E = math.log2(math.e)
ROW_TILE = 512
NEG = -1e30
INT_MIN = -2 ** 31

F32 = jnp.float32
BF16 = jnp.bfloat16
HI = lax.Precision.HIGHEST
NT = (((1,), (1,)), ((), ()))


def _silu(x):
    return x / (1.0 + jnp.exp(-x))


def _rms(x, w):
    return x * lax.rsqrt(jnp.mean(x * x, axis=-1, keepdims=True) + EPS) * w


def _split3(x):
    hi = x.astype(BF16)
    rest = x - hi.astype(F32)
    mid = rest.astype(BF16)
    return hi, mid, (rest - mid.astype(F32)).astype(BF16)


def _ada_kernel(c_ref, w_ref, b_ref, o_ref):
    s = _silu(c_ref[...])
    o_ref[...] = jnp.dot(s, w_ref[...], precision=HI, preferred_element_type=F32) + b_ref[...]


def _ada(c_pad, w, b):
    n = w.shape[1]
    tn = D_MODEL
    return pl.pallas_call(
        _ada_kernel,
        grid=(n // tn,),
        in_specs=[pl.BlockSpec((8, D_MODEL), lambda j: (0, 0)),
                  pl.BlockSpec((D_MODEL, tn), lambda j: (0, j)),
                  pl.BlockSpec((1, tn), lambda j: (0, j))],
        out_specs=pl.BlockSpec((8, tn), lambda j: (0, j)),
        out_shape=jax.ShapeDtypeStruct((8, n), F32),
        name="ada",
    )(c_pad, w, b)


def _inproj_kernel(x_ref, mod_ref, nw_ref, w_ref,
                   z_ref, xbc_ref, q_ref, k_ref, vt_ref, qi_ref, misc_ref, miscb_ref):
    x = x_ref[0]
    h = _rms(x, nw_ref[...]) * (1.0 + mod_ref[0, 1:2, :]) + mod_ref[0, 0:1, :]
    hb = h.astype(BF16)

    def seg(lo, width):
        return jnp.dot(hb, w_ref[:, lo:lo + width], preferred_element_type=F32)

    z_ref[0] = seg(OFF_Z, D_SSM)
    xbc_ref[0] = seg(OFF_XBC, D_XBC)
    rep = ATTN_HEADS // ATTN_KV_HEADS
    q_all = seg(OFF_Q, D_ATTN) * (ATTN_HEAD_DIM ** -0.5 * LOG2E)
    qi_all = seg(OFF_QI, IDX_HEADS * IDX_DIM)
    for j in range(x.shape[0] // BLK):
        q_t = jnp.transpose(q_all[j * BLK:(j + 1) * BLK, :])
        qi_t = jnp.transpose(qi_all[j * BLK:(j + 1) * BLK, :])
        for h in range(ATTN_HEADS):
            q_ref[0, j, h // rep, :, (h % rep) * BLK:(h % rep + 1) * BLK] = (
                q_t[h * ATTN_HEAD_DIM:(h + 1) * ATTN_HEAD_DIM, :].astype(BF16))
        for h in range(IDX_HEADS):
            qi_ref[0, j, :, h * BLK:(h + 1) * BLK] = qi_t[h * IDX_DIM:(h + 1) * IDX_DIM, :].astype(BF16)
    k_v_misc = seg(OFF_K, 2 * D_KV + MISC_W)
    k_ref[0] = k_v_misc[:, :D_KV].astype(BF16)
    v_misc = k_v_misc[:, D_KV:]
    ones_tile = jnp.where(lax.broadcasted_iota(jnp.int32, (V_ROWS - ATTN_HEAD_DIM, BLK), 0) == 0, 1.0, 0.0)
    for j in range(x.shape[0] // BLK):
        v_t = jnp.transpose(v_misc[j * BLK:(j + 1) * BLK, :D_KV])
        for g in range(ATTN_KV_HEADS):
            vt_ref[0, j, g, 0:ATTN_HEAD_DIM, :] = v_t[g * ATTN_HEAD_DIM:(g + 1) * ATTN_HEAD_DIM, :].astype(BF16)
            vt_ref[0, j, g, ATTN_HEAD_DIM:, :] = ones_tile.astype(BF16)
    misc = v_misc[:, D_KV:]
    misc_ref[0] = misc
    miscb_ref[0] = misc.astype(BF16)


def _inproj(x, mod, nw, w_perm):
    bsz, t, d = x.shape
    tm = min(ROW_TILE, t)

    def tok(width, dtype):
        return (pl.BlockSpec((1, tm, width), lambda b, i: (b, i, 0)),
                jax.ShapeDtypeStruct((bsz, t, width), dtype))

    vt = (pl.BlockSpec((1, tm // BLK, ATTN_KV_HEADS, V_ROWS, BLK), lambda b, i: (b, i, 0, 0, 0)),
          jax.ShapeDtypeStruct((bsz, t // BLK, ATTN_KV_HEADS, V_ROWS, BLK), BF16))
    gw = (ATTN_HEADS // ATTN_KV_HEADS) * BLK
    qt = (pl.BlockSpec((1, tm // BLK, ATTN_KV_HEADS, ATTN_HEAD_DIM, gw), lambda b, i: (b, i, 0, 0, 0)),
          jax.ShapeDtypeStruct((bsz, t // BLK, ATTN_KV_HEADS, ATTN_HEAD_DIM, gw), BF16))
    qit = (pl.BlockSpec((1, tm // BLK, IDX_DIM, IDX_HEADS * BLK), lambda b, i: (b, i, 0, 0)),
           jax.ShapeDtypeStruct((bsz, t // BLK, IDX_DIM, IDX_HEADS * BLK), BF16))
    outs = [tok(D_SSM, F32), tok(D_XBC, F32), qt, tok(D_KV, BF16), vt, qit,
            tok(MISC_W, F32), tok(MISC_W, BF16)]
    return pl.pallas_call(
        _inproj_kernel,
        grid=(bsz, t // tm),
        in_specs=[pl.BlockSpec((1, tm, d), lambda b, i: (b, i, 0)),
                  pl.BlockSpec((1, 6, d), lambda b, i: (b, 0, 0)),
                  pl.BlockSpec((1, d), lambda b, i: (0, 0)),
                  pl.BlockSpec((d, D_IN_PAD), lambda b, i: (0, 0))],
        out_specs=[o[0] for o in outs],
        out_shape=[o[1] for o in outs],
        compiler_params=pltpu.CompilerParams(
            dimension_semantics=("arbitrary", "arbitrary"), vmem_limit_bytes=VMEM_LIMIT),
        name="inproj",
    )(x, mod, nw, w_perm)


def _ssd_kernel(xbc_ref, z_ref, misc_ref, cw_ref, cb_ref, dtb_ref, a_ref, e_ref, dskip_ref, nw_ref,
                o_ref, xbuf, state):
    c = pl.program_id(1)
    lc = xbc_ref.shape[1]

    @pl.when(c == 0)
    def _():
        xbuf[0:8, :] = jnp.zeros((8, D_XBC), F32)
        state[...] = jnp.zeros(state.shape, F32)

    @pl.when(c > 0)
    def _():
        xbuf[0:8, :] = xbuf[lc:lc + 8, :]

    xbuf[8:lc + 8, :] = xbc_ref[0]
    u = cb_ref[...]
    xall = xbuf[...]
    for k in range(CONV_WIDTH):
        back = CONV_WIDTH - 1 - k
        xk = xall if back == 0 else pltpu.roll(xall, back, 0)
        u = u + xk[8:8 + lc] * cw_ref[k:k + 1, :]
    u = _silu(u)
    xs = u[:, :D_SSM]
    bm = u[:, D_SSM:D_SSM + D_BC]
    cm = u[:, D_SSM + D_BC:]

    raw = misc_ref[0] + dtb_ref[...]
    dt_t = jnp.maximum(raw, 0.0) + jnp.log1p(jnp.exp(-jnp.abs(raw)))
    adt_t = dt_t * a_ref[...]
    row = lax.broadcasted_iota(jnp.int32, (lc, lc), 0)
    col = lax.broadcasted_iota(jnp.int32, (lc, lc), 1)
    causal = col <= row
    tril = jnp.where(causal, 1.0, 0.0).astype(BF16)
    expand = e_ref[...]
    acs_t = sum(jnp.dot(tril, piece, preferred_element_type=F32) for piece in _split3(adt_t))
    dt_e = sum(jnp.dot(piece, expand, preferred_element_type=F32) for piece in _split3(dt_t))
    acs_e = sum(jnp.dot(piece, expand, preferred_element_type=F32) for piece in _split3(acs_t))
    acs_row = jnp.transpose(acs_t)

    xdt = xs * dt_e
    a_last = acs_e[lc - 1:lc, :]
    xdec = (xdt * jnp.exp2(a_last - acs_e)).astype(BF16)
    xdt_b = xdt.astype(BF16)
    bm_b = bm.astype(BF16)
    cm_b = cm.astype(BF16)
    bm_t = jnp.transpose(bm).astype(BF16)

    heads_per_group = SSM_HEADS // SSM_GROUPS
    gw = heads_per_group * SSM_HEAD_DIM
    y_parts = []
    off_parts = []
    for g in range(SSM_GROUPS):
        bg = bm_b[:, g * SSM_STATE:(g + 1) * SSM_STATE]
        cg = cm_b[:, g * SSM_STATE:(g + 1) * SSM_STATE]
        cb = lax.dot_general(cg, bg, NT, preferred_element_type=F32)
        for r in range(heads_per_group):
            hd = g * heads_per_group + r
            a_col = acs_t[:, MISC_DT + hd:MISC_DT + hd + 1]
            a_row = acs_row[MISC_DT + hd:MISC_DT + hd + 1, :]
            lmat = jnp.exp2(jnp.where(causal, a_col - a_row, -jnp.inf))
            mh = (cb * lmat).astype(BF16)
            y_parts.append(jnp.dot(mh, xdt_b[:, hd * SSM_HEAD_DIM:(hd + 1) * SSM_HEAD_DIM],
                                   preferred_element_type=F32))
        prev = state[g]
        off_parts.append(jnp.dot(cg, prev.astype(BF16), preferred_element_type=F32))
        st = jnp.dot(bm_t[g * SSM_STATE:(g + 1) * SSM_STATE, :], xdec[:, g * gw:(g + 1) * gw],
                     preferred_element_type=F32)
        state[g] = prev * jnp.exp2(a_last[:, g * gw:(g + 1) * gw]) + st
    y = (jnp.concatenate(y_parts, axis=1)
         + jnp.concatenate(off_parts, axis=1) * jnp.exp2(acs_e)
         + xs * dskip_ref[...])
    o_ref[0] = _rms(y * _silu(z_ref[0]), nw_ref[...]).astype(BF16)


def _ssd(xbc, z, misc, cw, cb, dtb_t, a_t, expand, dskip_e, nw):
    bsz, t, _ = xbc.shape
    lc = math.gcd(t, CHUNK)
    nc = t // lc
    gw = (SSM_HEADS // SSM_GROUPS) * SSM_HEAD_DIM

    def const(shape):
        return pl.BlockSpec(shape, lambda b, c: tuple(0 for _ in shape))

    return pl.pallas_call(
        _ssd_kernel,
        grid=(bsz, nc),
        in_specs=[pl.BlockSpec((1, lc, D_XBC), lambda b, c: (b, c, 0)),
                  pl.BlockSpec((1, lc, D_SSM), lambda b, c: (b, c, 0)),
                  pl.BlockSpec((1, lc, MISC_W), lambda b, c: (b, c, 0)),
                  const((CONV_WIDTH, D_XBC)), const((1, D_XBC)), const((1, MISC_W)), const((1, MISC_W)),
                  const((MISC_W, D_SSM)), const((1, D_SSM)), const((1, D_SSM))],
        out_specs=pl.BlockSpec((1, lc, D_SSM), lambda b, c: (b, c, 0)),
        out_shape=jax.ShapeDtypeStruct((bsz, t, D_SSM), BF16),
        scratch_shapes=[pltpu.VMEM((lc + 8, D_XBC), F32),
                        pltpu.VMEM((SSM_GROUPS, SSM_STATE, gw), F32)],
        compiler_params=pltpu.CompilerParams(
            dimension_semantics=("arbitrary", "arbitrary"), vmem_limit_bytes=VMEM_LIMIT),
        name="ssd",
    )(xbc, z, misc, cw, cb, dtb_t, a_t, expand, dskip_e, nw)


def _bucket_tiles():
    s = np.arange(BLK)[:, None]
    t = np.arange(BLK)[None, :]
    max_exact = N_BUCKETS // 2
    tiles = []
    for d0 in (0, BLK):
        n = np.maximum(d0 + t - s, 0)
        large = {}
        for dt_ in (np.float32, np.float64):
            nf = np.maximum(n, 1).astype(dt_)
            lg = max_exact + (np.log(nf / dt_(max_exact)) / dt_(math.log(MAX_DISTANCE / max_exact))
                              * dt_(N_BUCKETS - max_exact)).astype(np.int32)
            large[dt_] = np.minimum(lg, N_BUCKETS - 1)
        assert (large[np.float32] == large[np.float64]).all()
        tiles.append(np.where(n < max_exact, n, large[np.float64]).astype(np.int32))
    assert BLK + 1 >= MAX_DISTANCE
    return np.stack(tiles)


def _to_key(x):
    bits = lax.bitcast_convert_type(x, jnp.int32)
    key = bits ^ (lax.shift_right_arithmetic(bits, 31) & jnp.int32(0x7FFFFFFF))
    return jnp.where(key == -1, 0, key)


def _transpose32(rows):
    rows = list(rows)
    mask, j = 0x0000FFFF, 16
    while j:
        k = 0
        while k < 32:
            t = (rows[k] ^ lax.shift_right_logical(rows[k + j], jnp.int32(j))) & jnp.int32(mask)
            rows[k] = rows[k] ^ t
            rows[k + j] = rows[k + j] ^ lax.shift_left(t, jnp.int32(j))
            k = (k + j + 1) & ~j
        j >>= 1
        mask ^= (mask << j) & 0xFFFFFFFF
    return rows


def _dsa_kernel(relb_ref, bkt_ref, q_ref, qic_ref, miscc_ref, qin_ref, miscn_ref, k_ref, vt_ref, kidx_ref,
                nw_ref, o_ref, keys, planes, live, thr_s, cnt_s, bias_s, qcat, qicat, wcat, m_s, acc_s, p_s,
                *, topk, idx_bits, nq):
    b = pl.program_id(0)
    qi = pl.program_id(1)
    tq = BLK
    i32 = jnp.int32
    rep = ATTN_HEADS // ATTN_KV_HEADS
    gw = rep * tq
    ch = DSA_CHUNK
    ch_blks = ch // BLK
    rows_per_chunk = ch // 32

    @pl.when((b == 0) & (qi == 0))
    def _init_bias():
        for d in range(2):
            bk = bkt_ref[d]
            for h in range(ATTN_HEADS):
                far = relb_ref[N_BUCKETS - 1, h]
                acc = jnp.zeros((BLK, tq), F32)
                for n in range(N_BUCKETS - 1):
                    acc = jnp.where(bk == n, relb_ref[n, h] - far, acc)
                bias_s[d, h // rep, :, (h % rep) * BLK:(h % rep + 1) * BLK] = acc * LOG2E
        bias_s[2] = jnp.zeros((ATTN_KV_HEADS, BLK, gw), F32)
        for w in range(32):
            planes[w] = jnp.zeros(planes.shape[1:], i32)
        p_s[ZERO_SLOT] = jnp.zeros(p_s.shape[1:], BF16)

    row_c = lax.broadcasted_iota(i32, (ch, tq), 0)
    lane_c = lax.broadcasted_iota(i32, (ch, tq), 1)

    def chunk_start(c):
        return pl.multiple_of(c * ch, ch)

    def n_chunks(blk):
        return (blk + ch_blks) // ch_blks

    def load_indexer(qidx_ref, misc_ref):
        qicat[...] = qidx_ref[0, 0]
        w_t = jnp.transpose(misc_ref[0])[MISC_WIDX:MISC_WIDX + IDX_HEADS, :] * (
            IDX_HEADS ** -0.5 * IDX_DIM ** -0.5)
        for h in range(IDX_HEADS):
            wcat[:, h * BLK:(h + 1) * BLK] = jnp.broadcast_to(w_t[h:h + 1, :], (8, BLK))

    def score_chunk(c, blk, slot):
        s0 = chunk_start(c)
        kb = kidx_ref[0, pl.ds(s0, ch), MISC_KIDX:MISC_KIDX + IDX_DIM]
        s = jnp.dot(kb, qicat[...], preferred_element_type=F32)
        r = jnp.maximum(s, 0.0) * wcat[0:1, :]
        acc = r[:, 0:BLK]
        for h in range(1, IDX_HEADS):
            acc = acc + r[:, h * BLK:(h + 1) * BLK]
        acc = jnp.where(s0 + row_c <= blk * BLK + lane_c, acc, -jnp.inf)
        keys[slot, pl.ds(s0, ch), :] = _to_key(acc)

    def plane_chunk(c, slot):
        ku = (keys[slot, pl.ds(chunk_start(c), ch), :] ^ i32(INT_MIN)).reshape(ch // (32 * 8), 32, 8, tq)
        words = _transpose32([ku[:, j] for j in range(32)])
        r0 = pl.multiple_of(c * rows_per_chunk, rows_per_chunk)
        for w in range(32):
            planes[w, pl.ds(r0, rows_per_chunk), :] = words[w].reshape(rows_per_chunk, tq)

    def select(blk, slot):
        nch = n_chunks(blk)
        used = nch * rows_per_chunk

        def ones_in(words):
            return jnp.sum(lax.population_count(words), axis=0, keepdims=True)

        def passes(nr):
            live[0:nr] = jnp.where(lax.broadcasted_iota(i32, (nr, tq), 0) < used, i32(-1), i32(0))

            def bit_body(w, st):
                thr_u, n_above, n_set = st
                take = n_above + n_set >= topk
                thr_u = jnp.where(take, thr_u | lax.shift_left(i32(1), i32(31) - w), thr_u)
                n_above = jnp.where(take, n_above, n_above + n_set)
                still = live[0:nr] & (planes[w, 0:nr] ^ jnp.where(take, i32(0), i32(-1)))
                live[0:nr] = still
                return thr_u, n_above, ones_in(still & planes[jnp.minimum(w + 1, 31), 0:nr])

            zero = jnp.zeros((1, tq), i32)
            thr_u, n_above, _ = lax.fori_loop(
                0, 32, bit_body, (zero, zero, ones_in(live[0:nr] & planes[0, 0:nr])))
            thr_s[slot] = thr_u ^ i32(INT_MIN)
            cnt_s[0] = n_above
            cnt_s[1] = ones_in(live[0:nr])

        size_step = live.shape[0] // SEL_SIZES
        for v in range(SEL_SIZES):
            pl.when((used > v * size_step) & (used <= (v + 1) * size_step))(
                functools.partial(passes, (v + 1) * size_step))
        thr = thr_s[slot]
        n_gt = cnt_s[0]
        n_eq = cnt_s[1]
        need = topk - n_gt

        def break_ties():
            def count(pred):
                def body(c, cnt):
                    s0 = chunk_start(c)
                    hit = pred(keys[slot, pl.ds(s0, ch), :], s0).astype(i32)
                    n = ch
                    while n > 8:
                        n //= 2
                        hit = hit[:n] + hit[n:2 * n]
                    return cnt + hit
                cnt = lax.fori_loop(0, nch, body, jnp.zeros((8, tq), i32))
                return jnp.sum(cnt, axis=0, keepdims=True)

            def jbody(it, p):
                cand = p + lax.shift_left(i32(1), i32(idx_bits - 1) - it)
                c = count(lambda kk, s0: (kk == thr) & (s0 + row_c < cand))
                return jnp.where(c < need, cand, p)
            last = lax.fori_loop(0, idx_bits, jbody, jnp.zeros((1, tq), i32))

            def demote(c, carry):
                s0 = chunk_start(c)
                kk = keys[slot, pl.ds(s0, ch), :]
                keys[slot, pl.ds(s0, ch), :] = jnp.where((kk == thr) & (s0 + row_c > last), INT_MIN, kk)
                return carry
            lax.fori_loop(0, nch, demote, 0)

        return (n_eq > need).astype(i32), break_ties

    def any_lane(flags):
        return jnp.max(flags, axis=1, keepdims=True)

    @pl.when(qi == 0)
    def _first_block():
        load_indexer(qic_ref, miscc_ref)
        score_chunk(0, 0, 0)
        plane_chunk(0, 0)
        tied, break_ties = select(0, 0)
        pl.when(any_lane(tied)[0, 0] > 0)(break_ties)

    slot_b = qi % 2
    slot_a = 1 - slot_b
    blk_a = jnp.minimum(qi + 1, nq - 1)
    nch_b = n_chunks(qi)
    nch_a = n_chunks(blk_a)
    load_indexer(qin_ref, miscn_ref)
    thr = thr_s[slot_b]

    for g in range(ATTN_KV_HEADS):
        qcat[g] = q_ref[0, 0, g]

    def pen_of(s0, causal):
        sel = keys[slot_b, pl.ds(s0, ch), :] >= thr
        if causal:
            sel = sel & (s0 + row_c <= qi * BLK + lane_c)
        pen = jnp.where(sel, 0.0, NEG)
        return jnp.concatenate([pen] * rep, axis=1)

    def near_bias(c, g):
        return jnp.concatenate(
            [bias_s[jnp.clip(qi - (c * ch_blks + u), 0, 2), g] for u in range(ch_blks)], axis=0)

    def vt_chunk(c, g):
        return jnp.concatenate([vt_ref[0, c * ch_blks + u, g] for u in range(ch_blks)], axis=1)

    for g in range(ATTN_KV_HEADS):
        acc_s[g] = jnp.zeros((V_ROWS, gw), F32)

    def weigh_values(c, slot):
        for g in range(ATTN_KV_HEADS):
            acc_s[g] += jnp.dot(vt_chunk(c, g), p_s[slot, g], preferred_element_type=F32)

    def att_chunk(c, near, slots=None):
        prev_slot, cur_slot = slots if slots is not None else ((c + 1) % 2, c % 2)
        if not (isinstance(cur_slot, int) and cur_slot == 1):
            prev_slot = jnp.where(c == 0, ZERO_SLOT, prev_slot)
        weigh_values(jnp.maximum(c - 1, 0), prev_slot)
        s0 = chunk_start(c)
        pen = pen_of(s0, near)
        kblk = k_ref[0, pl.ds(s0, ch), :]
        for g in range(ATTN_KV_HEADS):
            s = jnp.dot(kblk[:, g * ATTN_HEAD_DIM:(g + 1) * ATTN_HEAD_DIM], qcat[g],
                        preferred_element_type=F32)
            if near:
                s = s + near_bias(c, g)
            p_s[cur_slot, g] = jnp.exp2(s + pen).astype(BF16)

    def fused(c, near, slots=None):
        plane_chunk(c, slot_a)
        att_chunk(c, near, slots)
        score_chunk(c + 1, blk_a, slot_a)

    def far_pair(j, carry):
        fused(2 * j, False, (1, 0))
        fused(2 * j + 1, False, (0, 1))
        return carry

    def far_quad(j, carry):
        far_pair(2 * j, carry)
        return far_pair(2 * j + 1, carry)

    def far_body(c, carry):
        fused(c, False, (1, 0))
        return carry

    def near_body(c, carry):
        fused(c, True)
        return carry

    score_chunk(0, blk_a, slot_a)
    n_far = jnp.maximum(qi - 1, 0) // ch_blks
    lax.fori_loop(0, n_far // 4, far_quad, 0)
    lax.fori_loop((n_far // 4) * 2, n_far // 2, far_pair, 0)
    lax.fori_loop((n_far // 2) * 2, n_far, far_body, 0)
    lax.fori_loop(n_far, nch_b - 1, near_body, 0)
    @pl.when(nch_a > nch_b)
    def _one_more_chunk():
        score_chunk(nch_b, blk_a, slot_a)
        plane_chunk(nch_b, slot_a)

    plane_chunk(nch_b - 1, slot_a)
    att_chunk(nch_b - 1, True)
    weigh_values(nch_b - 1, (nch_b - 1) % 2)
    tied, break_ties = select(blk_a, slot_a)

    n_ok = jnp.zeros((1, gw), i32)
    for g in range(ATTN_KV_HEADS):
        a = acc_s[g]
        den = a[ATTN_HEAD_DIM:ATTN_HEAD_DIM + 1, :]
        top = jnp.max(jnp.abs(a[0:ATTN_HEAD_DIM, :]), axis=0, keepdims=True)
        n_ok = n_ok + ((den > DEN_LO) & (den < DEN_HI) & (top < F32_HUGE)).astype(i32)

    rare = (any_lane(tied) + 2 * any_lane((n_ok < ATTN_KV_HEADS).astype(i32)))[0, 0]
    pl.when(rare % 2 == 1)(break_ties)

    @pl.when(rare >= 2)
    def _exact():
        for g in range(ATTN_KV_HEADS):
            m_s[g] = jnp.full((1, gw), NEG, F32)
            acc_s[g] = jnp.zeros((V_ROWS, gw), F32)

        def body(c, carry):
            s0 = chunk_start(c)
            pen = pen_of(s0, True)
            kblk = k_ref[0, pl.ds(s0, ch), :]
            for g in range(ATTN_KV_HEADS):
                s = jnp.dot(kblk[:, g * ATTN_HEAD_DIM:(g + 1) * ATTN_HEAD_DIM], qcat[g],
                            preferred_element_type=F32) + near_bias(c, g) + pen
                m_old = m_s[g]
                m_new = jnp.maximum(m_old, jnp.max(s, axis=0, keepdims=True))
                p = jnp.exp2(s - m_new).astype(BF16)
                acc_s[g] = acc_s[g] * jnp.exp2(m_old - m_new) + jnp.dot(
                    vt_chunk(c, g), p, preferred_element_type=F32)
                m_s[g] = m_new
            return carry
        lax.fori_loop(0, nch_b, body, 0)

    parts = []
    for g in range(ATTN_KV_HEADS):
        a = acc_s[g]
        o = a[0:ATTN_HEAD_DIM, :] * (1.0 / a[ATTN_HEAD_DIM:ATTN_HEAD_DIM + 1, :])
        parts += [o[:, r * tq:(r + 1) * tq] for r in range(rep)]
    out_t = jnp.concatenate(parts, axis=0)
    inv = lax.rsqrt(jnp.mean(out_t * out_t, axis=0, keepdims=True) + EPS)
    o_ref[0, 0] = (out_t * inv * nw_ref[...]).astype(BF16)


def _dsa(rel_bias, q, qi, misc, k, vt, miscb, nw):
    bsz, t, _ = k.shape
    nq = t // BLK
    topk = min(TOPK_MAX, t // 4)
    idx_bits = max(1, (t - 1).bit_length())
    gw = (ATTN_HEADS // ATTN_KV_HEADS) * BLK
    bkt = jnp.asarray(_bucket_tiles())
    kern = functools.partial(_dsa_kernel, topk=topk, idx_bits=idx_bits, nq=nq)

    def nxt(b, i):
        return (b, jnp.minimum(i + 1, nq - 1), 0)

    return pl.pallas_call(
        kern,
        grid=(bsz, nq),
        in_specs=[pl.BlockSpec(memory_space=pltpu.SMEM),
                  pl.BlockSpec((2, BLK, BLK), lambda b, i: (0, 0, 0)),
                  pl.BlockSpec((1, 1, ATTN_KV_HEADS, ATTN_HEAD_DIM, gw), lambda b, i: (b, i, 0, 0, 0)),
                  pl.BlockSpec((1, 1, IDX_DIM, IDX_HEADS * BLK), lambda b, i: (b, i, 0, 0)),
                  pl.BlockSpec((1, BLK, MISC_W), lambda b, i: (b, i, 0)),
                  pl.BlockSpec((1, 1, IDX_DIM, IDX_HEADS * BLK), lambda b, i: nxt(b, i) + (0,)),
                  pl.BlockSpec((1, BLK, MISC_W), nxt),
                  pl.BlockSpec((1, t, D_KV), lambda b, i: (b, 0, 0)),
                  pl.BlockSpec((1, nq, ATTN_KV_HEADS, V_ROWS, BLK), lambda b, i: (b, 0, 0, 0, 0)),
                  pl.BlockSpec((1, t, MISC_W), lambda b, i: (b, 0, 0)),
                  pl.BlockSpec((D_ATTN, BLK), lambda b, i: (0, 0))],
        out_specs=pl.BlockSpec((1, 1, D_ATTN, BLK), lambda b, i: (b, i, 0, 0)),
        out_shape=jax.ShapeDtypeStruct((bsz, nq, D_ATTN, BLK), BF16),
        scratch_shapes=[pltpu.VMEM((2, t, BLK), jnp.int32),
                        pltpu.VMEM((32, t // 32, BLK), jnp.int32),
                        pltpu.VMEM((t // 32, BLK), jnp.int32),
                        pltpu.VMEM((2, 1, BLK), jnp.int32),
                        pltpu.VMEM((2, 1, BLK), jnp.int32),
                        pltpu.VMEM((3, ATTN_KV_HEADS, BLK, gw), F32),
                        pltpu.VMEM((ATTN_KV_HEADS, ATTN_HEAD_DIM, gw), BF16),
                        pltpu.VMEM((IDX_DIM, IDX_HEADS * BLK), BF16),
                        pltpu.VMEM((8, IDX_HEADS * BLK), F32),
                        pltpu.VMEM((ATTN_KV_HEADS, 1, gw), F32),
                        pltpu.VMEM((ATTN_KV_HEADS, V_ROWS, gw), F32),
                        pltpu.VMEM((ZERO_SLOT + 1, ATTN_KV_HEADS, DSA_CHUNK, gw), BF16)],
        compiler_params=pltpu.CompilerParams(
            dimension_semantics=("arbitrary", "arbitrary"), vmem_limit_bytes=VMEM_LIMIT),
        name="dsa",
    )(rel_bias, bkt, q, qi, misc, qi, misc, k, vt, miscb, nw)


def _tail_kernel(x_ref, ys_ref, ya_ref, mod_ref, wo_ref, n1_ref, n2_ref, n3_ref, w1_ref, w2_ref, o_ref):
    x = x_ref[0]
    ya = jnp.concatenate([jnp.transpose(ya_ref[0, j].astype(F32)).astype(BF16)
                          for j in range(ya_ref.shape[1])], axis=0)
    mix = (jnp.dot(ys_ref[0], wo_ref[0:D_SSM, :], preferred_element_type=F32)
           + jnp.dot(ya, wo_ref[D_SSM:, :], preferred_element_type=F32))
    x1 = x + mod_ref[0, 2:3, :] * _rms(mix, n1_ref[...])
    h = (_rms(x1, n2_ref[...]) * (1.0 + mod_ref[0, 4:5, :]) + mod_ref[0, 3:4, :]).astype(BF16)
    f = jnp.zeros(x.shape, F32)
    step = D_MODEL
    for c0 in range(0, D_FF, step):
        a = jnp.maximum(jnp.dot(h, w1_ref[:, c0:c0 + step], preferred_element_type=F32), 0.0)
        f = f + jnp.dot((a * a).astype(BF16), w2_ref[c0:c0 + step, :], preferred_element_type=F32)
    o_ref[0] = x1 + mod_ref[0, 5:6, :] * _rms(f, n3_ref[...])


def _tail(x, ys, ya, mod, wo, n1, n2, n3, w1, w2):
    bsz, t, d = x.shape
    tm = min(ROW_TILE, t)

    def const(shape):
        return pl.BlockSpec(shape, lambda b, i: tuple(0 for _ in shape))

    return pl.pallas_call(
        _tail_kernel,
        grid=(bsz, t // tm),
        in_specs=[pl.BlockSpec((1, tm, d), lambda b, i: (b, i, 0)),
                  pl.BlockSpec((1, tm, D_SSM), lambda b, i: (b, i, 0)),
                  pl.BlockSpec((1, tm // BLK, D_ATTN, BLK), lambda b, i: (b, i, 0, 0)),
                  pl.BlockSpec((1, 6, d), lambda b, i: (b, 0, 0)),
                  const((D_SSM + D_ATTN, d)), const((1, d)), const((1, d)), const((1, d)),
                  const((d, D_FF)), const((D_FF, d))],
        out_specs=pl.BlockSpec((1, tm, d), lambda b, i: (b, i, 0)),
        out_shape=jax.ShapeDtypeStruct((bsz, t, d), F32),
        compiler_params=pltpu.CompilerParams(
            dimension_semantics=("arbitrary", "arbitrary"), vmem_limit_bytes=VMEM_LIMIT_TAIL),
        name="tail",
    )(x, ys, ya, mod, wo, n1, n2, n3, w1, w2)


def _permute_w_in(w):
    sizes = [D_SSM, D_XBC, SSM_HEADS, D_ATTN, D_KV, D_KV, IDX_HEADS * IDX_DIM, IDX_DIM, IDX_HEADS]
    pts = np.cumsum(sizes)[:-1].tolist()
    z, xbc, dt, q, k, v, qidx, kidx, widx = jnp.split(w, pts, axis=-1)
    pad = jnp.zeros((w.shape[0], MISC_W - IDX_DIM - IDX_HEADS - SSM_HEADS), w.dtype)
    return jnp.concatenate([z, xbc, q, qidx, k, v, kidx, widx, dt, pad], axis=-1).astype(BF16)


def _layer(x, c_pad, w_ada, b_ada, norm1_pre, norm1_post, w_in, conv_w, conv_b, dt_bias, a_log, d_skip,
           ssm_norm, rel_bias, attn_norm, w_out, norm2_pre, norm2_post, w_mlp_in, w_mlp_out):
    bsz, t, d = x.shape
    mod = _ada(c_pad, w_ada, b_ada[None, :])[:bsz].reshape(bsz, 6, d)
    z, xbc, q, k, vt, qi, misc, miscb = _inproj(x, mod, norm1_pre[None, :], _permute_w_in(w_in))

    lane = jnp.arange(MISC_W)
    is_dt = (lane >= MISC_DT) & (lane < MISC_DT + SSM_HEADS)
    head_of = jnp.clip(lane - MISC_DT, 0, SSM_HEADS - 1)
    dtb_t = jnp.where(is_dt, dt_bias[head_of], 0.0)[None, :]
    a_t = jnp.where(is_dt, (-jnp.exp(a_log.astype(F32)) * LOG2E)[head_of], 0.0)[None, :]
    expand = ((lane[:, None] - MISC_DT) == (jnp.arange(D_SSM)[None, :] // SSM_HEAD_DIM)).astype(BF16)
    dskip_e = jnp.repeat(d_skip.astype(F32), SSM_HEAD_DIM)[None, :]
    y_ssd = _ssd(xbc, z, misc, conv_w, conv_b[None, :], dtb_t, a_t, expand, dskip_e, ssm_norm[None, :])

    attn_gain = jnp.broadcast_to(attn_norm.astype(F32)[:, None], (D_ATTN, BLK))
    y_att = _dsa(rel_bias, q, qi, misc, k, vt, miscb, attn_gain)

    return _tail(x, y_ssd, y_att, mod, w_out.astype(BF16), norm1_post[None, :], norm2_pre[None, :],
                 norm2_post[None, :], w_mlp_in.astype(BF16), w_mlp_out.astype(BF16))


def kernel(x, c, w_ada, b_ada, norm1_pre, norm1_post, w_in, conv_w, conv_b, dt_bias, a_log, d_skip,
           ssm_norm, rel_bias, attn_norm, w_out, norm2_pre, norm2_post, w_mlp_in, w_mlp_out):
    bsz = x.shape[0]
    assert bsz <= 8 and x.shape[1] % DSA_CHUNK == 0 and x.shape[2] == D_MODEL
    c_pad = jnp.zeros((8, D_MODEL), F32).at[:bsz].set(c)
    for l in range(w_ada.shape[0]):
        x = _layer(x, c_pad, w_ada[l], b_ada[l], norm1_pre[l], norm1_post[l], w_in[l], conv_w[l], conv_b[l],
                   dt_bias[l], a_log[l], d_skip[l], ssm_norm[l], rel_bias, attn_norm[l], w_out[l],
                   norm2_pre[l], norm2_post[l], w_mlp_in[l], w_mlp_out[l])
    return x
```

```python
import functools
import math

import numpy as np
import jax
import jax.numpy as jnp
from jax import lax
from jax.experimental import pallas as pl
from jax.experimental.pallas import tpu as pltpu

D_MODEL = 1024
SSM_HEADS = 8
SSM_HEAD_DIM = 64
D_SSM = SSM_HEADS * SSM_HEAD_DIM
SSM_GROUPS = 2
SSM_STATE = 128
CONV_WIDTH = 4
CHUNK = 256
ATTN_HEADS = 8
ATTN_KV_HEADS = 2
ATTN_HEAD_DIM = 64
D_ATTN = ATTN_HEADS * ATTN_HEAD_DIM
D_KV = ATTN_KV_HEADS * ATTN_HEAD_DIM
IDX_HEADS = 8
IDX_DIM = 64
TOPK_MAX = 256
N_BUCKETS = 32
MAX_DISTANCE = 128
D_FF = 4 * D_MODEL
D_BC = SSM_GROUPS * SSM_STATE
D_XBC = D_SSM + 2 * D_BC
EPS = 1e-6

LANES = 128
V7X_VMEM_BYTES = 64 * 2 ** 20
VMEM_LIMIT = V7X_VMEM_BYTES * 3 // 4
VMEM_LIMIT_TAIL = V7X_VMEM_BYTES * 15 // 16
MISC_W = LANES
MISC_KIDX = 0
MISC_WIDX = IDX_DIM
MISC_DT = IDX_DIM + IDX_HEADS
OFF_Z = 0
OFF_XBC = OFF_Z + D_SSM
OFF_Q = OFF_XBC + D_XBC
OFF_QI = OFF_Q + D_ATTN
OFF_K = OFF_QI + IDX_HEADS * IDX_DIM
OFF_V = OFF_K + D_KV
OFF_MISC = OFF_V + D_KV
D_IN_PAD = OFF_MISC + MISC_W

BLK = 128
DSA_CHUNK = 512
V_ROWS = 80
ZERO_SLOT = 2
SEL_SIZES = 8
DEN_LO = 2.0 ** -80
DEN_HI = 2.0 ** 80
F32_HUGE = 3.0e38
LOG2E = math.log2(math.e)
ROW_TILE = 512
NEG = -1e30
INT_MIN = -2 ** 31

F32 = jnp.float32
BF16 = jnp.bfloat16
HI = lax.Precision.HIGHEST
NT = (((1,), (1,)), ((), ()))


def _silu(x):
    return x / (1.0 + jnp.exp(-x))


def _rms(x, w):
    return x * lax.rsqrt(jnp.mean(x * x, axis=-1, keepdims=True) + EPS) * w


def _split3(x):
    hi = x.astype(BF16)
    rest = x - hi.astype(F32)
    mid = rest.astype(BF16)
    return hi, mid, (rest - mid.astype(F32)).astype(BF16)


def _ada_kernel(c_ref, w_ref, b_ref, o_ref):
    s = _silu(c_ref[...])
    o_ref[...] = jnp.dot(s, w_ref[...], precision=HI, preferred_element_type=F32) + b_ref[...]


def _ada(c_pad, w, b):
    n = w.shape[1]
    tn = D_MODEL
    return pl.pallas_call(
        _ada_kernel,
        grid=(n // tn,),
        in_specs=[pl.BlockSpec((8, D_MODEL), lambda j: (0, 0)),
                  pl.BlockSpec((D_MODEL, tn), lambda j: (0, j)),
                  pl.BlockSpec((1, tn), lambda j: (0, j))],
        out_specs=pl.BlockSpec((8, tn), lambda j: (0, j)),
        out_shape=jax.ShapeDtypeStruct((8, n), F32),
        name="ada",
    )(c_pad, w, b)


def _inproj_kernel(x_ref, mod_ref, nw_ref, w_ref,
                   z_ref, xbc_ref, q_ref, k_ref, vt_ref, qi_ref, misc_ref, miscb_ref):
    x = x_ref[0]
    h = _rms(x, nw_ref[...]) * (1.0 + mod_ref[0, 1:2, :]) + mod_ref[0, 0:1, :]
    hb = h.astype(BF16)

    def seg(lo, width):
        return jnp.dot(hb, w_ref[:, lo:lo + width], preferred_element_type=F32)

    z_ref[0] = seg(OFF_Z, D_SSM)
    xbc_ref[0] = seg(OFF_XBC, D_XBC)
    rep = ATTN_HEADS // ATTN_KV_HEADS
    q_all = seg(OFF_Q, D_ATTN) * (ATTN_HEAD_DIM ** -0.5 * LOG2E)
    qi_all = seg(OFF_QI, IDX_HEADS * IDX_DIM)
    for j in range(x.shape[0] // BLK):
        q_t = jnp.transpose(q_all[j * BLK:(j + 1) * BLK, :])
        qi_t = jnp.transpose(qi_all[j * BLK:(j + 1) * BLK, :])
        for h in range(ATTN_HEADS):
            q_ref[0, j, h // rep, :, (h % rep) * BLK:(h % rep + 1) * BLK] = (
                q_t[h * ATTN_HEAD_DIM:(h + 1) * ATTN_HEAD_DIM, :].astype(BF16))
        for h in range(IDX_HEADS):
            qi_ref[0, j, :, h * BLK:(h + 1) * BLK] = qi_t[h * IDX_DIM:(h + 1) * IDX_DIM, :].astype(BF16)
    k_v_misc = seg(OFF_K, 2 * D_KV + MISC_W)
    k_ref[0] = k_v_misc[:, :D_KV].astype(BF16)
    v_misc = k_v_misc[:, D_KV:]
    ones_tile = jnp.where(lax.broadcasted_iota(jnp.int32, (V_ROWS - ATTN_HEAD_DIM, BLK), 0) == 0, 1.0, 0.0)
    for j in range(x.shape[0] // BLK):
        v_t = jnp.transpose(v_misc[j * BLK:(j + 1) * BLK, :D_KV])
        for g in range(ATTN_KV_HEADS):
            vt_ref[0, j, g, 0:ATTN_HEAD_DIM, :] = v_t[g * ATTN_HEAD_DIM:(g + 1) * ATTN_HEAD_DIM, :].astype(BF16)
            vt_ref[0, j, g, ATTN_HEAD_DIM:, :] = ones_tile.astype(BF16)
    misc = v_misc[:, D_KV:]
    misc_ref[0] = misc
    miscb_ref[0] = misc.astype(BF16)


def _inproj(x, mod, nw, w_perm):
    bsz, t, d = x.shape
    tm = min(ROW_TILE, t)

    def tok(width, dtype):
        return (pl.BlockSpec((1, tm, width), lambda b, i: (b, i, 0)),
                jax.ShapeDtypeStruct((bsz, t, width), dtype))

    vt = (pl.BlockSpec((1, tm // BLK, ATTN_KV_HEADS, V_ROWS, BLK), lambda b, i: (b, i, 0, 0, 0)),
          jax.ShapeDtypeStruct((bsz, t // BLK, ATTN_KV_HEADS, V_ROWS, BLK), BF16))
    gw = (ATTN_HEADS // ATTN_KV_HEADS) * BLK
    qt = (pl.BlockSpec((1, tm // BLK, ATTN_KV_HEADS, ATTN_HEAD_DIM, gw), lambda b, i: (b, i, 0, 0, 0)),
          jax.ShapeDtypeStruct((bsz, t // BLK, ATTN_KV_HEADS, ATTN_HEAD_DIM, gw), BF16))
    qit = (pl.BlockSpec((1, tm // BLK, IDX_DIM, IDX_HEADS * BLK), lambda b, i: (b, i, 0, 0)),
           jax.ShapeDtypeStruct((bsz, t // BLK, IDX_DIM, IDX_HEADS * BLK), BF16))
    outs = [tok(D_SSM, F32), tok(D_XBC, F32), qt, tok(D_KV, BF16), vt, qit,
            tok(MISC_W, F32), tok(MISC_W, BF16)]
    return pl.pallas_call(
        _inproj_kernel,
        grid=(bsz, t // tm),
        in_specs=[pl.BlockSpec((1, tm, d), lambda b, i: (b, i, 0)),
                  pl.BlockSpec((1, 6, d), lambda b, i: (b, 0, 0)),
                  pl.BlockSpec((1, d), lambda b, i: (0, 0)),
                  pl.BlockSpec((d, D_IN_PAD), lambda b, i: (0, 0))],
        out_specs=[o[0] for o in outs],
        out_shape=[o[1] for o in outs],
        compiler_params=pltpu.CompilerParams(
            dimension_semantics=("arbitrary", "arbitrary"), vmem_limit_bytes=VMEM_LIMIT),
        name="inproj",
    )(x, mod, nw, w_perm)


def _ssd_kernel(xbc_ref, z_ref, misc_ref, cw_ref, cb_ref, dtb_ref, a_ref, e_ref, dskip_ref, nw_ref,
                o_ref, xbuf, state):
    c = pl.program_id(1)
    lc = xbc_ref.shape[1]

    @pl.when(c == 0)
    def _():
        xbuf[0:8, :] = jnp.zeros((8, D_XBC), F32)
        state[...] = jnp.zeros(state.shape, F32)

    @pl.when(c > 0)
    def _():
        xbuf[0:8, :] = xbuf[lc:lc + 8, :]

    xbuf[8:lc + 8, :] = xbc_ref[0]
    u = cb_ref[...]
    xall = xbuf[...]
    for k in range(CONV_WIDTH):
        back = CONV_WIDTH - 1 - k
        xk = xall if back == 0 else pltpu.roll(xall, back, 0)
        u = u + xk[8:8 + lc] * cw_ref[k:k + 1, :]
    u = _silu(u)
    xs = u[:, :D_SSM]
    bm = u[:, D_SSM:D_SSM + D_BC]
    cm = u[:, D_SSM + D_BC:]

    raw = misc_ref[0] + dtb_ref[...]
    dt_t = jnp.maximum(raw, 0.0) + jnp.log1p(jnp.exp(-jnp.abs(raw)))
    adt_t = dt_t * a_ref[...]
    row = lax.broadcasted_iota(jnp.int32, (lc, lc), 0)
    col = lax.broadcasted_iota(jnp.int32, (lc, lc), 1)
    causal = col <= row
    tril = jnp.where(causal, 1.0, 0.0).astype(BF16)
    expand = e_ref[...]
    acs_t = sum(jnp.dot(tril, piece, preferred_element_type=F32) for piece in _split3(adt_t))
    dt_e = sum(jnp.dot(piece, expand, preferred_element_type=F32) for piece in _split3(dt_t))
    acs_e = sum(jnp.dot(piece, expand, preferred_element_type=F32) for piece in _split3(acs_t))
    acs_row = jnp.transpose(acs_t)

    xdt = xs * dt_e
    a_last = acs_e[lc - 1:lc, :]
    xdec = (xdt * jnp.exp2(a_last - acs_e)).astype(BF16)
    xdt_b = xdt.astype(BF16)
    bm_b = bm.astype(BF16)
    cm_b = cm.astype(BF16)
    bm_t = jnp.transpose(bm).astype(BF16)

    heads_per_group = SSM_HEADS // SSM_GROUPS
    gw = heads_per_group * SSM_HEAD_DIM
    y_parts = []
    off_parts = []
    for g in range(SSM_GROUPS):
        bg = bm_b[:, g * SSM_STATE:(g + 1) * SSM_STATE]
        cg = cm_b[:, g * SSM_STATE:(g + 1) * SSM_STATE]
        cb = lax.dot_general(cg, bg, NT, preferred_element_type=F32)
        for r in range(heads_per_group):
            hd = g * heads_per_group + r
            a_col = acs_t[:, MISC_DT + hd:MISC_DT + hd + 1]
            a_row = acs_row[MISC_DT + hd:MISC_DT + hd + 1, :]
            lmat = jnp.exp2(jnp.where(causal, a_col - a_row, -jnp.inf))
            mh = (cb * lmat).astype(BF16)
            y_parts.append(jnp.dot(mh, xdt_b[:, hd * SSM_HEAD_DIM:(hd + 1) * SSM_HEAD_DIM],
                                   preferred_element_type=F32))
        prev = state[g]
        off_parts.append(jnp.dot(cg, prev.astype(BF16), preferred_element_type=F32))
        st = jnp.dot(bm_t[g * SSM_STATE:(g + 1) * SSM_STATE, :], xdec[:, g * gw:(g + 1) * gw],
                     preferred_element_type=F32)
        state[g] = prev * jnp.exp2(a_last[:, g * gw:(g + 1) * gw]) + st
    y = (jnp.concatenate(y_parts, axis=1)
         + jnp.concatenate(off_parts, axis=1) * jnp.exp2(acs_e)
         + xs * dskip_ref[...])
    o_ref[0] = _rms(y * _silu(z_ref[0]), nw_ref[...]).astype(BF16)


def _ssd(xbc, z, misc, cw, cb, dtb_t, a_t, expand, dskip_e, nw):
    bsz, t, _ = xbc.shape
    lc = math.gcd(t, CHUNK)
    nc = t // lc
    gw = (SSM_HEADS // SSM_GROUPS) * SSM_HEAD_DIM

    def const(shape):
        return pl.BlockSpec(shape, lambda b, c: tuple(0 for _ in shape))

    return pl.pallas_call(
        _ssd_kernel,
        grid=(bsz, nc),
        in_specs=[pl.BlockSpec((1, lc, D_XBC), lambda b, c: (b, c, 0)),
                  pl.BlockSpec((1, lc, D_SSM), lambda b, c: (b, c, 0)),
                  pl.BlockSpec((1, lc, MISC_W), lambda b, c: (b, c, 0)),
                  const((CONV_WIDTH, D_XBC)), const((1, D_XBC)), const((1, MISC_W)), const((1, MISC_W)),
                  const((MISC_W, D_SSM)), const((1, D_SSM)), const((1, D_SSM))],
        out_specs=pl.BlockSpec((1, lc, D_SSM), lambda b, c: (b, c, 0)),
        out_shape=jax.ShapeDtypeStruct((bsz, t, D_SSM), BF16),
        scratch_shapes=[pltpu.VMEM((lc + 8, D_XBC), F32),
                        pltpu.VMEM((SSM_GROUPS, SSM_STATE, gw), F32)],
        compiler_params=pltpu.CompilerParams(
            dimension_semantics=("arbitrary", "arbitrary"), vmem_limit_bytes=VMEM_LIMIT),
        name="ssd",
    )(xbc, z, misc, cw, cb, dtb_t, a_t, expand, dskip_e, nw)


def _bucket_tiles():
    s = np.arange(BLK)[:, None]
    t = np.arange(BLK)[None, :]
    max_exact = N_BUCKETS // 2
    tiles = []
    for d0 in (0, BLK):
        n = np.maximum(d0 + t - s, 0)
        large = {}
        for dt_ in (np.float32, np.float64):
            nf = np.maximum(n, 1).astype(dt_)
            lg = max_exact + (np.log(nf / dt_(max_exact)) / dt_(math.log(MAX_DISTANCE / max_exact))
                              * dt_(N_BUCKETS - max_exact)).astype(np.int32)
            large[dt_] = np.minimum(lg, N_BUCKETS - 1)
        assert (large[np.float32] == large[np.float64]).all()
        tiles.append(np.where(n < max_exact, n, large[np.float64]).astype(np.int32))
    assert BLK + 1 >= MAX_DISTANCE
    return np.stack(tiles)


def _to_key(x):
    bits = lax.bitcast_convert_type(x, jnp.int32)
    key = bits ^ (lax.shift_right_arithmetic(bits, 31) & jnp.int32(0x7FFFFFFF))
    return jnp.where(key == -1, 0, key)


def _transpose32(rows):
    rows = list(rows)
    mask, j = 0x0000FFFF, 16
    while j:
        k = 0
        while k < 32:
            t = (rows[k] ^ lax.shift_right_logical(rows[k + j], jnp.int32(j))) & jnp.int32(mask)
            rows[k] = rows[k] ^ t
            rows[k + j] = rows[k + j] ^ lax.shift_left(t, jnp.int32(j))
            k = (k + j + 1) & ~j
        j >>= 1
        mask ^= (mask << j) & 0xFFFFFFFF
    return rows


def _dsa_kernel(relb_ref, bkt_ref, q_ref, qic_ref, miscc_ref, qin_ref, miscn_ref, k_ref, vt_ref, kidx_ref,
                nw_ref, o_ref, keys, planes, live, thr_s, cnt_s, bias_s, qcat, qicat, wcat, m_s, acc_s, p_s,
                *, topk, idx_bits, nq):
    b = pl.program_id(0)
    qi = pl.program_id(1)
    tq = BLK
    i32 = jnp.int32
    rep = ATTN_HEADS // ATTN_KV_HEADS
    gw = rep * tq
    ch = DSA_CHUNK
    ch_blks = ch // BLK
    rows_per_chunk = ch // 32

    @pl.when((b == 0) & (qi == 0))
    def _init_bias():
        for d in range(2):
            bk = bkt_ref[d]
            for h in range(ATTN_HEADS):
                far = relb_ref[N_BUCKETS - 1, h]
                acc = jnp.zeros((BLK, tq), F32)
                for n in range(N_BUCKETS - 1):
                    acc = jnp.where(bk == n, relb_ref[n, h] - far, acc)
                bias_s[d, h // rep, :, (h % rep) * BLK:(h % rep + 1) * BLK] = acc * LOG2E
        bias_s[2] = jnp.zeros((ATTN_KV_HEADS, BLK, gw), F32)
        for w in range(32):
            planes[w] = jnp.zeros(planes.shape[1:], i32)
        p_s[ZERO_SLOT] = jnp.zeros(p_s.shape[1:], BF16)

    row_c = lax.broadcasted_iota(i32, (ch, tq), 0)
    lane_c = lax.broadcasted_iota(i32, (ch, tq), 1)

    def chunk_start(c):
        return pl.multiple_of(c * ch, ch)

    def n_chunks(blk):
        return (blk + ch_blks) // ch_blks

    def load_indexer(qidx_ref, misc_ref):
        qicat[...] = qidx_ref[0, 0]
        w_t = jnp.transpose(misc_ref[0])[MISC_WIDX:MISC_WIDX + IDX_HEADS, :] * (
            IDX_HEADS ** -0.5 * IDX_DIM ** -0.5)
        for h in range(IDX_HEADS):
            wcat[:, h * BLK:(h + 1) * BLK] = jnp.broadcast_to(w_t[h:h + 1, :], (8, BLK))

    def score_chunk(c, blk, slot):
        s0 = chunk_start(c)
        kb = kidx_ref[0, pl.ds(s0, ch), MISC_KIDX:MISC_KIDX + IDX_DIM]
        s = jnp.dot(kb, qicat[...], preferred_element_type=F32)
        r = jnp.maximum(s, 0.0) * wcat[0:1, :]
        acc = r[:, 0:BLK]
        for h in range(1, IDX_HEADS):
            acc = acc + r[:, h * BLK:(h + 1) * BLK]
        acc = jnp.where(s0 + row_c <= blk * BLK + lane_c, acc, -jnp.inf)
        keys[slot, pl.ds(s0, ch), :] = _to_key(acc)

    def plane_chunk(c, slot):
        ku = (keys[slot, pl.ds(chunk_start(c), ch), :] ^ i32(INT_MIN)).reshape(ch // (32 * 8), 32, 8, tq)
        words = _transpose32([ku[:, j] for j in range(32)])
        r0 = pl.multiple_of(c * rows_per_chunk, rows_per_chunk)
        for w in range(32):
            planes[w, pl.ds(r0, rows_per_chunk), :] = words[w].reshape(rows_per_chunk, tq)

    def select(blk, slot):
        nch = n_chunks(blk)
        used = nch * rows_per_chunk

        def ones_in(words):
            return jnp.sum(lax.population_count(words), axis=0, keepdims=True)

        def passes(nr):
            live[0:nr] = jnp.where(lax.broadcasted_iota(i32, (nr, tq), 0) < used, i32(-1), i32(0))

            def bit_body(w, st):
                thr_u, n_above, n_set = st
                take = n_above + n_set >= topk
                thr_u = jnp.where(take, thr_u | lax.shift_left(i32(1), i32(31) - w), thr_u)
                n_above = jnp.where(take, n_above, n_above + n_set)
                still = live[0:nr] & (planes[w, 0:nr] ^ jnp.where(take, i32(0), i32(-1)))
                live[0:nr] = still
                return thr_u, n_above, ones_in(still & planes[jnp.minimum(w + 1, 31), 0:nr])

            zero = jnp.zeros((1, tq), i32)
            thr_u, n_above, _ = lax.fori_loop(
                0, 32, bit_body, (zero, zero, ones_in(live[0:nr] & planes[0, 0:nr])))
            thr_s[slot] = thr_u ^ i32(INT_MIN)
            cnt_s[0] = n_above
            cnt_s[1] = ones_in(live[0:nr])

        n_sizes = max(1, min(SEL_SIZES, live.shape[0] // 8))
        size_step = live.shape[0] // n_sizes
        for v in range(n_sizes):
            pl.when((used > v * size_step) & (used <= (v + 1) * size_step))(
                functools.partial(passes, (v + 1) * size_step))
        thr = thr_s[slot]
        n_gt = cnt_s[0]
        n_eq = cnt_s[1]
        need = topk - n_gt

        def break_ties():
            def count(pred):
                def body(c, cnt):
                    s0 = chunk_start(c)
                    hit = pred(keys[slot, pl.ds(s0, ch), :], s0).astype(i32)
                    n = ch
                    while n > 8:
                        n //= 2
                        hit = hit[:n] + hit[n:2 * n]
                    return cnt + hit
                cnt = lax.fori_loop(0, nch, body, jnp.zeros((8, tq), i32))
                return jnp.sum(cnt, axis=0, keepdims=True)

            def jbody(it, p):
                cand = p + lax.shift_left(i32(1), i32(idx_bits - 1) - it)
                c = count(lambda kk, s0: (kk == thr) & (s0 + row_c < cand))
                return jnp.where(c < need, cand, p)
            last = lax.fori_loop(0, idx_bits, jbody, jnp.zeros((1, tq), i32))

            def demote(c, carry):
                s0 = chunk_start(c)
                kk = keys[slot, pl.ds(s0, ch), :]
                keys[slot, pl.ds(s0, ch), :] = jnp.where((kk == thr) & (s0 + row_c > last), INT_MIN, kk)
                return carry
            lax.fori_loop(0, nch, demote, 0)

        return (n_eq > need).astype(i32), break_ties

    def any_lane(flags):
        return jnp.max(flags, axis=1, keepdims=True)

    @pl.when(qi == 0)
    def _first_block():
        load_indexer(qic_ref, miscc_ref)
        score_chunk(0, 0, 0)
        plane_chunk(0, 0)
        tied, break_ties = select(0, 0)
        pl.when(any_lane(tied)[0, 0] > 0)(break_ties)

    slot_b = qi % 2
    slot_a = 1 - slot_b
    blk_a = jnp.minimum(qi + 1, nq - 1)
    nch_b = n_chunks(qi)
    nch_a = n_chunks(blk_a)
    load_indexer(qin_ref, miscn_ref)
    thr = thr_s[slot_b]

    for g in range(ATTN_KV_HEADS):
        qcat[g] = q_ref[0, 0, g]

    def pen_of(s0, causal):
        sel = keys[slot_b, pl.ds(s0, ch), :] >= thr
        if causal:
            sel = sel & (s0 + row_c <= qi * BLK + lane_c)
        pen = jnp.where(sel, 0.0, NEG)
        return jnp.concatenate([pen] * rep, axis=1)

    def near_bias(c, g):
        return jnp.concatenate(
            [bias_s[jnp.clip(qi - (c * ch_blks + u), 0, 2), g] for u in range(ch_blks)], axis=0)

    def vt_chunk(c, g):
        return jnp.concatenate([vt_ref[0, c * ch_blks + u, g] for u in range(ch_blks)], axis=1)

    for g in range(ATTN_KV_HEADS):
        acc_s[g] = jnp.zeros((V_ROWS, gw), F32)

    def weigh_values(c, slot):
        for g in range(ATTN_KV_HEADS):
            acc_s[g] += jnp.dot(vt_chunk(c, g), p_s[slot, g], preferred_element_type=F32)

    def att_chunk(c, near, slots=None):
        prev_slot, cur_slot = slots if slots is not None else ((c + 1) % 2, c % 2)
        if not (isinstance(cur_slot, int) and cur_slot == 1):
            prev_slot = jnp.where(c == 0, ZERO_SLOT, prev_slot)
        weigh_values(jnp.maximum(c - 1, 0), prev_slot)
        s0 = chunk_start(c)
        pen = pen_of(s0, near)
        kblk = k_ref[0, pl.ds(s0, ch), :]
        for g in range(ATTN_KV_HEADS):
            s = jnp.dot(kblk[:, g * ATTN_HEAD_DIM:(g + 1) * ATTN_HEAD_DIM], qcat[g],
                        preferred_element_type=F32)
            if near:
                s = s + near_bias(c, g)
            p_s[cur_slot, g] = jnp.exp2(s + pen).astype(BF16)

    def fused(c, near, slots=None):
        plane_chunk(c, slot_a)
        att_chunk(c, near, slots)
        score_chunk(c + 1, blk_a, slot_a)

    def far_pair(j, carry):
        fused(2 * j, False, (1, 0))
        fused(2 * j + 1, False, (0, 1))
        return carry

    def far_quad(j, carry):
        far_pair(2 * j, carry)
        return far_pair(2 * j + 1, carry)

    def far_body(c, carry):
        fused(c, False, (1, 0))
        return carry

    def near_body(c, carry):
        fused(c, True)
        return carry

    score_chunk(0, blk_a, slot_a)
    n_far = jnp.maximum(qi - 1, 0) // ch_blks
    lax.fori_loop(0, n_far // 4, far_quad, 0)
    lax.fori_loop((n_far // 4) * 2, n_far // 2, far_pair, 0)
    lax.fori_loop((n_far // 2) * 2, n_far, far_body, 0)
    lax.fori_loop(n_far, nch_b - 1, near_body, 0)
    @pl.when(nch_a > nch_b)
    def _last_with_one_more_chunk():
        fused(nch_b - 1, True)
        plane_chunk(nch_b, slot_a)

    @pl.when(nch_a <= nch_b)
    def _last():
        plane_chunk(nch_b - 1, slot_a)
        att_chunk(nch_b - 1, True)

    weigh_values(nch_b - 1, (nch_b - 1) % 2)
    tied, break_ties = select(blk_a, slot_a)

    n_ok = jnp.zeros((1, gw), i32)
    for g in range(ATTN_KV_HEADS):
        a = acc_s[g]
        den = a[ATTN_HEAD_DIM:ATTN_HEAD_DIM + 1, :]
        top = jnp.max(jnp.abs(a[0:ATTN_HEAD_DIM, :]), axis=0, keepdims=True)
        n_ok = n_ok + ((den > DEN_LO) & (den < DEN_HI) & (top < F32_HUGE)).astype(i32)

    rare = (any_lane(tied) + 2 * any_lane((n_ok < ATTN_KV_HEADS).astype(i32)))[0, 0]
    pl.when(rare % 2 == 1)(break_ties)

    @pl.when(rare >= 2)
    def _exact():
        for g in range(ATTN_KV_HEADS):
            m_s[g] = jnp.full((1, gw), NEG, F32)
            acc_s[g] = jnp.zeros((V_ROWS, gw), F32)

        def body(c, carry):
            s0 = chunk_start(c)
            pen = pen_of(s0, True)
            kblk = k_ref[0, pl.ds(s0, ch), :]
            for g in range(ATTN_KV_HEADS):
                s = jnp.dot(kblk[:, g * ATTN_HEAD_DIM:(g + 1) * ATTN_HEAD_DIM], qcat[g],
                            preferred_element_type=F32) + near_bias(c, g) + pen
                m_old = m_s[g]
                m_new = jnp.maximum(m_old, jnp.max(s, axis=0, keepdims=True))
                p = jnp.exp2(s - m_new).astype(BF16)
                acc_s[g] = acc_s[g] * jnp.exp2(m_old - m_new) + jnp.dot(
                    vt_chunk(c, g), p, preferred_element_type=F32)
                m_s[g] = m_new
            return carry
        lax.fori_loop(0, nch_b, body, 0)

    parts = []
    for g in range(ATTN_KV_HEADS):
        a = acc_s[g]
        o = a[0:ATTN_HEAD_DIM, :] * (1.0 / a[ATTN_HEAD_DIM:ATTN_HEAD_DIM + 1, :])
        parts += [o[:, r * tq:(r + 1) * tq] for r in range(rep)]
    out_t = jnp.concatenate(parts, axis=0)
    inv = lax.rsqrt(jnp.mean(out_t * out_t, axis=0, keepdims=True) + EPS)
    o_ref[0, 0] = (out_t * inv * nw_ref[...]).astype(BF16)


def _dsa(rel_bias, q, qi, misc, k, vt, miscb, nw):
    bsz, t, _ = k.shape
    nq = t // BLK
    topk = min(TOPK_MAX, t // 4)
    idx_bits = max(1, (t - 1).bit_length())
    gw = (ATTN_HEADS // ATTN_KV_HEADS) * BLK
    bkt = jnp.asarray(_bucket_tiles())
    kern = functools.partial(_dsa_kernel, topk=topk, idx_bits=idx_bits, nq=nq)

    def nxt(b, i):
        return (b, jnp.minimum(i + 1, nq - 1), 0)

    return pl.pallas_call(
        kern,
        grid=(bsz, nq),
        in_specs=[pl.BlockSpec(memory_space=pltpu.SMEM),
                  pl.BlockSpec((2, BLK, BLK), lambda b, i: (0, 0, 0)),
                  pl.BlockSpec((1, 1, ATTN_KV_HEADS, ATTN_HEAD_DIM, gw), lambda b, i: (b, i, 0, 0, 0)),
                  pl.BlockSpec((1, 1, IDX_DIM, IDX_HEADS * BLK), lambda b, i: (b, i, 0, 0)),
                  pl.BlockSpec((1, BLK, MISC_W), lambda b, i: (b, i, 0)),
                  pl.BlockSpec((1, 1, IDX_DIM, IDX_HEADS * BLK), lambda b, i: nxt(b, i) + (0,)),
                  pl.BlockSpec((1, BLK, MISC_W), nxt),
                  pl.BlockSpec((1, t, D_KV), lambda b, i: (b, 0, 0)),
                  pl.BlockSpec((1, nq, ATTN_KV_HEADS, V_ROWS, BLK), lambda b, i: (b, 0, 0, 0, 0)),
                  pl.BlockSpec((1, t, MISC_W), lambda b, i: (b, 0, 0)),
                  pl.BlockSpec((D_ATTN, BLK), lambda b, i: (0, 0))],
        out_specs=pl.BlockSpec((1, 1, D_ATTN, BLK), lambda b, i: (b, i, 0, 0)),
        out_shape=jax.ShapeDtypeStruct((bsz, nq, D_ATTN, BLK), BF16),
        scratch_shapes=[pltpu.VMEM((2, t, BLK), jnp.int32),
                        pltpu.VMEM((32, t // 32, BLK), jnp.int32),
                        pltpu.VMEM((t // 32, BLK), jnp.int32),
                        pltpu.VMEM((2, 1, BLK), jnp.int32),
                        pltpu.VMEM((2, 1, BLK), jnp.int32),
                        pltpu.VMEM((3, ATTN_KV_HEADS, BLK, gw), F32),
                        pltpu.VMEM((ATTN_KV_HEADS, ATTN_HEAD_DIM, gw), BF16),
                        pltpu.VMEM((IDX_DIM, IDX_HEADS * BLK), BF16),
                        pltpu.VMEM((8, IDX_HEADS * BLK), F32),
                        pltpu.VMEM((ATTN_KV_HEADS, 1, gw), F32),
                        pltpu.VMEM((ATTN_KV_HEADS, V_ROWS, gw), F32),
                        pltpu.VMEM((ZERO_SLOT + 1, ATTN_KV_HEADS, DSA_CHUNK, gw), BF16)],
        compiler_params=pltpu.CompilerParams(
            dimension_semantics=("arbitrary", "arbitrary"), vmem_limit_bytes=VMEM_LIMIT),
        name="dsa",
    )(rel_bias, bkt, q, qi, misc, qi, misc, k, vt, miscb, nw)


def _tail_kernel(x_ref, ys_ref, ya_ref, mod_ref, wo_ref, n1_ref, n2_ref, n3_ref, w1_ref, w2_ref, o_ref):
    x = x_ref[0]
    ya = jnp.concatenate([jnp.transpose(ya_ref[0, j].astype(F32)).astype(BF16)
                          for j in range(ya_ref.shape[1])], axis=0)
    mix = (jnp.dot(ys_ref[0], wo_ref[0:D_SSM, :], preferred_element_type=F32)
           + jnp.dot(ya, wo_ref[D_SSM:, :], preferred_element_type=F32))
    x1 = x + mod_ref[0, 2:3, :] * _rms(mix, n1_ref[...])
    h = (_rms(x1, n2_ref[...]) * (1.0 + mod_ref[0, 4:5, :]) + mod_ref[0, 3:4, :]).astype(BF16)
    f = jnp.zeros(x.shape, F32)
    step = D_MODEL
    for c0 in range(0, D_FF, step):
        a = jnp.maximum(jnp.dot(h, w1_ref[:, c0:c0 + step], preferred_element_type=F32), 0.0)
        f = f + jnp.dot((a * a).astype(BF16), w2_ref[c0:c0 + step, :], preferred_element_type=F32)
    o_ref[0] = x1 + mod_ref[0, 5:6, :] * _rms(f, n3_ref[...])


def _tail(x, ys, ya, mod, wo, n1, n2, n3, w1, w2):
    bsz, t, d = x.shape
    tm = min(ROW_TILE, t)

    def const(shape):
        return pl.BlockSpec(shape, lambda b, i: tuple(0 for _ in shape))

    return pl.pallas_call(
        _tail_kernel,
        grid=(bsz, t // tm),
        in_specs=[pl.BlockSpec((1, tm, d), lambda b, i: (b, i, 0)),
                  pl.BlockSpec((1, tm, D_SSM), lambda b, i: (b, i, 0)),
                  pl.BlockSpec((1, tm // BLK, D_ATTN, BLK), lambda b, i: (b, i, 0, 0)),
                  pl.BlockSpec((1, 6, d), lambda b, i: (b, 0, 0)),
                  const((D_SSM + D_ATTN, d)), const((1, d)), const((1, d)), const((1, d)),
                  const((d, D_FF)), const((D_FF, d))],
        out_specs=pl.BlockSpec((1, tm, d), lambda b, i: (b, i, 0)),
        out_shape=jax.ShapeDtypeStruct((bsz, t, d), F32),
        compiler_params=pltpu.CompilerParams(
            dimension_semantics=("arbitrary", "arbitrary"), vmem_limit_bytes=VMEM_LIMIT_TAIL),
        name="tail",
    )(x, ys, ya, mod, wo, n1, n2, n3, w1, w2)


def _permute_w_in(w):
    sizes = [D_SSM, D_XBC, SSM_HEADS, D_ATTN, D_KV, D_KV, IDX_HEADS * IDX_DIM, IDX_DIM, IDX_HEADS]
    pts = np.cumsum(sizes)[:-1].tolist()
    z, xbc, dt, q, k, v, qidx, kidx, widx = jnp.split(w, pts, axis=-1)
    pad = jnp.zeros((w.shape[0], MISC_W - IDX_DIM - IDX_HEADS - SSM_HEADS), w.dtype)
    return jnp.concatenate([z, xbc, q, qidx, k, v, kidx, widx, dt, pad], axis=-1).astype(BF16)


def _layer(x, c_pad, w_ada, b_ada, norm1_pre, norm1_post, w_in, conv_w, conv_b, dt_bias, a_log, d_skip,
           ssm_norm, rel_bias, attn_norm, w_out, norm2_pre, norm2_post, w_mlp_in, w_mlp_out):
    bsz, t, d = x.shape
    mod = _ada(c_pad, w_ada, b_ada[None, :])[:bsz].reshape(bsz, 6, d)
    z, xbc, q, k, vt, qi, misc, miscb = _inproj(x, mod, norm1_pre[None, :], _permute_w_in(w_in))

    lane = jnp.arange(MISC_W)
    is_dt = (lane >= MISC_DT) & (lane < MISC_DT + SSM_HEADS)
    head_of = jnp.clip(lane - MISC_DT, 0, SSM_HEADS - 1)
    dtb_t = jnp.where(is_dt, dt_bias[head_of], 0.0)[None, :]
    a_t = jnp.where(is_dt, (-jnp.exp(a_log.astype(F32)) * LOG2E)[head_of], 0.0)[None, :]
    expand = ((lane[:, None] - MISC_DT) == (jnp.arange(D_SSM)[None, :] // SSM_HEAD_DIM)).astype(BF16)
    dskip_e = jnp.repeat(d_skip.astype(F32), SSM_HEAD_DIM)[None, :]
    y_ssd = _ssd(xbc, z, misc, conv_w, conv_b[None, :], dtb_t, a_t, expand, dskip_e, ssm_norm[None, :])

    attn_gain = jnp.broadcast_to(attn_norm.astype(F32)[:, None], (D_ATTN, BLK))
    y_att = _dsa(rel_bias, q, qi, misc, k, vt, miscb, attn_gain)

    return _tail(x, y_ssd, y_att, mod, w_out.astype(BF16), norm1_post[None, :], norm2_pre[None, :],
                 norm2_post[None, :], w_mlp_in.astype(BF16), w_mlp_out.astype(BF16))


def kernel(x, c, w_ada, b_ada, norm1_pre, norm1_post, w_in, conv_w, conv_b, dt_bias, a_log, d_skip,
           ssm_norm, rel_bias, attn_norm, w_out, norm2_pre, norm2_post, w_mlp_in, w_mlp_out):
    bsz = x.shape[0]
    assert bsz <= 8 and x.shape[1] % DSA_CHUNK == 0 and x.shape[2] == D_MODEL
    c_pad = jnp.zeros((8, D_MODEL), F32).at[:bsz].set(c)
    for l in range(w_ada.shape[0]):
        x = _layer(x, c_pad, w_ada[l], b_ada[l], norm1_pre[l], norm1_post[l], w_in[l], conv_w[l], conv_b[l],
                   dt_bias[l], a_log[l], d_skip[l], ssm_norm[l], rel_bias, attn_norm[l], w_out[l],
                   norm2_pre[l], norm2_post[l], w_mlp_in[l], w_mlp_out[l])
    return x
```

```python
import functools
import math

import numpy as np
import jax
import jax.numpy as jnp
from jax import lax
from jax.experimental import pallas as pl
from jax.experimental.pallas import tpu as pltpu

D_MODEL = 1024
SSM_HEADS = 8
SSM_HEAD_DIM = 64
D_SSM = SSM_HEADS * SSM_HEAD_DIM
SSM_GROUPS = 2
SSM_STATE = 128
CONV_WIDTH = 4
CHUNK = 256
ATTN_HEADS = 8
ATTN_KV_HEADS = 2
ATTN_HEAD_DIM = 64
D_ATTN = ATTN_HEADS * ATTN_HEAD_DIM
D_KV = ATTN_KV_HEADS * ATTN_HEAD_DIM
IDX_HEADS = 8
IDX_DIM = 64
TOPK_MAX = 256
N_BUCKETS = 32
MAX_DISTANCE = 128
D_FF = 4 * D_MODEL
D_BC = SSM_GROUPS * SSM_STATE
D_XBC = D_SSM + 2 * D_BC
EPS = 1e-6

LANES = 128
V7X_VMEM_BYTES = 64 * 2 ** 20
VMEM_LIMIT = V7X_VMEM_BYTES * 3 // 4
VMEM_LIMIT_TAIL = V7X_VMEM_BYTES * 15 // 16
MISC_W = LANES
MISC_KIDX = 0
MISC_WIDX = IDX_DIM
MISC_DT = IDX_DIM + IDX_HEADS
OFF_Z = 0
OFF_XBC = OFF_Z + D_SSM
OFF_Q = OFF_XBC + D_XBC
OFF_QI = OFF_Q + D_ATTN
OFF_K = OFF_QI + IDX_HEADS * IDX_DIM
OFF_V = OFF_K + D_KV
OFF_MISC = OFF_V + D_KV
D_IN_PAD = OFF_MISC + MISC_W

BLK = 128
DSA_CHUNK = 512
V_ROWS = 80
ZERO_SLOT = 2
SEL_SIZES = 4
DEN_LO = 2.0 ** -80
DEN_HI = 2.0 ** 80
F32_HUGE = 3.0e38
LOG2E = math.log2(math.e)
ROW_TILE = 512
NEG = -1e30
INT_MIN = -2 ** 31

F32 = jnp.float32
BF16 = jnp.bfloat16
HI = lax.Precision.HIGHEST
NT = (((1,), (1,)), ((), ()))


def _silu(x):
    return x / (1.0 + jnp.exp(-x))


def _rms(x, w):
    return x * lax.rsqrt(jnp.mean(x * x, axis=-1, keepdims=True) + EPS) * w


def _split3(x):
    hi = x.astype(BF16)
    rest = x - hi.astype(F32)
    mid = rest.astype(BF16)
    return hi, mid, (rest - mid.astype(F32)).astype(BF16)


def _ada_kernel(c_ref, w_ref, b_ref, o_ref):
    s = _silu(c_ref[...])
    o_ref[...] = jnp.dot(s, w_ref[...], precision=HI, preferred_element_type=F32) + b_ref[...]


def _ada(c_pad, w, b):
    n = w.shape[1]
    tn = D_MODEL
    return pl.pallas_call(
        _ada_kernel,
        grid=(n // tn,),
        in_specs=[pl.BlockSpec((8, D_MODEL), lambda j: (0, 0)),
                  pl.BlockSpec((D_MODEL, tn), lambda j: (0, j)),
                  pl.BlockSpec((1, tn), lambda j: (0, j))],
        out_specs=pl.BlockSpec((8, tn), lambda j: (0, j)),
        out_shape=jax.ShapeDtypeStruct((8, n), F32),
        name="ada",
    )(c_pad, w, b)


def _inproj_kernel(x_ref, mod_ref, nw_ref, w_ref,
                   z_ref, xbc_ref, q_ref, k_ref, vt_ref, qi_ref, misc_ref, miscb_ref):
    x = x_ref[0]
    h = _rms(x, nw_ref[...]) * (1.0 + mod_ref[0, 1:2, :]) + mod_ref[0, 0:1, :]
    hb = h.astype(BF16)

    def seg(lo, width):
        return jnp.dot(hb, w_ref[:, lo:lo + width], preferred_element_type=F32)

    z_ref[0] = seg(OFF_Z, D_SSM)
    xbc_ref[0] = seg(OFF_XBC, D_XBC)
    rep = ATTN_HEADS // ATTN_KV_HEADS
    q_all = seg(OFF_Q, D_ATTN) * (ATTN_HEAD_DIM ** -0.5 * LOG2E)
    qi_all = seg(OFF_QI, IDX_HEADS * IDX_DIM)
    for j in range(x.shape[0] // BLK):
        q_t = jnp.transpose(q_all[j * BLK:(j + 1) * BLK, :])
        qi_t = jnp.transpose(qi_all[j * BLK:(j + 1) * BLK, :])
        for h in range(ATTN_HEADS):
            q_ref[0, j, h // rep, :, (h % rep) * BLK:(h % rep + 1) * BLK] = (
                q_t[h * ATTN_HEAD_DIM:(h + 1) * ATTN_HEAD_DIM, :].astype(BF16))
        for h in range(IDX_HEADS):
            qi_ref[0, j, :, h * BLK:(h + 1) * BLK] = qi_t[h * IDX_DIM:(h + 1) * IDX_DIM, :].astype(BF16)
    k_v_misc = seg(OFF_K, 2 * D_KV + MISC_W)
    k_ref[0] = k_v_misc[:, :D_KV].astype(BF16)
    v_misc = k_v_misc[:, D_KV:]
    ones_tile = jnp.where(lax.broadcasted_iota(jnp.int32, (V_ROWS - ATTN_HEAD_DIM, BLK), 0) == 0, 1.0, 0.0)
    for j in range(x.shape[0] // BLK):
        v_t = jnp.transpose(v_misc[j * BLK:(j + 1) * BLK, :D_KV])
        for g in range(ATTN_KV_HEADS):
            vt_ref[0, j, g, 0:ATTN_HEAD_DIM, :] = v_t[g * ATTN_HEAD_DIM:(g + 1) * ATTN_HEAD_DIM, :].astype(BF16)
            vt_ref[0, j, g, ATTN_HEAD_DIM:, :] = ones_tile.astype(BF16)
    misc = v_misc[:, D_KV:]
    misc_ref[0] = misc
    miscb_ref[0] = misc.astype(BF16)


def _inproj(x, mod, nw, w_perm):
    bsz, t, d = x.shape
    tm = min(ROW_TILE, t)

    def tok(width, dtype):
        return (pl.BlockSpec((1, tm, width), lambda b, i: (b, i, 0)),
                jax.ShapeDtypeStruct((bsz, t, width), dtype))

    vt = (pl.BlockSpec((1, tm // BLK, ATTN_KV_HEADS, V_ROWS, BLK), lambda b, i: (b, i, 0, 0, 0)),
          jax.ShapeDtypeStruct((bsz, t // BLK, ATTN_KV_HEADS, V_ROWS, BLK), BF16))
    gw = (ATTN_HEADS // ATTN_KV_HEADS) * BLK
    qt = (pl.BlockSpec((1, tm // BLK, ATTN_KV_HEADS, ATTN_HEAD_DIM, gw), lambda b, i: (b, i, 0, 0, 0)),
          jax.ShapeDtypeStruct((bsz, t // BLK, ATTN_KV_HEADS, ATTN_HEAD_DIM, gw), BF16))
    qit = (pl.BlockSpec((1, tm // BLK, IDX_DIM, IDX_HEADS * BLK), lambda b, i: (b, i, 0, 0)),
           jax.ShapeDtypeStruct((bsz, t // BLK, IDX_DIM, IDX_HEADS * BLK), BF16))
    outs = [tok(D_SSM, F32), tok(D_XBC, F32), qt, tok(D_KV, BF16), vt, qit,
            tok(MISC_W, F32), tok(MISC_W, BF16)]
    return pl.pallas_call(
        _inproj_kernel,
        grid=(bsz, t // tm),
        in_specs=[pl.BlockSpec((1, tm, d), lambda b, i: (b, i, 0)),
                  pl.BlockSpec((1, 6, d), lambda b, i: (b, 0, 0)),
                  pl.BlockSpec((1, d), lambda b, i: (0, 0)),
                  pl.BlockSpec((d, D_IN_PAD), lambda b, i: (0, 0))],
        out_specs=[o[0] for o in outs],
        out_shape=[o[1] for o in outs],
        compiler_params=pltpu.CompilerParams(
            dimension_semantics=("arbitrary", "arbitrary"), vmem_limit_bytes=VMEM_LIMIT),
        name="inproj",
    )(x, mod, nw, w_perm)


def _ssd_kernel(xbc_ref, z_ref, misc_ref, cw_ref, cb_ref, dtb_ref, a_ref, e_ref, dskip_ref, nw_ref,
                o_ref, xbuf, state):
    c = pl.program_id(1)
    lc = xbc_ref.shape[1]

    @pl.when(c == 0)
    def _():
        xbuf[0:8, :] = jnp.zeros((8, D_XBC), F32)
        state[...] = jnp.zeros(state.shape, F32)

    @pl.when(c > 0)
    def _():
        xbuf[0:8, :] = xbuf[lc:lc + 8, :]

    xbuf[8:lc + 8, :] = xbc_ref[0]
    u = cb_ref[...]
    xall = xbuf[...]
    for k in range(CONV_WIDTH):
        back = CONV_WIDTH - 1 - k
        xk = xall if back == 0 else pltpu.roll(xall, back, 0)
        u = u + xk[8:8 + lc] * cw_ref[k:k + 1, :]
    u = _silu(u)
    xs = u[:, :D_SSM]
    bm = u[:, D_SSM:D_SSM + D_BC]
    cm = u[:, D_SSM + D_BC:]

    raw = misc_ref[0] + dtb_ref[...]
    dt_t = jnp.maximum(raw, 0.0) + jnp.log1p(jnp.exp(-jnp.abs(raw)))
    adt_t = dt_t * a_ref[...]
    row = lax.broadcasted_iota(jnp.int32, (lc, lc), 0)
    col = lax.broadcasted_iota(jnp.int32, (lc, lc), 1)
    causal = col <= row
    tril = jnp.where(causal, 1.0, 0.0).astype(BF16)
    expand = e_ref[...]
    acs_t = sum(jnp.dot(tril, piece, preferred_element_type=F32) for piece in _split3(adt_t))
    dt_e = sum(jnp.dot(piece, expand, preferred_element_type=F32) for piece in _split3(dt_t))
    acs_e = sum(jnp.dot(piece, expand, preferred_element_type=F32) for piece in _split3(acs_t))
    acs_row = jnp.transpose(acs_t)

    xdt = xs * dt_e
    a_last = acs_e[lc - 1:lc, :]
    xdec = (xdt * jnp.exp2(a_last - acs_e)).astype(BF16)
    xdt_b = xdt.astype(BF16)
    bm_b = bm.astype(BF16)
    cm_b = cm.astype(BF16)
    bm_t = jnp.transpose(bm).astype(BF16)

    heads_per_group = SSM_HEADS // SSM_GROUPS
    gw = heads_per_group * SSM_HEAD_DIM
    y_parts = []
    off_parts = []
    for g in range(SSM_GROUPS):
        bg = bm_b[:, g * SSM_STATE:(g + 1) * SSM_STATE]
        cg = cm_b[:, g * SSM_STATE:(g + 1) * SSM_STATE]
        cb = lax.dot_general(cg, bg, NT, preferred_element_type=F32)
        for r in range(heads_per_group):
            hd = g * heads_per_group + r
            a_col = acs_t[:, MISC_DT + hd:MISC_DT + hd + 1]
            a_row = acs_row[MISC_DT + hd:MISC_DT + hd + 1, :]
            lmat = jnp.exp2(jnp.where(causal, a_col - a_row, -jnp.inf))
            mh = (cb * lmat).astype(BF16)
            y_parts.append(jnp.dot(mh, xdt_b[:, hd * SSM_HEAD_DIM:(hd + 1) * SSM_HEAD_DIM],
                                   preferred_element_type=F32))
        prev = state[g]
        off_parts.append(jnp.dot(cg, prev.astype(BF16), preferred_element_type=F32))
        st = jnp.dot(bm_t[g * SSM_STATE:(g + 1) * SSM_STATE, :], xdec[:, g * gw:(g + 1) * gw],
                     preferred_element_type=F32)
        state[g] = prev * jnp.exp2(a_last[:, g * gw:(g + 1) * gw]) + st
    y = (jnp.concatenate(y_parts, axis=1)
         + jnp.concatenate(off_parts, axis=1) * jnp.exp2(acs_e)
         + xs * dskip_ref[...])
    o_ref[0] = _rms(y * _silu(z_ref[0]), nw_ref[...]).astype(BF16)


def _ssd(xbc, z, misc, cw, cb, dtb_t, a_t, expand, dskip_e, nw):
    bsz, t, _ = xbc.shape
    lc = math.gcd(t, CHUNK)
    nc = t // lc
    gw = (SSM_HEADS // SSM_GROUPS) * SSM_HEAD_DIM

    def const(shape):
        return pl.BlockSpec(shape, lambda b, c: tuple(0 for _ in shape))

    return pl.pallas_call(
        _ssd_kernel,
        grid=(bsz, nc),
        in_specs=[pl.BlockSpec((1, lc, D_XBC), lambda b, c: (b, c, 0)),
                  pl.BlockSpec((1, lc, D_SSM), lambda b, c: (b, c, 0)),
                  pl.BlockSpec((1, lc, MISC_W), lambda b, c: (b, c, 0)),
                  const((CONV_WIDTH, D_XBC)), const((1, D_XBC)), const((1, MISC_W)), const((1, MISC_W)),
                  const((MISC_W, D_SSM)), const((1, D_SSM)), const((1, D_SSM))],
        out_specs=pl.BlockSpec((1, lc, D_SSM), lambda b, c: (b, c, 0)),
        out_shape=jax.ShapeDtypeStruct((bsz, t, D_SSM), BF16),
        scratch_shapes=[pltpu.VMEM((lc + 8, D_XBC), F32),
                        pltpu.VMEM((SSM_GROUPS, SSM_STATE, gw), F32)],
        compiler_params=pltpu.CompilerParams(
            dimension_semantics=("arbitrary", "arbitrary"), vmem_limit_bytes=VMEM_LIMIT),
        name="ssd",
    )(xbc, z, misc, cw, cb, dtb_t, a_t, expand, dskip_e, nw)


def _bucket_tiles():
    s = np.arange(BLK)[:, None]
    t = np.arange(BLK)[None, :]
    max_exact = N_BUCKETS // 2
    tiles = []
    for d0 in (0, BLK):
        n = np.maximum(d0 + t - s, 0)
        large = {}
        for dt_ in (np.float32, np.float64):
            nf = np.maximum(n, 1).astype(dt_)
            lg = max_exact + (np.log(nf / dt_(max_exact)) / dt_(math.log(MAX_DISTANCE / max_exact))
                              * dt_(N_BUCKETS - max_exact)).astype(np.int32)
            large[dt_] = np.minimum(lg, N_BUCKETS - 1)
        assert (large[np.float32] == large[np.float64]).all()
        tiles.append(np.where(n < max_exact, n, large[np.float64]).astype(np.int32))
    assert BLK + 1 >= MAX_DISTANCE
    return np.stack(tiles)


def _to_key(x):
    bits = lax.bitcast_convert_type(x, jnp.int32)
    key = bits ^ (lax.shift_right_arithmetic(bits, 31) & jnp.int32(0x7FFFFFFF))
    return jnp.where(key == -1, 0, key)


def _transpose32(rows):
    rows = list(rows)
    mask, j = 0x0000FFFF, 16
    while j:
        k = 0
        while k < 32:
            t = (rows[k] ^ lax.shift_right_logical(rows[k + j], jnp.int32(j))) & jnp.int32(mask)
            rows[k] = rows[k] ^ t
            rows[k + j] = rows[k + j] ^ lax.shift_left(t, jnp.int32(j))
            k = (k + j + 1) & ~j
        j >>= 1
        mask ^= (mask << j) & 0xFFFFFFFF
    return rows


def _dsa_kernel(relb_ref, bkt_ref, q_ref, qic_ref, miscc_ref, qin_ref, miscn_ref, k_ref, vt_ref, kidx_ref,
                nw_ref, o_ref, keys, planes, live, thr_s, cnt_s, bias_s, qcat, qicat, wcat, m_s, acc_s, p_s,
                *, topk, idx_bits, nq):
    b = pl.program_id(0)
    qi = pl.program_id(1)
    tq = BLK
    i32 = jnp.int32
    rep = ATTN_HEADS // ATTN_KV_HEADS
    gw = rep * tq
    ch = DSA_CHUNK
    ch_blks = ch // BLK
    rows_per_chunk = ch // 32

    @pl.when((b == 0) & (qi == 0))
    def _init_bias():
        for d in range(2):
            bk = bkt_ref[d]
            for h in range(ATTN_HEADS):
                far = relb_ref[N_BUCKETS - 1, h]
                acc = jnp.zeros((BLK, tq), F32)
                for n in range(N_BUCKETS - 1):
                    acc = jnp.where(bk == n, relb_ref[n, h] - far, acc)
                bias_s[d, h // rep, :, (h % rep) * BLK:(h % rep + 1) * BLK] = acc * LOG2E
        bias_s[2] = jnp.zeros((ATTN_KV_HEADS, BLK, gw), F32)
        for w in range(32):
            planes[w] = jnp.zeros(planes.shape[1:], i32)
        p_s[ZERO_SLOT] = jnp.zeros(p_s.shape[1:], BF16)

    row_c = lax.broadcasted_iota(i32, (ch, tq), 0)
    lane_c = lax.broadcasted_iota(i32, (ch, tq), 1)

    def chunk_start(c):
        return pl.multiple_of(c * ch, ch)

    def n_chunks(blk):
        return (blk + ch_blks) // ch_blks

    def load_indexer(qidx_ref, misc_ref):
        qicat[...] = qidx_ref[0, 0]
        w_t = jnp.transpose(misc_ref[0])[MISC_WIDX:MISC_WIDX + IDX_HEADS, :] * (
            IDX_HEADS ** -0.5 * IDX_DIM ** -0.5)
        for h in range(IDX_HEADS):
            wcat[:, h * BLK:(h + 1) * BLK] = jnp.broadcast_to(w_t[h:h + 1, :], (8, BLK))

    def score_chunk(c, blk, slot):
        s0 = chunk_start(c)
        kb = kidx_ref[0, pl.ds(s0, ch), MISC_KIDX:MISC_KIDX + IDX_DIM]
        s = jnp.dot(kb, qicat[...], preferred_element_type=F32)
        r = jnp.maximum(s, 0.0) * wcat[0:1, :]
        acc = r[:, 0:BLK]
        for h in range(1, IDX_HEADS):
            acc = acc + r[:, h * BLK:(h + 1) * BLK]
        acc = jnp.where(s0 + row_c <= blk * BLK + lane_c, acc, -jnp.inf)
        keys[slot, pl.ds(s0, ch), :] = _to_key(acc)

    def plane_chunk(c, slot):
        ku = (keys[slot, pl.ds(chunk_start(c), ch), :] ^ i32(INT_MIN)).reshape(ch // (32 * 8), 32, 8, tq)
        words = _transpose32([ku[:, j] for j in range(32)])
        r0 = pl.multiple_of(c * rows_per_chunk, rows_per_chunk)
        for w in range(32):
            planes[w, pl.ds(r0, rows_per_chunk), :] = words[w].reshape(rows_per_chunk, tq)

    def select(blk, slot):
        nch = n_chunks(blk)
        used = nch * rows_per_chunk

        def ones_in(words):
            return jnp.sum(lax.population_count(words), axis=0, keepdims=True)

        def passes(nr):
            live[0:nr] = jnp.where(lax.broadcasted_iota(i32, (nr, tq), 0) < used, i32(-1), i32(0))

            def bit_body(w, st):
                thr_u, n_above, n_set = st
                take = n_above + n_set >= topk
                thr_u = jnp.where(take, thr_u | lax.shift_left(i32(1), i32(31) - w), thr_u)
                n_above = jnp.where(take, n_above, n_above + n_set)
                still = live[0:nr] & (planes[w, 0:nr] ^ jnp.where(take, i32(0), i32(-1)))
                live[0:nr] = still
                return thr_u, n_above, ones_in(still & planes[jnp.minimum(w + 1, 31), 0:nr])

            zero = jnp.zeros((1, tq), i32)
            thr_u, n_above, _ = lax.fori_loop(
                0, 32, bit_body, (zero, zero, ones_in(live[0:nr] & planes[0, 0:nr])))
            thr_s[slot] = thr_u ^ i32(INT_MIN)
            cnt_s[0] = n_above
            cnt_s[1] = ones_in(live[0:nr])

        size_step = live.shape[0] // SEL_SIZES
        for v in range(SEL_SIZES):
            pl.when((used > v * size_step) & (used <= (v + 1) * size_step))(
                functools.partial(passes, (v + 1) * size_step))
        thr = thr_s[slot]
        n_gt = cnt_s[0]
        n_eq = cnt_s[1]
        need = topk - n_gt

        def break_ties():
            def count(pred):
                def body(c, cnt):
                    s0 = chunk_start(c)
                    hit = pred(keys[slot, pl.ds(s0, ch), :], s0).astype(i32)
                    n = ch
                    while n > 8:
                        n //= 2
                        hit = hit[:n] + hit[n:2 * n]
                    return cnt + hit
                cnt = lax.fori_loop(0, nch, body, jnp.zeros((8, tq), i32))
                return jnp.sum(cnt, axis=0, keepdims=True)

            def jbody(it, p):
                cand = p + lax.shift_left(i32(1), i32(idx_bits - 1) - it)
                c = count(lambda kk, s0: (kk == thr) & (s0 + row_c < cand))
                return jnp.where(c < need, cand, p)
            last = lax.fori_loop(0, idx_bits, jbody, jnp.zeros((1, tq), i32))

            def demote(c, carry):
                s0 = chunk_start(c)
                kk = keys[slot, pl.ds(s0, ch), :]
                keys[slot, pl.ds(s0, ch), :] = jnp.where((kk == thr) & (s0 + row_c > last), INT_MIN, kk)
                return carry
            lax.fori_loop(0, nch, demote, 0)

        return (n_eq > need).astype(i32), break_ties

    def any_lane(flags):
        return jnp.max(flags, axis=1, keepdims=True)

    @pl.when(qi == 0)
    def _first_block():
        load_indexer(qic_ref, miscc_ref)
        score_chunk(0, 0, 0)
        plane_chunk(0, 0)
        tied, break_ties = select(0, 0)
        pl.when(any_lane(tied)[0, 0] > 0)(break_ties)

    slot_b = qi % 2
    slot_a = 1 - slot_b
    blk_a = jnp.minimum(qi + 1, nq - 1)
    nch_b = n_chunks(qi)
    nch_a = n_chunks(blk_a)
    load_indexer(qin_ref, miscn_ref)
    thr = thr_s[slot_b]

    for g in range(ATTN_KV_HEADS):
        qcat[g] = q_ref[0, 0, g]

    def pen_of(s0, causal):
        sel = keys[slot_b, pl.ds(s0, ch), :] >= thr
        if causal:
            sel = sel & (s0 + row_c <= qi * BLK + lane_c)
        pen = jnp.where(sel, 0.0, NEG)
        return jnp.concatenate([pen] * rep, axis=1)

    def near_bias(c, g):
        return jnp.concatenate(
            [bias_s[jnp.clip(qi - (c * ch_blks + u), 0, 2), g] for u in range(ch_blks)], axis=0)

    def vt_chunk(c, g):
        return jnp.concatenate([vt_ref[0, c * ch_blks + u, g] for u in range(ch_blks)], axis=1)

    for g in range(ATTN_KV_HEADS):
        acc_s[g] = jnp.zeros((V_ROWS, gw), F32)

    def weigh_values(c, slot):
        for g in range(ATTN_KV_HEADS):
            acc_s[g] += jnp.dot(vt_chunk(c, g), p_s[slot, g], preferred_element_type=F32)

    def att_chunk(c, near, slots=None):
        prev_slot, cur_slot = slots if slots is not None else ((c + 1) % 2, c % 2)
        if not (isinstance(cur_slot, int) and cur_slot == 1):
            prev_slot = jnp.where(c == 0, ZERO_SLOT, prev_slot)
        weigh_values(jnp.maximum(c - 1, 0), prev_slot)
        s0 = chunk_start(c)
        pen = pen_of(s0, near)
        kblk = k_ref[0, pl.ds(s0, ch), :]
        for g in range(ATTN_KV_HEADS):
            s = jnp.dot(kblk[:, g * ATTN_HEAD_DIM:(g + 1) * ATTN_HEAD_DIM], qcat[g],
                        preferred_element_type=F32)
            if near:
                s = s + near_bias(c, g)
            p_s[cur_slot, g] = jnp.exp2(s + pen).astype(BF16)

    def fused(c, near, slots=None):
        plane_chunk(c, slot_a)
        att_chunk(c, near, slots)
        score_chunk(c + 1, blk_a, slot_a)

    def far_pair(j, carry):
        fused(2 * j, False, (1, 0))
        fused(2 * j + 1, False, (0, 1))
        return carry

    def far_quad(j, carry):
        far_pair(2 * j, carry)
        return far_pair(2 * j + 1, carry)

    def far_oct(j, carry):
        far_quad(2 * j, carry)
        return far_quad(2 * j + 1, carry)

    def far_body(c, carry):
        fused(c, False, (1, 0))
        return carry

    def near_body(c, carry):
        fused(c, True)
        return carry

    score_chunk(0, blk_a, slot_a)
    n_far = jnp.maximum(qi - 1, 0) // ch_blks
    lax.fori_loop(0, n_far // 8, far_oct, 0)
    lax.fori_loop((n_far // 8) * 2, n_far // 4, far_quad, 0)
    lax.fori_loop((n_far // 4) * 2, n_far // 2, far_pair, 0)
    lax.fori_loop((n_far // 2) * 2, n_far, far_body, 0)
    lax.fori_loop(n_far, nch_b - 1, near_body, 0)
    @pl.when(nch_a > nch_b)
    def _one_more_chunk():
        score_chunk(nch_b, blk_a, slot_a)
        plane_chunk(nch_b, slot_a)

    plane_chunk(nch_b - 1, slot_a)
    att_chunk(nch_b - 1, True)
    weigh_values(nch_b - 1, (nch_b - 1) % 2)
    tied, break_ties = select(blk_a, slot_a)

    n_ok = jnp.zeros((1, gw), i32)
    for g in range(ATTN_KV_HEADS):
        a = acc_s[g]
        den = a[ATTN_HEAD_DIM:ATTN_HEAD_DIM + 1, :]
        top = jnp.max(jnp.abs(a[0:ATTN_HEAD_DIM, :]), axis=0, keepdims=True)
        n_ok = n_ok + ((den > DEN_LO) & (den < DEN_HI) & (top < F32_HUGE)).astype(i32)

    rare = (any_lane(tied) + 2 * any_lane((n_ok < ATTN_KV_HEADS).astype(i32)))[0, 0]
    pl.when(rare % 2 == 1)(break_ties)

    @pl.when(rare >= 2)
    def _exact():
        for g in range(ATTN_KV_HEADS):
            m_s[g] = jnp.full((1, gw), NEG, F32)
            acc_s[g] = jnp.zeros((V_ROWS, gw), F32)

        def body(c, carry):
            s0 = chunk_start(c)
            pen = pen_of(s0, True)
            kblk = k_ref[0, pl.ds(s0, ch), :]
            for g in range(ATTN_KV_HEADS):
                s = jnp.dot(kblk[:, g * ATTN_HEAD_DIM:(g + 1) * ATTN_HEAD_DIM], qcat[g],
                            preferred_element_type=F32) + near_bias(c, g) + pen
                m_old = m_s[g]
                m_new = jnp.maximum(m_old, jnp.max(s, axis=0, keepdims=True))
                p = jnp.exp2(s - m_new).astype(BF16)
                acc_s[g] = acc_s[g] * jnp.exp2(m_old - m_new) + jnp.dot(
                    vt_chunk(c, g), p, preferred_element_type=F32)
                m_s[g] = m_new
            return carry
        lax.fori_loop(0, nch_b, body, 0)

    parts = []
    for g in range(ATTN_KV_HEADS):
        a = acc_s[g]
        o = a[0:ATTN_HEAD_DIM, :] * (1.0 / a[ATTN_HEAD_DIM:ATTN_HEAD_DIM + 1, :])
        parts += [o[:, r * tq:(r + 1) * tq] for r in range(rep)]
    out_t = jnp.concatenate(parts, axis=0)
    inv = lax.rsqrt(jnp.mean(out_t * out_t, axis=0, keepdims=True) + EPS)
    o_ref[0, 0] = (out_t * inv * nw_ref[...]).astype(BF16)


def _dsa(rel_bias, q, qi, misc, k, vt, miscb, nw):
    bsz, t, _ = k.shape
    nq = t // BLK
    topk = min(TOPK_MAX, t // 4)
    idx_bits = max(1, (t - 1).bit_length())
    gw = (ATTN_HEADS // ATTN_KV_HEADS) * BLK
    bkt = jnp.asarray(_bucket_tiles())
    kern = functools.partial(_dsa_kernel, topk=topk, idx_bits=idx_bits, nq=nq)

    def nxt(b, i):
        return (b, jnp.minimum(i + 1, nq - 1), 0)

    return pl.pallas_call(
        kern,
        grid=(bsz, nq),
        in_specs=[pl.BlockSpec(memory_space=pltpu.SMEM),
                  pl.BlockSpec((2, BLK, BLK), lambda b, i: (0, 0, 0)),
                  pl.BlockSpec((1, 1, ATTN_KV_HEADS, ATTN_HEAD_DIM, gw), lambda b, i: (b, i, 0, 0, 0)),
                  pl.BlockSpec((1, 1, IDX_DIM, IDX_HEADS * BLK), lambda b, i: (b, i, 0, 0)),
                  pl.BlockSpec((1, BLK, MISC_W), lambda b, i: (b, i, 0)),
                  pl.BlockSpec((1, 1, IDX_DIM, IDX_HEADS * BLK), lambda b, i: nxt(b, i) + (0,)),
                  pl.BlockSpec((1, BLK, MISC_W), nxt),
                  pl.BlockSpec((1, t, D_KV), lambda b, i: (b, 0, 0)),
                  pl.BlockSpec((1, nq, ATTN_KV_HEADS, V_ROWS, BLK), lambda b, i: (b, 0, 0, 0, 0)),
                  pl.BlockSpec((1, t, MISC_W), lambda b, i: (b, 0, 0)),
                  pl.BlockSpec((D_ATTN, BLK), lambda b, i: (0, 0))],
        out_specs=pl.BlockSpec((1, 1, D_ATTN, BLK), lambda b, i: (b, i, 0, 0)),
        out_shape=jax.ShapeDtypeStruct((bsz, nq, D_ATTN, BLK), BF16),
        scratch_shapes=[pltpu.VMEM((2, t, BLK), jnp.int32),
                        pltpu.VMEM((32, t // 32, BLK), jnp.int32),
                        pltpu.VMEM((t // 32, BLK), jnp.int32),
                        pltpu.VMEM((2, 1, BLK), jnp.int32),
                        pltpu.VMEM((2, 1, BLK), jnp.int32),
                        pltpu.VMEM((3, ATTN_KV_HEADS, BLK, gw), F32),
                        pltpu.VMEM((ATTN_KV_HEADS, ATTN_HEAD_DIM, gw), BF16),
                        pltpu.VMEM((IDX_DIM, IDX_HEADS * BLK), BF16),
                        pltpu.VMEM((8, IDX_HEADS * BLK), F32),
                        pltpu.VMEM((ATTN_KV_HEADS, 1, gw), F32),
                        pltpu.VMEM((ATTN_KV_HEADS, V_ROWS, gw), F32),
                        pltpu.VMEM((ZERO_SLOT + 1, ATTN_KV_HEADS, DSA_CHUNK, gw), BF16)],
        compiler_params=pltpu.CompilerParams(
            dimension_semantics=("arbitrary", "arbitrary"), vmem_limit_bytes=VMEM_LIMIT),
        name="dsa",
    )(rel_bias, bkt, q, qi, misc, qi, misc, k, vt, miscb, nw)


def _tail_kernel(x_ref, ys_ref, ya_ref, mod_ref, wo_ref, n1_ref, n2_ref, n3_ref, w1_ref, w2_ref, o_ref):
    x = x_ref[0]
    ya = jnp.concatenate([jnp.transpose(ya_ref[0, j].astype(F32)).astype(BF16)
                          for j in range(ya_ref.shape[1])], axis=0)
    mix = (jnp.dot(ys_ref[0], wo_ref[0:D_SSM, :], preferred_element_type=F32)
           + jnp.dot(ya, wo_ref[D_SSM:, :], preferred_element_type=F32))
    x1 = x + mod_ref[0, 2:3, :] * _rms(mix, n1_ref[...])
    h = (_rms(x1, n2_ref[...]) * (1.0 + mod_ref[0, 4:5, :]) + mod_ref[0, 3:4, :]).astype(BF16)
    f = jnp.zeros(x.shape, F32)
    step = D_MODEL
    for c0 in range(0, D_FF, step):
        a = jnp.maximum(jnp.dot(h, w1_ref[:, c0:c0 + step], preferred_element_type=F32), 0.0)
        f = f + jnp.dot((a * a).astype(BF16), w2_ref[c0:c0 + step, :], preferred_element_type=F32)
    o_ref[0] = x1 + mod_ref[0, 5:6, :] * _rms(f, n3_ref[...])


def _tail(x, ys, ya, mod, wo, n1, n2, n3, w1, w2):
    bsz, t, d = x.shape
    tm = min(ROW_TILE, t)

    def const(shape):
        return pl.BlockSpec(shape, lambda b, i: tuple(0 for _ in shape))

    return pl.pallas_call(
        _tail_kernel,
        grid=(bsz, t // tm),
        in_specs=[pl.BlockSpec((1, tm, d), lambda b, i: (b, i, 0)),
                  pl.BlockSpec((1, tm, D_SSM), lambda b, i: (b, i, 0)),
                  pl.BlockSpec((1, tm // BLK, D_ATTN, BLK), lambda b, i: (b, i, 0, 0)),
                  pl.BlockSpec((1, 6, d), lambda b, i: (b, 0, 0)),
                  const((D_SSM + D_ATTN, d)), const((1, d)), const((1, d)), const((1, d)),
                  const((d, D_FF)), const((D_FF, d))],
        out_specs=pl.BlockSpec((1, tm, d), lambda b, i: (b, i, 0)),
        out_shape=jax.ShapeDtypeStruct((bsz, t, d), F32),
        compiler_params=pltpu.CompilerParams(
            dimension_semantics=("arbitrary", "arbitrary"), vmem_limit_bytes=VMEM_LIMIT_TAIL),
        name="tail",
    )(x, ys, ya, mod, wo, n1, n2, n3, w1, w2)


def _permute_w_in(w):
    sizes = [D_SSM, D_XBC, SSM_HEADS, D_ATTN, D_KV, D_KV, IDX_HEADS * IDX_DIM, IDX_DIM, IDX_HEADS]
    pts = np.cumsum(sizes)[:-1].tolist()
    z, xbc, dt, q, k, v, qidx, kidx, widx = jnp.split(w, pts, axis=-1)
    pad = jnp.zeros((w.shape[0], MISC_W - IDX_DIM - IDX_HEADS - SSM_HEADS), w.dtype)
    return jnp.concatenate([z, xbc, q, qidx, k, v, kidx, widx, dt, pad], axis=-1).astype(BF16)


def _layer(x, c_pad, w_ada, b_ada, norm1_pre, norm1_post, w_in, conv_w, conv_b, dt_bias, a_log, d_skip,
           ssm_norm, rel_bias, attn_norm, w_out, norm2_pre, norm2_post, w_mlp_in, w_mlp_out):
    bsz, t, d = x.shape
    mod = _ada(c_pad, w_ada, b_ada[None, :])[:bsz].reshape(bsz, 6, d)
    z, xbc, q, k, vt, qi, misc, miscb = _inproj(x, mod, norm1_pre[None, :], _permute_w_in(w_in))

    lane = jnp.arange(MISC_W)
    is_dt = (lane >= MISC_DT) & (lane < MISC_DT + SSM_HEADS)
    head_of = jnp.clip(lane - MISC_DT, 0, SSM_HEADS - 1)
    dtb_t = jnp.where(is_dt, dt_bias[head_of], 0.0)[None, :]
    a_t = jnp.where(is_dt, (-jnp.exp(a_log.astype(F32)) * LOG2E)[head_of], 0.0)[None, :]
    expand = ((lane[:, None] - MISC_DT) == (jnp.arange(D_SSM)[None, :] // SSM_HEAD_DIM)).astype(BF16)
    dskip_e = jnp.repeat(d_skip.astype(F32), SSM_HEAD_DIM)[None, :]
    y_ssd = _ssd(xbc, z, misc, conv_w, conv_b[None, :], dtb_t, a_t, expand, dskip_e, ssm_norm[None, :])

    attn_gain = jnp.broadcast_to(attn_norm.astype(F32)[:, None], (D_ATTN, BLK))
    y_att = _dsa(rel_bias, q, qi, misc, k, vt, miscb, attn_gain)

    return _tail(x, y_ssd, y_att, mod, w_out.astype(BF16), norm1_post[None, :], norm2_pre[None, :],
                 norm2_post[None, :], w_mlp_in.astype(BF16), w_mlp_out.astype(BF16))


def kernel(x, c, w_ada, b_ada, norm1_pre, norm1_post, w_in, conv_w, conv_b, dt_bias, a_log, d_skip,
           ssm_norm, rel_bias, attn_norm, w_out, norm2_pre, norm2_post, w_mlp_in, w_mlp_out):
    bsz = x.shape[0]
    assert bsz <= 8 and x.shape[1] % DSA_CHUNK == 0 and x.shape[2] == D_MODEL
    c_pad = jnp.zeros((8, D_MODEL), F32).at[:bsz].set(c)
    for l in range(w_ada.shape[0]):
        x = _layer(x, c_pad, w_ada[l], b_ada[l], norm1_pre[l], norm1_post[l], w_in[l], conv_w[l], conv_b[l],
                   dt_bias[l], a_log[l], d_skip[l], ssm_norm[l], rel_bias, attn_norm[l], w_out[l],
                   norm2_pre[l], norm2_post[l], w_mlp_in[l], w_mlp_out[l])
    return x
```

```python
import functools
import math

import numpy as np
import jax
import jax.numpy as jnp
from jax import lax
from jax.experimental import pallas as pl
from jax.experimental.pallas import tpu as pltpu

D_MODEL = 1024
SSM_HEADS = 8
SSM_HEAD_DIM = 64
D_SSM = SSM_HEADS * SSM_HEAD_DIM
SSM_GROUPS = 2
SSM_STATE = 128
CONV_WIDTH = 4
CHUNK = 256
ATTN_HEADS = 8
ATTN_KV_HEADS = 2
ATTN_HEAD_DIM = 64
D_ATTN = ATTN_HEADS * ATTN_HEAD_DIM
D_KV = ATTN_KV_HEADS * ATTN_HEAD_DIM
IDX_HEADS = 8
IDX_DIM = 64
TOPK_MAX = 256
N_BUCKETS = 32
MAX_DISTANCE = 128
D_FF = 4 * D_MODEL
D_BC = SSM_GROUPS * SSM_STATE
D_XBC = D_SSM + 2 * D_BC
EPS = 1e-6

LANES = 128
V7X_VMEM_BYTES = 64 * 2 ** 20
VMEM_LIMIT = V7X_VMEM_BYTES * 3 // 4
VMEM_LIMIT_TAIL = V7X_VMEM_BYTES * 15 // 16
MISC_W = LANES
MISC_KIDX = 0
MISC_WIDX = IDX_DIM
MISC_DT = IDX_DIM + IDX_HEADS
OFF_Z = 0
OFF_XBC = OFF_Z + D_SSM
OFF_Q = OFF_XBC + D_XBC
OFF_QI = OFF_Q + D_ATTN
OFF_K = OFF_QI + IDX_HEADS * IDX_DIM
OFF_V = OFF_K + D_KV
OFF_MISC = OFF_V + D_KV
D_IN_PAD = OFF_MISC + MISC_W

BLK = 128
DSA_CHUNK = 512
V_ROWS = 80
ZERO_SLOT = 2
SEL_SIZES = 4
DEN_LO = 2.0 ** -80
DEN_HI = 2.0 ** 80
F32_HUGE = 3.0e38
LOG2E = math.log2(math.e)
ROW_TILE = 512
NEG = -1e30
INT_MIN = -2 ** 31

F32 = jnp.float32
BF16 = jnp.bfloat16
HI = lax.Precision.HIGHEST
NT = (((1,), (1,)), ((), ()))


def _silu(x):
    return x / (1.0 + jnp.exp(-x))


def _rms(x, w):
    return x * lax.rsqrt(jnp.mean(x * x, axis=-1, keepdims=True) + EPS) * w


def _split3(x):
    hi = x.astype(BF16)
    rest = x - hi.astype(F32)
    mid = rest.astype(BF16)
    return hi, mid, (rest - mid.astype(F32)).astype(BF16)


def _ada_kernel(c_ref, w_ref, b_ref, o_ref):
    s = _silu(c_ref[...])
    o_ref[...] = jnp.dot(s, w_ref[...], precision=HI, preferred_element_type=F32) + b_ref[...]


def _ada(c_pad, w, b):
    n = w.shape[1]
    tn = D_MODEL
    return pl.pallas_call(
        _ada_kernel,
        grid=(n // tn,),
        in_specs=[pl.BlockSpec((8, D_MODEL), lambda j: (0, 0)),
                  pl.BlockSpec((D_MODEL, tn), lambda j: (0, j)),
                  pl.BlockSpec((1, tn), lambda j: (0, j))],
        out_specs=pl.BlockSpec((8, tn), lambda j: (0, j)),
        out_shape=jax.ShapeDtypeStruct((8, n), F32),
        name="ada",
    )(c_pad, w, b)


def _inproj_kernel(x_ref, mod_ref, nw_ref, w_ref,
                   z_ref, xbc_ref, q_ref, k_ref, vt_ref, qi_ref, misc_ref, miscb_ref):
    x = x_ref[0]
    h = _rms(x, nw_ref[...]) * (1.0 + mod_ref[0, 1:2, :]) + mod_ref[0, 0:1, :]
    hb = h.astype(BF16)

    def seg(lo, width):
        return jnp.dot(hb, w_ref[:, lo:lo + width], preferred_element_type=F32)

    z_ref[0] = seg(OFF_Z, D_SSM)
    xbc_ref[0] = seg(OFF_XBC, D_XBC)
    rep = ATTN_HEADS // ATTN_KV_HEADS
    q_all = seg(OFF_Q, D_ATTN) * (ATTN_HEAD_DIM ** -0.5 * LOG2E)
    qi_all = seg(OFF_QI, IDX_HEADS * IDX_DIM)
    for j in range(x.shape[0] // BLK):
        q_t = jnp.transpose(q_all[j * BLK:(j + 1) * BLK, :])
        qi_t = jnp.transpose(qi_all[j * BLK:(j + 1) * BLK, :])
        for h in range(ATTN_HEADS):
            q_ref[0, j, h // rep, :, (h % rep) * BLK:(h % rep + 1) * BLK] = (
                q_t[h * ATTN_HEAD_DIM:(h + 1) * ATTN_HEAD_DIM, :].astype(BF16))
        for h in range(IDX_HEADS):
            qi_ref[0, j, :, h * BLK:(h + 1) * BLK] = qi_t[h * IDX_DIM:(h + 1) * IDX_DIM, :].astype(BF16)
    k_v_misc = seg(OFF_K, 2 * D_KV + MISC_W)
    k_ref[0] = k_v_misc[:, :D_KV].astype(BF16)
    v_misc = k_v_misc[:, D_KV:]
    ones_tile = jnp.where(lax.broadcasted_iota(jnp.int32, (V_ROWS - ATTN_HEAD_DIM, BLK), 0) == 0, 1.0, 0.0)
    for j in range(x.shape[0] // BLK):
        v_t = jnp.transpose(v_misc[j * BLK:(j + 1) * BLK, :D_KV])
        for g in range(ATTN_KV_HEADS):
            vt_ref[0, j, g, 0:ATTN_HEAD_DIM, :] = v_t[g * ATTN_HEAD_DIM:(g + 1) * ATTN_HEAD_DIM, :].astype(BF16)
            vt_ref[0, j, g, ATTN_HEAD_DIM:, :] = ones_tile.astype(BF16)
    misc = v_misc[:, D_KV:]
    misc_ref[0] = misc
    miscb_ref[0] = misc.astype(BF16)


def _inproj(x, mod, nw, w_perm):
    bsz, t, d = x.shape
    tm = min(ROW_TILE, t)

    def tok(width, dtype):
        return (pl.BlockSpec((1, tm, width), lambda b, i: (b, i, 0)),
                jax.ShapeDtypeStruct((bsz, t, width), dtype))

    vt = (pl.BlockSpec((1, tm // BLK, ATTN_KV_HEADS, V_ROWS, BLK), lambda b, i: (b, i, 0, 0, 0)),
          jax.ShapeDtypeStruct((bsz, t // BLK, ATTN_KV_HEADS, V_ROWS, BLK), BF16))
    gw = (ATTN_HEADS // ATTN_KV_HEADS) * BLK
    qt = (pl.BlockSpec((1, tm // BLK, ATTN_KV_HEADS, ATTN_HEAD_DIM, gw), lambda b, i: (b, i, 0, 0, 0)),
          jax.ShapeDtypeStruct((bsz, t // BLK, ATTN_KV_HEADS, ATTN_HEAD_DIM, gw), BF16))
    qit = (pl.BlockSpec((1, tm // BLK, IDX_DIM, IDX_HEADS * BLK), lambda b, i: (b, i, 0, 0)),
           jax.ShapeDtypeStruct((bsz, t // BLK, IDX_DIM, IDX_HEADS * BLK), BF16))
    outs = [tok(D_SSM, F32), tok(D_XBC, F32), qt, tok(D_KV, BF16), vt, qit,
            tok(MISC_W, F32), tok(MISC_W, BF16)]
    return pl.pallas_call(
        _inproj_kernel,
        grid=(bsz, t // tm),
        in_specs=[pl.BlockSpec((1, tm, d), lambda b, i: (b, i, 0)),
                  pl.BlockSpec((1, 6, d), lambda b, i: (b, 0, 0)),
                  pl.BlockSpec((1, d), lambda b, i: (0, 0)),
                  pl.BlockSpec((d, D_IN_PAD), lambda b, i: (0, 0))],
        out_specs=[o[0] for o in outs],
        out_shape=[o[1] for o in outs],
        compiler_params=pltpu.CompilerParams(
            dimension_semantics=("arbitrary", "arbitrary"), vmem_limit_bytes=VMEM_LIMIT),
        name="inproj",
    )(x, mod, nw, w_perm)


def _ssd_kernel(xbc_ref, z_ref, misc_ref, cw_ref, cb_ref, dtb_ref, a_ref, e_ref, dskip_ref, nw_ref,
                o_ref, xbuf, state):
    c = pl.program_id(1)
    lc = xbc_ref.shape[1]

    @pl.when(c == 0)
    def _():
        xbuf[0:8, :] = jnp.zeros((8, D_XBC), F32)
        state[...] = jnp.zeros(state.shape, F32)

    @pl.when(c > 0)
    def _():
        xbuf[0:8, :] = xbuf[lc:lc + 8, :]

    xbuf[8:lc + 8, :] = xbc_ref[0]
    u = cb_ref[...]
    xall = xbuf[...]
    for k in range(CONV_WIDTH):
        back = CONV_WIDTH - 1 - k
        xk = xall if back == 0 else pltpu.roll(xall, back, 0)
        u = u + xk[8:8 + lc] * cw_ref[k:k + 1, :]
    u = _silu(u)
    xs = u[:, :D_SSM]
    bm = u[:, D_SSM:D_SSM + D_BC]
    cm = u[:, D_SSM + D_BC:]

    raw = misc_ref[0] + dtb_ref[...]
    dt_t = jnp.maximum(raw, 0.0) + jnp.log1p(jnp.exp(-jnp.abs(raw)))
    adt_t = dt_t * a_ref[...]
    row = lax.broadcasted_iota(jnp.int32, (lc, lc), 0)
    col = lax.broadcasted_iota(jnp.int32, (lc, lc), 1)
    causal = col <= row
    tril = jnp.where(causal, 1.0, 0.0).astype(BF16)
    expand = e_ref[...]
    acs_t = sum(jnp.dot(tril, piece, preferred_element_type=F32) for piece in _split3(adt_t))
    dt_e = sum(jnp.dot(piece, expand, preferred_element_type=F32) for piece in _split3(dt_t))
    acs_e = sum(jnp.dot(piece, expand, preferred_element_type=F32) for piece in _split3(acs_t))
    acs_row = jnp.transpose(acs_t)

    xdt = xs * dt_e
    a_last = acs_e[lc - 1:lc, :]
    xdec = (xdt * jnp.exp2(a_last - acs_e)).astype(BF16)
    xdt_b = xdt.astype(BF16)
    bm_b = bm.astype(BF16)
    cm_b = cm.astype(BF16)
    bm_t = jnp.transpose(bm).astype(BF16)

    heads_per_group = SSM_HEADS // SSM_GROUPS
    gw = heads_per_group * SSM_HEAD_DIM
    y_parts = []
    off_parts = []
    for g in range(SSM_GROUPS):
        bg = bm_b[:, g * SSM_STATE:(g + 1) * SSM_STATE]
        cg = cm_b[:, g * SSM_STATE:(g + 1) * SSM_STATE]
        cb = lax.dot_general(cg, bg, NT, preferred_element_type=F32)
        for r in range(heads_per_group):
            hd = g * heads_per_group + r
            a_col = acs_t[:, MISC_DT + hd:MISC_DT + hd + 1]
            a_row = acs_row[MISC_DT + hd:MISC_DT + hd + 1, :]
            lmat = jnp.exp2(jnp.where(causal, a_col - a_row, -jnp.inf))
            mh = (cb * lmat).astype(BF16)
            y_parts.append(jnp.dot(mh, xdt_b[:, hd * SSM_HEAD_DIM:(hd + 1) * SSM_HEAD_DIM],
                                   preferred_element_type=F32))
        prev = state[g]
        off_parts.append(jnp.dot(cg, prev.astype(BF16), preferred_element_type=F32))
        st = jnp.dot(bm_t[g * SSM_STATE:(g + 1) * SSM_STATE, :], xdec[:, g * gw:(g + 1) * gw],
                     preferred_element_type=F32)
        state[g] = prev * jnp.exp2(a_last[:, g * gw:(g + 1) * gw]) + st
    y = (jnp.concatenate(y_parts, axis=1)
         + jnp.concatenate(off_parts, axis=1) * jnp.exp2(acs_e)
         + xs * dskip_ref[...])
    o_ref[0] = _rms(y * _silu(z_ref[0]), nw_ref[...]).astype(BF16)


def _ssd(xbc, z, misc, cw, cb, dtb_t, a_t, expand, dskip_e, nw):
    bsz, t, _ = xbc.shape
    lc = math.gcd(t, CHUNK)
    nc = t // lc
    gw = (SSM_HEADS // SSM_GROUPS) * SSM_HEAD_DIM

    def const(shape):
        return pl.BlockSpec(shape, lambda b, c: tuple(0 for _ in shape))

    return pl.pallas_call(
        _ssd_kernel,
        grid=(bsz, nc),
        in_specs=[pl.BlockSpec((1, lc, D_XBC), lambda b, c: (b, c, 0)),
                  pl.BlockSpec((1, lc, D_SSM), lambda b, c: (b, c, 0)),
                  pl.BlockSpec((1, lc, MISC_W), lambda b, c: (b, c, 0)),
                  const((CONV_WIDTH, D_XBC)), const((1, D_XBC)), const((1, MISC_W)), const((1, MISC_W)),
                  const((MISC_W, D_SSM)), const((1, D_SSM)), const((1, D_SSM))],
        out_specs=pl.BlockSpec((1, lc, D_SSM), lambda b, c: (b, c, 0)),
        out_shape=jax.ShapeDtypeStruct((bsz, t, D_SSM), BF16),
        scratch_shapes=[pltpu.VMEM((lc + 8, D_XBC), F32),
                        pltpu.VMEM((SSM_GROUPS, SSM_STATE, gw), F32)],
        compiler_params=pltpu.CompilerParams(
            dimension_semantics=("arbitrary", "arbitrary"), vmem_limit_bytes=VMEM_LIMIT),
        name="ssd",
    )(xbc, z, misc, cw, cb, dtb_t, a_t, expand, dskip_e, nw)


def _bucket_tiles():
    s = np.arange(BLK)[:, None]
    t = np.arange(BLK)[None, :]
    max_exact = N_BUCKETS // 2
    tiles = []
    for d0 in (0, BLK):
        n = np.maximum(d0 + t - s, 0)
        large = {}
        for dt_ in (np.float32, np.float64):
            nf = np.maximum(n, 1).astype(dt_)
            lg = max_exact + (np.log(nf / dt_(max_exact)) / dt_(math.log(MAX_DISTANCE / max_exact))
                              * dt_(N_BUCKETS - max_exact)).astype(np.int32)
            large[dt_] = np.minimum(lg, N_BUCKETS - 1)
        assert (large[np.float32] == large[np.float64]).all()
        tiles.append(np.where(n < max_exact, n, large[np.float64]).astype(np.int32))
    assert BLK + 1 >= MAX_DISTANCE
    return np.stack(tiles)


def _to_key(x):
    bits = lax.bitcast_convert_type(x, jnp.int32)
    key = bits ^ (lax.shift_right_arithmetic(bits, 31) & jnp.int32(0x7FFFFFFF))
    return jnp.where(key == -1, 0, key)


def _transpose32(rows):
    rows = list(rows)
    mask, j = 0x0000FFFF, 16
    while j:
        k = 0
        while k < 32:
            t = (rows[k] ^ lax.shift_right_logical(rows[k + j], jnp.int32(j))) & jnp.int32(mask)
            rows[k] = rows[k] ^ t
            rows[k + j] = rows[k + j] ^ lax.shift_left(t, jnp.int32(j))
            k = (k + j + 1) & ~j
        j >>= 1
        mask ^= (mask << j) & 0xFFFFFFFF
    return rows


def _dsa_kernel(relb_ref, bkt_ref, q_ref, qic_ref, miscc_ref, qin_ref, miscn_ref, k_ref, vt_ref, kidx_ref,
                nw_ref, o_ref, keys, planes, live, thr_s, cnt_s, bias_s, qcat, qicat, wcat, m_s, acc_s, p_s,
                *, topk, idx_bits, nq):
    b = pl.program_id(0)
    qi = pl.program_id(1)
    tq = BLK
    i32 = jnp.int32
    rep = ATTN_HEADS // ATTN_KV_HEADS
    gw = rep * tq
    ch = DSA_CHUNK
    ch_blks = ch // BLK
    rows_per_chunk = ch // 32

    @pl.when((b == 0) & (qi == 0))
    def _init_bias():
        for d in range(2):
            bk = bkt_ref[d]
            for h in range(ATTN_HEADS):
                far = relb_ref[N_BUCKETS - 1, h]
                acc = jnp.zeros((BLK, tq), F32)
                for n in range(N_BUCKETS - 1):
                    acc = jnp.where(bk == n, relb_ref[n, h] - far, acc)
                bias_s[d, h // rep, :, (h % rep) * BLK:(h % rep + 1) * BLK] = acc * LOG2E
        bias_s[2] = jnp.zeros((ATTN_KV_HEADS, BLK, gw), F32)
        for w in range(32):
            planes[w] = jnp.zeros(planes.shape[1:], i32)
        p_s[ZERO_SLOT] = jnp.zeros(p_s.shape[1:], BF16)

    row_c = lax.broadcasted_iota(i32, (ch, tq), 0)
    lane_c = lax.broadcasted_iota(i32, (ch, tq), 1)

    def chunk_start(c):
        return pl.multiple_of(c * ch, ch)

    def n_chunks(blk):
        return (blk + ch_blks) // ch_blks

    def load_indexer(qidx_ref, misc_ref):
        qicat[...] = qidx_ref[0, 0]
        w_t = jnp.transpose(misc_ref[0])[MISC_WIDX:MISC_WIDX + IDX_HEADS, :] * (
            IDX_HEADS ** -0.5 * IDX_DIM ** -0.5)
        for h in range(IDX_HEADS):
            wcat[:, h * BLK:(h + 1) * BLK] = jnp.broadcast_to(w_t[h:h + 1, :], (8, BLK))

    def score_chunk(c, blk, slot):
        s0 = chunk_start(c)
        kb = kidx_ref[0, pl.ds(s0, ch), MISC_KIDX:MISC_KIDX + IDX_DIM]
        s = jnp.dot(kb, qicat[...], preferred_element_type=F32)
        r = jnp.maximum(s, 0.0) * wcat[0:1, :]
        acc = r[:, 0:BLK]
        for h in range(1, IDX_HEADS):
            acc = acc + r[:, h * BLK:(h + 1) * BLK]
        acc = jnp.where(s0 + row_c <= blk * BLK + lane_c, acc, -jnp.inf)
        keys[slot, pl.ds(s0, ch), :] = _to_key(acc)

    def plane_chunk(c, slot):
        ku = (keys[slot, pl.ds(chunk_start(c), ch), :] ^ i32(INT_MIN)).reshape(ch // (32 * 8), 32, 8, tq)
        words = _transpose32([ku[:, j] for j in range(32)])
        r0 = pl.multiple_of(c * rows_per_chunk, rows_per_chunk)
        for w in range(32):
            planes[w, pl.ds(r0, rows_per_chunk), :] = words[w].reshape(rows_per_chunk, tq)

    def select(blk, slot):
        nch = n_chunks(blk)
        used = nch * rows_per_chunk

        def ones_in(words):
            return jnp.sum(lax.population_count(words), axis=0, keepdims=True)

        def passes(nr):
            live[0:nr] = jnp.where(lax.broadcasted_iota(i32, (nr, tq), 0) < used, i32(-1), i32(0))

            def bit_body(w, st):
                thr_u, n_above, n_set = st
                take = n_above + n_set >= topk
                thr_u = jnp.where(take, thr_u | lax.shift_left(i32(1), i32(31) - w), thr_u)
                n_above = jnp.where(take, n_above, n_above + n_set)
                still = live[0:nr] & (planes[w, 0:nr] ^ jnp.where(take, i32(0), i32(-1)))
                live[0:nr] = still
                return thr_u, n_above, ones_in(still & planes[jnp.minimum(w + 1, 31), 0:nr])

            zero = jnp.zeros((1, tq), i32)
            thr_u, n_above, _ = lax.fori_loop(
                0, 32, bit_body, (zero, zero, ones_in(live[0:nr] & planes[0, 0:nr])))
            thr_s[slot] = thr_u ^ i32(INT_MIN)
            cnt_s[0] = n_above
            cnt_s[1] = ones_in(live[0:nr])

        size_step = live.shape[0] // SEL_SIZES
        for v in range(SEL_SIZES):
            pl.when((used > v * size_step) & (used <= (v + 1) * size_step))(
                functools.partial(passes, (v + 1) * size_step))
        thr = thr_s[slot]
        n_gt = cnt_s[0]
        n_eq = cnt_s[1]
        need = topk - n_gt

        def break_ties():
            def count(pred):
                def body(c, cnt):
                    s0 = chunk_start(c)
                    hit = pred(keys[slot, pl.ds(s0, ch), :], s0).astype(i32)
                    n = ch
                    while n > 8:
                        n //= 2
                        hit = hit[:n] + hit[n:2 * n]
                    return cnt + hit
                cnt = lax.fori_loop(0, nch, body, jnp.zeros((8, tq), i32))
                return jnp.sum(cnt, axis=0, keepdims=True)

            def jbody(it, p):
                cand = p + lax.shift_left(i32(1), i32(idx_bits - 1) - it)
                c = count(lambda kk, s0: (kk == thr) & (s0 + row_c < cand))
                return jnp.where(c < need, cand, p)
            last = lax.fori_loop(0, idx_bits, jbody, jnp.zeros((1, tq), i32))

            def demote(c, carry):
                s0 = chunk_start(c)
                kk = keys[slot, pl.ds(s0, ch), :]
                keys[slot, pl.ds(s0, ch), :] = jnp.where((kk == thr) & (s0 + row_c > last), INT_MIN, kk)
                return carry
            lax.fori_loop(0, nch, demote, 0)

        return (n_eq > need).astype(i32), break_ties

    def any_lane(flags):
        return jnp.max(flags, axis=1, keepdims=True)

    @pl.when(qi == 0)
    def _first_block():
        load_indexer(qic_ref, miscc_ref)
        score_chunk(0, 0, 0)
        plane_chunk(0, 0)
        tied, break_ties = select(0, 0)
        pl.when(any_lane(tied)[0, 0] > 0)(break_ties)

    slot_b = qi % 2
    slot_a = 1 - slot_b
    blk_a = jnp.minimum(qi + 1, nq - 1)
    nch_b = n_chunks(qi)
    nch_a = n_chunks(blk_a)
    load_indexer(qin_ref, miscn_ref)
    thr = thr_s[slot_b]

    for g in range(ATTN_KV_HEADS):
        qcat[g] = q_ref[0, 0, g]

    def pen_of(s0, causal):
        sel = keys[slot_b, pl.ds(s0, ch), :] >= thr
        if causal:
            sel = sel & (s0 + row_c <= qi * BLK + lane_c)
        pen = jnp.where(sel, 0.0, NEG)
        return jnp.concatenate([pen] * rep, axis=1)

    def near_bias(c, g):
        return jnp.concatenate(
            [bias_s[jnp.clip(qi - (c * ch_blks + u), 0, 2), g] for u in range(ch_blks)], axis=0)

    def vt_chunk(c, g):
        return jnp.concatenate([vt_ref[0, c * ch_blks + u, g] for u in range(ch_blks)], axis=1)

    for g in range(ATTN_KV_HEADS):
        acc_s[g] = jnp.zeros((V_ROWS, gw), F32)

    def weigh_values(c, slot):
        for g in range(ATTN_KV_HEADS):
            acc_s[g] += jnp.dot(vt_chunk(c, g), p_s[slot, g], preferred_element_type=F32)

    def att_chunk(c, near, slots=None):
        prev_slot, cur_slot = slots if slots is not None else ((c + 1) % 2, c % 2)
        if not (isinstance(cur_slot, int) and cur_slot == 1):
            prev_slot = jnp.where(c == 0, ZERO_SLOT, prev_slot)
        weigh_values(jnp.maximum(c - 1, 0), prev_slot)
        s0 = chunk_start(c)
        pen = pen_of(s0, near)
        kblk = k_ref[0, pl.ds(s0, ch), :]
        for g in range(ATTN_KV_HEADS):
            s = jnp.dot(kblk[:, g * ATTN_HEAD_DIM:(g + 1) * ATTN_HEAD_DIM], qcat[g],
                        preferred_element_type=F32)
            if near:
                s = s + near_bias(c, g)
            p_s[cur_slot, g] = jnp.exp2(s + pen).astype(BF16)

    def fused(c, near, slots=None):
        plane_chunk(c, slot_a)
        att_chunk(c, near, slots)
        score_chunk(c + 1, blk_a, slot_a)

    def far_pair(j, carry):
        fused(2 * j, False, (1, 0))
        fused(2 * j + 1, False, (0, 1))
        return carry

    def far_quad(j, carry):
        far_pair(2 * j, carry)
        return far_pair(2 * j + 1, carry)

    def far_body(c, carry):
        fused(c, False, (1, 0))
        return carry

    def near_body(c, carry):
        fused(c, True)
        return carry

    score_chunk(0, blk_a, slot_a)
    n_far = jnp.maximum(qi - 1, 0) // ch_blks
    lax.fori_loop(0, n_far // 4, far_quad, 0)
    lax.fori_loop((n_far // 4) * 2, n_far // 2, far_pair, 0)
    lax.fori_loop((n_far // 2) * 2, n_far, far_body, 0)
    lax.fori_loop(n_far, nch_b - 1, near_body, 0)
    @pl.when(nch_a > nch_b)
    def _last_with_one_more_chunk():
        fused(nch_b - 1, True)
        plane_chunk(nch_b, slot_a)

    @pl.when(nch_a <= nch_b)
    def _last():
        plane_chunk(nch_b - 1, slot_a)
        att_chunk(nch_b - 1, True)

    weigh_values(nch_b - 1, (nch_b - 1) % 2)
    tied, break_ties = select(blk_a, slot_a)

    n_ok = jnp.zeros((1, gw), i32)
    for g in range(ATTN_KV_HEADS):
        a = acc_s[g]
        den = a[ATTN_HEAD_DIM:ATTN_HEAD_DIM + 1, :]
        top = jnp.max(jnp.abs(a[0:ATTN_HEAD_DIM, :]), axis=0, keepdims=True)
        n_ok = n_ok + ((den > DEN_LO) & (den < DEN_HI) & (top < F32_HUGE)).astype(i32)

    rare = (any_lane(tied) + 2 * any_lane((n_ok < ATTN_KV_HEADS).astype(i32)))[0, 0]
    pl.when(rare % 2 == 1)(break_ties)

    @pl.when(rare >= 2)
    def _exact():
        for g in range(ATTN_KV_HEADS):
            m_s[g] = jnp.full((1, gw), NEG, F32)
            acc_s[g] = jnp.zeros((V_ROWS, gw), F32)

        def body(c, carry):
            s0 = chunk_start(c)
            pen = pen_of(s0, True)
            kblk = k_ref[0, pl.ds(s0, ch), :]
            for g in range(ATTN_KV_HEADS):
                s = jnp.dot(kblk[:, g * ATTN_HEAD_DIM:(g + 1) * ATTN_HEAD_DIM], qcat[g],
                            preferred_element_type=F32) + near_bias(c, g) + pen
                m_old = m_s[g]
                m_new = jnp.maximum(m_old, jnp.max(s, axis=0, keepdims=True))
                p = jnp.exp2(s - m_new).astype(BF16)
                acc_s[g] = acc_s[g] * jnp.exp2(m_old - m_new) + jnp.dot(
                    vt_chunk(c, g), p, preferred_element_type=F32)
                m_s[g] = m_new
            return carry
        lax.fori_loop(0, nch_b, body, 0)

    parts = []
    for g in range(ATTN_KV_HEADS):
        a = acc_s[g]
        o = a[0:ATTN_HEAD_DIM, :] * (1.0 / a[ATTN_HEAD_DIM:ATTN_HEAD_DIM + 1, :])
        parts += [o[:, r * tq:(r + 1) * tq] for r in range(rep)]
    out_t = jnp.concatenate(parts, axis=0)
    inv = lax.rsqrt(jnp.mean(out_t * out_t, axis=0, keepdims=True) + EPS)
    o_ref[0, 0] = (out_t * inv * nw_ref[...]).astype(BF16)


def _dsa(rel_bias, q, qi, misc, k, vt, miscb, nw):
    bsz, t, _ = k.shape
    nq = t // BLK
    topk = min(TOPK_MAX, t // 4)
    idx_bits = max(1, (t - 1).bit_length())
    gw = (ATTN_HEADS // ATTN_KV_HEADS) * BLK
    bkt = jnp.asarray(_bucket_tiles())
    kern = functools.partial(_dsa_kernel, topk=topk, idx_bits=idx_bits, nq=nq)

    def nxt(b, i):
        return (b, jnp.minimum(i + 1, nq - 1), 0)

    return pl.pallas_call(
        kern,
        grid=(bsz, nq),
        in_specs=[pl.BlockSpec(memory_space=pltpu.SMEM),
                  pl.BlockSpec((2, BLK, BLK), lambda b, i: (0, 0, 0)),
                  pl.BlockSpec((1, 1, ATTN_KV_HEADS, ATTN_HEAD_DIM, gw), lambda b, i: (b, i, 0, 0, 0)),
                  pl.BlockSpec((1, 1, IDX_DIM, IDX_HEADS * BLK), lambda b, i: (b, i, 0, 0)),
                  pl.BlockSpec((1, BLK, MISC_W), lambda b, i: (b, i, 0)),
                  pl.BlockSpec((1, 1, IDX_DIM, IDX_HEADS * BLK), lambda b, i: nxt(b, i) + (0,)),
                  pl.BlockSpec((1, BLK, MISC_W), nxt),
                  pl.BlockSpec((1, t, D_KV), lambda b, i: (b, 0, 0)),
                  pl.BlockSpec((1, nq, ATTN_KV_HEADS, V_ROWS, BLK), lambda b, i: (b, 0, 0, 0, 0)),
                  pl.BlockSpec((1, t, MISC_W), lambda b, i: (b, 0, 0)),
                  pl.BlockSpec((D_ATTN, BLK), lambda b, i: (0, 0))],
        out_specs=pl.BlockSpec((1, 1, D_ATTN, BLK), lambda b, i: (b, i, 0, 0)),
        out_shape=jax.ShapeDtypeStruct((bsz, nq, D_ATTN, BLK), BF16),
        scratch_shapes=[pltpu.VMEM((2, t, BLK), jnp.int32),
                        pltpu.VMEM((32, t // 32, BLK), jnp.int32),
                        pltpu.VMEM((t // 32, BLK), jnp.int32),
                        pltpu.VMEM((2, 1, BLK), jnp.int32),
                        pltpu.VMEM((2, 1, BLK), jnp.int32),
                        pltpu.VMEM((3, ATTN_KV_HEADS, BLK, gw), F32),
                        pltpu.VMEM((ATTN_KV_HEADS, ATTN_HEAD_DIM, gw), BF16),
                        pltpu.VMEM((IDX_DIM, IDX_HEADS * BLK), BF16),
                        pltpu.VMEM((8, IDX_HEADS * BLK), F32),
                        pltpu.VMEM((ATTN_KV_HEADS, 1, gw), F32),
                        pltpu.VMEM((ATTN_KV_HEADS, V_ROWS, gw), F32),
                        pltpu.VMEM((ZERO_SLOT + 1, ATTN_KV_HEADS, DSA_CHUNK, gw), BF16)],
        compiler_params=pltpu.CompilerParams(
            dimension_semantics=("arbitrary", "arbitrary"), vmem_limit_bytes=VMEM_LIMIT),
        name="dsa",
    )(rel_bias, bkt, q, qi, misc, qi, misc, k, vt, miscb, nw)


def _tail_kernel(x_ref, ys_ref, ya_ref, mod_ref, wo_ref, n1_ref, n2_ref, n3_ref, w1_ref, w2_ref, o_ref):
    x = x_ref[0]
    ya = jnp.concatenate([jnp.transpose(ya_ref[0, j].astype(F32)).astype(BF16)
                          for j in range(ya_ref.shape[1])], axis=0)
    mix = (jnp.dot(ys_ref[0], wo_ref[0:D_SSM, :], preferred_element_type=F32)
           + jnp.dot(ya, wo_ref[D_SSM:, :], preferred_element_type=F32))
    x1 = x + mod_ref[0, 2:3, :] * _rms(mix, n1_ref[...])
    h = (_rms(x1, n2_ref[...]) * (1.0 + mod_ref[0, 4:5, :]) + mod_ref[0, 3:4, :]).astype(BF16)
    f = jnp.zeros(x.shape, F32)
    step = D_MODEL
    for c0 in range(0, D_FF, step):
        a = jnp.maximum(jnp.dot(h, w1_ref[:, c0:c0 + step], preferred_element_type=F32), 0.0)
        f = f + jnp.dot((a * a).astype(BF16), w2_ref[c0:c0 + step, :], preferred_element_type=F32)
    o_ref[0] = x1 + mod_ref[0, 5:6, :] * _rms(f, n3_ref[...])


def _tail(x, ys, ya, mod, wo, n1, n2, n3, w1, w2):
    bsz, t, d = x.shape
    tm = min(ROW_TILE, t)

    def const(shape):
        return pl.BlockSpec(shape, lambda b, i: tuple(0 for _ in shape))

    return pl.pallas_call(
        _tail_kernel,
        grid=(bsz, t // tm),
        in_specs=[pl.BlockSpec((1, tm, d), lambda b, i: (b, i, 0)),
                  pl.BlockSpec((1, tm, D_SSM), lambda b, i: (b, i, 0)),
                  pl.BlockSpec((1, tm // BLK, D_ATTN, BLK), lambda b, i: (b, i, 0, 0)),
                  pl.BlockSpec((1, 6, d), lambda b, i: (b, 0, 0)),
                  const((D_SSM + D_ATTN, d)), const((1, d)), const((1, d)), const((1, d)),
                  const((d, D_FF)), const((D_FF, d))],
        out_specs=pl.BlockSpec((1, tm, d), lambda b, i: (b, i, 0)),
        out_shape=jax.ShapeDtypeStruct((bsz, t, d), F32),
        compiler_params=pltpu.CompilerParams(
            dimension_semantics=("arbitrary", "arbitrary"), vmem_limit_bytes=VMEM_LIMIT_TAIL),
        name="tail",
    )(x, ys, ya, mod, wo, n1, n2, n3, w1, w2)


def _permute_w_in(w):
    sizes = [D_SSM, D_XBC, SSM_HEADS, D_ATTN, D_KV, D_KV, IDX_HEADS * IDX_DIM, IDX_DIM, IDX_HEADS]
    pts = np.cumsum(sizes)[:-1].tolist()
    z, xbc, dt, q, k, v, qidx, kidx, widx = jnp.split(w, pts, axis=-1)
    pad = jnp.zeros((w.shape[0], MISC_W - IDX_DIM - IDX_HEADS - SSM_HEADS), w.dtype)
    return jnp.concatenate([z, xbc, q, qidx, k, v, kidx, widx, dt, pad], axis=-1).astype(BF16)


def _layer(x, c_pad, w_ada, b_ada, norm1_pre, norm1_post, w_in, conv_w, conv_b, dt_bias, a_log, d_skip,
           ssm_norm, rel_bias, attn_norm, w_out, norm2_pre, norm2_post, w_mlp_in, w_mlp_out):
    bsz, t, d = x.shape
    mod = _ada(c_pad, w_ada, b_ada[None, :])[:bsz].reshape(bsz, 6, d)
    z, xbc, q, k, vt, qi, misc, miscb = _inproj(x, mod, norm1_pre[None, :], _permute_w_in(w_in))

    lane = jnp.arange(MISC_W)
    is_dt = (lane >= MISC_DT) & (lane < MISC_DT + SSM_HEADS)
    head_of = jnp.clip(lane - MISC_DT, 0, SSM_HEADS - 1)
    dtb_t = jnp.where(is_dt, dt_bias[head_of], 0.0)[None, :]
    a_t = jnp.where(is_dt, (-jnp.exp(a_log.astype(F32)) * LOG2E)[head_of], 0.0)[None, :]
    expand = ((lane[:, None] - MISC_DT) == (jnp.arange(D_SSM)[None, :] // SSM_HEAD_DIM)).astype(BF16)
    dskip_e = jnp.repeat(d_skip.astype(F32), SSM_HEAD_DIM)[None, :]
    y_ssd = _ssd(xbc, z, misc, conv_w, conv_b[None, :], dtb_t, a_t, expand, dskip_e, ssm_norm[None, :])

    attn_gain = jnp.broadcast_to(attn_norm.astype(F32)[:, None], (D_ATTN, BLK))
    y_att = _dsa(rel_bias, q, qi, misc, k, vt, miscb, attn_gain)

    return _tail(x, y_ssd, y_att, mod, w_out.astype(BF16), norm1_post[None, :], norm2_pre[None, :],
                 norm2_post[None, :], w_mlp_in.astype(BF16), w_mlp_out.astype(BF16))


def kernel(x, c, w_ada, b_ada, norm1_pre, norm1_post, w_in, conv_w, conv_b, dt_bias, a_log, d_skip,
           ssm_norm, rel_bias, attn_norm, w_out, norm2_pre, norm2_post, w_mlp_in, w_mlp_out):
    bsz = x.shape[0]
    assert bsz <= 8 and x.shape[1] % DSA_CHUNK == 0 and x.shape[2] == D_MODEL
    c_pad = jnp.zeros((8, D_MODEL), F32).at[:bsz].set(c)
    for l in range(w_ada.shape[0]):
        x = _layer(x, c_pad, w_ada[l], b_ada[l], norm1_pre[l], norm1_post[l], w_in[l], conv_w[l], conv_b[l],
                   dt_bias[l], a_log[l], d_skip[l], ssm_norm[l], rel_bias, attn_norm[l], w_out[l],
                   norm2_pre[l], norm2_post[l], w_mlp_in[l], w_mlp_out[l])
    return x
```

```python
import functools
import math

import numpy as np
import jax
import jax.numpy as jnp
from jax import lax
from jax.experimental import pallas as pl
from jax.experimental.pallas import tpu as pltpu

D_MODEL = 1024
SSM_HEADS = 8
SSM_HEAD_DIM = 64
D_SSM = SSM_HEADS * SSM_HEAD_DIM
SSM_GROUPS = 2
SSM_STATE = 128
CONV_WIDTH = 4
CHUNK = 256
ATTN_HEADS = 8
ATTN_KV_HEADS = 2
ATTN_HEAD_DIM = 64
D_ATTN = ATTN_HEADS * ATTN_HEAD_DIM
D_KV = ATTN_KV_HEADS * ATTN_HEAD_DIM
IDX_HEADS = 8
IDX_DIM = 64
TOPK_MAX = 256
N_BUCKETS = 32
MAX_DISTANCE = 128
D_FF = 4 * D_MODEL
D_BC = SSM_GROUPS * SSM_STATE
D_XBC = D_SSM + 2 * D_BC
EPS = 1e-6

LANES = 128
V7X_VMEM_BYTES = 64 * 2 ** 20
VMEM_LIMIT = V7X_VMEM_BYTES * 3 // 4
VMEM_LIMIT_TAIL = V7X_VMEM_BYTES * 15 // 16
MISC_W = LANES
MISC_KIDX = 0
MISC_WIDX = IDX_DIM
MISC_DT = IDX_DIM + IDX_HEADS
OFF_Z = 0
OFF_XBC = OFF_Z + D_SSM
OFF_Q = OFF_XBC + D_XBC
OFF_QI = OFF_Q + D_ATTN
OFF_K = OFF_QI + IDX_HEADS * IDX_DIM
OFF_V = OFF_K + D_KV
OFF_MISC = OFF_V + D_KV
D_IN_PAD = OFF_MISC + MISC_W

BLK = 128
DSA_CHUNK = 512
V_ROWS = 80
ZERO_SLOT = 2
SEL_SIZES = 8
DEN_LO = 2.0 ** -80
DEN_HI = 2.0 ** 80
F32_HUGE = 3.0e38
LOG2E = math.log2(math.e)
ROW_TILE = 512
NEG = -1e30
INT_MIN = -2 ** 31

F32 = jnp.float32
BF16 = jnp.bfloat16
HI = lax.Precision.HIGHEST
NT = (((1,), (1,)), ((), ()))


def _silu(x):
    return x / (1.0 + jnp.exp(-x))


def _rms(x, w):
    return x * lax.rsqrt(jnp.mean(x * x, axis=-1, keepdims=True) + EPS) * w


def _split3(x):
    hi = x.astype(BF16)
    rest = x - hi.astype(F32)
    mid = rest.astype(BF16)
    return hi, mid, (rest - mid.astype(F32)).astype(BF16)


def _ada_kernel(c_ref, w_ref, b_ref, o_ref):
    s = _silu(c_ref[...])
    o_ref[...] = jnp.dot(s, w_ref[...], precision=HI, preferred_element_type=F32) + b_ref[...]


def _ada(c_pad, w, b):
    n = w.shape[1]
    tn = D_MODEL
    return pl.pallas_call(
        _ada_kernel,
        grid=(n // tn,),
        in_specs=[pl.BlockSpec((8, D_MODEL), lambda j: (0, 0)),
                  pl.BlockSpec((D_MODEL, tn), lambda j: (0, j)),
                  pl.BlockSpec((1, tn), lambda j: (0, j))],
        out_specs=pl.BlockSpec((8, tn), lambda j: (0, j)),
        out_shape=jax.ShapeDtypeStruct((8, n), F32),
        name="ada",
    )(c_pad, w, b)


def _inproj_kernel(x_ref, mod_ref, nw_ref, w_ref,
                   z_ref, xbc_ref, q_ref, k_ref, vt_ref, qi_ref, misc_ref, miscb_ref):
    x = x_ref[0]
    h = _rms(x, nw_ref[...]) * (1.0 + mod_ref[0, 1:2, :]) + mod_ref[0, 0:1, :]
    hb = h.astype(BF16)

    def seg(lo, width):
        return jnp.dot(hb, w_ref[:, lo:lo + width], preferred_element_type=F32)

    z_ref[0] = seg(OFF_Z, D_SSM)
    xbc_ref[0] = seg(OFF_XBC, D_XBC)
    rep = ATTN_HEADS // ATTN_KV_HEADS
    q_all = seg(OFF_Q, D_ATTN) * (ATTN_HEAD_DIM ** -0.5 * LOG2E)
    qi_all = seg(OFF_QI, IDX_HEADS * IDX_DIM)
    for j in range(x.shape[0] // BLK):
        q_t = jnp.transpose(q_all[j * BLK:(j + 1) * BLK, :])
        qi_t = jnp.transpose(qi_all[j * BLK:(j + 1) * BLK, :])
        for h in range(ATTN_HEADS):
            q_ref[0, j, h // rep, :, (h % rep) * BLK:(h % rep + 1) * BLK] = (
                q_t[h * ATTN_HEAD_DIM:(h + 1) * ATTN_HEAD_DIM, :].astype(BF16))
        for h in range(IDX_HEADS):
            qi_ref[0, j, :, h * BLK:(h + 1) * BLK] = qi_t[h * IDX_DIM:(h + 1) * IDX_DIM, :].astype(BF16)
    k_v_misc = seg(OFF_K, 2 * D_KV + MISC_W)
    k_ref[0] = k_v_misc[:, :D_KV].astype(BF16)
    v_misc = k_v_misc[:, D_KV:]
    ones_tile = jnp.where(lax.broadcasted_iota(jnp.int32, (V_ROWS - ATTN_HEAD_DIM, BLK), 0) == 0, 1.0, 0.0)
    for j in range(x.shape[0] // BLK):
        v_t = jnp.transpose(v_misc[j * BLK:(j + 1) * BLK, :D_KV])
        for g in range(ATTN_KV_HEADS):
            vt_ref[0, j, g, 0:ATTN_HEAD_DIM, :] = v_t[g * ATTN_HEAD_DIM:(g + 1) * ATTN_HEAD_DIM, :].astype(BF16)
            vt_ref[0, j, g, ATTN_HEAD_DIM:, :] = ones_tile.astype(BF16)
    misc = v_misc[:, D_KV:]
    misc_ref[0] = misc
    miscb_ref[0] = misc.astype(BF16)


def _inproj(x, mod, nw, w_perm):
    bsz, t, d = x.shape
    tm = min(ROW_TILE, t)

    def tok(width, dtype):
        return (pl.BlockSpec((1, tm, width), lambda b, i: (b, i, 0)),
                jax.ShapeDtypeStruct((bsz, t, width), dtype))

    vt = (pl.BlockSpec((1, tm // BLK, ATTN_KV_HEADS, V_ROWS, BLK), lambda b, i: (b, i, 0, 0, 0)),
          jax.ShapeDtypeStruct((bsz, t // BLK, ATTN_KV_HEADS, V_ROWS, BLK), BF16))
    gw = (ATTN_HEADS // ATTN_KV_HEADS) * BLK
    qt = (pl.BlockSpec((1, tm // BLK, ATTN_KV_HEADS, ATTN_HEAD_DIM, gw), lambda b, i: (b, i, 0, 0, 0)),
          jax.ShapeDtypeStruct((bsz, t // BLK, ATTN_KV_HEADS, ATTN_HEAD_DIM, gw), BF16))
    qit = (pl.BlockSpec((1, tm // BLK, IDX_DIM, IDX_HEADS * BLK), lambda b, i: (b, i, 0, 0)),
           jax.ShapeDtypeStruct((bsz, t // BLK, IDX_DIM, IDX_HEADS * BLK), BF16))
    outs = [tok(D_SSM, F32), tok(D_XBC, F32), qt, tok(D_KV, BF16), vt, qit,
            tok(MISC_W, F32), tok(MISC_W, BF16)]
    return pl.pallas_call(
        _inproj_kernel,
        grid=(bsz, t // tm),
        in_specs=[pl.BlockSpec((1, tm, d), lambda b, i: (b, i, 0)),
                  pl.BlockSpec((1, 6, d), lambda b, i: (b, 0, 0)),
                  pl.BlockSpec((1, d), lambda b, i: (0, 0)),
                  pl.BlockSpec((d, D_IN_PAD), lambda b, i: (0, 0))],
        out_specs=[o[0] for o in outs],
        out_shape=[o[1] for o in outs],
        compiler_params=pltpu.CompilerParams(
            dimension_semantics=("arbitrary", "arbitrary"), vmem_limit_bytes=VMEM_LIMIT),
        name="inproj",
    )(x, mod, nw, w_perm)


def _ssd_kernel(xbc_ref, z_ref, misc_ref, cw_ref, cb_ref, dtb_ref, a_ref, e_ref, dskip_ref, nw_ref,
                o_ref, xbuf, state):
    c = pl.program_id(1)
    lc = xbc_ref.shape[1]

    @pl.when(c == 0)
    def _():
        xbuf[0:8, :] = jnp.zeros((8, D_XBC), F32)
        state[...] = jnp.zeros(state.shape, F32)

    @pl.when(c > 0)
    def _():
        xbuf[0:8, :] = xbuf[lc:lc + 8, :]

    xbuf[8:lc + 8, :] = xbc_ref[0]
    u = cb_ref[...]
    xall = xbuf[...]
    for k in range(CONV_WIDTH):
        back = CONV_WIDTH - 1 - k
        xk = xall if back == 0 else pltpu.roll(xall, back, 0)
        u = u + xk[8:8 + lc] * cw_ref[k:k + 1, :]
    u = _silu(u)
    xs = u[:, :D_SSM]
    bm = u[:, D_SSM:D_SSM + D_BC]
    cm = u[:, D_SSM + D_BC:]

    raw = misc_ref[0] + dtb_ref[...]
    dt_t = jnp.maximum(raw, 0.0) + jnp.log1p(jnp.exp(-jnp.abs(raw)))
    adt_t = dt_t * a_ref[...]
    row = lax.broadcasted_iota(jnp.int32, (lc, lc), 0)
    col = lax.broadcasted_iota(jnp.int32, (lc, lc), 1)
    causal = col <= row
    tril = jnp.where(causal, 1.0, 0.0).astype(BF16)
    expand = e_ref[...]
    acs_t = sum(jnp.dot(tril, piece, preferred_element_type=F32) for piece in _split3(adt_t))
    dt_e = sum(jnp.dot(piece, expand, preferred_element_type=F32) for piece in _split3(dt_t))
    acs_e = sum(jnp.dot(piece, expand, preferred_element_type=F32) for piece in _split3(acs_t))
    acs_row = jnp.transpose(acs_t)

    xdt = xs * dt_e
    a_last = acs_e[lc - 1:lc, :]
    xdec = (xdt * jnp.exp2(a_last - acs_e)).astype(BF16)
    xdt_b = xdt.astype(BF16)
    bm_b = bm.astype(BF16)
    cm_b = cm.astype(BF16)
    bm_t = jnp.transpose(bm).astype(BF16)

    heads_per_group = SSM_HEADS // SSM_GROUPS
    gw = heads_per_group * SSM_HEAD_DIM
    y_parts = []
    off_parts = []
    for g in range(SSM_GROUPS):
        bg = bm_b[:, g * SSM_STATE:(g + 1) * SSM_STATE]
        cg = cm_b[:, g * SSM_STATE:(g + 1) * SSM_STATE]
        cb = lax.dot_general(cg, bg, NT, preferred_element_type=F32)
        for r in range(heads_per_group):
            hd = g * heads_per_group + r
            a_col = acs_t[:, MISC_DT + hd:MISC_DT + hd + 1]
            a_row = acs_row[MISC_DT + hd:MISC_DT + hd + 1, :]
            lmat = jnp.exp2(jnp.where(causal, a_col - a_row, -jnp.inf))
            mh = (cb * lmat).astype(BF16)
            y_parts.append(jnp.dot(mh, xdt_b[:, hd * SSM_HEAD_DIM:(hd + 1) * SSM_HEAD_DIM],
                                   preferred_element_type=F32))
        prev = state[g]
        off_parts.append(jnp.dot(cg, prev.astype(BF16), preferred_element_type=F32))
        st = jnp.dot(bm_t[g * SSM_STATE:(g + 1) * SSM_STATE, :], xdec[:, g * gw:(g + 1) * gw],
                     preferred_element_type=F32)
        state[g] = prev * jnp.exp2(a_last[:, g * gw:(g + 1) * gw]) + st
    y = (jnp.concatenate(y_parts, axis=1)
         + jnp.concatenate(off_parts, axis=1) * jnp.exp2(acs_e)
         + xs * dskip_ref[...])
    o_ref[0] = _rms(y * _silu(z_ref[0]), nw_ref[...]).astype(BF16)


def _ssd(xbc, z, misc, cw, cb, dtb_t, a_t, expand, dskip_e, nw):
    bsz, t, _ = xbc.shape
    lc = math.gcd(t, CHUNK)
    nc = t // lc
    gw = (SSM_HEADS // SSM_GROUPS) * SSM_HEAD_DIM

    def const(shape):
        return pl.BlockSpec(shape, lambda b, c: tuple(0 for _ in shape))

    return pl.pallas_call(
        _ssd_kernel,
        grid=(bsz, nc),
        in_specs=[pl.BlockSpec((1, lc, D_XBC), lambda b, c: (b, c, 0)),
                  pl.BlockSpec((1, lc, D_SSM), lambda b, c: (b, c, 0)),
                  pl.BlockSpec((1, lc, MISC_W), lambda b, c: (b, c, 0)),
                  const((CONV_WIDTH, D_XBC)), const((1, D_XBC)), const((1, MISC_W)), const((1, MISC_W)),
                  const((MISC_W, D_SSM)), const((1, D_SSM)), const((1, D_SSM))],
        out_specs=pl.BlockSpec((1, lc, D_SSM), lambda b, c: (b, c, 0)),
        out_shape=jax.ShapeDtypeStruct((bsz, t, D_SSM), BF16),
        scratch_shapes=[pltpu.VMEM((lc + 8, D_XBC), F32),
                        pltpu.VMEM((SSM_GROUPS, SSM_STATE, gw), F32)],
        compiler_params=pltpu.CompilerParams(
            dimension_semantics=("arbitrary", "arbitrary"), vmem_limit_bytes=VMEM_LIMIT),
        name="ssd",
    )(xbc, z, misc, cw, cb, dtb_t, a_t, expand, dskip_e, nw)


def _bucket_tiles():
    s = np.arange(BLK)[:, None]
    t = np.arange(BLK)[None, :]
    max_exact = N_BUCKETS // 2
    tiles = []
    for d0 in (0, BLK):
        n = np.maximum(d0 + t - s, 0)
        large = {}
        for dt_ in (np.float32, np.float64):
            nf = np.maximum(n, 1).astype(dt_)
            lg = max_exact + (np.log(nf / dt_(max_exact)) / dt_(math.log(MAX_DISTANCE / max_exact))
                              * dt_(N_BUCKETS - max_exact)).astype(np.int32)
            large[dt_] = np.minimum(lg, N_BUCKETS - 1)
        assert (large[np.float32] == large[np.float64]).all()
        tiles.append(np.where(n < max_exact, n, large[np.float64]).astype(np.int32))
    assert BLK + 1 >= MAX_DISTANCE
    return np.stack(tiles)


def _to_key(x):
    bits = lax.bitcast_convert_type(x, jnp.int32)
    key = bits ^ (lax.shift_right_arithmetic(bits, 31) & jnp.int32(0x7FFFFFFF))
    return jnp.where(key == -1, 0, key)


def _transpose32(rows):
    rows = list(rows)
    mask, j = 0x0000FFFF, 16
    while j:
        k = 0
        while k < 32:
            t = (rows[k] ^ lax.shift_right_logical(rows[k + j], jnp.int32(j))) & jnp.int32(mask)
            rows[k] = rows[k] ^ t
            rows[k + j] = rows[k + j] ^ lax.shift_left(t, jnp.int32(j))
            k = (k + j + 1) & ~j
        j >>= 1
        mask ^= (mask << j) & 0xFFFFFFFF
    return rows


def _dsa_kernel(relb_ref, bkt_ref, q_ref, qic_ref, miscc_ref, qin_ref, miscn_ref, k_ref, vt_ref, kidx_ref,
                nw_ref, o_ref, keys, planes, live, thr_s, cnt_s, bias_s, qcat, qicat, wcat, m_s, acc_s, p_s,
                *, topk, idx_bits, nq):
    b = pl.program_id(0)
    qi = pl.program_id(1)
    tq = BLK
    i32 = jnp.int32
    rep = ATTN_HEADS // ATTN_KV_HEADS
    gw = rep * tq
    ch = DSA_CHUNK
    ch_blks = ch // BLK
    rows_per_chunk = ch // 32

    @pl.when((b == 0) & (qi == 0))
    def _init_bias():
        for d in range(2):
            bk = bkt_ref[d]
            for h in range(ATTN_HEADS):
                far = relb_ref[N_BUCKETS - 1, h]
                acc = jnp.zeros((BLK, tq), F32)
                for n in range(N_BUCKETS - 1):
                    acc = jnp.where(bk == n, relb_ref[n, h] - far, acc)
                bias_s[d, h // rep, :, (h % rep) * BLK:(h % rep + 1) * BLK] = acc * LOG2E
        bias_s[2] = jnp.zeros((ATTN_KV_HEADS, BLK, gw), F32)
        for w in range(32):
            planes[w] = jnp.zeros(planes.shape[1:], i32)
        p_s[ZERO_SLOT] = jnp.zeros(p_s.shape[1:], BF16)

    row_c = lax.broadcasted_iota(i32, (ch, tq), 0)
    lane_c = lax.broadcasted_iota(i32, (ch, tq), 1)

    def chunk_start(c):
        return pl.multiple_of(c * ch, ch)

    def n_chunks(blk):
        return (blk + ch_blks) // ch_blks

    def load_indexer(qidx_ref, misc_ref):
        qicat[...] = qidx_ref[0, 0]
        w_t = jnp.transpose(misc_ref[0])[MISC_WIDX:MISC_WIDX + IDX_HEADS, :] * (
            IDX_HEADS ** -0.5 * IDX_DIM ** -0.5)
        for h in range(IDX_HEADS):
            wcat[:, h * BLK:(h + 1) * BLK] = jnp.broadcast_to(w_t[h:h + 1, :], (8, BLK))

    def score_chunk(c, blk, slot):
        s0 = chunk_start(c)
        kb = kidx_ref[0, pl.ds(s0, ch), MISC_KIDX:MISC_KIDX + IDX_DIM]
        s = jnp.dot(kb, qicat[...], preferred_element_type=F32)
        r = jnp.maximum(s, 0.0) * wcat[0:1, :]
        acc = r[:, 0:BLK]
        for h in range(1, IDX_HEADS):
            acc = acc + r[:, h * BLK:(h + 1) * BLK]
        acc = jnp.where(s0 + row_c <= blk * BLK + lane_c, acc, -jnp.inf)
        keys[slot, pl.ds(s0, ch), :] = _to_key(acc)

    def plane_chunk(c, slot):
        ku = (keys[slot, pl.ds(chunk_start(c), ch), :] ^ i32(INT_MIN)).reshape(ch // (32 * 8), 32, 8, tq)
        words = _transpose32([ku[:, j] for j in range(32)])
        r0 = pl.multiple_of(c * rows_per_chunk, rows_per_chunk)
        for w in range(32):
            planes[w, pl.ds(r0, rows_per_chunk), :] = words[w].reshape(rows_per_chunk, tq)

    def select(blk, slot):
        nch = n_chunks(blk)
        used = nch * rows_per_chunk

        def ones_in(words):
            return jnp.sum(lax.population_count(words), axis=0, keepdims=True)

        def passes(nr):
            live[0:nr] = jnp.where(lax.broadcasted_iota(i32, (nr, tq), 0) < used, i32(-1), i32(0))

            def bit_body(w, st):
                thr_u, n_above, n_set = st
                take = n_above + n_set >= topk
                thr_u = jnp.where(take, thr_u | lax.shift_left(i32(1), i32(31) - w), thr_u)
                n_above = jnp.where(take, n_above, n_above + n_set)
                still = live[0:nr] & (planes[w, 0:nr] ^ jnp.where(take, i32(0), i32(-1)))
                live[0:nr] = still
                return thr_u, n_above, ones_in(still & planes[jnp.minimum(w + 1, 31), 0:nr])

            zero = jnp.zeros((1, tq), i32)
            thr_u, n_above, _ = lax.fori_loop(
                0, 32, bit_body, (zero, zero, ones_in(live[0:nr] & planes[0, 0:nr])))
            thr_s[slot] = thr_u ^ i32(INT_MIN)
            cnt_s[0] = n_above
            cnt_s[1] = ones_in(live[0:nr])

        n_sizes = max(1, min(SEL_SIZES, live.shape[0] // 8))
        size_step = live.shape[0] // n_sizes
        for v in range(n_sizes):
            pl.when((used > v * size_step) & (used <= (v + 1) * size_step))(
                functools.partial(passes, (v + 1) * size_step))
        thr = thr_s[slot]
        n_gt = cnt_s[0]
        n_eq = cnt_s[1]
        need = topk - n_gt

        def break_ties():
            def count(pred):
                def body(c, cnt):
                    s0 = chunk_start(c)
                    hit = pred(keys[slot, pl.ds(s0, ch), :], s0).astype(i32)
                    n = ch
                    while n > 8:
                        n //= 2
                        hit = hit[:n] + hit[n:2 * n]
                    return cnt + hit
                cnt = lax.fori_loop(0, nch, body, jnp.zeros((8, tq), i32))
                return jnp.sum(cnt, axis=0, keepdims=True)

            def jbody(it, p):
                cand = p + lax.shift_left(i32(1), i32(idx_bits - 1) - it)
                c = count(lambda kk, s0: (kk == thr) & (s0 + row_c < cand))
                return jnp.where(c < need, cand, p)
            last = lax.fori_loop(0, idx_bits, jbody, jnp.zeros((1, tq), i32))

            def demote(c, carry):
                s0 = chunk_start(c)
                kk = keys[slot, pl.ds(s0, ch), :]
                keys[slot, pl.ds(s0, ch), :] = jnp.where((kk == thr) & (s0 + row_c > last), INT_MIN, kk)
                return carry
            lax.fori_loop(0, nch, demote, 0)

        return (n_eq > need).astype(i32), break_ties

    def any_lane(flags):
        return jnp.max(flags, axis=1, keepdims=True)

    @pl.when(qi == 0)
    def _first_block():
        load_indexer(qic_ref, miscc_ref)
        score_chunk(0, 0, 0)
        plane_chunk(0, 0)
        tied, break_ties = select(0, 0)
        pl.when(any_lane(tied)[0, 0] > 0)(break_ties)

    slot_b = qi % 2
    slot_a = 1 - slot_b
    blk_a = jnp.minimum(qi + 1, nq - 1)
    nch_b = n_chunks(qi)
    nch_a = n_chunks(blk_a)
    load_indexer(qin_ref, miscn_ref)
    thr = thr_s[slot_b]

    for g in range(ATTN_KV_HEADS):
        qcat[g] = q_ref[0, 0, g]

    def pen_of(s0, causal):
        sel = keys[slot_b, pl.ds(s0, ch), :] >= thr
        if causal:
            sel = sel & (s0 + row_c <= qi * BLK + lane_c)
        pen = jnp.where(sel, 0.0, NEG)
        return jnp.concatenate([pen] * rep, axis=1)

    def near_bias(c, g):
        return jnp.concatenate(
            [bias_s[jnp.clip(qi - (c * ch_blks + u), 0, 2), g] for u in range(ch_blks)], axis=0)

    def vt_chunk(c, g):
        return jnp.concatenate([vt_ref[0, c * ch_blks + u, g] for u in range(ch_blks)], axis=1)

    for g in range(ATTN_KV_HEADS):
        acc_s[g] = jnp.zeros((V_ROWS, gw), F32)

    def weigh_values(c, slot):
        for g in range(ATTN_KV_HEADS):
            acc_s[g] += jnp.dot(vt_chunk(c, g), p_s[slot, g], preferred_element_type=F32)

    def att_chunk(c, near, slots=None):
        prev_slot, cur_slot = slots if slots is not None else ((c + 1) % 2, c % 2)
        if not (isinstance(cur_slot, int) and cur_slot == 1):
            prev_slot = jnp.where(c == 0, ZERO_SLOT, prev_slot)
        weigh_values(jnp.maximum(c - 1, 0), prev_slot)
        s0 = chunk_start(c)
        pen = pen_of(s0, near)
        kblk = k_ref[0, pl.ds(s0, ch), :]
        for g in range(ATTN_KV_HEADS):
            s = jnp.dot(kblk[:, g * ATTN_HEAD_DIM:(g + 1) * ATTN_HEAD_DIM], qcat[g],
                        preferred_element_type=F32)
            if near:
                s = s + near_bias(c, g)
            p_s[cur_slot, g] = jnp.exp2(s + pen).astype(BF16)

    def fused(c, near, slots=None):
        plane_chunk(c, slot_a)
        att_chunk(c, near, slots)
        score_chunk(c + 1, blk_a, slot_a)

    def far_pair(j, carry):
        fused(2 * j, False, (1, 0))
        fused(2 * j + 1, False, (0, 1))
        return carry

    def far_quad(j, carry):
        far_pair(2 * j, carry)
        return far_pair(2 * j + 1, carry)

    def far_body(c, carry):
        fused(c, False, (1, 0))
        return carry

    def near_body(c, carry):
        fused(c, True)
        return carry

    score_chunk(0, blk_a, slot_a)
    n_far = jnp.maximum(qi - 1, 0) // ch_blks
    lax.fori_loop(0, n_far // 4, far_quad, 0)
    lax.fori_loop((n_far // 4) * 2, n_far // 2, far_pair, 0)
    lax.fori_loop((n_far // 2) * 2, n_far, far_body, 0)
    lax.fori_loop(n_far, nch_b - 1, near_body, 0)
    @pl.when(nch_a > nch_b)
    def _one_more_chunk():
        score_chunk(nch_b, blk_a, slot_a)
        plane_chunk(nch_b, slot_a)

    plane_chunk(nch_b - 1, slot_a)
    att_chunk(nch_b - 1, True)
    weigh_values(nch_b - 1, (nch_b - 1) % 2)
    tied, break_ties = select(blk_a, slot_a)

    n_ok = jnp.zeros((1, gw), i32)
    for g in range(ATTN_KV_HEADS):
        a = acc_s[g]
        den = a[ATTN_HEAD_DIM:ATTN_HEAD_DIM + 1, :]
        top = jnp.max(jnp.abs(a[0:ATTN_HEAD_DIM, :]), axis=0, keepdims=True)
        n_ok = n_ok + ((den > DEN_LO) & (den < DEN_HI) & (top < F32_HUGE)).astype(i32)

    rare = (any_lane(tied) + 2 * any_lane((n_ok < ATTN_KV_HEADS).astype(i32)))[0, 0]
    pl.when(rare % 2 == 1)(break_ties)

    @pl.when(rare >= 2)
    def _exact():
        for g in range(ATTN_KV_HEADS):
            m_s[g] = jnp.full((1, gw), NEG, F32)
            acc_s[g] = jnp.zeros((V_ROWS, gw), F32)

        def body(c, carry):
            s0 = chunk_start(c)
            pen = pen_of(s0, True)
            kblk = k_ref[0, pl.ds(s0, ch), :]
            for g in range(ATTN_KV_HEADS):
                s = jnp.dot(kblk[:, g * ATTN_HEAD_DIM:(g + 1) * ATTN_HEAD_DIM], qcat[g],
                            preferred_element_type=F32) + near_bias(c, g) + pen
                m_old = m_s[g]
                m_new = jnp.maximum(m_old, jnp.max(s, axis=0, keepdims=True))
                p = jnp.exp2(s - m_new).astype(BF16)
                acc_s[g] = acc_s[g] * jnp.exp2(m_old - m_new) + jnp.dot(
                    vt_chunk(c, g), p, preferred_element_type=F32)
                m_s[g] = m_new
            return carry
        lax.fori_loop(0, nch_b, body, 0)

    parts = []
    for g in range(ATTN_KV_HEADS):
        a = acc_s[g]
        o = a[0:ATTN_HEAD_DIM, :] * (1.0 / a[ATTN_HEAD_DIM:ATTN_HEAD_DIM + 1, :])
        parts += [o[:, r * tq:(r + 1) * tq] for r in range(rep)]
    out_t = jnp.concatenate(parts, axis=0)
    inv = lax.rsqrt(jnp.mean(out_t * out_t, axis=0, keepdims=True) + EPS)
    o_ref[0, 0] = (out_t * inv * nw_ref[...]).astype(BF16)


def _dsa(rel_bias, q, qi, misc, k, vt, miscb, nw):
    bsz, t, _ = k.shape
    nq = t // BLK
    topk = min(TOPK_MAX, t // 4)
    idx_bits = max(1, (t - 1).bit_length())
    gw = (ATTN_HEADS // ATTN_KV_HEADS) * BLK
    bkt = jnp.asarray(_bucket_tiles())
    kern = functools.partial(_dsa_kernel, topk=topk, idx_bits=idx_bits, nq=nq)

    def nxt(b, i):
        return (b, jnp.minimum(i + 1, nq - 1), 0)

    return pl.pallas_call(
        kern,
        grid=(bsz, nq),
        in_specs=[pl.BlockSpec(memory_space=pltpu.SMEM),
                  pl.BlockSpec((2, BLK, BLK), lambda b, i: (0, 0, 0)),
                  pl.BlockSpec((1, 1, ATTN_KV_HEADS, ATTN_HEAD_DIM, gw), lambda b, i: (b, i, 0, 0, 0)),
                  pl.BlockSpec((1, 1, IDX_DIM, IDX_HEADS * BLK), lambda b, i: (b, i, 0, 0)),
                  pl.BlockSpec((1, BLK, MISC_W), lambda b, i: (b, i, 0)),
                  pl.BlockSpec((1, 1, IDX_DIM, IDX_HEADS * BLK), lambda b, i: nxt(b, i) + (0,)),
                  pl.BlockSpec((1, BLK, MISC_W), nxt),
                  pl.BlockSpec((1, t, D_KV), lambda b, i: (b, 0, 0)),
                  pl.BlockSpec((1, nq, ATTN_KV_HEADS, V_ROWS, BLK), lambda b, i: (b, 0, 0, 0, 0)),
                  pl.BlockSpec((1, t, MISC_W), lambda b, i: (b, 0, 0)),
                  pl.BlockSpec((D_ATTN, BLK), lambda b, i: (0, 0))],
        out_specs=pl.BlockSpec((1, 1, D_ATTN, BLK), lambda b, i: (b, i, 0, 0)),
        out_shape=jax.ShapeDtypeStruct((bsz, nq, D_ATTN, BLK), BF16),
        scratch_shapes=[pltpu.VMEM((2, t, BLK), jnp.int32),
                        pltpu.VMEM((32, t // 32, BLK), jnp.int32),
                        pltpu.VMEM((t // 32, BLK), jnp.int32),
                        pltpu.VMEM((2, 1, BLK), jnp.int32),
                        pltpu.VMEM((2, 1, BLK), jnp.int32),
                        pltpu.VMEM((3, ATTN_KV_HEADS, BLK, gw), F32),
                        pltpu.VMEM((ATTN_KV_HEADS, ATTN_HEAD_DIM, gw), BF16),
                        pltpu.VMEM((IDX_DIM, IDX_HEADS * BLK), BF16),
                        pltpu.VMEM((8, IDX_HEADS * BLK), F32),
                        pltpu.VMEM((ATTN_KV_HEADS, 1, gw), F32),
                        pltpu.VMEM((ATTN_KV_HEADS, V_ROWS, gw), F32),
                        pltpu.VMEM((ZERO_SLOT + 1, ATTN_KV_HEADS, DSA_CHUNK, gw), BF16)],
        compiler_params=pltpu.CompilerParams(
            dimension_semantics=("arbitrary", "arbitrary"), vmem_limit_bytes=VMEM_LIMIT),
        name="dsa",
    )(rel_bias, bkt, q, qi, misc, qi, misc, k, vt, miscb, nw)


def _tail_kernel(x_ref, ys_ref, ya_ref, mod_ref, wo_ref, n1_ref, n2_ref, n3_ref, w1_ref, w2_ref, o_ref):
    x = x_ref[0]
    ya = jnp.concatenate([jnp.transpose(ya_ref[0, j].astype(F32)).astype(BF16)
                          for j in range(ya_ref.shape[1])], axis=0)
    mix = (jnp.dot(ys_ref[0], wo_ref[0:D_SSM, :], preferred_element_type=F32)
           + jnp.dot(ya, wo_ref[D_SSM:, :], preferred_element_type=F32))
    x1 = x + mod_ref[0, 2:3, :] * _rms(mix, n1_ref[...])
    h = (_rms(x1, n2_ref[...]) * (1.0 + mod_ref[0, 4:5, :]) + mod_ref[0, 3:4, :]).astype(BF16)
    f = jnp.zeros(x.shape, F32)
    step = D_MODEL
    for c0 in range(0, D_FF, step):
        a = jnp.maximum(jnp.dot(h, w1_ref[:, c0:c0 + step], preferred_element_type=F32), 0.0)
        f = f + jnp.dot((a * a).astype(BF16), w2_ref[c0:c0 + step, :], preferred_element_type=F32)
    o_ref[0] = x1 + mod_ref[0, 5:6, :] * _rms(f, n3_ref[...])


def _tail(x, ys, ya, mod, wo, n1, n2, n3, w1, w2):
    bsz, t, d = x.shape
    tm = min(ROW_TILE, t)

    def const(shape):
        return pl.BlockSpec(shape, lambda b, i: tuple(0 for _ in shape))

    return pl.pallas_call(
        _tail_kernel,
        grid=(bsz, t // tm),
        in_specs=[pl.BlockSpec((1, tm, d), lambda b, i: (b, i, 0)),
                  pl.BlockSpec((1, tm, D_SSM), lambda b, i: (b, i, 0)),
                  pl.BlockSpec((1, tm // BLK, D_ATTN, BLK), lambda b, i: (b, i, 0, 0)),
                  pl.BlockSpec((1, 6, d), lambda b, i: (b, 0, 0)),
                  const((D_SSM + D_ATTN, d)), const((1, d)), const((1, d)), const((1, d)),
                  const((d, D_FF)), const((D_FF, d))],
        out_specs=pl.BlockSpec((1, tm, d), lambda b, i: (b, i, 0)),
        out_shape=jax.ShapeDtypeStruct((bsz, t, d), F32),
        compiler_params=pltpu.CompilerParams(
            dimension_semantics=("arbitrary", "arbitrary"), vmem_limit_bytes=VMEM_LIMIT_TAIL),
        name="tail",
    )(x, ys, ya, mod, wo, n1, n2, n3, w1, w2)


def _permute_w_in(w):
    sizes = [D_SSM, D_XBC, SSM_HEADS, D_ATTN, D_KV, D_KV, IDX_HEADS * IDX_DIM, IDX_DIM, IDX_HEADS]
    pts = np.cumsum(sizes)[:-1].tolist()
    z, xbc, dt, q, k, v, qidx, kidx, widx = jnp.split(w, pts, axis=-1)
    pad = jnp.zeros((w.shape[0], MISC_W - IDX_DIM - IDX_HEADS - SSM_HEADS), w.dtype)
    return jnp.concatenate([z, xbc, q, qidx, k, v, kidx, widx, dt, pad], axis=-1).astype(BF16)


def _layer(x, c_pad, w_ada, b_ada, norm1_pre, norm1_post, w_in, conv_w, conv_b, dt_bias, a_log, d_skip,
           ssm_norm, rel_bias, attn_norm, w_out, norm2_pre, norm2_post, w_mlp_in, w_mlp_out):
    bsz, t, d = x.shape
    mod = _ada(c_pad, w_ada, b_ada[None, :])[:bsz].reshape(bsz, 6, d)
    z, xbc, q, k, vt, qi, misc, miscb = _inproj(x, mod, norm1_pre[None, :], _permute_w_in(w_in))

    lane = jnp.arange(MISC_W)
    is_dt = (lane >= MISC_DT) & (lane < MISC_DT + SSM_HEADS)
    head_of = jnp.clip(lane - MISC_DT, 0, SSM_HEADS - 1)
    dtb_t = jnp.where(is_dt, dt_bias[head_of], 0.0)[None, :]
    a_t = jnp.where(is_dt, (-jnp.exp(a_log.astype(F32)) * LOG2E)[head_of], 0.0)[None, :]
    expand = ((lane[:, None] - MISC_DT) == (jnp.arange(D_SSM)[None, :] // SSM_HEAD_DIM)).astype(BF16)
    dskip_e = jnp.repeat(d_skip.astype(F32), SSM_HEAD_DIM)[None, :]
    y_ssd = _ssd(xbc, z, misc, conv_w, conv_b[None, :], dtb_t, a_t, expand, dskip_e, ssm_norm[None, :])

    attn_gain = jnp.broadcast_to(attn_norm.astype(F32)[:, None], (D_ATTN, BLK))
    y_att = _dsa(rel_bias, q, qi, misc, k, vt, miscb, attn_gain)

    return _tail(x, y_ssd, y_att, mod, w_out.astype(BF16), norm1_post[None, :], norm2_pre[None, :],
                 norm2_post[None, :], w_mlp_in.astype(BF16), w_mlp_out.astype(BF16))


def kernel(x, c, w_ada, b_ada, norm1_pre, norm1_post, w_in, conv_w, conv_b, dt_bias, a_log, d_skip,
           ssm_norm, rel_bias, attn_norm, w_out, norm2_pre, norm2_post, w_mlp_in, w_mlp_out):
    bsz = x.shape[0]
    assert bsz <= 8 and x.shape[1] % DSA_CHUNK == 0 and x.shape[2] == D_MODEL
    c_pad = jnp.zeros((8, D_MODEL), F32).at[:bsz].set(c)
    for l in range(w_ada.shape[0]):
        x = _layer(x, c_pad, w_ada[l], b_ada[l], norm1_pre[l], norm1_post[l], w_in[l], conv_w[l], conv_b[l],
                   dt_bias[l], a_log[l], d_skip[l], ssm_norm[l], rel_bias, attn_norm[l], w_out[l],
                   norm2_pre[l], norm2_post[l], w_mlp_in[l], w_mlp_out[l])
    return x
```

```python
import functools
import math

import numpy as np
import jax
import jax.numpy as jnp
from jax import lax
from jax.experimental import pallas as pl
from jax.experimental.pallas import tpu as pltpu

D_MODEL = 1024
SSM_HEADS = 8
SSM_HEAD_DIM = 64
D_SSM = SSM_HEADS * SSM_HEAD_DIM
SSM_GROUPS = 2
SSM_STATE = 128
CONV_WIDTH = 4
CHUNK = 256
ATTN_HEADS = 8
ATTN_KV_HEADS = 2
ATTN_HEAD_DIM = 64
D_ATTN = ATTN_HEADS * ATTN_HEAD_DIM
D_KV = ATTN_KV_HEADS * ATTN_HEAD_DIM
IDX_HEADS = 8
IDX_DIM = 64
TOPK_MAX = 256
N_BUCKETS = 32
MAX_DISTANCE = 128
D_FF = 4 * D_MODEL
D_BC = SSM_GROUPS * SSM_STATE
D_XBC = D_SSM + 2 * D_BC
EPS = 1e-6

LANES = 128
V7X_VMEM_BYTES = 64 * 2 ** 20
VMEM_LIMIT = V7X_VMEM_BYTES * 3 // 4
VMEM_LIMIT_TAIL = V7X_VMEM_BYTES * 15 // 16
MISC_W = LANES
MISC_KIDX = 0
MISC_WIDX = IDX_DIM
MISC_DT = IDX_DIM + IDX_HEADS
OFF_Z = 0
OFF_XBC = OFF_Z + D_SSM
OFF_Q = OFF_XBC + D_XBC
OFF_QI = OFF_Q + D_ATTN
OFF_K = OFF_QI + IDX_HEADS * IDX_DIM
OFF_V = OFF_K + D_KV
OFF_MISC = OFF_V + D_KV
D_IN_PAD = OFF_MISC + MISC_W

BLK = 128
DSA_CHUNK = 512
V_ROWS = 80
ZERO_SLOT = 2
SEL_SIZES = 16
DEN_LO = 2.0 ** -80
DEN_HI = 2.0 ** 80
F32_HUGE = 3.0e38
LOG2E = math.log2(math.e)
ROW_TILE = 512
NEG = -1e30
INT_MIN = -2 ** 31

F32 = jnp.float32
BF16 = jnp.bfloat16
HI = lax.Precision.HIGHEST
NT = (((1,), (1,)), ((), ()))


def _silu(x):
    return x / (1.0 + jnp.exp(-x))


def _rms(x, w):
    return x * lax.rsqrt(jnp.mean(x * x, axis=-1, keepdims=True) + EPS) * w


def _split3(x):
    hi = x.astype(BF16)
    rest = x - hi.astype(F32)
    mid = rest.astype(BF16)
    return hi, mid, (rest - mid.astype(F32)).astype(BF16)


def _ada_kernel(c_ref, w_ref, b_ref, o_ref):
    s = _silu(c_ref[...])
    o_ref[...] = jnp.dot(s, w_ref[...], precision=HI, preferred_element_type=F32) + b_ref[...]


def _ada(c_pad, w, b):
    n = w.shape[1]
    tn = D_MODEL
    return pl.pallas_call(
        _ada_kernel,
        grid=(n // tn,),
        in_specs=[pl.BlockSpec((8, D_MODEL), lambda j: (0, 0)),
                  pl.BlockSpec((D_MODEL, tn), lambda j: (0, j)),
                  pl.BlockSpec((1, tn), lambda j: (0, j))],
        out_specs=pl.BlockSpec((8, tn), lambda j: (0, j)),
        out_shape=jax.ShapeDtypeStruct((8, n), F32),
        name="ada",
    )(c_pad, w, b)


def _inproj_kernel(x_ref, mod_ref, nw_ref, w_ref,
                   z_ref, xbc_ref, q_ref, k_ref, vt_ref, qi_ref, misc_ref, miscb_ref):
    x = x_ref[0]
    h = _rms(x, nw_ref[...]) * (1.0 + mod_ref[0, 1:2, :]) + mod_ref[0, 0:1, :]
    hb = h.astype(BF16)

    def seg(lo, width):
        return jnp.dot(hb, w_ref[:, lo:lo + width], preferred_element_type=F32)

    z_ref[0] = seg(OFF_Z, D_SSM)
    xbc_ref[0] = seg(OFF_XBC, D_XBC)
    rep = ATTN_HEADS // ATTN_KV_HEADS
    q_all = seg(OFF_Q, D_ATTN) * (ATTN_HEAD_DIM ** -0.5 * LOG2E)
    qi_all = seg(OFF_QI, IDX_HEADS * IDX_DIM)
    for j in range(x.shape[0] // BLK):
        q_t = jnp.transpose(q_all[j * BLK:(j + 1) * BLK, :])
        qi_t = jnp.transpose(qi_all[j * BLK:(j + 1) * BLK, :])
        for h in range(ATTN_HEADS):
            q_ref[0, j, h // rep, :, (h % rep) * BLK:(h % rep + 1) * BLK] = (
                q_t[h * ATTN_HEAD_DIM:(h + 1) * ATTN_HEAD_DIM, :].astype(BF16))
        for h in range(IDX_HEADS):
            qi_ref[0, j, :, h * BLK:(h + 1) * BLK] = qi_t[h * IDX_DIM:(h + 1) * IDX_DIM, :].astype(BF16)
    k_v_misc = seg(OFF_K, 2 * D_KV + MISC_W)
    k_ref[0] = k_v_misc[:, :D_KV].astype(BF16)
    v_misc = k_v_misc[:, D_KV:]
    ones_tile = jnp.where(lax.broadcasted_iota(jnp.int32, (V_ROWS - ATTN_HEAD_DIM, BLK), 0) == 0, 1.0, 0.0)
    for j in range(x.shape[0] // BLK):
        v_t = jnp.transpose(v_misc[j * BLK:(j + 1) * BLK, :D_KV])
        for g in range(ATTN_KV_HEADS):
            vt_ref[0, j, g, 0:ATTN_HEAD_DIM, :] = v_t[g * ATTN_HEAD_DIM:(g + 1) * ATTN_HEAD_DIM, :].astype(BF16)
            vt_ref[0, j, g, ATTN_HEAD_DIM:, :] = ones_tile.astype(BF16)
    misc = v_misc[:, D_KV:]
    misc_ref[0] = misc
    miscb_ref[0] = misc.astype(BF16)


def _inproj(x, mod, nw, w_perm):
    bsz, t, d = x.shape
    tm = min(ROW_TILE, t)

    def tok(width, dtype):
        return (pl.BlockSpec((1, tm, width), lambda b, i: (b, i, 0)),
                jax.ShapeDtypeStruct((bsz, t, width), dtype))

    vt = (pl.BlockSpec((1, tm // BLK, ATTN_KV_HEADS, V_ROWS, BLK), lambda b, i: (b, i, 0, 0, 0)),
          jax.ShapeDtypeStruct((bsz, t // BLK, ATTN_KV_HEADS, V_ROWS, BLK), BF16))
    gw = (ATTN_HEADS // ATTN_KV_HEADS) * BLK
    qt = (pl.BlockSpec((1, tm // BLK, ATTN_KV_HEADS, ATTN_HEAD_DIM, gw), lambda b, i: (b, i, 0, 0, 0)),
          jax.ShapeDtypeStruct((bsz, t // BLK, ATTN_KV_HEADS, ATTN_HEAD_DIM, gw), BF16))
    qit = (pl.BlockSpec((1, tm // BLK, IDX_DIM, IDX_HEADS * BLK), lambda b, i: (b, i, 0, 0)),
           jax.ShapeDtypeStruct((bsz, t // BLK, IDX_DIM, IDX_HEADS * BLK), BF16))
    outs = [tok(D_SSM, F32), tok(D_XBC, F32), qt, tok(D_KV, BF16), vt, qit,
            tok(MISC_W, F32), tok(MISC_W, BF16)]
    return pl.pallas_call(
        _inproj_kernel,
        grid=(bsz, t // tm),
        in_specs=[pl.BlockSpec((1, tm, d), lambda b, i: (b, i, 0)),
                  pl.BlockSpec((1, 6, d), lambda b, i: (b, 0, 0)),
                  pl.BlockSpec((1, d), lambda b, i: (0, 0)),
                  pl.BlockSpec((d, D_IN_PAD), lambda b, i: (0, 0))],
        out_specs=[o[0] for o in outs],
        out_shape=[o[1] for o in outs],
        compiler_params=pltpu.CompilerParams(
            dimension_semantics=("arbitrary", "arbitrary"), vmem_limit_bytes=VMEM_LIMIT),
        name="inproj",
    )(x, mod, nw, w_perm)


def _ssd_kernel(xbc_ref, z_ref, misc_ref, cw_ref, cb_ref, dtb_ref, a_ref, e_ref, dskip_ref, nw_ref,
                o_ref, xbuf, state):
    c = pl.program_id(1)
    lc = xbc_ref.shape[1]

    @pl.when(c == 0)
    def _():
        xbuf[0:8, :] = jnp.zeros((8, D_XBC), F32)
        state[...] = jnp.zeros(state.shape, F32)

    @pl.when(c > 0)
    def _():
        xbuf[0:8, :] = xbuf[lc:lc + 8, :]

    xbuf[8:lc + 8, :] = xbc_ref[0]
    u = cb_ref[...]
    xall = xbuf[...]
    for k in range(CONV_WIDTH):
        back = CONV_WIDTH - 1 - k
        xk = xall if back == 0 else pltpu.roll(xall, back, 0)
        u = u + xk[8:8 + lc] * cw_ref[k:k + 1, :]
    u = _silu(u)
    xs = u[:, :D_SSM]
    bm = u[:, D_SSM:D_SSM + D_BC]
    cm = u[:, D_SSM + D_BC:]

    raw = misc_ref[0] + dtb_ref[...]
    dt_t = jnp.maximum(raw, 0.0) + jnp.log1p(jnp.exp(-jnp.abs(raw)))
    adt_t = dt_t * a_ref[...]
    row = lax.broadcasted_iota(jnp.int32, (lc, lc), 0)
    col = lax.broadcasted_iota(jnp.int32, (lc, lc), 1)
    causal = col <= row
    tril = jnp.where(causal, 1.0, 0.0).astype(BF16)
    expand = e_ref[...]
    acs_t = sum(jnp.dot(tril, piece, preferred_element_type=F32) for piece in _split3(adt_t))
    dt_e = sum(jnp.dot(piece, expand, preferred_element_type=F32) for piece in _split3(dt_t))
    acs_e = sum(jnp.dot(piece, expand, preferred_element_type=F32) for piece in _split3(acs_t))
    acs_row = jnp.transpose(acs_t)

    xdt = xs * dt_e
    a_last = acs_e[lc - 1:lc, :]
    xdec = (xdt * jnp.exp2(a_last - acs_e)).astype(BF16)
    xdt_b = xdt.astype(BF16)
    bm_b = bm.astype(BF16)
    cm_b = cm.astype(BF16)
    bm_t = jnp.transpose(bm).astype(BF16)

    heads_per_group = SSM_HEADS // SSM_GROUPS
    gw = heads_per_group * SSM_HEAD_DIM
    y_parts = []
    off_parts = []
    for g in range(SSM_GROUPS):
        bg = bm_b[:, g * SSM_STATE:(g + 1) * SSM_STATE]
        cg = cm_b[:, g * SSM_STATE:(g + 1) * SSM_STATE]
        cb = lax.dot_general(cg, bg, NT, preferred_element_type=F32)
        for r in range(heads_per_group):
            hd = g * heads_per_group + r
            a_col = acs_t[:, MISC_DT + hd:MISC_DT + hd + 1]
            a_row = acs_row[MISC_DT + hd:MISC_DT + hd + 1, :]
            lmat = jnp.exp2(jnp.where(causal, a_col - a_row, -jnp.inf))
            mh = (cb * lmat).astype(BF16)
            y_parts.append(jnp.dot(mh, xdt_b[:, hd * SSM_HEAD_DIM:(hd + 1) * SSM_HEAD_DIM],
                                   preferred_element_type=F32))
        prev = state[g]
        off_parts.append(jnp.dot(cg, prev.astype(BF16), preferred_element_type=F32))
        st = jnp.dot(bm_t[g * SSM_STATE:(g + 1) * SSM_STATE, :], xdec[:, g * gw:(g + 1) * gw],
                     preferred_element_type=F32)
        state[g] = prev * jnp.exp2(a_last[:, g * gw:(g + 1) * gw]) + st
    y = (jnp.concatenate(y_parts, axis=1)
         + jnp.concatenate(off_parts, axis=1) * jnp.exp2(acs_e)
         + xs * dskip_ref[...])
    o_ref[0] = _rms(y * _silu(z_ref[0]), nw_ref[...]).astype(BF16)


def _ssd(xbc, z, misc, cw, cb, dtb_t, a_t, expand, dskip_e, nw):
    bsz, t, _ = xbc.shape
    lc = math.gcd(t, CHUNK)
    nc = t // lc
    gw = (SSM_HEADS // SSM_GROUPS) * SSM_HEAD_DIM

    def const(shape):
        return pl.BlockSpec(shape, lambda b, c: tuple(0 for _ in shape))

    return pl.pallas_call(
        _ssd_kernel,
        grid=(bsz, nc),
        in_specs=[pl.BlockSpec((1, lc, D_XBC), lambda b, c: (b, c, 0)),
                  pl.BlockSpec((1, lc, D_SSM), lambda b, c: (b, c, 0)),
                  pl.BlockSpec((1, lc, MISC_W), lambda b, c: (b, c, 0)),
                  const((CONV_WIDTH, D_XBC)), const((1, D_XBC)), const((1, MISC_W)), const((1, MISC_W)),
                  const((MISC_W, D_SSM)), const((1, D_SSM)), const((1, D_SSM))],
        out_specs=pl.BlockSpec((1, lc, D_SSM), lambda b, c: (b, c, 0)),
        out_shape=jax.ShapeDtypeStruct((bsz, t, D_SSM), BF16),
        scratch_shapes=[pltpu.VMEM((lc + 8, D_XBC), F32),
                        pltpu.VMEM((SSM_GROUPS, SSM_STATE, gw), F32)],
        compiler_params=pltpu.CompilerParams(
            dimension_semantics=("arbitrary", "arbitrary"), vmem_limit_bytes=VMEM_LIMIT),
        name="ssd",
    )(xbc, z, misc, cw, cb, dtb_t, a_t, expand, dskip_e, nw)


def _bucket_tiles():
    s = np.arange(BLK)[:, None]
    t = np.arange(BLK)[None, :]
    max_exact = N_BUCKETS // 2
    tiles = []
    for d0 in (0, BLK):
        n = np.maximum(d0 + t - s, 0)
        large = {}
        for dt_ in (np.float32, np.float64):
            nf = np.maximum(n, 1).astype(dt_)
            lg = max_exact + (np.log(nf / dt_(max_exact)) / dt_(math.log(MAX_DISTANCE / max_exact))
                              * dt_(N_BUCKETS - max_exact)).astype(np.int32)
            large[dt_] = np.minimum(lg, N_BUCKETS - 1)
        assert (large[np.float32] == large[np.float64]).all()
        tiles.append(np.where(n < max_exact, n, large[np.float64]).astype(np.int32))
    assert BLK + 1 >= MAX_DISTANCE
    return np.stack(tiles)


def _to_key(x):
    bits = lax.bitcast_convert_type(x, jnp.int32)
    key = bits ^ (lax.shift_right_arithmetic(bits, 31) & jnp.int32(0x7FFFFFFF))
    return jnp.where(key == -1, 0, key)


def _transpose32(rows):
    rows = list(rows)
    mask, j = 0x0000FFFF, 16
    while j:
        k = 0
        while k < 32:
            t = (rows[k] ^ lax.shift_right_logical(rows[k + j], jnp.int32(j))) & jnp.int32(mask)
            rows[k] = rows[k] ^ t
            rows[k + j] = rows[k + j] ^ lax.shift_left(t, jnp.int32(j))
            k = (k + j + 1) & ~j
        j >>= 1
        mask ^= (mask << j) & 0xFFFFFFFF
    return rows


def _dsa_kernel(relb_ref, bkt_ref, q_ref, qic_ref, miscc_ref, qin_ref, miscn_ref, k_ref, vt_ref, kidx_ref,
                nw_ref, o_ref, keys, planes, live, thr_s, cnt_s, bias_s, qcat, qicat, wcat, m_s, acc_s, p_s,
                *, topk, idx_bits, nq):
    b = pl.program_id(0)
    qi = pl.program_id(1)
    tq = BLK
    i32 = jnp.int32
    rep = ATTN_HEADS // ATTN_KV_HEADS
    gw = rep * tq
    ch = DSA_CHUNK
    ch_blks = ch // BLK
    rows_per_chunk = ch // 32

    @pl.when((b == 0) & (qi == 0))
    def _init_bias():
        for d in range(2):
            bk = bkt_ref[d]
            for h in range(ATTN_HEADS):
                far = relb_ref[N_BUCKETS - 1, h]
                acc = jnp.zeros((BLK, tq), F32)
                for n in range(N_BUCKETS - 1):
                    acc = jnp.where(bk == n, relb_ref[n, h] - far, acc)
                bias_s[d, h // rep, :, (h % rep) * BLK:(h % rep + 1) * BLK] = acc * LOG2E
        bias_s[2] = jnp.zeros((ATTN_KV_HEADS, BLK, gw), F32)
        for w in range(32):
            planes[w] = jnp.zeros(planes.shape[1:], i32)
        p_s[ZERO_SLOT] = jnp.zeros(p_s.shape[1:], BF16)

    row_c = lax.broadcasted_iota(i32, (ch, tq), 0)
    lane_c = lax.broadcasted_iota(i32, (ch, tq), 1)

    def chunk_start(c):
        return pl.multiple_of(c * ch, ch)

    def n_chunks(blk):
        return (blk + ch_blks) // ch_blks

    def load_indexer(qidx_ref, misc_ref):
        qicat[...] = qidx_ref[0, 0]
        w_t = jnp.transpose(misc_ref[0])[MISC_WIDX:MISC_WIDX + IDX_HEADS, :] * (
            IDX_HEADS ** -0.5 * IDX_DIM ** -0.5)
        for h in range(IDX_HEADS):
            wcat[:, h * BLK:(h + 1) * BLK] = jnp.broadcast_to(w_t[h:h + 1, :], (8, BLK))

    def score_chunk(c, blk, slot):
        s0 = chunk_start(c)
        kb = kidx_ref[0, pl.ds(s0, ch), MISC_KIDX:MISC_KIDX + IDX_DIM]
        s = jnp.dot(kb, qicat[...], preferred_element_type=F32)
        r = jnp.maximum(s, 0.0) * wcat[0:1, :]
        acc = r[:, 0:BLK]
        for h in range(1, IDX_HEADS):
            acc = acc + r[:, h * BLK:(h + 1) * BLK]
        acc = jnp.where(s0 + row_c <= blk * BLK + lane_c, acc, -jnp.inf)
        keys[slot, pl.ds(s0, ch), :] = _to_key(acc)

    def plane_chunk(c, slot):
        ku = (keys[slot, pl.ds(chunk_start(c), ch), :] ^ i32(INT_MIN)).reshape(ch // (32 * 8), 32, 8, tq)
        words = _transpose32([ku[:, j] for j in range(32)])
        r0 = pl.multiple_of(c * rows_per_chunk, rows_per_chunk)
        for w in range(32):
            planes[w, pl.ds(r0, rows_per_chunk), :] = words[w].reshape(rows_per_chunk, tq)

    def select(blk, slot):
        nch = n_chunks(blk)
        used = nch * rows_per_chunk

        def ones_in(words):
            return jnp.sum(lax.population_count(words), axis=0, keepdims=True)

        def passes(nr):
            live[0:nr] = jnp.where(lax.broadcasted_iota(i32, (nr, tq), 0) < used, i32(-1), i32(0))

            def bit_body(w, st):
                thr_u, n_above, n_set = st
                take = n_above + n_set >= topk
                thr_u = jnp.where(take, thr_u | lax.shift_left(i32(1), i32(31) - w), thr_u)
                n_above = jnp.where(take, n_above, n_above + n_set)
                still = live[0:nr] & (planes[w, 0:nr] ^ jnp.where(take, i32(0), i32(-1)))
                live[0:nr] = still
                return thr_u, n_above, ones_in(still & planes[jnp.minimum(w + 1, 31), 0:nr])

            zero = jnp.zeros((1, tq), i32)
            thr_u, n_above, _ = lax.fori_loop(
                0, 32, bit_body, (zero, zero, ones_in(live[0:nr] & planes[0, 0:nr])))
            thr_s[slot] = thr_u ^ i32(INT_MIN)
            cnt_s[0] = n_above
            cnt_s[1] = ones_in(live[0:nr])

        n_sizes = max(1, min(SEL_SIZES, live.shape[0] // 8))
        size_step = live.shape[0] // n_sizes
        for v in range(n_sizes):
            pl.when((used > v * size_step) & (used <= (v + 1) * size_step))(
                functools.partial(passes, (v + 1) * size_step))
        thr = thr_s[slot]
        n_gt = cnt_s[0]
        n_eq = cnt_s[1]
        need = topk - n_gt

        def break_ties():
            def count(pred):
                def body(c, cnt):
                    s0 = chunk_start(c)
                    hit = pred(keys[slot, pl.ds(s0, ch), :], s0).astype(i32)
                    n = ch
                    while n > 8:
                        n //= 2
                        hit = hit[:n] + hit[n:2 * n]
                    return cnt + hit
                cnt = lax.fori_loop(0, nch, body, jnp.zeros((8, tq), i32))
                return jnp.sum(cnt, axis=0, keepdims=True)

            def jbody(it, p):
                cand = p + lax.shift_left(i32(1), i32(idx_bits - 1) - it)
                c = count(lambda kk, s0: (kk == thr) & (s0 + row_c < cand))
                return jnp.where(c < need, cand, p)
            last = lax.fori_loop(0, idx_bits, jbody, jnp.zeros((1, tq), i32))

            def demote(c, carry):
                s0 = chunk_start(c)
                kk = keys[slot, pl.ds(s0, ch), :]
                keys[slot, pl.ds(s0, ch), :] = jnp.where((kk == thr) & (s0 + row_c > last), INT_MIN, kk)
                return carry
            lax.fori_loop(0, nch, demote, 0)

        return (n_eq > need).astype(i32), break_ties

    def any_lane(flags):
        return jnp.max(flags, axis=1, keepdims=True)

    @pl.when(qi == 0)
    def _first_block():
        load_indexer(qic_ref, miscc_ref)
        score_chunk(0, 0, 0)
        plane_chunk(0, 0)
        tied, break_ties = select(0, 0)
        pl.when(any_lane(tied)[0, 0] > 0)(break_ties)

    slot_b = qi % 2
    slot_a = 1 - slot_b
    blk_a = jnp.minimum(qi + 1, nq - 1)
    nch_b = n_chunks(qi)
    nch_a = n_chunks(blk_a)
    load_indexer(qin_ref, miscn_ref)
    thr = thr_s[slot_b]

    for g in range(ATTN_KV_HEADS):
        qcat[g] = q_ref[0, 0, g]

    def pen_of(s0, causal):
        sel = keys[slot_b, pl.ds(s0, ch), :] >= thr
        if causal:
            sel = sel & (s0 + row_c <= qi * BLK + lane_c)
        pen = jnp.where(sel, 0.0, NEG)
        return jnp.concatenate([pen] * rep, axis=1)

    def near_bias(c, g):
        return jnp.concatenate(
            [bias_s[jnp.clip(qi - (c * ch_blks + u), 0, 2), g] for u in range(ch_blks)], axis=0)

    def vt_chunk(c, g):
        return jnp.concatenate([vt_ref[0, c * ch_blks + u, g] for u in range(ch_blks)], axis=1)

    for g in range(ATTN_KV_HEADS):
        acc_s[g] = jnp.zeros((V_ROWS, gw), F32)

    def weigh_values(c, slot):
        for g in range(ATTN_KV_HEADS):
            acc_s[g] += jnp.dot(vt_chunk(c, g), p_s[slot, g], preferred_element_type=F32)

    def att_chunk(c, near, slots=None):
        prev_slot, cur_slot = slots if slots is not None else ((c + 1) % 2, c % 2)
        if not (isinstance(cur_slot, int) and cur_slot == 1):
            prev_slot = jnp.where(c == 0, ZERO_SLOT, prev_slot)
        weigh_values(jnp.maximum(c - 1, 0), prev_slot)
        s0 = chunk_start(c)
        pen = pen_of(s0, near)
        kblk = k_ref[0, pl.ds(s0, ch), :]
        for g in range(ATTN_KV_HEADS):
            s = jnp.dot(kblk[:, g * ATTN_HEAD_DIM:(g + 1) * ATTN_HEAD_DIM], qcat[g],
                        preferred_element_type=F32)
            if near:
                s = s + near_bias(c, g)
            p_s[cur_slot, g] = jnp.exp2(s + pen).astype(BF16)

    def fused(c, near, slots=None):
        plane_chunk(c, slot_a)
        att_chunk(c, near, slots)
        score_chunk(c + 1, blk_a, slot_a)

    def far_pair(j, carry):
        fused(2 * j, False, (1, 0))
        fused(2 * j + 1, False, (0, 1))
        return carry

    def far_quad(j, carry):
        far_pair(2 * j, carry)
        return far_pair(2 * j + 1, carry)

    def far_body(c, carry):
        fused(c, False, (1, 0))
        return carry

    def near_body(c, carry):
        fused(c, True)
        return carry

    score_chunk(0, blk_a, slot_a)
    n_far = jnp.maximum(qi - 1, 0) // ch_blks
    lax.fori_loop(0, n_far // 4, far_quad, 0)
    lax.fori_loop((n_far // 4) * 2, n_far // 2, far_pair, 0)
    lax.fori_loop((n_far // 2) * 2, n_far, far_body, 0)
    lax.fori_loop(n_far, nch_b - 1, near_body, 0)
    @pl.when(nch_a > nch_b)
    def _one_more_chunk():
        score_chunk(nch_b, blk_a, slot_a)
        plane_chunk(nch_b, slot_a)

    plane_chunk(nch_b - 1, slot_a)
    att_chunk(nch_b - 1, True)
    weigh_values(nch_b - 1, (nch_b - 1) % 2)
    tied, break_ties = select(blk_a, slot_a)

    n_ok = jnp.zeros((1, gw), i32)
    for g in range(ATTN_KV_HEADS):
        a = acc_s[g]
        den = a[ATTN_HEAD_DIM:ATTN_HEAD_DIM + 1, :]
        top = jnp.max(jnp.abs(a[0:ATTN_HEAD_DIM, :]), axis=0, keepdims=True)
        n_ok = n_ok + ((den > DEN_LO) & (den < DEN_HI) & (top < F32_HUGE)).astype(i32)

    rare = (any_lane(tied) + 2 * any_lane((n_ok < ATTN_KV_HEADS).astype(i32)))[0, 0]
    pl.when(rare % 2 == 1)(break_ties)

    @pl.when(rare >= 2)
    def _exact():
        for g in range(ATTN_KV_HEADS):
            m_s[g] = jnp.full((1, gw), NEG, F32)
            acc_s[g] = jnp.zeros((V_ROWS, gw), F32)

        def body(c, carry):
            s0 = chunk_start(c)
            pen = pen_of(s0, True)
            kblk = k_ref[0, pl.ds(s0, ch), :]
            for g in range(ATTN_KV_HEADS):
                s = jnp.dot(kblk[:, g * ATTN_HEAD_DIM:(g + 1) * ATTN_HEAD_DIM], qcat[g],
                            preferred_element_type=F32) + near_bias(c, g) + pen
                m_old = m_s[g]
                m_new = jnp.maximum(m_old, jnp.max(s, axis=0, keepdims=True))
                p = jnp.exp2(s - m_new).astype(BF16)
                acc_s[g] = acc_s[g] * jnp.exp2(m_old - m_new) + jnp.dot(
                    vt_chunk(c, g), p, preferred_element_type=F32)
                m_s[g] = m_new
            return carry
        lax.fori_loop(0, nch_b, body, 0)

    parts = []
    for g in range(ATTN_KV_HEADS):
        a = acc_s[g]
        o = a[0:ATTN_HEAD_DIM, :] * (1.0 / a[ATTN_HEAD_DIM:ATTN_HEAD_DIM + 1, :])
        parts += [o[:, r * tq:(r + 1) * tq] for r in range(rep)]
    out_t = jnp.concatenate(parts, axis=0)
    inv = lax.rsqrt(jnp.mean(out_t * out_t, axis=0, keepdims=True) + EPS)
    o_ref[0, 0] = (out_t * inv * nw_ref[...]).astype(BF16)


def _dsa(rel_bias, q, qi, misc, k, vt, miscb, nw):
    bsz, t, _ = k.shape
    nq = t // BLK
    topk = min(TOPK_MAX, t // 4)
    idx_bits = max(1, (t - 1).bit_length())
    gw = (ATTN_HEADS // ATTN_KV_HEADS) * BLK
    bkt = jnp.asarray(_bucket_tiles())
    kern = functools.partial(_dsa_kernel, topk=topk, idx_bits=idx_bits, nq=nq)

    def nxt(b, i):
        return (b, jnp.minimum(i + 1, nq - 1), 0)

    return pl.pallas_call(
        kern,
        grid=(bsz, nq),
        in_specs=[pl.BlockSpec(memory_space=pltpu.SMEM),
                  pl.BlockSpec((2, BLK, BLK), lambda b, i: (0, 0, 0)),
                  pl.BlockSpec((1, 1, ATTN_KV_HEADS, ATTN_HEAD_DIM, gw), lambda b, i: (b, i, 0, 0, 0)),
                  pl.BlockSpec((1, 1, IDX_DIM, IDX_HEADS * BLK), lambda b, i: (b, i, 0, 0)),
                  pl.BlockSpec((1, BLK, MISC_W), lambda b, i: (b, i, 0)),
                  pl.BlockSpec((1, 1, IDX_DIM, IDX_HEADS * BLK), lambda b, i: nxt(b, i) + (0,)),
                  pl.BlockSpec((1, BLK, MISC_W), nxt),
                  pl.BlockSpec((1, t, D_KV), lambda b, i: (b, 0, 0)),
                  pl.BlockSpec((1, nq, ATTN_KV_HEADS, V_ROWS, BLK), lambda b, i: (b, 0, 0, 0, 0)),
                  pl.BlockSpec((1, t, MISC_W), lambda b, i: (b, 0, 0)),
                  pl.BlockSpec((D_ATTN, BLK), lambda b, i: (0, 0))],
        out_specs=pl.BlockSpec((1, 1, D_ATTN, BLK), lambda b, i: (b, i, 0, 0)),
        out_shape=jax.ShapeDtypeStruct((bsz, nq, D_ATTN, BLK), BF16),
        scratch_shapes=[pltpu.VMEM((2, t, BLK), jnp.int32),
                        pltpu.VMEM((32, t // 32, BLK), jnp.int32),
                        pltpu.VMEM((t // 32, BLK), jnp.int32),
                        pltpu.VMEM((2, 1, BLK), jnp.int32),
                        pltpu.VMEM((2, 1, BLK), jnp.int32),
                        pltpu.VMEM((3, ATTN_KV_HEADS, BLK, gw), F32),
                        pltpu.VMEM((ATTN_KV_HEADS, ATTN_HEAD_DIM, gw), BF16),
                        pltpu.VMEM((IDX_DIM, IDX_HEADS * BLK), BF16),
                        pltpu.VMEM((8, IDX_HEADS * BLK), F32),
                        pltpu.VMEM((ATTN_KV_HEADS, 1, gw), F32),
                        pltpu.VMEM((ATTN_KV_HEADS, V_ROWS, gw), F32),
                        pltpu.VMEM((ZERO_SLOT + 1, ATTN_KV_HEADS, DSA_CHUNK, gw), BF16)],
        compiler_params=pltpu.CompilerParams(
            dimension_semantics=("arbitrary", "arbitrary"), vmem_limit_bytes=VMEM_LIMIT),
        name="dsa",
    )(rel_bias, bkt, q, qi, misc, qi, misc, k, vt, miscb, nw)


def _tail_kernel(x_ref, ys_ref, ya_ref, mod_ref, wo_ref, n1_ref, n2_ref, n3_ref, w1_ref, w2_ref, o_ref):
    x = x_ref[0]
    ya = jnp.concatenate([jnp.transpose(ya_ref[0, j].astype(F32)).astype(BF16)
                          for j in range(ya_ref.shape[1])], axis=0)
    mix = (jnp.dot(ys_ref[0], wo_ref[0:D_SSM, :], preferred_element_type=F32)
           + jnp.dot(ya, wo_ref[D_SSM:, :], preferred_element_type=F32))
    x1 = x + mod_ref[0, 2:3, :] * _rms(mix, n1_ref[...])
    h = (_rms(x1, n2_ref[...]) * (1.0 + mod_ref[0, 4:5, :]) + mod_ref[0, 3:4, :]).astype(BF16)
    f = jnp.zeros(x.shape, F32)
    step = D_MODEL
    for c0 in range(0, D_FF, step):
        a = jnp.maximum(jnp.dot(h, w1_ref[:, c0:c0 + step], preferred_element_type=F32), 0.0)
        f = f + jnp.dot((a * a).astype(BF16), w2_ref[c0:c0 + step, :], preferred_element_type=F32)
    o_ref[0] = x1 + mod_ref[0, 5:6, :] * _rms(f, n3_ref[...])


def _tail(x, ys, ya, mod, wo, n1, n2, n3, w1, w2):
    bsz, t, d = x.shape
    tm = min(ROW_TILE, t)

    def const(shape):
        return pl.BlockSpec(shape, lambda b, i: tuple(0 for _ in shape))

    return pl.pallas_call(
        _tail_kernel,
        grid=(bsz, t // tm),
        in_specs=[pl.BlockSpec((1, tm, d), lambda b, i: (b, i, 0)),
                  pl.BlockSpec((1, tm, D_SSM), lambda b, i: (b, i, 0)),
                  pl.BlockSpec((1, tm // BLK, D_ATTN, BLK), lambda b, i: (b, i, 0, 0)),
                  pl.BlockSpec((1, 6, d), lambda b, i: (b, 0, 0)),
                  const((D_SSM + D_ATTN, d)), const((1, d)), const((1, d)), const((1, d)),
                  const((d, D_FF)), const((D_FF, d))],
        out_specs=pl.BlockSpec((1, tm, d), lambda b, i: (b, i, 0)),
        out_shape=jax.ShapeDtypeStruct((bsz, t, d), F32),
        compiler_params=pltpu.CompilerParams(
            dimension_semantics=("arbitrary", "arbitrary"), vmem_limit_bytes=VMEM_LIMIT_TAIL),
        name="tail",
    )(x, ys, ya, mod, wo, n1, n2, n3, w1, w2)


def _permute_w_in(w):
    sizes = [D_SSM, D_XBC, SSM_HEADS, D_ATTN, D_KV, D_KV, IDX_HEADS * IDX_DIM, IDX_DIM, IDX_HEADS]
    pts = np.cumsum(sizes)[:-1].tolist()
    z, xbc, dt, q, k, v, qidx, kidx, widx = jnp.split(w, pts, axis=-1)
    pad = jnp.zeros((w.shape[0], MISC_W - IDX_DIM - IDX_HEADS - SSM_HEADS), w.dtype)
    return jnp.concatenate([z, xbc, q, qidx, k, v, kidx, widx, dt, pad], axis=-1).astype(BF16)


def _layer(x, c_pad, w_ada, b_ada, norm1_pre, norm1_post, w_in, conv_w, conv_b, dt_bias, a_log, d_skip,
           ssm_norm, rel_bias, attn_norm, w_out, norm2_pre, norm2_post, w_mlp_in, w_mlp_out):
    bsz, t, d = x.shape
    mod = _ada(c_pad, w_ada, b_ada[None, :])[:bsz].reshape(bsz, 6, d)
    z, xbc, q, k, vt, qi, misc, miscb = _inproj(x, mod, norm1_pre[None, :], _permute_w_in(w_in))

    lane = jnp.arange(MISC_W)
    is_dt = (lane >= MISC_DT) & (lane < MISC_DT + SSM_HEADS)
    head_of = jnp.clip(lane - MISC_DT, 0, SSM_HEADS - 1)
    dtb_t = jnp.where(is_dt, dt_bias[head_of], 0.0)[None, :]
    a_t = jnp.where(is_dt, (-jnp.exp(a_log.astype(F32)) * LOG2E)[head_of], 0.0)[None, :]
    expand = ((lane[:, None] - MISC_DT) == (jnp.arange(D_SSM)[None, :] // SSM_HEAD_DIM)).astype(BF16)
    dskip_e = jnp.repeat(d_skip.astype(F32), SSM_HEAD_DIM)[None, :]
    y_ssd = _ssd(xbc, z, misc, conv_w, conv_b[None, :], dtb_t, a_t, expand, dskip_e, ssm_norm[None, :])

    attn_gain = jnp.broadcast_to(attn_norm.astype(F32)[:, None], (D_ATTN, BLK))
    y_att = _dsa(rel_bias, q, qi, misc, k, vt, miscb, attn_gain)

    return _tail(x, y_ssd, y_att, mod, w_out.astype(BF16), norm1_post[None, :], norm2_pre[None, :],
                 norm2_post[None, :], w_mlp_in.astype(BF16), w_mlp_out.astype(BF16))


def kernel(x, c, w_ada, b_ada, norm1_pre, norm1_post, w_in, conv_w, conv_b, dt_bias, a_log, d_skip,
           ssm_norm, rel_bias, attn_norm, w_out, norm2_pre, norm2_post, w_mlp_in, w_mlp_out):
    bsz = x.shape[0]
    assert bsz <= 8 and x.shape[1] % DSA_CHUNK == 0 and x.shape[2] == D_MODEL
    c_pad = jnp.zeros((8, D_MODEL), F32).at[:bsz].set(c)
    for l in range(w_ada.shape[0]):
        x = _layer(x, c_pad, w_ada[l], b_ada[l], norm1_pre[l], norm1_post[l], w_in[l], conv_w[l], conv_b[l],
                   dt_bias[l], a_log[l], d_skip[l], ssm_norm[l], rel_bias, attn_norm[l], w_out[l],
                   norm2_pre[l], norm2_post[l], w_mlp_in[l], w_mlp_out[l])
    return x
```

```python
import functools
import math

import numpy as np
import jax
import jax.numpy as jnp
from jax import lax
from jax.experimental import pallas as pl
from jax.experimental.pallas import tpu as pltpu

D_MODEL = 1024
SSM_HEADS = 8
SSM_HEAD_DIM = 64
D_SSM = SSM_HEADS * SSM_HEAD_DIM
SSM_GROUPS = 2
SSM_STATE = 128
CONV_WIDTH = 4
CHUNK = 256
ATTN_HEADS = 8
ATTN_KV_HEADS = 2
ATTN_HEAD_DIM = 64
D_ATTN = ATTN_HEADS * ATTN_HEAD_DIM
D_KV = ATTN_KV_HEADS * ATTN_HEAD_DIM
IDX_HEADS = 8
IDX_DIM = 64
TOPK_MAX = 256
N_BUCKETS = 32
MAX_DISTANCE = 128
D_FF = 4 * D_MODEL
D_BC = SSM_GROUPS * SSM_STATE
D_XBC = D_SSM + 2 * D_BC
EPS = 1e-6

LANES = 128
V7X_VMEM_BYTES = 64 * 2 ** 20
VMEM_LIMIT = V7X_VMEM_BYTES * 3 // 4
VMEM_LIMIT_TAIL = V7X_VMEM_BYTES * 15 // 16
MISC_W = LANES
MISC_KIDX = 0
MISC_WIDX = IDX_DIM
MISC_DT = IDX_DIM + IDX_HEADS
OFF_Z = 0
OFF_XBC = OFF_Z + D_SSM
OFF_Q = OFF_XBC + D_XBC
OFF_QI = OFF_Q + D_ATTN
OFF_K = OFF_QI + IDX_HEADS * IDX_DIM
OFF_V = OFF_K + D_KV
OFF_MISC = OFF_V + D_KV
D_IN_PAD = OFF_MISC + MISC_W

BLK = 128
DSA_CHUNK = 256
V_ROWS = 80
ZERO_SLOT = 2
SEL_SIZES = 16
DEN_LO = 2.0 ** -80
DEN_HI = 2.0 ** 80
F32_HUGE = 3.0e38
LOG2E = math.log2(math.e)
ROW_TILE = 512
NEG = -1e30
INT_MIN = -2 ** 31

F32 = jnp.float32
BF16 = jnp.bfloat16
HI = lax.Precision.HIGHEST
NT = (((1,), (1,)), ((), ()))


def _silu(x):
    return x / (1.0 + jnp.exp(-x))


def _rms(x, w):
    return x * lax.rsqrt(jnp.mean(x * x, axis=-1, keepdims=True) + EPS) * w


def _split3(x):
    hi = x.astype(BF16)
    rest = x - hi.astype(F32)
    mid = rest.astype(BF16)
    return hi, mid, (rest - mid.astype(F32)).astype(BF16)


def _ada_kernel(c_ref, w_ref, b_ref, o_ref):
    s = _silu(c_ref[...])
    o_ref[...] = jnp.dot(s, w_ref[...], precision=HI, preferred_element_type=F32) + b_ref[...]


def _ada(c_pad, w, b):
    n = w.shape[1]
    tn = D_MODEL
    return pl.pallas_call(
        _ada_kernel,
        grid=(n // tn,),
        in_specs=[pl.BlockSpec((8, D_MODEL), lambda j: (0, 0)),
                  pl.BlockSpec((D_MODEL, tn), lambda j: (0, j)),
                  pl.BlockSpec((1, tn), lambda j: (0, j))],
        out_specs=pl.BlockSpec((8, tn), lambda j: (0, j)),
        out_shape=jax.ShapeDtypeStruct((8, n), F32),
        name="ada",
    )(c_pad, w, b)


def _inproj_kernel(x_ref, mod_ref, nw_ref, w_ref,
                   z_ref, xbc_ref, q_ref, k_ref, vt_ref, qi_ref, misc_ref, miscb_ref):
    x = x_ref[0]
    h = _rms(x, nw_ref[...]) * (1.0 + mod_ref[0, 1:2, :]) + mod_ref[0, 0:1, :]
    hb = h.astype(BF16)

    def seg(lo, width):
        return jnp.dot(hb, w_ref[:, lo:lo + width], preferred_element_type=F32)

    z_ref[0] = seg(OFF_Z, D_SSM)
    xbc_ref[0] = seg(OFF_XBC, D_XBC)
    rep = ATTN_HEADS // ATTN_KV_HEADS
    q_all = seg(OFF_Q, D_ATTN) * (ATTN_HEAD_DIM ** -0.5 * LOG2E)
    qi_all = seg(OFF_QI, IDX_HEADS * IDX_DIM)
    for j in range(x.shape[0] // BLK):
        q_t = jnp.transpose(q_all[j * BLK:(j + 1) * BLK, :])
        qi_t = jnp.transpose(qi_all[j * BLK:(j + 1) * BLK, :])
        for h in range(ATTN_HEADS):
            q_ref[0, j, h // rep, :, (h % rep) * BLK:(h % rep + 1) * BLK] = (
                q_t[h * ATTN_HEAD_DIM:(h + 1) * ATTN_HEAD_DIM, :].astype(BF16))
        for h in range(IDX_HEADS):
            qi_ref[0, j, :, h * BLK:(h + 1) * BLK] = qi_t[h * IDX_DIM:(h + 1) * IDX_DIM, :].astype(BF16)
    k_v_misc = seg(OFF_K, 2 * D_KV + MISC_W)
    k_ref[0] = k_v_misc[:, :D_KV].astype(BF16)
    v_misc = k_v_misc[:, D_KV:]
    ones_tile = jnp.where(lax.broadcasted_iota(jnp.int32, (V_ROWS - ATTN_HEAD_DIM, BLK), 0) == 0, 1.0, 0.0)
    for j in range(x.shape[0] // BLK):
        v_t = jnp.transpose(v_misc[j * BLK:(j + 1) * BLK, :D_KV])
        for g in range(ATTN_KV_HEADS):
            vt_ref[0, j, g, 0:ATTN_HEAD_DIM, :] = v_t[g * ATTN_HEAD_DIM:(g + 1) * ATTN_HEAD_DIM, :].astype(BF16)
            vt_ref[0, j, g, ATTN_HEAD_DIM:, :] = ones_tile.astype(BF16)
    misc = v_misc[:, D_KV:]
    misc_ref[0] = misc
    miscb_ref[0] = misc.astype(BF16)


def _inproj(x, mod, nw, w_perm):
    bsz, t, d = x.shape
    tm = min(ROW_TILE, t)

    def tok(width, dtype):
        return (pl.BlockSpec((1, tm, width), lambda b, i: (b, i, 0)),
                jax.ShapeDtypeStruct((bsz, t, width), dtype))

    vt = (pl.BlockSpec((1, tm // BLK, ATTN_KV_HEADS, V_ROWS, BLK), lambda b, i: (b, i, 0, 0, 0)),
          jax.ShapeDtypeStruct((bsz, t // BLK, ATTN_KV_HEADS, V_ROWS, BLK), BF16))
    gw = (ATTN_HEADS // ATTN_KV_HEADS) * BLK
    qt = (pl.BlockSpec((1, tm // BLK, ATTN_KV_HEADS, ATTN_HEAD_DIM, gw), lambda b, i: (b, i, 0, 0, 0)),
          jax.ShapeDtypeStruct((bsz, t // BLK, ATTN_KV_HEADS, ATTN_HEAD_DIM, gw), BF16))
    qit = (pl.BlockSpec((1, tm // BLK, IDX_DIM, IDX_HEADS * BLK), lambda b, i: (b, i, 0, 0)),
           jax.ShapeDtypeStruct((bsz, t // BLK, IDX_DIM, IDX_HEADS * BLK), BF16))
    outs = [tok(D_SSM, F32), tok(D_XBC, F32), qt, tok(D_KV, BF16), vt, qit,
            tok(MISC_W, F32), tok(MISC_W, BF16)]
    return pl.pallas_call(
        _inproj_kernel,
        grid=(bsz, t // tm),
        in_specs=[pl.BlockSpec((1, tm, d), lambda b, i: (b, i, 0)),
                  pl.BlockSpec((1, 6, d), lambda b, i: (b, 0, 0)),
                  pl.BlockSpec((1, d), lambda b, i: (0, 0)),
                  pl.BlockSpec((d, D_IN_PAD), lambda b, i: (0, 0))],
        out_specs=[o[0] for o in outs],
        out_shape=[o[1] for o in outs],
        compiler_params=pltpu.CompilerParams(
            dimension_semantics=("arbitrary", "arbitrary"), vmem_limit_bytes=VMEM_LIMIT),
        name="inproj",
    )(x, mod, nw, w_perm)


def _ssd_kernel(xbc_ref, z_ref, misc_ref, cw_ref, cb_ref, dtb_ref, a_ref, e_ref, dskip_ref, nw_ref,
                o_ref, xbuf, state):
    c = pl.program_id(1)
    lc = xbc_ref.shape[1]

    @pl.when(c == 0)
    def _():
        xbuf[0:8, :] = jnp.zeros((8, D_XBC), F32)
        state[...] = jnp.zeros(state.shape, F32)

    @pl.when(c > 0)
    def _():
        xbuf[0:8, :] = xbuf[lc:lc + 8, :]

    xbuf[8:lc + 8, :] = xbc_ref[0]
    u = cb_ref[...]
    xall = xbuf[...]
    for k in range(CONV_WIDTH):
        back = CONV_WIDTH - 1 - k
        xk = xall if back == 0 else pltpu.roll(xall, back, 0)
        u = u + xk[8:8 + lc] * cw_ref[k:k + 1, :]
    u = _silu(u)
    xs = u[:, :D_SSM]
    bm = u[:, D_SSM:D_SSM + D_BC]
    cm = u[:, D_SSM + D_BC:]

    raw = misc_ref[0] + dtb_ref[...]
    dt_t = jnp.maximum(raw, 0.0) + jnp.log1p(jnp.exp(-jnp.abs(raw)))
    adt_t = dt_t * a_ref[...]
    row = lax.broadcasted_iota(jnp.int32, (lc, lc), 0)
    col = lax.broadcasted_iota(jnp.int32, (lc, lc), 1)
    causal = col <= row
    tril = jnp.where(causal, 1.0, 0.0).astype(BF16)
    expand = e_ref[...]
    acs_t = sum(jnp.dot(tril, piece, preferred_element_type=F32) for piece in _split3(adt_t))
    dt_e = sum(jnp.dot(piece, expand, preferred_element_type=F32) for piece in _split3(dt_t))
    acs_e = sum(jnp.dot(piece, expand, preferred_element_type=F32) for piece in _split3(acs_t))
    acs_row = jnp.transpose(acs_t)

    xdt = xs * dt_e
    a_last = acs_e[lc - 1:lc, :]
    xdec = (xdt * jnp.exp2(a_last - acs_e)).astype(BF16)
    xdt_b = xdt.astype(BF16)
    bm_b = bm.astype(BF16)
    cm_b = cm.astype(BF16)
    bm_t = jnp.transpose(bm).astype(BF16)

    heads_per_group = SSM_HEADS // SSM_GROUPS
    gw = heads_per_group * SSM_HEAD_DIM
    y_parts = []
    off_parts = []
    for g in range(SSM_GROUPS):
        bg = bm_b[:, g * SSM_STATE:(g + 1) * SSM_STATE]
        cg = cm_b[:, g * SSM_STATE:(g + 1) * SSM_STATE]
        cb = lax.dot_general(cg, bg, NT, preferred_element_type=F32)
        for r in range(heads_per_group):
            hd = g * heads_per_group + r
            a_col = acs_t[:, MISC_DT + hd:MISC_DT + hd + 1]
            a_row = acs_row[MISC_DT + hd:MISC_DT + hd + 1, :]
            lmat = jnp.exp2(jnp.where(causal, a_col - a_row, -jnp.inf))
            mh = (cb * lmat).astype(BF16)
            y_parts.append(jnp.dot(mh, xdt_b[:, hd * SSM_HEAD_DIM:(hd + 1) * SSM_HEAD_DIM],
                                   preferred_element_type=F32))
        prev = state[g]
        off_parts.append(jnp.dot(cg, prev.astype(BF16), preferred_element_type=F32))
        st = jnp.dot(bm_t[g * SSM_STATE:(g + 1) * SSM_STATE, :], xdec[:, g * gw:(g + 1) * gw],
                     preferred_element_type=F32)
        state[g] = prev * jnp.exp2(a_last[:, g * gw:(g + 1) * gw]) + st
    y = (jnp.concatenate(y_parts, axis=1)
         + jnp.concatenate(off_parts, axis=1) * jnp.exp2(acs_e)
         + xs * dskip_ref[...])
    o_ref[0] = _rms(y * _silu(z_ref[0]), nw_ref[...]).astype(BF16)


def _ssd(xbc, z, misc, cw, cb, dtb_t, a_t, expand, dskip_e, nw):
    bsz, t, _ = xbc.shape
    lc = math.gcd(t, CHUNK)
    nc = t // lc
    gw = (SSM_HEADS // SSM_GROUPS) * SSM_HEAD_DIM

    def const(shape):
        return pl.BlockSpec(shape, lambda b, c: tuple(0 for _ in shape))

    return pl.pallas_call(
        _ssd_kernel,
        grid=(bsz, nc),
        in_specs=[pl.BlockSpec((1, lc, D_XBC), lambda b, c: (b, c, 0)),
                  pl.BlockSpec((1, lc, D_SSM), lambda b, c: (b, c, 0)),
                  pl.BlockSpec((1, lc, MISC_W), lambda b, c: (b, c, 0)),
                  const((CONV_WIDTH, D_XBC)), const((1, D_XBC)), const((1, MISC_W)), const((1, MISC_W)),
                  const((MISC_W, D_SSM)), const((1, D_SSM)), const((1, D_SSM))],
        out_specs=pl.BlockSpec((1, lc, D_SSM), lambda b, c: (b, c, 0)),
        out_shape=jax.ShapeDtypeStruct((bsz, t, D_SSM), BF16),
        scratch_shapes=[pltpu.VMEM((lc + 8, D_XBC), F32),
                        pltpu.VMEM((SSM_GROUPS, SSM_STATE, gw), F32)],
        compiler_params=pltpu.CompilerParams(
            dimension_semantics=("arbitrary", "arbitrary"), vmem_limit_bytes=VMEM_LIMIT),
        name="ssd",
    )(xbc, z, misc, cw, cb, dtb_t, a_t, expand, dskip_e, nw)


def _bucket_tiles():
    s = np.arange(BLK)[:, None]
    t = np.arange(BLK)[None, :]
    max_exact = N_BUCKETS // 2
    tiles = []
    for d0 in (0, BLK):
        n = np.maximum(d0 + t - s, 0)
        large = {}
        for dt_ in (np.float32, np.float64):
            nf = np.maximum(n, 1).astype(dt_)
            lg = max_exact + (np.log(nf / dt_(max_exact)) / dt_(math.log(MAX_DISTANCE / max_exact))
                              * dt_(N_BUCKETS - max_exact)).astype(np.int32)
            large[dt_] = np.minimum(lg, N_BUCKETS - 1)
        assert (large[np.float32] == large[np.float64]).all()
        tiles.append(np.where(n < max_exact, n, large[np.float64]).astype(np.int32))
    assert BLK + 1 >= MAX_DISTANCE
    return np.stack(tiles)


def _to_key(x):
    bits = lax.bitcast_convert_type(x, jnp.int32)
    key = bits ^ (lax.shift_right_arithmetic(bits, 31) & jnp.int32(0x7FFFFFFF))
    return jnp.where(key == -1, 0, key)


def _transpose32(rows):
    rows = list(rows)
    mask, j = 0x0000FFFF, 16
    while j:
        k = 0
        while k < 32:
            t = (rows[k] ^ lax.shift_right_logical(rows[k + j], jnp.int32(j))) & jnp.int32(mask)
            rows[k] = rows[k] ^ t
            rows[k + j] = rows[k + j] ^ lax.shift_left(t, jnp.int32(j))
            k = (k + j + 1) & ~j
        j >>= 1
        mask ^= (mask << j) & 0xFFFFFFFF
    return rows


def _dsa_kernel(relb_ref, bkt_ref, q_ref, qic_ref, miscc_ref, qin_ref, miscn_ref, k_ref, vt_ref, kidx_ref,
                nw_ref, o_ref, keys, planes, live, thr_s, cnt_s, bias_s, qcat, qicat, wcat, m_s, acc_s, p_s,
                *, topk, idx_bits, nq):
    b = pl.program_id(0)
    qi = pl.program_id(1)
    tq = BLK
    i32 = jnp.int32
    rep = ATTN_HEADS // ATTN_KV_HEADS
    gw = rep * tq
    ch = DSA_CHUNK
    ch_blks = ch // BLK
    rows_per_chunk = ch // 32

    @pl.when((b == 0) & (qi == 0))
    def _init_bias():
        for d in range(2):
            bk = bkt_ref[d]
            for h in range(ATTN_HEADS):
                far = relb_ref[N_BUCKETS - 1, h]
                acc = jnp.zeros((BLK, tq), F32)
                for n in range(N_BUCKETS - 1):
                    acc = jnp.where(bk == n, relb_ref[n, h] - far, acc)
                bias_s[d, h // rep, :, (h % rep) * BLK:(h % rep + 1) * BLK] = acc * LOG2E
        bias_s[2] = jnp.zeros((ATTN_KV_HEADS, BLK, gw), F32)
        for w in range(32):
            planes[w] = jnp.zeros(planes.shape[1:], i32)
        p_s[ZERO_SLOT] = jnp.zeros(p_s.shape[1:], BF16)

    row_c = lax.broadcasted_iota(i32, (ch, tq), 0)
    lane_c = lax.broadcasted_iota(i32, (ch, tq), 1)

    def chunk_start(c):
        return pl.multiple_of(c * ch, ch)

    def n_chunks(blk):
        return (blk + ch_blks) // ch_blks

    def load_indexer(qidx_ref, misc_ref):
        qicat[...] = qidx_ref[0, 0]
        w_t = jnp.transpose(misc_ref[0])[MISC_WIDX:MISC_WIDX + IDX_HEADS, :] * (
            IDX_HEADS ** -0.5 * IDX_DIM ** -0.5)
        for h in range(IDX_HEADS):
            wcat[:, h * BLK:(h + 1) * BLK] = jnp.broadcast_to(w_t[h:h + 1, :], (8, BLK))

    def score_chunk(c, blk, slot):
        s0 = chunk_start(c)
        kb = kidx_ref[0, pl.ds(s0, ch), MISC_KIDX:MISC_KIDX + IDX_DIM]
        s = jnp.dot(kb, qicat[...], preferred_element_type=F32)
        r = jnp.maximum(s, 0.0) * wcat[0:1, :]
        acc = r[:, 0:BLK]
        for h in range(1, IDX_HEADS):
            acc = acc + r[:, h * BLK:(h + 1) * BLK]
        acc = jnp.where(s0 + row_c <= blk * BLK + lane_c, acc, -jnp.inf)
        keys[slot, pl.ds(s0, ch), :] = _to_key(acc)

    def plane_chunk(c, slot):
        ku = (keys[slot, pl.ds(chunk_start(c), ch), :] ^ i32(INT_MIN)).reshape(ch // (32 * 8), 32, 8, tq)
        words = _transpose32([ku[:, j] for j in range(32)])
        r0 = pl.multiple_of(c * rows_per_chunk, rows_per_chunk)
        for w in range(32):
            planes[w, pl.ds(r0, rows_per_chunk), :] = words[w].reshape(rows_per_chunk, tq)

    def select(blk, slot):
        nch = n_chunks(blk)
        used = nch * rows_per_chunk

        def ones_in(words):
            return jnp.sum(lax.population_count(words), axis=0, keepdims=True)

        def passes(nr):
            live[0:nr] = jnp.where(lax.broadcasted_iota(i32, (nr, tq), 0) < used, i32(-1), i32(0))

            def bit_body(w, st):
                thr_u, n_above, n_set = st
                take = n_above + n_set >= topk
                thr_u = jnp.where(take, thr_u | lax.shift_left(i32(1), i32(31) - w), thr_u)
                n_above = jnp.where(take, n_above, n_above + n_set)
                still = live[0:nr] & (planes[w, 0:nr] ^ jnp.where(take, i32(0), i32(-1)))
                live[0:nr] = still
                return thr_u, n_above, ones_in(still & planes[jnp.minimum(w + 1, 31), 0:nr])

            zero = jnp.zeros((1, tq), i32)
            thr_u, n_above, _ = lax.fori_loop(
                0, 32, bit_body, (zero, zero, ones_in(live[0:nr] & planes[0, 0:nr])))
            thr_s[slot] = thr_u ^ i32(INT_MIN)
            cnt_s[0] = n_above
            cnt_s[1] = ones_in(live[0:nr])

        n_sizes = max(1, min(SEL_SIZES, live.shape[0] // 8))
        size_step = live.shape[0] // n_sizes
        for v in range(n_sizes):
            pl.when((used > v * size_step) & (used <= (v + 1) * size_step))(
                functools.partial(passes, (v + 1) * size_step))
        thr = thr_s[slot]
        n_gt = cnt_s[0]
        n_eq = cnt_s[1]
        need = topk - n_gt

        def break_ties():
            def count(pred):
                def body(c, cnt):
                    s0 = chunk_start(c)
                    hit = pred(keys[slot, pl.ds(s0, ch), :], s0).astype(i32)
                    n = ch
                    while n > 8:
                        n //= 2
                        hit = hit[:n] + hit[n:2 * n]
                    return cnt + hit
                cnt = lax.fori_loop(0, nch, body, jnp.zeros((8, tq), i32))
                return jnp.sum(cnt, axis=0, keepdims=True)

            def jbody(it, p):
                cand = p + lax.shift_left(i32(1), i32(idx_bits - 1) - it)
                c = count(lambda kk, s0: (kk == thr) & (s0 + row_c < cand))
                return jnp.where(c < need, cand, p)
            last = lax.fori_loop(0, idx_bits, jbody, jnp.zeros((1, tq), i32))

            def demote(c, carry):
                s0 = chunk_start(c)
                kk = keys[slot, pl.ds(s0, ch), :]
                keys[slot, pl.ds(s0, ch), :] = jnp.where((kk == thr) & (s0 + row_c > last), INT_MIN, kk)
                return carry
            lax.fori_loop(0, nch, demote, 0)

        return (n_eq > need).astype(i32), break_ties

    def any_lane(flags):
        return jnp.max(flags, axis=1, keepdims=True)

    @pl.when(qi == 0)
    def _first_block():
        load_indexer(qic_ref, miscc_ref)
        score_chunk(0, 0, 0)
        plane_chunk(0, 0)
        tied, break_ties = select(0, 0)
        pl.when(any_lane(tied)[0, 0] > 0)(break_ties)

    slot_b = qi % 2
    slot_a = 1 - slot_b
    blk_a = jnp.minimum(qi + 1, nq - 1)
    nch_b = n_chunks(qi)
    nch_a = n_chunks(blk_a)
    load_indexer(qin_ref, miscn_ref)
    thr = thr_s[slot_b]

    for g in range(ATTN_KV_HEADS):
        qcat[g] = q_ref[0, 0, g]

    def pen_of(s0, causal):
        sel = keys[slot_b, pl.ds(s0, ch), :] >= thr
        if causal:
            sel = sel & (s0 + row_c <= qi * BLK + lane_c)
        pen = jnp.where(sel, 0.0, NEG)
        return jnp.concatenate([pen] * rep, axis=1)

    def near_bias(c, g):
        return jnp.concatenate(
            [bias_s[jnp.clip(qi - (c * ch_blks + u), 0, 2), g] for u in range(ch_blks)], axis=0)

    def vt_chunk(c, g):
        return jnp.concatenate([vt_ref[0, c * ch_blks + u, g] for u in range(ch_blks)], axis=1)

    for g in range(ATTN_KV_HEADS):
        acc_s[g] = jnp.zeros((V_ROWS, gw), F32)

    def weigh_values(c, slot):
        for g in range(ATTN_KV_HEADS):
            acc_s[g] += jnp.dot(vt_chunk(c, g), p_s[slot, g], preferred_element_type=F32)

    def att_chunk(c, near, slots=None):
        prev_slot, cur_slot = slots if slots is not None else ((c + 1) % 2, c % 2)
        if not (isinstance(cur_slot, int) and cur_slot == 1):
            prev_slot = jnp.where(c == 0, ZERO_SLOT, prev_slot)
        weigh_values(jnp.maximum(c - 1, 0), prev_slot)
        s0 = chunk_start(c)
        pen = pen_of(s0, near)
        kblk = k_ref[0, pl.ds(s0, ch), :]
        for g in range(ATTN_KV_HEADS):
            s = jnp.dot(kblk[:, g * ATTN_HEAD_DIM:(g + 1) * ATTN_HEAD_DIM], qcat[g],
                        preferred_element_type=F32)
            if near:
                s = s + near_bias(c, g)
            p_s[cur_slot, g] = jnp.exp2(s + pen).astype(BF16)

    def fused(c, near, slots=None):
        plane_chunk(c, slot_a)
        att_chunk(c, near, slots)
        score_chunk(c + 1, blk_a, slot_a)

    def far_pair(j, carry):
        fused(2 * j, False, (1, 0))
        fused(2 * j + 1, False, (0, 1))
        return carry

    def far_quad(j, carry):
        far_pair(2 * j, carry)
        return far_pair(2 * j + 1, carry)

    def far_oct(j, carry):
        far_quad(2 * j, carry)
        return far_quad(2 * j + 1, carry)

    def far_body(c, carry):
        fused(c, False, (1, 0))
        return carry

    def near_body(c, carry):
        fused(c, True)
        return carry

    score_chunk(0, blk_a, slot_a)
    n_far = jnp.maximum(qi - 1, 0) // ch_blks
    lax.fori_loop(0, n_far // 8, far_oct, 0)
    lax.fori_loop((n_far // 8) * 2, n_far // 4, far_quad, 0)
    lax.fori_loop((n_far // 4) * 2, n_far // 2, far_pair, 0)
    lax.fori_loop((n_far // 2) * 2, n_far, far_body, 0)
    lax.fori_loop(n_far, nch_b - 1, near_body, 0)
    @pl.when(nch_a > nch_b)
    def _one_more_chunk():
        score_chunk(nch_b, blk_a, slot_a)
        plane_chunk(nch_b, slot_a)

    plane_chunk(nch_b - 1, slot_a)
    att_chunk(nch_b - 1, True)
    weigh_values(nch_b - 1, (nch_b - 1) % 2)
    tied, break_ties = select(blk_a, slot_a)

    n_ok = jnp.zeros((1, gw), i32)
    for g in range(ATTN_KV_HEADS):
        a = acc_s[g]
        den = a[ATTN_HEAD_DIM:ATTN_HEAD_DIM + 1, :]
        top = jnp.max(jnp.abs(a[0:ATTN_HEAD_DIM, :]), axis=0, keepdims=True)
        n_ok = n_ok + ((den > DEN_LO) & (den < DEN_HI) & (top < F32_HUGE)).astype(i32)

    rare = (any_lane(tied) + 2 * any_lane((n_ok < ATTN_KV_HEADS).astype(i32)))[0, 0]
    pl.when(rare % 2 == 1)(break_ties)

    @pl.when(rare >= 2)
    def _exact():
        for g in range(ATTN_KV_HEADS):
            m_s[g] = jnp.full((1, gw), NEG, F32)
            acc_s[g] = jnp.zeros((V_ROWS, gw), F32)

        def body(c, carry):
            s0 = chunk_start(c)
            pen = pen_of(s0, True)
            kblk = k_ref[0, pl.ds(s0, ch), :]
            for g in range(ATTN_KV_HEADS):
                s = jnp.dot(kblk[:, g * ATTN_HEAD_DIM:(g + 1) * ATTN_HEAD_DIM], qcat[g],
                            preferred_element_type=F32) + near_bias(c, g) + pen
                m_old = m_s[g]
                m_new = jnp.maximum(m_old, jnp.max(s, axis=0, keepdims=True))
                p = jnp.exp2(s - m_new).astype(BF16)
                acc_s[g] = acc_s[g] * jnp.exp2(m_old - m_new) + jnp.dot(
                    vt_chunk(c, g), p, preferred_element_type=F32)
                m_s[g] = m_new
            return carry
        lax.fori_loop(0, nch_b, body, 0)

    parts = []
    for g in range(ATTN_KV_HEADS):
        a = acc_s[g]
        o = a[0:ATTN_HEAD_DIM, :] * (1.0 / a[ATTN_HEAD_DIM:ATTN_HEAD_DIM + 1, :])
        parts += [o[:, r * tq:(r + 1) * tq] for r in range(rep)]
    out_t = jnp.concatenate(parts, axis=0)
    inv = lax.rsqrt(jnp.mean(out_t * out_t, axis=0, keepdims=True) + EPS)
    o_ref[0, 0] = (out_t * inv * nw_ref[...]).astype(BF16)


def _dsa(rel_bias, q, qi, misc, k, vt, miscb, nw):
    bsz, t, _ = k.shape
    nq = t // BLK
    topk = min(TOPK_MAX, t // 4)
    idx_bits = max(1, (t - 1).bit_length())
    gw = (ATTN_HEADS // ATTN_KV_HEADS) * BLK
    bkt = jnp.asarray(_bucket_tiles())
    kern = functools.partial(_dsa_kernel, topk=topk, idx_bits=idx_bits, nq=nq)

    def nxt(b, i):
        return (b, jnp.minimum(i + 1, nq - 1), 0)

    return pl.pallas_call(
        kern,
        grid=(bsz, nq),
        in_specs=[pl.BlockSpec(memory_space=pltpu.SMEM),
                  pl.BlockSpec((2, BLK, BLK), lambda b, i: (0, 0, 0)),
                  pl.BlockSpec((1, 1, ATTN_KV_HEADS, ATTN_HEAD_DIM, gw), lambda b, i: (b, i, 0, 0, 0)),
                  pl.BlockSpec((1, 1, IDX_DIM, IDX_HEADS * BLK), lambda b, i: (b, i, 0, 0)),
                  pl.BlockSpec((1, BLK, MISC_W), lambda b, i: (b, i, 0)),
                  pl.BlockSpec((1, 1, IDX_DIM, IDX_HEADS * BLK), lambda b, i: nxt(b, i) + (0,)),
                  pl.BlockSpec((1, BLK, MISC_W), nxt),
                  pl.BlockSpec((1, t, D_KV), lambda b, i: (b, 0, 0)),
                  pl.BlockSpec((1, nq, ATTN_KV_HEADS, V_ROWS, BLK), lambda b, i: (b, 0, 0, 0, 0)),
                  pl.BlockSpec((1, t, MISC_W), lambda b, i: (b, 0, 0)),
                  pl.BlockSpec((D_ATTN, BLK), lambda b, i: (0, 0))],
        out_specs=pl.BlockSpec((1, 1, D_ATTN, BLK), lambda b, i: (b, i, 0, 0)),
        out_shape=jax.ShapeDtypeStruct((bsz, nq, D_ATTN, BLK), BF16),
        scratch_shapes=[pltpu.VMEM((2, t, BLK), jnp.int32),
                        pltpu.VMEM((32, t // 32, BLK), jnp.int32),
                        pltpu.VMEM((t // 32, BLK), jnp.int32),
                        pltpu.VMEM((2, 1, BLK), jnp.int32),
                        pltpu.VMEM((2, 1, BLK), jnp.int32),
                        pltpu.VMEM((3, ATTN_KV_HEADS, BLK, gw), F32),
                        pltpu.VMEM((ATTN_KV_HEADS, ATTN_HEAD_DIM, gw), BF16),
                        pltpu.VMEM((IDX_DIM, IDX_HEADS * BLK), BF16),
                        pltpu.VMEM((8, IDX_HEADS * BLK), F32),
                        pltpu.VMEM((ATTN_KV_HEADS, 1, gw), F32),
                        pltpu.VMEM((ATTN_KV_HEADS, V_ROWS, gw), F32),
                        pltpu.VMEM((ZERO_SLOT + 1, ATTN_KV_HEADS, DSA_CHUNK, gw), BF16)],
        compiler_params=pltpu.CompilerParams(
            dimension_semantics=("arbitrary", "arbitrary"), vmem_limit_bytes=VMEM_LIMIT),
        name="dsa",
    )(rel_bias, bkt, q, qi, misc, qi, misc, k, vt, miscb, nw)


def _tail_kernel(x_ref, ys_ref, ya_ref, mod_ref, wo_ref, n1_ref, n2_ref, n3_ref, w1_ref, w2_ref, o_ref):
    x = x_ref[0]
    ya = jnp.concatenate([jnp.transpose(ya_ref[0, j].astype(F32)).astype(BF16)
                          for j in range(ya_ref.shape[1])], axis=0)
    mix = (jnp.dot(ys_ref[0], wo_ref[0:D_SSM, :], preferred_element_type=F32)
           + jnp.dot(ya, wo_ref[D_SSM:, :], preferred_element_type=F32))
    x1 = x + mod_ref[0, 2:3, :] * _rms(mix, n1_ref[...])
    h = (_rms(x1, n2_ref[...]) * (1.0 + mod_ref[0, 4:5, :]) + mod_ref[0, 3:4, :]).astype(BF16)
    f = jnp.zeros(x.shape, F32)
    step = D_MODEL
    for c0 in range(0, D_FF, step):
        a = jnp.maximum(jnp.dot(h, w1_ref[:, c0:c0 + step], preferred_element_type=F32), 0.0)
        f = f + jnp.dot((a * a).astype(BF16), w2_ref[c0:c0 + step, :], preferred_element_type=F32)
    o_ref[0] = x1 + mod_ref[0, 5:6, :] * _rms(f, n3_ref[...])


def _tail(x, ys, ya, mod, wo, n1, n2, n3, w1, w2):
    bsz, t, d = x.shape
    tm = min(ROW_TILE, t)

    def const(shape):
        return pl.BlockSpec(shape, lambda b, i: tuple(0 for _ in shape))

    return pl.pallas_call(
        _tail_kernel,
        grid=(bsz, t // tm),
        in_specs=[pl.BlockSpec((1, tm, d), lambda b, i: (b, i, 0)),
                  pl.BlockSpec((1, tm, D_SSM), lambda b, i: (b, i, 0)),
                  pl.BlockSpec((1, tm // BLK, D_ATTN, BLK), lambda b, i: (b, i, 0, 0)),
                  pl.BlockSpec((1, 6, d), lambda b, i: (b, 0, 0)),
                  const((D_SSM + D_ATTN, d)), const((1, d)), const((1, d)), const((1, d)),
                  const((d, D_FF)), const((D_FF, d))],
        out_specs=pl.BlockSpec((1, tm, d), lambda b, i: (b, i, 0)),
        out_shape=jax.ShapeDtypeStruct((bsz, t, d), F32),
        compiler_params=pltpu.CompilerParams(
            dimension_semantics=("arbitrary", "arbitrary"), vmem_limit_bytes=VMEM_LIMIT_TAIL),
        name="tail",
    )(x, ys, ya, mod, wo, n1, n2, n3, w1, w2)


def _permute_w_in(w):
    sizes = [D_SSM, D_XBC, SSM_HEADS, D_ATTN, D_KV, D_KV, IDX_HEADS * IDX_DIM, IDX_DIM, IDX_HEADS]
    pts = np.cumsum(sizes)[:-1].tolist()
    z, xbc, dt, q, k, v, qidx, kidx, widx = jnp.split(w, pts, axis=-1)
    pad = jnp.zeros((w.shape[0], MISC_W - IDX_DIM - IDX_HEADS - SSM_HEADS), w.dtype)
    return jnp.concatenate([z, xbc, q, qidx, k, v, kidx, widx, dt, pad], axis=-1).astype(BF16)


def _layer(x, c_pad, w_ada, b_ada, norm1_pre, norm1_post, w_in, conv_w, conv_b, dt_bias, a_log, d_skip,
           ssm_norm, rel_bias, attn_norm, w_out, norm2_pre, norm2_post, w_mlp_in, w_mlp_out):
    bsz, t, d = x.shape
    mod = _ada(c_pad, w_ada, b_ada[None, :])[:bsz].reshape(bsz, 6, d)
    z, xbc, q, k, vt, qi, misc, miscb = _inproj(x, mod, norm1_pre[None, :], _permute_w_in(w_in))

    lane = jnp.arange(MISC_W)
    is_dt = (lane >= MISC_DT) & (lane < MISC_DT + SSM_HEADS)
    head_of = jnp.clip(lane - MISC_DT, 0, SSM_HEADS - 1)
    dtb_t = jnp.where(is_dt, dt_bias[head_of], 0.0)[None, :]
    a_t = jnp.where(is_dt, (-jnp.exp(a_log.astype(F32)) * LOG2E)[head_of], 0.0)[None, :]
    expand = ((lane[:, None] - MISC_DT) == (jnp.arange(D_SSM)[None, :] // SSM_HEAD_DIM)).astype(BF16)
    dskip_e = jnp.repeat(d_skip.astype(F32), SSM_HEAD_DIM)[None, :]
    y_ssd = _ssd(xbc, z, misc, conv_w, conv_b[None, :], dtb_t, a_t, expand, dskip_e, ssm_norm[None, :])

    attn_gain = jnp.broadcast_to(attn_norm.astype(F32)[:, None], (D_ATTN, BLK))
    y_att = _dsa(rel_bias, q, qi, misc, k, vt, miscb, attn_gain)

    return _tail(x, y_ssd, y_att, mod, w_out.astype(BF16), norm1_post[None, :], norm2_pre[None, :],
                 norm2_post[None, :], w_mlp_in.astype(BF16), w_mlp_out.astype(BF16))


def kernel(x, c, w_ada, b_ada, norm1_pre, norm1_post, w_in, conv_w, conv_b, dt_bias, a_log, d_skip,
           ssm_norm, rel_bias, attn_norm, w_out, norm2_pre, norm2_post, w_mlp_in, w_mlp_out):
    bsz = x.shape[0]
    assert bsz <= 8 and x.shape[1] % DSA_CHUNK == 0 and x.shape[2] == D_MODEL
    c_pad = jnp.zeros((8, D_MODEL), F32).at[:bsz].set(c)
    for l in range(w_ada.shape[0]):
        x = _layer(x, c_pad, w_ada[l], b_ada[l], norm1_pre[l], norm1_post[l], w_in[l], conv_w[l], conv_b[l],
                   dt_bias[l], a_log[l], d_skip[l], ssm_norm[l], rel_bias, attn_norm[l], w_out[l],
                   norm2_pre[l], norm2_post[l], w_mlp_in[l], w_mlp_out[l])
    return x
```
